```python
import math
import jax, jax.numpy as jnp
from jax import lax
import numpy as np

D_MODEL = 2048
BATCH = 2
SEQ = 8192
DEPTH = 1
DEC_BATCH = 32
DEC_SEQ = 64
PAST_LEN = 2048

CHUNK = 64
D_CONV = 1024
CONV_WIDTH = 3
HEAD_DIM = 64
N_HEADS = 16
N_KV_HEADS = 2
Q_PER_KV = N_HEADS // N_KV_HEADS
D_ATTN = N_HEADS * HEAD_DIM
D_KV = N_KV_HEADS * HEAD_DIM
WINDOW = 128
WINDOW_CHUNKS = WINDOW // CHUNK
ATTN_SCALE = 1.0 / math.sqrt(HEAD_DIM)
N_BUCKETS = 32
MAX_DISTANCE = 128
N_GROUPS = 4
EXPERTS_PER_GROUP = 8
N_EXPERTS = N_GROUPS * EXPERTS_PER_GROUP
TOP_K = 2
D_EXPERT = 512
D_IN_ALL = 3 * D_CONV + D_ATTN + 2 * D_KV + 2 * D_MODEL
EPS = 1e-6
NEG_INF = -1e30

kernel_name = "hybrid_gated_conv_swa_hiermoe_stream_step"


def rms_norm(x, g):
    xf = x.astype(jnp.float32)
    r = lax.rsqrt(jnp.mean(xf * xf, axis=-1, keepdims=True) + EPS)
    return (xf * r).astype(x.dtype) * g


def rel_bucket(rel):
    nb = N_BUCKETS // 2
    ret = (rel > 0).astype(jnp.int32) * nb
    n = jnp.abs(rel)
    max_exact = nb // 2
    nf = jnp.maximum(n, 1).astype(jnp.float32)
    large = max_exact + (jnp.log(nf / max_exact) / math.log(MAX_DISTANCE / max_exact)
                         * (nb - max_exact)).astype(jnp.int32)
    large = jnp.minimum(large, nb - 1)
    return ret + jnp.where(n < max_exact, n, large)


def rel_bias(q_pos, k_pos, table):
    b = rel_bucket(k_pos[None, :] - q_pos[:, None])
    bias = table[b].astype(jnp.float32)
    bias = jnp.transpose(bias, (2, 0, 1))
    return bias.reshape(N_KV_HEADS, Q_PER_KV, q_pos.shape[0], k_pos.shape[0])


def sink_attention(q, k, v, bias, mask, sinks):
    s = jnp.einsum('...qngd,...knd->...ngqk', q, k).astype(jnp.float32) * ATTN_SCALE + bias
    if mask is not None:
        s = jnp.where(mask, s, NEG_INF)
    sink = jnp.broadcast_to(sinks.astype(jnp.float32).reshape(N_KV_HEADS, Q_PER_KV, 1, 1),
                            s.shape[:-1] + (1,))
    p = jax.nn.softmax(jnp.concatenate([s, sink], axis=-1), axis=-1)[..., :-1]
    return jnp.einsum('...ngqk,...knd->...qngd', p.astype(v.dtype), v)


def attn_prompt(q, k, v, sinks, table):
    b, s = q.shape[0], q.shape[1]
    n_c = s // CHUNK
    span = (WINDOW_CHUNKS + 1) * CHUNK
    qb = q.reshape(b, n_c, CHUNK, N_KV_HEADS, Q_PER_KV, HEAD_DIM)

    def band(a):
        ap = jnp.pad(a, ((0, 0), (WINDOW, 0), (0, 0), (0, 0)))
        ap = ap.reshape(b, n_c + WINDOW_CHUNKS, CHUNK, N_KV_HEADS, HEAD_DIM)
        return jnp.concatenate([ap[:, j:j + n_c] for j in range(WINDOW_CHUNKS + 1)], axis=2)

    kb, vb = band(k), band(v)
    k_off = jnp.arange(span, dtype=jnp.int32) - WINDOW
    bias = rel_bias(jnp.arange(CHUNK, dtype=jnp.int32), k_off, table)
    key_pos = jnp.arange(n_c, dtype=jnp.int32)[:, None] * CHUNK + k_off[None, :]
    mask = (key_pos >= 0)[None, :, None, None, None, :]
    o = sink_attention(qb, kb, vb, bias, mask, sinks)
    return o.reshape(b, s, D_ATTN), k[:, -WINDOW:], v[:, -WINDOW:]


def attn_sample(q, k, v, k_cache, v_cache, sinks, table):
    b, t = q.shape[0], q.shape[1]
    qg = q.reshape(b, t, N_KV_HEADS, Q_PER_KV, HEAD_DIM)
    kk = jnp.concatenate([k_cache, k], axis=1)
    vv = jnp.concatenate([v_cache, v], axis=1)
    q_pos = PAST_LEN + jnp.arange(t, dtype=jnp.int32)
    k_pos = PAST_LEN - WINDOW + jnp.arange(WINDOW + t, dtype=jnp.int32)
    bias = rel_bias(q_pos, k_pos, table)
    o = sink_attention(qg, kk, vv, bias, None, sinks)
    return o.reshape(b, t, D_ATTN), kk[:, -WINDOW:], vv[:, -WINDOW:]


def short_conv(u, prev, w):
    t = u.shape[1]
    up = jnp.concatenate([prev, u], axis=1)
    y = sum(w[i] * up[:, i:i + t] for i in range(CONV_WIDTH))
    return y, up[:, -(CONV_WIDTH - 1):]


def token_mix(h, conv_prev, attn_fn, w_in, conv_w, w_conv_out, w_attn_out, w_o):
    sizes = (D_CONV, D_CONV, D_CONV, D_ATTN, D_KV, D_KV, D_MODEL, D_MODEL)
    idx = [int(i) for i in np.cumsum(sizes)[:-1]]
    b_g, c_g, x_c, q, k, v, g_a, g_b = jnp.split(h @ w_in, idx, axis=-1)
    bsz, t = h.shape[0], h.shape[1]
    conv_y, conv_new = short_conv(c_g * x_c, conv_prev, conv_w)
    y_a = (b_g * conv_y) @ w_conv_out
    o, k_new, v_new = attn_fn(q.reshape(bsz, t, N_HEADS, HEAD_DIM),
                              k.reshape(bsz, t, N_KV_HEADS, HEAD_DIM),
                              v.reshape(bsz, t, N_KV_HEADS, HEAD_DIM))
    y_b = o @ w_attn_out
    m = jax.nn.sigmoid(g_a) * y_a + jax.nn.sigmoid(g_b) * y_b
    return m @ w_o, conv_new, k_new, v_new


def hier_moe(h, w_group, b_group, w_er, b_er, w_gate, w_up, w_down):
    shp = h.shape
    t = h.reshape(-1, D_MODEL)
    gp = jax.nn.softmax((t @ w_group).astype(jnp.float32) + b_group, axis=-1)
    gw, gsel = lax.top_k(gp, 1)
    el = ((t @ w_er).astype(jnp.float32) + b_er).reshape(-1, N_GROUPS, EXPERTS_PER_GROUP)
    el_sel = jnp.take_along_axis(el, gsel[:, :, None], axis=1)[:, 0]
    top_v, top_i = lax.top_k(el_sel, TOP_K)
    comb = gw * jax.nn.softmax(top_v, axis=-1)
    e_idx = gsel * EXPERTS_PER_GROUP + top_i
    comb_mat = jnp.sum(jax.nn.one_hot(e_idx, N_EXPERTS, dtype=jnp.float32) * comb[..., None], axis=1)
    comb_mat = comb_mat.astype(t.dtype)
    out = jnp.zeros_like(t)
    for e in range(N_EXPERTS):
        hid = jax.nn.silu(t @ w_gate[e]) * (t @ w_up[e])
        out = out + comb_mat[:, e:e + 1] * (hid @ w_down[e])
    return out.reshape(shp)


def setup_inputs(seed: int = 0) -> dict:
    key = jax.random.key(seed)
    ks = jax.random.split(key, 24)

    def nrm(k, shape, scale):
        return jax.random.normal(k, shape, jnp.float32) * scale

    return {
        "x_prompt": nrm(ks[0], (BATCH, SEQ, D_MODEL), 1.0),
        "x_sample": nrm(ks[1], (DEC_BATCH, DEC_SEQ, D_MODEL), 1.0),
        "cache_conv": nrm(ks[2], (DEPTH, DEC_BATCH, CONV_WIDTH - 1, D_CONV), 1.0),
        "cache_k": nrm(ks[3], (DEPTH, DEC_BATCH, WINDOW, N_KV_HEADS, HEAD_DIM), 1.0),
        "cache_v": nrm(ks[4], (DEPTH, DEC_BATCH, WINDOW, N_KV_HEADS, HEAD_DIM), 1.0),
        "rel_bias_table": nrm(ks[5], (N_BUCKETS, N_HEADS), 0.5),
        "norm_mix_g": 1.0 + nrm(ks[6], (DEPTH, D_MODEL), 0.02),
        "w_in": nrm(ks[7], (DEPTH, D_MODEL, D_IN_ALL), D_MODEL ** -0.5),
        "conv_w": nrm(ks[8], (DEPTH, CONV_WIDTH, D_CONV), CONV_WIDTH ** -0.5),
        "w_conv_out": nrm(ks[9], (DEPTH, D_CONV, D_MODEL), D_CONV ** -0.5),
        "attn_sinks": nrm(ks[10], (DEPTH, N_HEADS), 1.0),
        "w_attn_out": nrm(ks[11], (DEPTH, D_ATTN, D_MODEL), D_ATTN ** -0.5),
        "w_o": nrm(ks[12], (DEPTH, D_MODEL, D_MODEL), D_MODEL ** -0.5),
        "norm_ffn_g": 1.0 + nrm(ks[13], (DEPTH, D_MODEL), 0.02),
        "w_group": nrm(ks[14], (DEPTH, D_MODEL, N_GROUPS), D_MODEL ** -0.5),
        "b_group": nrm(ks[15], (DEPTH, N_GROUPS), 0.01),
        "w_expert_router": nrm(ks[16], (DEPTH, D_MODEL, N_EXPERTS), D_MODEL ** -0.5),
        "b_expert_router": nrm(ks[17], (DEPTH, N_EXPERTS), 0.01),
        "w_gate": nrm(ks[18], (DEPTH, N_EXPERTS, D_MODEL, D_EXPERT), D_MODEL ** -0.5),
        "w_up": nrm(ks[19], (DEPTH, N_EXPERTS, D_MODEL, D_EXPERT), D_MODEL ** -0.5),
        "w_down": nrm(ks[20], (DEPTH, N_EXPERTS, D_EXPERT, D_MODEL), D_EXPERT ** -0.5),
        "final_norm_g": 1.0 + nrm(ks[21], (D_MODEL,), 0.02),
    }


def reference(x_prompt, x_sample, cache_conv, cache_k, cache_v, rel_bias_table, norm_mix_g, w_in,
              conv_w, w_conv_out, attn_sinks, w_attn_out, w_o, norm_ffn_g, w_group, b_group,
              w_expert_router, b_expert_router, w_gate, w_up, w_down, final_norm_g):
    xp, xs = x_prompt, x_sample
    conv_p, k_p, v_p, conv_s, k_s, v_s = [], [], [], [], [], []
    for l in range(DEPTH):
        sinks_l = attn_sinks[l]
        hp = rms_norm(xp, norm_mix_g[l])
        zeros_prev = jnp.zeros((xp.shape[0], CONV_WIDTH - 1, D_CONV), dtype=hp.dtype)
        mp, cnp, knp, vnp = token_mix(
            hp, zeros_prev,
            lambda q, k, v: attn_prompt(q, k, v, sinks_l, rel_bias_table),
            w_in[l], conv_w[l], w_conv_out[l], w_attn_out[l], w_o[l])
        xp = xp + mp
        hs = rms_norm(xs, norm_mix_g[l])
        ck_l, cv_l = cache_k[l], cache_v[l]
        ms, cns, kns, vns = token_mix(
            hs, cache_conv[l],
            lambda q, k, v: attn_sample(q, k, v, ck_l, cv_l, sinks_l, rel_bias_table),
            w_in[l], conv_w[l], w_conv_out[l], w_attn_out[l], w_o[l])
        xs = xs + ms
        xp = xp + hier_moe(rms_norm(xp, norm_ffn_g[l]), w_group[l], b_group[l], w_expert_router[l],
                           b_expert_router[l], w_gate[l], w_up[l], w_down[l])
        xs = xs + hier_moe(rms_norm(xs, norm_ffn_g[l]), w_group[l], b_group[l], w_expert_router[l],
                           b_expert_router[l], w_gate[l], w_up[l], w_down[l])
        conv_p.append(cnp); k_p.append(knp); v_p.append(vnp)
        conv_s.append(cns); k_s.append(kns); v_s.append(vns)
    y_prompt = rms_norm(xp, final_norm_g)
    y_sample = rms_norm(xs, final_norm_g)
    return (y_prompt, y_sample, jnp.stack(conv_p), jnp.stack(k_p), jnp.stack(v_p),
            jnp.stack(conv_s), jnp.stack(k_s), jnp.stack(v_s))
```

```python
import functools
import math

import jax
import jax.numpy as jnp
from jax import lax
from jax.experimental import pallas as pl
from jax.experimental.pallas import tpu as pltpu

F32, BF16, I32 = jnp.float32, jnp.bfloat16, jnp.int32

CHUNK = 64
N_BUCKETS = 32
MAX_DISTANCE = 128
EPS = 1e-6
NEG_INF = -1e30
TOP_K = 2

V7X_VMEM_BYTES = 64 * 1024 * 1024
SUBLANES = 8
LANES = 128

INPROJ_ROWS = 1024
INPROJ_COLS = 1024
MIX_ROWS = 256
SLOT_ROWS = 256
ROUTE_BLOCK = 256
GROUP_ROW0 = 8


def _vmem_limit(nbytes):
    return int(min(V7X_VMEM_BYTES - (4 << 20), max(nbytes, 32 << 20)))


def _rms_rows(x, g):
    r = lax.rsqrt(jnp.mean(x * x, axis=-1, keepdims=True) + EPS)
    return (x * r) * g


def _inproj_kernel(x_ref, g_ref, wm_ref, wkv_ref, proj_ref, kv_ref, h_sc):
    @pl.when(pl.program_id(1) == 0)
    def _():
        rows = 128
        def body(i, carry):
            r0 = pl.multiple_of(i * rows, rows)
            h_sc[pl.ds(r0, rows), :] = _rms_rows(x_ref[pl.ds(r0, rows), :], g_ref[...]).astype(BF16)
            return carry
        lax.fori_loop(0, x_ref.shape[0] // rows, body, 0)
        kv_ref[...] = jnp.dot(h_sc[...], wkv_ref[...], preferred_element_type=F32)

    proj_ref[...] = jnp.dot(h_sc[...], wm_ref[...], preferred_element_type=F32).astype(BF16)


def _inproj(x2d, g, w_main, w_kv):
    n, d = x2d.shape
    tm = min(INPROJ_ROWS, n)
    tn = INPROJ_COLS
    n_main, n_kv = w_main.shape[1], w_kv.shape[1]
    est = 2 * tm * d * 4 + tm * d * 2 + 2 * d * tn * 2 + 2 * d * n_kv * 2 + 2 * tm * tn * 2 + 2 * tm * n_kv * 4 + tm * tn * 4
    return pl.pallas_call(
        _inproj_kernel,
        grid=(n // tm, n_main // tn),
        in_specs=[
            pl.BlockSpec((tm, d), lambda i, j: (i, 0)),
            pl.BlockSpec((1, d), lambda i, j: (0, 0)),
            pl.BlockSpec((d, tn), lambda i, j: (0, j)),
            pl.BlockSpec((d, n_kv), lambda i, j: (0, 0)),
        ],
        out_specs=[
            pl.BlockSpec((tm, tn), lambda i, j: (i, j)),
            pl.BlockSpec((tm, n_kv), lambda i, j: (i, 0)),
        ],
        out_shape=[jax.ShapeDtypeStruct((n, n_main), BF16), jax.ShapeDtypeStruct((n, n_kv), F32)],
        scratch_shapes=[pltpu.VMEM((tm, d), BF16)],
        compiler_params=pltpu.CompilerParams(
            dimension_semantics=("arbitrary", "arbitrary"), vmem_limit_bytes=_vmem_limit(est + (8 << 20))),
        name="inproj",
    )(x2d, g, w_main, w_kv)


def _conv_rows(u, prev2, prev1, w):
    row = lax.broadcasted_iota(I32, u.shape, 0)
    u1 = jnp.where(row == 0, prev1, pltpu.roll(u, 1, axis=0))
    u2 = jnp.where(row == 0, prev2, jnp.where(row == 1, prev1, pltpu.roll(u, 2, axis=0)))
    return (w[0:1] * u2 + w[1:2] * u1) + w[2:3] * u


def _mix_kernel(*refs, sample, n_heads, n_kv, head_dim, window, n_groups, n_exp):
    if sample:
        (x_ref, ga_ref, gb_ref, b_ref, c_ref, xc_ref, q_ref, kv_ref, cconv_ref, ck_ref, cv_ref,
         convw_ref, wco_ref, wao_ref, wo_ref, bucket_ref, table_ref, sinks_ref, g2_ref, wr_ref, br_ref,
         xp_ref, h2_ref, re_ref, rc_ref, sconv_ref, sk_ref, sv_ref,
         kbuf, vbuf, o_sc, ya_sc, bias_sc, carry_u) = refs
    else:
        (x_ref, ga_ref, gb_ref, b_ref, c_ref, xc_ref, q_ref, kv_ref,
         convw_ref, wco_ref, wao_ref, wo_ref, bucket_ref, table_ref, sinks_ref, g2_ref, wr_ref, br_ref,
         xp_ref, h2_ref, re_ref, rc_ref, sconv_ref, sk_ref, sv_ref,
         kbuf, vbuf, o_sc, ya_sc, bias_sc, carry_u) = refs
    t_rows, d_model = x_ref.shape
    n_chunks = t_rows // CHUNK
    span = window + CHUNK
    q_per_kv = n_heads // n_kv
    d_kv = n_kv * head_dim
    scale = 1.0 / math.sqrt(head_dim)
    epg = n_exp // n_groups
    first_step = jnp.logical_and(pl.program_id(0) == 0, pl.program_id(1) == 0)
    seq_start = pl.program_id(1) == 0

    @pl.when(first_step)
    def _():
        bk = bucket_ref[...]
        for h in range(n_heads):
            acc = jnp.zeros(bk.shape, F32)
            for b in range(N_BUCKETS):
                acc = jnp.where(bk == b, table_ref[b, h], acc)
            bias_sc[h] = acc

    w_conv = convw_ref[...]
    if sample:
        for s in range(n_chunks):
            rows = slice(s * CHUNK, (s + 1) * CHUNK)
            u = c_ref[rows, :].astype(F32) * xc_ref[rows, :].astype(F32)
            y = _conv_rows(u, cconv_ref[s, 0:1, :], cconv_ref[s, 1:2, :], w_conv)
            ya_sc[rows, :] = (b_ref[rows, :].astype(F32) * y).astype(BF16)
            sconv_ref[s] = u[CHUNK - 2:CHUNK, :]
    else:
        u = c_ref[...].astype(F32) * xc_ref[...].astype(F32)
        prev = jnp.where(seq_start, 0.0, carry_u[...])
        y = _conv_rows(u, prev[SUBLANES - 2:SUBLANES - 1], prev[SUBLANES - 1:SUBLANES], w_conv)
        ya_sc[...] = (b_ref[...].astype(F32) * y).astype(BF16)
        carry_u[...] = u[t_rows - SUBLANES:t_rows, :]
        sconv_ref[0] = u[t_rows - 2:t_rows, :]
    y_a = jnp.dot(ya_sc[...], wco_ref[...], preferred_element_type=F32)

    if sample:
        for s in range(n_chunks):
            kbuf[s * span:s * span + window, :] = ck_ref[s].astype(BF16)
            vbuf[s * span:s * span + window, :] = cv_ref[s].astype(BF16)
            kbuf[s * span + window:(s + 1) * span, :] = kv_ref[s * CHUNK:(s + 1) * CHUNK, 0:d_kv].astype(BF16)
            vbuf[s * span + window:(s + 1) * span, :] = kv_ref[s * CHUNK:(s + 1) * CHUNK, d_kv:2 * d_kv].astype(BF16)
            sk_ref[s, 0:window - CHUNK, :] = ck_ref[s, CHUNK:window, :]
            sv_ref[s, 0:window - CHUNK, :] = cv_ref[s, CHUNK:window, :]
            sk_ref[s, window - CHUNK:window, :] = kv_ref[s * CHUNK:(s + 1) * CHUNK, 0:d_kv]
            sv_ref[s, window - CHUNK:window, :] = kv_ref[s * CHUNK:(s + 1) * CHUNK, d_kv:2 * d_kv]
        k_stride = span
    else:
        @pl.when(seq_start)
        def _():
            kbuf[0:window, :] = jnp.zeros((window, d_kv), BF16)
            vbuf[0:window, :] = jnp.zeros((window, d_kv), BF16)
        kbuf[window:window + t_rows, :] = kv_ref[:, 0:d_kv].astype(BF16)
        vbuf[window:window + t_rows, :] = kv_ref[:, d_kv:2 * d_kv].astype(BF16)
        sk_ref[0] = kv_ref[t_rows - window:t_rows, 0:d_kv]
        sv_ref[0] = kv_ref[t_rows - window:t_rows, d_kv:2 * d_kv]
        k_stride = CHUNK

    def chunk_attn(c, carry):
        q0 = pl.multiple_of(c * CHUNK, CHUNK)
        k0 = pl.multiple_of(c * k_stride, CHUNK)
        if not sample:
            first_key = (pl.program_id(1) * n_chunks + c) * CHUNK - window
            valid = lax.broadcasted_iota(I32, (CHUNK, span), 1) + first_key >= 0
        for n in range(n_kv):
            kw = kbuf[pl.ds(k0, span), n * head_dim:(n + 1) * head_dim]
            vw = vbuf[pl.ds(k0, span), n * head_dim:(n + 1) * head_dim]
            for g in range(q_per_kv):
                h = n * q_per_kv + g
                qh = q_ref[pl.ds(q0, CHUNK), h * head_dim:(h + 1) * head_dim]
                s = lax.dot_general(qh, kw, (((1,), (1,)), ((), ())), preferred_element_type=F32)
                s = s * scale + bias_sc[h]
                if not sample:
                    s = jnp.where(valid, s, NEG_INF)
                sink = sinks_ref[0, h]
                mx = jnp.maximum(jnp.max(s, axis=-1, keepdims=True), sink)
                e = jnp.exp(s - mx)
                den = jnp.sum(e, axis=-1, keepdims=True) + jnp.exp(sink - mx)
                p = e / den
                o = jnp.dot(p.astype(BF16), vw, preferred_element_type=F32)
                o_sc[pl.ds(q0, CHUNK), h * head_dim:(h + 1) * head_dim] = o.astype(BF16)
        return carry

    lax.fori_loop(0, n_chunks, chunk_attn, 0)
    if not sample:
        kbuf[0:window, :] = kbuf[t_rows:t_rows + window, :]
        vbuf[0:window, :] = vbuf[t_rows:t_rows + window, :]
    y_b = jnp.dot(o_sc[...], wao_ref[...], preferred_element_type=F32)

    m = jax.nn.sigmoid(ga_ref[...].astype(F32)) * y_a + jax.nn.sigmoid(gb_ref[...].astype(F32)) * y_b
    xp = x_ref[...] + jnp.dot(m.astype(BF16), wo_ref[...], preferred_element_type=F32)
    xp_ref[...] = xp

    h2 = _rms_rows(xp, g2_ref[...])
    h2_ref[...] = h2.reshape(t_rows, 1, d_model)
    lt = lax.dot_general(wr_ref[...], h2.astype(BF16), (((1,), (1,)), ((), ())),
                         preferred_element_type=F32) + br_ref[...]
    lg = lt[0:n_groups]
    eg = jnp.exp(lg - jnp.max(lg, axis=0, keepdims=True))
    gp = eg / jnp.sum(eg, axis=0, keepdims=True)
    gw = jnp.max(gp, axis=0, keepdims=True)
    gi = lax.broadcasted_iota(I32, gp.shape, 0).astype(F32)
    gsel = jnp.min(jnp.where(gp == gw, gi, float(n_groups)), axis=0, keepdims=True)
    el = jnp.zeros((epg, t_rows), F32)
    for g in range(n_groups):
        el = jnp.where(gsel == float(g), lt[GROUP_ROW0 + g * epg:GROUP_ROW0 + (g + 1) * epg], el)
    ei = lax.broadcasted_iota(I32, el.shape, 0).astype(F32)
    v1 = jnp.max(el, axis=0, keepdims=True)
    i1 = jnp.min(jnp.where(el == v1, ei, float(epg)), axis=0, keepdims=True)
    el2 = jnp.where(ei == i1, -jnp.inf, el)
    v2 = jnp.max(el2, axis=0, keepdims=True)
    i2 = jnp.min(jnp.where(el2 == v2, ei, float(epg)), axis=0, keepdims=True)
    a1 = jnp.exp(v1 - v1)
    a2 = jnp.exp(v2 - v1)
    den = a1 + a2
    c1 = gw * (a1 / den)
    c2 = gw * (a2 / den)
    e1 = (gsel * float(epg) + i1).astype(I32)
    e2 = (gsel * float(epg) + i2).astype(I32)
    row8 = lax.broadcasted_iota(I32, (SUBLANES, t_rows), 0)
    re_ref[...] = jnp.where(row8 == 0, e1, jnp.where(row8 == 1, e2, 0))
    rc_ref[...] = jnp.where(row8 == 0, c1, jnp.where(row8 == 1, c2, 0.0))


def _mix(x2d, proj, kv, caches, weights, *, n_seq, sample, dims):
    n_heads, n_kv, head_dim, window, n_groups, n_exp, d_conv = dims
    n, d = x2d.shape
    t = MIX_ROWS
    n_chunks = t // CHUNK
    span = window + CHUNK
    d_attn = n_heads * head_dim
    d_kv = n_kv * head_dim
    convw, wco, wao, wo, bucket, table, sinks, g2, wr, br = weights
    if sample:
        n_t = n // t
        grid = (n_t, 1)
        tok = lambda i, j: (i, 0)
        n_state = n // CHUNK
        state_blk = n_chunks
        st = lambda i, j: (i, 0, 0)
    else:
        n_t = (n // n_seq) // t
        grid = (n_seq, n_t)
        tok = lambda i, j: (i * n_t + j, 0)
        n_state = n_seq
        state_blk = 1
        st = lambda i, j: (i, 0, 0)
    const2 = lambda i, j: (0, 0)

    def col(width, idx):
        return pl.BlockSpec((t, width), lambda i, j: (tok(i, j)[0], idx))

    def resident(shape):
        return pl.BlockSpec(shape, const2, pipeline_mode=pl.Buffered(1))

    in_specs = [
        pl.BlockSpec((t, d), tok),
        col(d, 2), col(d, 3),
        col(d_conv, 0), col(d_conv, 1), col(d_conv, 2), col(d_attn, 3),
        pl.BlockSpec((t, 2 * d_kv), tok),
    ]
    args = [x2d, proj, proj, proj, proj, proj, proj, kv]
    if sample:
        cconv, ck, cv = caches
        in_specs += [
            pl.BlockSpec((n_chunks, cconv.shape[1], d_conv), st),
            pl.BlockSpec((n_chunks, window, d_kv), st),
            pl.BlockSpec((n_chunks, window, d_kv), st),
        ]
        args += [cconv, ck, cv]
    in_specs += [
        resident(convw.shape), resident(wco.shape), resident(wao.shape), resident(wo.shape),
        resident(bucket.shape),
        pl.BlockSpec(memory_space=pltpu.SMEM), pl.BlockSpec(memory_space=pltpu.SMEM),
        resident(g2.shape), resident(wr.shape), resident(br.shape),
    ]
    args += [convw, wco, wao, wo, bucket, table, sinks, g2, wr, br]
    out_specs = [
        pl.BlockSpec((t, d), tok),
        pl.BlockSpec((t, 1, d), lambda i, j: (tok(i, j)[0], 0, 0)),
        pl.BlockSpec((SUBLANES, t), lambda i, j: (0, tok(i, j)[0])),
        pl.BlockSpec((SUBLANES, t), lambda i, j: (0, tok(i, j)[0])),
        pl.BlockSpec((state_blk, 2, d_conv), st),
        pl.BlockSpec((state_blk, window, d_kv), st),
        pl.BlockSpec((state_blk, window, d_kv), st),
    ]
    out_shape = [
        jax.ShapeDtypeStruct((n, d), F32),
        jax.ShapeDtypeStruct((n, 1, d), F32),
        jax.ShapeDtypeStruct((SUBLANES, n), I32),
        jax.ShapeDtypeStruct((SUBLANES, n), F32),
        jax.ShapeDtypeStruct((n_state, 2, d_conv), F32),
        jax.ShapeDtypeStruct((n_state, window, d_kv), F32),
        jax.ShapeDtypeStruct((n_state, window, d_kv), F32),
    ]
    kv_rows = n_chunks * span if sample else window + t
    scratch = [
        pltpu.VMEM((kv_rows, d_kv), BF16), pltpu.VMEM((kv_rows, d_kv), BF16),
        pltpu.VMEM((t, d_attn), BF16), pltpu.VMEM((t, d_conv), BF16),
        pltpu.VMEM((n_heads, CHUNK, span), F32), pltpu.VMEM((SUBLANES, d_conv), F32),
    ]
    est = (2 * t * d * 4 * 3 + 2 * 2 * t * d * 2 + 2 * 4 * t * d_conv * 2
           + (2 * d_conv * d + d * d) * 2 + 12 * t * d * 4)
    kern = functools.partial(_mix_kernel, sample=sample, n_heads=n_heads, n_kv=n_kv, head_dim=head_dim,
                             window=window, n_groups=n_groups, n_exp=n_exp)
    return pl.pallas_call(
        kern, grid=grid, in_specs=in_specs, out_specs=out_specs, out_shape=out_shape,
        scratch_shapes=scratch,
        compiler_params=pltpu.CompilerParams(
            dimension_semantics=("arbitrary", "arbitrary"), vmem_limit_bytes=_vmem_limit(est)),
        name="mix_sample" if sample else "mix_prompt",
    )(*args)


def _route_kernel(re_ref, pos_ref, tmap_ref, *, n_exp, tile_rows, blk):
    n = re_ref.shape[1]
    n_blk = n // blk
    erow = lax.broadcasted_iota(I32, (n_exp, blk), 0)

    def onehots(j):
        c0 = pl.multiple_of(j * blk, blk)
        oh0 = (erow == re_ref[0:1, pl.ds(c0, blk)]).astype(F32)
        oh1 = (erow == re_ref[1:2, pl.ds(c0, blk)]).astype(F32)
        return c0, oh0, oh1

    def count_body(j, cnt):
        _, oh0, oh1 = onehots(j)
        return cnt + jnp.sum(oh0 + oh1, axis=1, keepdims=True)

    cnt = lax.fori_loop(0, n_blk, count_body, jnp.zeros((n_exp, 1), F32))
    cnt = jnp.broadcast_to(cnt, (n_exp, LANES))
    padded = jnp.ceil(cnt / tile_rows) * tile_rows
    ends = padded
    prow = lax.broadcasted_iota(I32, ends.shape, 0)
    step = 1
    while step < n_exp:
        ends = ends + jnp.where(prow >= step, pltpu.roll(ends, step, axis=0), 0.0)
        step *= 2
    offs = ends - padded
    off1 = offs[:, 0:1]

    tri = (lax.broadcasted_iota(I32, (blk, blk), 0) <= lax.broadcasted_iota(I32, (blk, blk), 1)).astype(BF16)
    row8 = lax.broadcasted_iota(I32, (SUBLANES, blk), 0)

    def pos_body(j, run):
        c0, oh0, oh1 = onehots(j)
        both = oh0 + oh1
        csum = jnp.dot(both.astype(BF16), tri, preferred_element_type=F32) + run
        slot = off1 + csum - 1.0
        p0 = jnp.sum(oh0 * slot, axis=0, keepdims=True).astype(I32)
        p1 = jnp.sum(oh1 * slot, axis=0, keepdims=True).astype(I32)
        pos_ref[:, pl.ds(c0, blk)] = jnp.where(row8 == 0, p0, jnp.where(row8 == 1, p1, 0))
        return run + jnp.sum(both, axis=1, keepdims=True)

    lax.fori_loop(0, n_blk, pos_body, jnp.zeros((n_exp, 1), F32))

    n_tiles_pad = tmap_ref.shape[1]
    start = (lax.broadcasted_iota(I32, (1, n_tiles_pad), 1) * tile_rows).astype(F32)
    end1 = ends[:, 0:1]
    te = jnp.sum((end1 <= start).astype(F32), axis=0, keepdims=True)
    trow = lax.broadcasted_iota(I32, (n_exp, n_tiles_pad), 0).astype(F32)
    used_end = jnp.sum(jnp.where(trow == te, off1 + cnt[:, 0:1], 0.0), axis=0, keepdims=True)
    n_rows = jnp.clip(used_end - start, 0.0, float(tile_rows))
    te = jnp.minimum(te, float(n_exp - 1))
    r8 = lax.broadcasted_iota(I32, (SUBLANES, n_tiles_pad), 0)
    tmap_ref[...] = jnp.where(r8 == 0, te.astype(I32), jnp.where(r8 == 1, n_rows.astype(I32), 0))


def _route(re, n_exp, n_tiles):
    n = re.shape[1]
    n_tiles_pad = pl.cdiv(n_tiles, LANES) * LANES
    kern = functools.partial(_route_kernel, n_exp=n_exp, tile_rows=SLOT_ROWS, blk=ROUTE_BLOCK)
    return pl.pallas_call(
        kern,
        out_shape=[jax.ShapeDtypeStruct((SUBLANES, n), I32), jax.ShapeDtypeStruct((SUBLANES, n_tiles_pad), I32)],
        name="route",
    )(re)


def _dispatch_kernel(*refs, aliased):
    if aliased:
        p0_ref, p1_ref, h_ref, _, xs_ref, sem = refs
    else:
        p0_ref, p1_ref, h_ref, xs_ref, sem = refs
    rows = h_ref.shape[0]

    def start(t, carry):
        pltpu.make_async_copy(h_ref.at[t], xs_ref.at[p0_ref[t]], sem).start()
        pltpu.make_async_copy(h_ref.at[t], xs_ref.at[p1_ref[t]], sem).start()
        return carry

    def wait(t, carry):
        pltpu.make_async_copy(h_ref.at[t], xs_ref.at[p0_ref[t]], sem).wait()
        pltpu.make_async_copy(h_ref.at[t], xs_ref.at[p1_ref[t]], sem).wait()
        return carry

    lax.fori_loop(0, rows, start, 0)
    lax.fori_loop(0, rows, wait, 0)


def _dispatch(h2, p0, p1, xs, n_slots):
    n, _, d = h2.shape
    t = MIX_ROWS
    aliased = xs is not None
    in_specs = [
        pl.BlockSpec((t,), lambda i: (i,), memory_space=pltpu.SMEM),
        pl.BlockSpec((t,), lambda i: (i,), memory_space=pltpu.SMEM),
        pl.BlockSpec((t, 1, d), lambda i: (i, 0, 0)),
    ]
    args = [p0, p1, h2]
    if aliased:
        in_specs.append(pl.BlockSpec(memory_space=pl.ANY))
        args.append(xs)
    return pl.pallas_call(
        functools.partial(_dispatch_kernel, aliased=aliased),
        grid=(n // t,),
        in_specs=in_specs,
        out_specs=pl.BlockSpec(memory_space=pl.ANY),
        out_shape=jax.ShapeDtypeStruct((n_slots, 1, d), F32),
        scratch_shapes=[pltpu.SemaphoreType.DMA(())],
        input_output_aliases={3: 0} if aliased else {},
        compiler_params=pltpu.CompilerParams(dimension_semantics=("arbitrary",), has_side_effects=True),
        name="dispatch",
    )(*args)


def _experts_kernel(te_ref, nr_ref, xs_ref, wg_ref, wu_ref, wd_ref, y_ref, wg_sc, wu_sc, wd_sc):
    i = pl.program_id(0)
    rows, _, d = xs_ref.shape
    changed = jnp.logical_or(i == 0, te_ref[i] != te_ref[jnp.maximum(i - 1, 0)])

    @pl.when(changed)
    def _():
        wg_sc[...] = wg_ref[0].astype(BF16)
        wu_sc[...] = wu_ref[0].astype(BF16)
        wd_sc[...] = wd_ref[0].astype(BF16)

    n_rows = nr_ref[i]

    @pl.when(n_rows > 0)
    def _():
        live = lax.broadcasted_iota(I32, (rows, d), 0) < n_rows
        x = jnp.where(live, xs_ref[...].reshape(rows, d), 0.0).astype(BF16)
        gate = jnp.dot(x, wg_sc[...], preferred_element_type=F32)
        up = jnp.dot(x, wu_sc[...], preferred_element_type=F32)
        hid = (jax.nn.silu(gate) * up).astype(BF16)
        y_ref[...] = jnp.dot(hid, wd_sc[...], preferred_element_type=F32).reshape(rows, 1, d)

    @pl.when(n_rows <= 0)
    def _():
        y_ref[...] = jnp.zeros(y_ref.shape, F32)


def _experts(xs, te, nr, w_gate, w_up, w_down):
    n_slots, _, d = xs.shape
    n_exp, _, d_e = w_gate.shape
    t = SLOT_ROWS
    est = 2 * 2 * t * d * 4 + 2 * 3 * d * d_e * 4 + 3 * d * d_e * 2 + 6 * t * d * 4
    return pl.pallas_call(
        _experts_kernel,
        grid_spec=pltpu.PrefetchScalarGridSpec(
            num_scalar_prefetch=2,
            grid=(n_slots // t,),
            in_specs=[
                pl.BlockSpec((t, 1, d), lambda i, te, nr: (i, 0, 0)),
                pl.BlockSpec((1, d, d_e), lambda i, te, nr: (te[i], 0, 0)),
                pl.BlockSpec((1, d, d_e), lambda i, te, nr: (te[i], 0, 0)),
                pl.BlockSpec((1, d_e, d), lambda i, te, nr: (te[i], 0, 0)),
            ],
            out_specs=pl.BlockSpec((t, 1, d), lambda i, te, nr: (i, 0, 0)),
            scratch_shapes=[pltpu.VMEM((d, d_e), BF16), pltpu.VMEM((d, d_e), BF16), pltpu.VMEM((d_e, d), BF16)],
        ),
        out_shape=jax.ShapeDtypeStruct((n_slots, 1, d), F32),
        compiler_params=pltpu.CompilerParams(
            dimension_semantics=("arbitrary",), vmem_limit_bytes=_vmem_limit(est)),
        name="experts",
    )(te, nr, xs, w_gate, w_up, w_down)


def _combine_kernel(p0_ref, p1_ref, xp_ref, rc_ref, g_ref, y_hbm, out_ref, y0_buf, y1_buf, sem):
    rows, d = xp_ref.shape

    def start(t, carry):
        pltpu.make_async_copy(y_hbm.at[p0_ref[t]], y0_buf.at[t], sem).start()
        pltpu.make_async_copy(y_hbm.at[p1_ref[t]], y1_buf.at[t], sem).start()
        return carry

    def wait(t, carry):
        pltpu.make_async_copy(y_hbm.at[p0_ref[t]], y0_buf.at[t], sem).wait()
        pltpu.make_async_copy(y_hbm.at[p1_ref[t]], y1_buf.at[t], sem).wait()
        return carry

    lax.fori_loop(0, rows, start, 0)
    lax.fori_loop(0, rows, wait, 0)
    ct = rc_ref[...].T
    moe = ct[:, 0:1] * y0_buf[...].reshape(rows, d) + ct[:, 1:2] * y1_buf[...].reshape(rows, d)
    out_ref[...] = _rms_rows(xp_ref[...] + moe, g_ref[...])


def _combine(xp, rc, p0, p1, y, g):
    n, d = xp.shape
    t = MIX_ROWS
    est = 2 * 2 * t * d * 4 + 2 * t * d * 4 + 6 * t * d * 4
    return pl.pallas_call(
        _combine_kernel,
        grid=(n // t,),
        in_specs=[
            pl.BlockSpec((t,), lambda i: (i,), memory_space=pltpu.SMEM),
            pl.BlockSpec((t,), lambda i: (i,), memory_space=pltpu.SMEM),
            pl.BlockSpec((t, d), lambda i: (i, 0)),
            pl.BlockSpec((SUBLANES, t), lambda i: (0, i)),
            pl.BlockSpec((1, d), lambda i: (0, 0)),
            pl.BlockSpec(memory_space=pl.ANY),
        ],
        out_specs=pl.BlockSpec((t, d), lambda i: (i, 0)),
        out_shape=jax.ShapeDtypeStruct((n, d), F32),
        scratch_shapes=[pltpu.VMEM((t, 1, d), F32), pltpu.VMEM((t, 1, d), F32), pltpu.SemaphoreType.DMA(())],
        compiler_params=pltpu.CompilerParams(
            dimension_semantics=("arbitrary",), vmem_limit_bytes=_vmem_limit(est)),
        name="combine",
    )(p0, p1, xp, rc, g, y)


def _rel_buckets(window):
    rel = (jnp.arange(window + CHUNK, dtype=I32) - window)[None, :] - jnp.arange(CHUNK, dtype=I32)[:, None]
    nb = N_BUCKETS // 2
    ret = (rel > 0).astype(I32) * nb
    n = jnp.abs(rel)
    max_exact = nb // 2
    nf = jnp.maximum(n, 1).astype(F32)
    large = max_exact + (jnp.log(nf / max_exact) / math.log(MAX_DISTANCE / max_exact)
                         * (nb - max_exact)).astype(I32)
    large = jnp.minimum(large, nb - 1)
    return ret + jnp.where(n < max_exact, n, large)


def kernel(x_prompt, x_sample, cache_conv, cache_k, cache_v, rel_bias_table, norm_mix_g, w_in, conv_w, w_conv_out, attn_sinks, w_attn_out, w_o, norm_ffn_g, w_group, b_group, w_expert_router, b_expert_router, w_gate, w_up, w_down, final_norm_g):
    assert w_in.shape[0] == 1, "single-layer step"
    batch, seq, d = x_prompt.shape
    dec_batch, dec_seq, _ = x_sample.shape
    assert dec_seq == CHUNK and seq % MIX_ROWS == 0 and (dec_batch * dec_seq) % MIX_ROWS == 0
    d_conv = conv_w.shape[-1]
    window, n_kv, head_dim = cache_k.shape[2], cache_k.shape[3], cache_k.shape[4]
    n_heads = attn_sinks.shape[-1]
    d_attn, d_kv = n_heads * head_dim, n_kv * head_dim
    n_groups, n_exp = w_group.shape[-1], w_expert_router.shape[-1]
    epg = n_exp // n_groups
    assert n_groups <= GROUP_ROW0 and d_conv == d_attn and 2 * d_conv == d
    dims = (n_heads, n_kv, head_dim, window, n_groups, n_exp, d_conv)

    w = w_in[0]
    kv0 = 3 * d_conv + d_attn
    w_main = jnp.concatenate([w[:, :kv0], w[:, kv0 + 2 * d_kv:]], axis=1).astype(BF16)
    w_kv = w[:, kv0:kv0 + 2 * d_kv].astype(BF16)
    g1 = norm_mix_g[0][None, :]
    wr = jnp.zeros((GROUP_ROW0 + n_exp, d), F32)
    wr = wr.at[:n_groups].set(w_group[0].T).at[GROUP_ROW0:].set(w_expert_router[0].T).astype(BF16)
    br = jnp.zeros((GROUP_ROW0 + n_exp, 1), F32)
    br = br.at[:n_groups, 0].set(b_group[0]).at[GROUP_ROW0:, 0].set(b_expert_router[0])
    weights = (conv_w[0], w_conv_out[0].astype(BF16), w_attn_out[0].astype(BF16), w_o[0].astype(BF16),
               _rel_buckets(window), rel_bias_table, attn_sinks, norm_ffn_g[0][None, :], wr, br)

    xp2d = x_prompt.reshape(batch * seq, d)
    xs2d = x_sample.reshape(dec_batch * dec_seq, d)
    n_p, n_s = xp2d.shape[0], xs2d.shape[0]
    n_tok = n_p + n_s

    proj_p, kv_p = _inproj(xp2d, g1, w_main, w_kv)
    proj_s, kv_s = _inproj(xs2d, g1, w_main, w_kv)
    xres_p, h2_p, re_p, rc_p, conv_p, k_p, v_p = _mix(
        xp2d, proj_p, kv_p, None, weights, n_seq=batch, sample=False, dims=dims)
    caches = (cache_conv[0], cache_k[0].reshape(dec_batch, window, d_kv), cache_v[0].reshape(dec_batch, window, d_kv))
    xres_s, h2_s, re_s, rc_s, conv_s, k_s, v_s = _mix(
        xs2d, proj_s, kv_s, caches, weights, n_seq=dec_batch, sample=True, dims=dims)

    n_tiles = (TOP_K * n_tok) // SLOT_ROWS + n_exp
    n_slots = n_tiles * SLOT_ROWS
    pos, tmap = _route(jnp.concatenate([re_p, re_s], axis=1), n_exp, n_tiles)
    p0, p1 = pos[0], pos[1]
    xs = _dispatch(h2_p, p0[:n_p], p1[:n_p], None, n_slots)
    xs = _dispatch(h2_s, p0[n_p:], p1[n_p:], xs, n_slots)
    y = _experts(xs, tmap[0, :n_tiles], tmap[1, :n_tiles], w_gate[0], w_up[0], w_down[0])
    gf = final_norm_g[None, :]
    y_prompt = _combine(xres_p, rc_p, p0[:n_p], p1[:n_p], y, gf).reshape(batch, seq, d)
    y_sample = _combine(xres_s, rc_s, p0[n_p:], p1[n_p:], y, gf).reshape(dec_batch, dec_seq, d)

    kv_shape = (1, -1, window, n_kv, head_dim)
    return (y_prompt, y_sample, conv_p[None], k_p.reshape(kv_shape), v_p.reshape(kv_shape),
            conv_s[None], k_s.reshape(kv_shape), v_s.reshape(kv_shape))
```

```python
import functools
import math

import jax
import jax.numpy as jnp
from jax import lax
from jax.experimental import pallas as pl
from jax.experimental.pallas import tpu as pltpu

F32, BF16, I32 = jnp.float32, jnp.bfloat16, jnp.int32

CHUNK = 64
N_BUCKETS = 32
MAX_DISTANCE = 128
EPS = 1e-6
NEG_INF = -1e30
TOP_K = 2

V7X_VMEM_BYTES = 64 * 1024 * 1024
SUBLANES = 8
LANES = 128

INPROJ_ROWS = 1024
INPROJ_COLS = 1024
MIX_ROWS = 256
SLOT_ROWS = 256
ROUTE_BLOCK = 256
GROUP_ROW0 = 8
KEY_PAD = 256
KV_VARIANTS = 4


def _vmem_limit(nbytes):
    return int(min(V7X_VMEM_BYTES - (4 << 20), max(nbytes, 32 << 20)))


def _rms_rows(x, g):
    r = lax.rsqrt(jnp.mean(x * x, axis=-1, keepdims=True) + EPS)
    return (x * r) * g


def _inproj_kernel(x_ref, g_ref, wm_ref, wkv_ref, proj_ref, kv_ref, h_sc):
    @pl.when(pl.program_id(1) == 0)
    def _():
        rows = 128
        def body(i, carry):
            r0 = pl.multiple_of(i * rows, rows)
            h_sc[pl.ds(r0, rows), :] = _rms_rows(x_ref[pl.ds(r0, rows), :], g_ref[...]).astype(BF16)
            return carry
        lax.fori_loop(0, x_ref.shape[0] // rows, body, 0)
        kv_ref[...] = jnp.dot(h_sc[...], wkv_ref[...], preferred_element_type=F32)

    proj_ref[...] = jnp.dot(h_sc[...], wm_ref[...], preferred_element_type=F32).astype(BF16)


def _inproj(x2d, g, w_main, w_kv):
    n, d = x2d.shape
    tm = min(INPROJ_ROWS, n)
    tn = INPROJ_COLS
    n_main, n_kv = w_main.shape[1], w_kv.shape[1]
    est = 2 * tm * d * 4 + tm * d * 2 + 2 * d * tn * 2 + 2 * d * n_kv * 2 + 2 * tm * tn * 2 + 2 * tm * n_kv * 4 + tm * tn * 4
    return pl.pallas_call(
        _inproj_kernel,
        grid=(n // tm, n_main // tn),
        in_specs=[
            pl.BlockSpec((tm, d), lambda i, j: (i, 0)),
            pl.BlockSpec((1, d), lambda i, j: (0, 0)),
            pl.BlockSpec((d, tn), lambda i, j: (0, j)),
            pl.BlockSpec((d, n_kv), lambda i, j: (0, 0)),
        ],
        out_specs=[
            pl.BlockSpec((tm, tn), lambda i, j: (i, j)),
            pl.BlockSpec((tm, n_kv), lambda i, j: (i, 0)),
        ],
        out_shape=[jax.ShapeDtypeStruct((n, n_main), BF16), jax.ShapeDtypeStruct((n, n_kv), F32)],
        scratch_shapes=[pltpu.VMEM((tm, d), BF16)],
        compiler_params=pltpu.CompilerParams(
            dimension_semantics=("arbitrary", "arbitrary"), vmem_limit_bytes=_vmem_limit(est + (8 << 20))),
        name="inproj",
    )(x2d, g, w_main, w_kv)


def _conv_rows(u, prev2, prev1, w):
    row = lax.broadcasted_iota(I32, u.shape, 0)
    u1 = jnp.where(row == 0, prev1, pltpu.roll(u, 1, axis=0))
    u2 = jnp.where(row == 0, prev2, jnp.where(row == 1, prev1, pltpu.roll(u, 2, axis=0)))
    return (w[0:1] * u2 + w[1:2] * u1) + w[2:3] * u


def _mix_kernel(*refs, sample, n_heads, n_kv, head_dim, window, n_groups, n_exp):
    if sample:
        (x_ref, ga_ref, gb_ref, b_ref, c_ref, xc_ref, q_ref, kv_ref, cconv_ref, ck_ref, cv_ref,
         convw_ref, wco_ref, wao_ref, wo_ref, bucket_ref, table_ref, sinks_ref, g2_ref, wr_ref, br_ref,
         xp_ref, h2_ref, re_ref, rc_ref, sconv_ref, sk_ref, sv_ref,
         kvar, o_sc, ya_sc, bias_sc, carry_u) = refs
    else:
        (x_ref, ga_ref, gb_ref, b_ref, c_ref, xc_ref, q_ref, kv_ref,
         convw_ref, wco_ref, wao_ref, wo_ref, bucket_ref, table_ref, sinks_ref, g2_ref, wr_ref, br_ref,
         xp_ref, h2_ref, re_ref, rc_ref, sconv_ref, sk_ref, sv_ref,
         kvar, o_sc, ya_sc, bias_sc, carry_u) = refs
    t_rows, d_model = x_ref.shape
    n_chunks = t_rows // CHUNK
    span = window + CHUNK
    q_per_kv = n_heads // n_kv
    d_kv = n_kv * head_dim
    scale = 1.0 / math.sqrt(head_dim)
    epg = n_exp // n_groups
    first_step = jnp.logical_and(pl.program_id(0) == 0, pl.program_id(1) == 0)
    seq_start = pl.program_id(1) == 0

    @pl.when(first_step)
    def _():
        bk = bucket_ref[...]
        for h in range(n_heads):
            acc = jnp.where(bk < 0, NEG_INF, 0.0)
            for b in range(N_BUCKETS):
                acc = jnp.where(bk == b, table_ref[b, h], acc)
            n, g = divmod(h, q_per_kv)
            pair, half = divmod(g, 2)
            bias_sc[n, pair * CHUNK:(pair + 1) * CHUNK, half * KEY_PAD:(half + 1) * KEY_PAD] = acc

    w_conv = convw_ref[...]
    if sample:
        for s in range(n_chunks):
            rows = slice(s * CHUNK, (s + 1) * CHUNK)
            u = c_ref[rows, :].astype(F32) * xc_ref[rows, :].astype(F32)
            y = _conv_rows(u, cconv_ref[s, 0:1, :], cconv_ref[s, 1:2, :], w_conv)
            ya_sc[rows, :] = (b_ref[rows, :].astype(F32) * y).astype(BF16)
            sconv_ref[s] = u[CHUNK - 2:CHUNK, :]
    else:
        u = c_ref[...].astype(F32) * xc_ref[...].astype(F32)
        prev = jnp.where(seq_start, 0.0, carry_u[...])
        y = _conv_rows(u, prev[SUBLANES - 2:SUBLANES - 1], prev[SUBLANES - 1:SUBLANES], w_conv)
        ya_sc[...] = (b_ref[...].astype(F32) * y).astype(BF16)
        carry_u[...] = u[t_rows - SUBLANES:t_rows, :]
        sconv_ref[0] = u[t_rows - 2:t_rows, :]
    y_a = jnp.dot(ya_sc[...], wco_ref[...], preferred_element_type=F32)

    def store_kv(row0, k_rows, v_rows):
        n_rows = k_rows.shape[0]
        low = lax.broadcasted_iota(I32, k_rows.shape, 1) < head_dim
        for base, a in ((0, k_rows), (KV_VARIANTS, v_rows)):
            b = pltpu.roll(a, head_dim, axis=1)
            kvar[base + 0, row0:row0 + n_rows, :] = jnp.where(low, a, 0.0).astype(BF16)
            kvar[base + 1, row0:row0 + n_rows, :] = jnp.where(low, 0.0, a).astype(BF16)
            kvar[base + 2, row0:row0 + n_rows, :] = jnp.where(low, b, 0.0).astype(BF16)
            kvar[base + 3, row0:row0 + n_rows, :] = jnp.where(low, 0.0, b).astype(BF16)

    if sample:
        for s in range(n_chunks):
            rows = slice(s * CHUNK, (s + 1) * CHUNK)
            store_kv(s * span, ck_ref[s], cv_ref[s])
            store_kv(s * span + window, kv_ref[rows, 0:d_kv], kv_ref[rows, d_kv:2 * d_kv])
            sk_ref[s, 0:window - CHUNK, :] = ck_ref[s, CHUNK:window, :]
            sv_ref[s, 0:window - CHUNK, :] = cv_ref[s, CHUNK:window, :]
            sk_ref[s, window - CHUNK:window, :] = kv_ref[rows, 0:d_kv]
            sv_ref[s, window - CHUNK:window, :] = kv_ref[rows, d_kv:2 * d_kv]
        k_stride = span
    else:
        @pl.when(seq_start)
        def _():
            kvar[:, 0:window, :] = jnp.zeros((2 * KV_VARIANTS, window, d_kv), BF16)
        store_kv(window, kv_ref[:, 0:d_kv], kv_ref[:, d_kv:2 * d_kv])
        sk_ref[0] = kv_ref[t_rows - window:t_rows, 0:d_kv]
        sv_ref[0] = kv_ref[t_rows - window:t_rows, d_kv:2 * d_kv]
        k_stride = CHUNK

    n_pairs = q_per_kv // 2
    key_pad_rows = jnp.zeros((KEY_PAD - span, d_kv), BF16)
    for c in range(n_chunks):
        q_rows = slice(c * CHUNK, (c + 1) * CHUNK)
        k_rows = slice(c * k_stride, c * k_stride + span)
        masked = (not sample) and c * CHUNK < window
        if masked:
            first_key = (pl.program_id(1) * n_chunks + c) * CHUNK - window
            valid = lax.broadcasted_iota(I32, (CHUNK, KEY_PAD), 1) + first_key >= 0
        for n in range(n_kv):
            top, bot = (0, 3) if n == 0 else (2, 1)
            kd = jnp.concatenate([kvar[top, k_rows, :], key_pad_rows, kvar[bot, k_rows, :], key_pad_rows], axis=0)
            vd = jnp.concatenate([kvar[KV_VARIANTS + top, k_rows, :], key_pad_rows,
                                  kvar[KV_VARIANTS + bot, k_rows, :], key_pad_rows], axis=0)
            q4 = jnp.concatenate(
                [q_ref[q_rows, (n * n_pairs + j) * LANES:(n * n_pairs + j + 1) * LANES] for j in range(n_pairs)], axis=0)
            s4 = lax.dot_general(q4, kd, (((1,), (1,)), ((), ())), preferred_element_type=F32)
            p_rows = []
            for j in range(n_pairs):
                p_halves = []
                for half in range(2):
                    blk = (slice(j * CHUNK, (j + 1) * CHUNK), slice(half * KEY_PAD, (half + 1) * KEY_PAD))
                    s = s4[blk] * scale + bias_sc[n, blk[0], blk[1]]
                    if masked:
                        s = jnp.where(valid, s, NEG_INF)
                    sink = sinks_ref[0, n * q_per_kv + 2 * j + half]
                    mx = jnp.maximum(jnp.max(s, axis=-1, keepdims=True), sink)
                    e = jnp.exp(s - mx)
                    den = jnp.sum(e, axis=-1, keepdims=True) + jnp.exp(sink - mx)
                    p_halves.append((e * (1.0 / den)).astype(BF16))
                p_rows.append(jnp.concatenate(p_halves, axis=1))
            o4 = jnp.dot(jnp.concatenate(p_rows, axis=0), vd, preferred_element_type=F32)
            for j in range(n_pairs):
                o_sc[q_rows, (n * n_pairs + j) * LANES:(n * n_pairs + j + 1) * LANES] = (
                    o4[j * CHUNK:(j + 1) * CHUNK].astype(BF16))
    if not sample:
        kvar[:, 0:window, :] = kvar[:, t_rows:t_rows + window, :]
    y_b = jnp.dot(o_sc[...], wao_ref[...], preferred_element_type=F32)

    m = jax.nn.sigmoid(ga_ref[...].astype(F32)) * y_a + jax.nn.sigmoid(gb_ref[...].astype(F32)) * y_b
    xp = x_ref[...] + jnp.dot(m.astype(BF16), wo_ref[...], preferred_element_type=F32)
    xp_ref[...] = xp

    h2 = _rms_rows(xp, g2_ref[...])
    h2_ref[...] = h2.reshape(t_rows, 1, d_model)
    lt = lax.dot_general(wr_ref[...], h2.astype(BF16), (((1,), (1,)), ((), ())),
                         preferred_element_type=F32) + br_ref[...]
    lg = lt[0:n_groups]
    eg = jnp.exp(lg - jnp.max(lg, axis=0, keepdims=True))
    gp = eg / jnp.sum(eg, axis=0, keepdims=True)
    gw = jnp.max(gp, axis=0, keepdims=True)
    gi = lax.broadcasted_iota(I32, gp.shape, 0).astype(F32)
    gsel = jnp.min(jnp.where(gp == gw, gi, float(n_groups)), axis=0, keepdims=True)
    el = jnp.zeros((epg, t_rows), F32)
    for g in range(n_groups):
        el = jnp.where(gsel == float(g), lt[GROUP_ROW0 + g * epg:GROUP_ROW0 + (g + 1) * epg], el)
    ei = lax.broadcasted_iota(I32, el.shape, 0).astype(F32)
    v1 = jnp.max(el, axis=0, keepdims=True)
    i1 = jnp.min(jnp.where(el == v1, ei, float(epg)), axis=0, keepdims=True)
    el2 = jnp.where(ei == i1, -jnp.inf, el)
    v2 = jnp.max(el2, axis=0, keepdims=True)
    i2 = jnp.min(jnp.where(el2 == v2, ei, float(epg)), axis=0, keepdims=True)
    a1 = jnp.exp(v1 - v1)
    a2 = jnp.exp(v2 - v1)
    den = a1 + a2
    c1 = gw * (a1 / den)
    c2 = gw * (a2 / den)
    e1 = (gsel * float(epg) + i1).astype(I32)
    e2 = (gsel * float(epg) + i2).astype(I32)
    row8 = lax.broadcasted_iota(I32, (SUBLANES, t_rows), 0)
    re_ref[...] = jnp.where(row8 == 0, e1, jnp.where(row8 == 1, e2, 0))
    rc_ref[...] = jnp.where(row8 == 0, c1, jnp.where(row8 == 1, c2, 0.0))


def _mix(x2d, proj, kv, caches, weights, *, n_seq, sample, dims):
    n_heads, n_kv, head_dim, window, n_groups, n_exp, d_conv = dims
    n, d = x2d.shape
    t = MIX_ROWS
    n_chunks = t // CHUNK
    span = window + CHUNK
    d_attn = n_heads * head_dim
    d_kv = n_kv * head_dim
    convw, wco, wao, wo, bucket, table, sinks, g2, wr, br = weights
    if sample:
        n_t = n // t
        grid = (n_t, 1)
        tok = lambda i, j: (i, 0)
        n_state = n // CHUNK
        state_blk = n_chunks
        st = lambda i, j: (i, 0, 0)
    else:
        n_t = (n // n_seq) // t
        grid = (n_seq, n_t)
        tok = lambda i, j: (i * n_t + j, 0)
        n_state = n_seq
        state_blk = 1
        st = lambda i, j: (i, 0, 0)
    const2 = lambda i, j: (0, 0)

    def col(width, idx):
        return pl.BlockSpec((t, width), lambda i, j: (tok(i, j)[0], idx))

    def resident(shape):
        return pl.BlockSpec(shape, const2, pipeline_mode=pl.Buffered(1))

    in_specs = [
        pl.BlockSpec((t, d), tok),
        col(d, 2), col(d, 3),
        col(d_conv, 0), col(d_conv, 1), col(d_conv, 2), col(d_attn, 3),
        pl.BlockSpec((t, 2 * d_kv), tok),
    ]
    args = [x2d, proj, proj, proj, proj, proj, proj, kv]
    if sample:
        cconv, ck, cv = caches
        in_specs += [
            pl.BlockSpec((n_chunks, cconv.shape[1], d_conv), st),
            pl.BlockSpec((n_chunks, window, d_kv), st),
            pl.BlockSpec((n_chunks, window, d_kv), st),
        ]
        args += [cconv, ck, cv]
    in_specs += [
        resident(convw.shape), resident(wco.shape), resident(wao.shape), resident(wo.shape),
        resident(bucket.shape),
        pl.BlockSpec(memory_space=pltpu.SMEM), pl.BlockSpec(memory_space=pltpu.SMEM),
        resident(g2.shape), resident(wr.shape), resident(br.shape),
    ]
    args += [convw, wco, wao, wo, bucket, table, sinks, g2, wr, br]
    out_specs = [
        pl.BlockSpec((t, d), tok),
        pl.BlockSpec((t, 1, d), lambda i, j: (tok(i, j)[0], 0, 0)),
        pl.BlockSpec((SUBLANES, t), lambda i, j: (0, tok(i, j)[0])),
        pl.BlockSpec((SUBLANES, t), lambda i, j: (0, tok(i, j)[0])),
        pl.BlockSpec((state_blk, 2, d_conv), st),
        pl.BlockSpec((state_blk, window, d_kv), st),
        pl.BlockSpec((state_blk, window, d_kv), st),
    ]
    out_shape = [
        jax.ShapeDtypeStruct((n, d), F32),
        jax.ShapeDtypeStruct((n, 1, d), F32),
        jax.ShapeDtypeStruct((SUBLANES, n), I32),
        jax.ShapeDtypeStruct((SUBLANES, n), F32),
        jax.ShapeDtypeStruct((n_state, 2, d_conv), F32),
        jax.ShapeDtypeStruct((n_state, window, d_kv), F32),
        jax.ShapeDtypeStruct((n_state, window, d_kv), F32),
    ]
    kv_rows = n_chunks * span if sample else window + t
    scratch = [
        pltpu.VMEM((2 * KV_VARIANTS, kv_rows, d_kv), BF16),
        pltpu.VMEM((t, d_attn), BF16), pltpu.VMEM((t, d_conv), BF16),
        pltpu.VMEM((n_kv, (n_heads // n_kv // 2) * CHUNK, 2 * KEY_PAD), F32), pltpu.VMEM((SUBLANES, d_conv), F32),
    ]
    est = (2 * t * d * 4 * 3 + 2 * 2 * t * d * 2 + 2 * 4 * t * d_conv * 2
           + (2 * d_conv * d + d * d) * 2 + 12 * t * d * 4)
    kern = functools.partial(_mix_kernel, sample=sample, n_heads=n_heads, n_kv=n_kv, head_dim=head_dim,
                             window=window, n_groups=n_groups, n_exp=n_exp)
    return pl.pallas_call(
        kern, grid=grid, in_specs=in_specs, out_specs=out_specs, out_shape=out_shape,
        scratch_shapes=scratch,
        compiler_params=pltpu.CompilerParams(
            dimension_semantics=("arbitrary", "arbitrary"), vmem_limit_bytes=_vmem_limit(est)),
        name="mix_sample" if sample else "mix_prompt",
    )(*args)


def _route_kernel(re_ref, pos_ref, tmap_ref, *, n_exp, tile_rows, blk):
    n = re_ref.shape[1]
    n_blk = n // blk
    erow = lax.broadcasted_iota(I32, (n_exp, blk), 0)

    def onehots(j):
        c0 = pl.multiple_of(j * blk, blk)
        oh0 = (erow == re_ref[0:1, pl.ds(c0, blk)]).astype(F32)
        oh1 = (erow == re_ref[1:2, pl.ds(c0, blk)]).astype(F32)
        return c0, oh0, oh1

    def count_body(j, cnt):
        _, oh0, oh1 = onehots(j)
        return cnt + jnp.sum(oh0 + oh1, axis=1, keepdims=True)

    cnt = lax.fori_loop(0, n_blk, count_body, jnp.zeros((n_exp, 1), F32))
    cnt = jnp.broadcast_to(cnt, (n_exp, LANES))
    padded = jnp.ceil(cnt / tile_rows) * tile_rows
    ends = padded
    prow = lax.broadcasted_iota(I32, ends.shape, 0)
    step = 1
    while step < n_exp:
        ends = ends + jnp.where(prow >= step, pltpu.roll(ends, step, axis=0), 0.0)
        step *= 2
    offs = ends - padded
    off1 = offs[:, 0:1]

    tri = (lax.broadcasted_iota(I32, (blk, blk), 0) <= lax.broadcasted_iota(I32, (blk, blk), 1)).astype(BF16)
    row8 = lax.broadcasted_iota(I32, (SUBLANES, blk), 0)

    def pos_body(j, run):
        c0, oh0, oh1 = onehots(j)
        both = oh0 + oh1
        csum = jnp.dot(both.astype(BF16), tri, preferred_element_type=F32) + run
        slot = off1 + csum - 1.0
        p0 = jnp.sum(oh0 * slot, axis=0, keepdims=True).astype(I32)
        p1 = jnp.sum(oh1 * slot, axis=0, keepdims=True).astype(I32)
        pos_ref[:, pl.ds(c0, blk)] = jnp.where(row8 == 0, p0, jnp.where(row8 == 1, p1, 0))
        return run + jnp.sum(both, axis=1, keepdims=True)

    lax.fori_loop(0, n_blk, pos_body, jnp.zeros((n_exp, 1), F32))

    n_tiles_pad = tmap_ref.shape[1]
    start = (lax.broadcasted_iota(I32, (1, n_tiles_pad), 1) * tile_rows).astype(F32)
    end1 = ends[:, 0:1]
    te = jnp.sum((end1 <= start).astype(F32), axis=0, keepdims=True)
    trow = lax.broadcasted_iota(I32, (n_exp, n_tiles_pad), 0).astype(F32)
    used_end = jnp.sum(jnp.where(trow == te, off1 + cnt[:, 0:1], 0.0), axis=0, keepdims=True)
    n_rows = jnp.clip(used_end - start, 0.0, float(tile_rows))
    te = jnp.minimum(te, float(n_exp - 1))
    r8 = lax.broadcasted_iota(I32, (SUBLANES, n_tiles_pad), 0)
    tmap_ref[...] = jnp.where(r8 == 0, te.astype(I32), jnp.where(r8 == 1, n_rows.astype(I32), 0))


def _route(re, n_exp, n_tiles):
    n = re.shape[1]
    n_tiles_pad = pl.cdiv(n_tiles, LANES) * LANES
    kern = functools.partial(_route_kernel, n_exp=n_exp, tile_rows=SLOT_ROWS, blk=ROUTE_BLOCK)
    return pl.pallas_call(
        kern,
        out_shape=[jax.ShapeDtypeStruct((SUBLANES, n), I32), jax.ShapeDtypeStruct((SUBLANES, n_tiles_pad), I32)],
        name="route",
    )(re)


def _dispatch_kernel(*refs, aliased):
    if aliased:
        p0_ref, p1_ref, h_ref, _, xs_ref, sem = refs
    else:
        p0_ref, p1_ref, h_ref, xs_ref, sem = refs
    rows = h_ref.shape[0]

    def start(t, carry):
        pltpu.make_async_copy(h_ref.at[t], xs_ref.at[p0_ref[t]], sem).start()
        pltpu.make_async_copy(h_ref.at[t], xs_ref.at[p1_ref[t]], sem).start()
        return carry

    def wait(t, carry):
        pltpu.make_async_copy(h_ref.at[t], xs_ref.at[p0_ref[t]], sem).wait()
        pltpu.make_async_copy(h_ref.at[t], xs_ref.at[p1_ref[t]], sem).wait()
        return carry

    lax.fori_loop(0, rows, start, 0)
    lax.fori_loop(0, rows, wait, 0)


def _dispatch(h2, p0, p1, xs, n_slots):
    n, _, d = h2.shape
    t = MIX_ROWS
    aliased = xs is not None
    in_specs = [
        pl.BlockSpec((t,), lambda i: (i,), memory_space=pltpu.SMEM),
        pl.BlockSpec((t,), lambda i: (i,), memory_space=pltpu.SMEM),
        pl.BlockSpec((t, 1, d), lambda i: (i, 0, 0)),
    ]
    args = [p0, p1, h2]
    if aliased:
        in_specs.append(pl.BlockSpec(memory_space=pl.ANY))
        args.append(xs)
    return pl.pallas_call(
        functools.partial(_dispatch_kernel, aliased=aliased),
        grid=(n // t,),
        in_specs=in_specs,
        out_specs=pl.BlockSpec(memory_space=pl.ANY),
        out_shape=jax.ShapeDtypeStruct((n_slots, 1, d), F32),
        scratch_shapes=[pltpu.SemaphoreType.DMA(())],
        input_output_aliases={3: 0} if aliased else {},
        compiler_params=pltpu.CompilerParams(dimension_semantics=("arbitrary",), has_side_effects=True),
        name="dispatch",
    )(*args)


def _experts_kernel(te_ref, nr_ref, xs_ref, wg_ref, wu_ref, wd_ref, y_ref, wg_sc, wu_sc, wd_sc):
    i = pl.program_id(0)
    rows, _, d = xs_ref.shape
    changed = jnp.logical_or(i == 0, te_ref[i] != te_ref[jnp.maximum(i - 1, 0)])

    @pl.when(changed)
    def _():
        wg_sc[...] = wg_ref[0].astype(BF16)
        wu_sc[...] = wu_ref[0].astype(BF16)
        wd_sc[...] = wd_ref[0].astype(BF16)

    n_rows = nr_ref[i]

    @pl.when(n_rows > 0)
    def _():
        live = lax.broadcasted_iota(I32, (rows, d), 0) < n_rows
        x = jnp.where(live, xs_ref[...].reshape(rows, d), 0.0).astype(BF16)
        gate = jnp.dot(x, wg_sc[...], preferred_element_type=F32)
        up = jnp.dot(x, wu_sc[...], preferred_element_type=F32)
        hid = (jax.nn.silu(gate) * up).astype(BF16)
        y_ref[...] = jnp.dot(hid, wd_sc[...], preferred_element_type=F32).reshape(rows, 1, d)

    @pl.when(n_rows <= 0)
    def _():
        y_ref[...] = jnp.zeros(y_ref.shape, F32)


def _experts(xs, te, nr, w_gate, w_up, w_down):
    n_slots, _, d = xs.shape
    n_exp, _, d_e = w_gate.shape
    t = SLOT_ROWS
    est = 2 * 2 * t * d * 4 + 2 * 3 * d * d_e * 4 + 3 * d * d_e * 2 + 6 * t * d * 4
    return pl.pallas_call(
        _experts_kernel,
        grid_spec=pltpu.PrefetchScalarGridSpec(
            num_scalar_prefetch=2,
            grid=(n_slots // t,),
            in_specs=[
                pl.BlockSpec((t, 1, d), lambda i, te, nr: (i, 0, 0)),
                pl.BlockSpec((1, d, d_e), lambda i, te, nr: (te[i], 0, 0)),
                pl.BlockSpec((1, d, d_e), lambda i, te, nr: (te[i], 0, 0)),
                pl.BlockSpec((1, d_e, d), lambda i, te, nr: (te[i], 0, 0)),
            ],
            out_specs=pl.BlockSpec((t, 1, d), lambda i, te, nr: (i, 0, 0)),
            scratch_shapes=[pltpu.VMEM((d, d_e), BF16), pltpu.VMEM((d, d_e), BF16), pltpu.VMEM((d_e, d), BF16)],
        ),
        out_shape=jax.ShapeDtypeStruct((n_slots, 1, d), F32),
        compiler_params=pltpu.CompilerParams(
            dimension_semantics=("arbitrary",), vmem_limit_bytes=_vmem_limit(est)),
        name="experts",
    )(te, nr, xs, w_gate, w_up, w_down)


def _combine_kernel(p0_ref, p1_ref, xp_ref, rc_ref, g_ref, y_hbm, out_ref, y0_buf, y1_buf, sem):
    rows, d = xp_ref.shape

    def start(t, carry):
        pltpu.make_async_copy(y_hbm.at[p0_ref[t]], y0_buf.at[t], sem).start()
        pltpu.make_async_copy(y_hbm.at[p1_ref[t]], y1_buf.at[t], sem).start()
        return carry

    def wait(t, carry):
        pltpu.make_async_copy(y_hbm.at[p0_ref[t]], y0_buf.at[t], sem).wait()
        pltpu.make_async_copy(y_hbm.at[p1_ref[t]], y1_buf.at[t], sem).wait()
        return carry

    lax.fori_loop(0, rows, start, 0)
    lax.fori_loop(0, rows, wait, 0)
    ct = rc_ref[...].T
    moe = ct[:, 0:1] * y0_buf[...].reshape(rows, d) + ct[:, 1:2] * y1_buf[...].reshape(rows, d)
    out_ref[...] = _rms_rows(xp_ref[...] + moe, g_ref[...])


def _combine(xp, rc, p0, p1, y, g):
    n, d = xp.shape
    t = MIX_ROWS
    est = 2 * 2 * t * d * 4 + 2 * t * d * 4 + 6 * t * d * 4
    return pl.pallas_call(
        _combine_kernel,
        grid=(n // t,),
        in_specs=[
            pl.BlockSpec((t,), lambda i: (i,), memory_space=pltpu.SMEM),
            pl.BlockSpec((t,), lambda i: (i,), memory_space=pltpu.SMEM),
            pl.BlockSpec((t, d), lambda i: (i, 0)),
            pl.BlockSpec((SUBLANES, t), lambda i: (0, i)),
            pl.BlockSpec((1, d), lambda i: (0, 0)),
            pl.BlockSpec(memory_space=pl.ANY),
        ],
        out_specs=pl.BlockSpec((t, d), lambda i: (i, 0)),
        out_shape=jax.ShapeDtypeStruct((n, d), F32),
        scratch_shapes=[pltpu.VMEM((t, 1, d), F32), pltpu.VMEM((t, 1, d), F32), pltpu.SemaphoreType.DMA(())],
        compiler_params=pltpu.CompilerParams(
            dimension_semantics=("arbitrary",), vmem_limit_bytes=_vmem_limit(est)),
        name="combine",
    )(p0, p1, xp, rc, g, y)


def _rel_buckets(window):
    rel = (jnp.arange(window + CHUNK, dtype=I32) - window)[None, :] - jnp.arange(CHUNK, dtype=I32)[:, None]
    nb = N_BUCKETS // 2
    ret = (rel > 0).astype(I32) * nb
    n = jnp.abs(rel)
    max_exact = nb // 2
    nf = jnp.maximum(n, 1).astype(F32)
    large = max_exact + (jnp.log(nf / max_exact) / math.log(MAX_DISTANCE / max_exact)
                         * (nb - max_exact)).astype(I32)
    large = jnp.minimum(large, nb - 1)
    bucket = ret + jnp.where(n < max_exact, n, large)
    return jnp.pad(bucket, ((0, 0), (0, KEY_PAD - bucket.shape[1])), constant_values=-1)


def kernel(x_prompt, x_sample, cache_conv, cache_k, cache_v, rel_bias_table, norm_mix_g, w_in, conv_w, w_conv_out, attn_sinks, w_attn_out, w_o, norm_ffn_g, w_group, b_group, w_expert_router, b_expert_router, w_gate, w_up, w_down, final_norm_g):
    assert w_in.shape[0] == 1, "single-layer step"
    batch, seq, d = x_prompt.shape
    dec_batch, dec_seq, _ = x_sample.shape
    assert dec_seq == CHUNK and seq % MIX_ROWS == 0 and (dec_batch * dec_seq) % MIX_ROWS == 0
    d_conv = conv_w.shape[-1]
    window, n_kv, head_dim = cache_k.shape[2], cache_k.shape[3], cache_k.shape[4]
    n_heads = attn_sinks.shape[-1]
    d_attn, d_kv = n_heads * head_dim, n_kv * head_dim
    n_groups, n_exp = w_group.shape[-1], w_expert_router.shape[-1]
    epg = n_exp // n_groups
    assert n_groups <= GROUP_ROW0 and d_conv == d_attn and 2 * d_conv == d
    assert n_kv == 2 and d_kv == LANES and (n_heads // n_kv) % 2 == 0 and window + CHUNK <= KEY_PAD
    dims = (n_heads, n_kv, head_dim, window, n_groups, n_exp, d_conv)

    w = w_in[0]
    kv0 = 3 * d_conv + d_attn
    w_main = jnp.concatenate([w[:, :kv0], w[:, kv0 + 2 * d_kv:]], axis=1).astype(BF16)
    w_kv = w[:, kv0:kv0 + 2 * d_kv].astype(BF16)
    g1 = norm_mix_g[0][None, :]
    wr = jnp.zeros((GROUP_ROW0 + n_exp, d), F32)
    wr = wr.at[:n_groups].set(w_group[0].T).at[GROUP_ROW0:].set(w_expert_router[0].T).astype(BF16)
    br = jnp.zeros((GROUP_ROW0 + n_exp, 1), F32)
    br = br.at[:n_groups, 0].set(b_group[0]).at[GROUP_ROW0:, 0].set(b_expert_router[0])
    weights = (conv_w[0], w_conv_out[0].astype(BF16), w_attn_out[0].astype(BF16), w_o[0].astype(BF16),
               _rel_buckets(window), rel_bias_table, attn_sinks, norm_ffn_g[0][None, :], wr, br)

    xp2d = x_prompt.reshape(batch * seq, d)
    xs2d = x_sample.reshape(dec_batch * dec_seq, d)
    n_p, n_s = xp2d.shape[0], xs2d.shape[0]
    n_tok = n_p + n_s

    proj_p, kv_p = _inproj(xp2d, g1, w_main, w_kv)
    proj_s, kv_s = _inproj(xs2d, g1, w_main, w_kv)
    xres_p, h2_p, re_p, rc_p, conv_p, k_p, v_p = _mix(
        xp2d, proj_p, kv_p, None, weights, n_seq=batch, sample=False, dims=dims)
    caches = (cache_conv[0], cache_k[0].reshape(dec_batch, window, d_kv), cache_v[0].reshape(dec_batch, window, d_kv))
    xres_s, h2_s, re_s, rc_s, conv_s, k_s, v_s = _mix(
        xs2d, proj_s, kv_s, caches, weights, n_seq=dec_batch, sample=True, dims=dims)

    n_tiles = (TOP_K * n_tok) // SLOT_ROWS + n_exp
    n_slots = n_tiles * SLOT_ROWS
    pos, tmap = _route(jnp.concatenate([re_p, re_s], axis=1), n_exp, n_tiles)
    p0, p1 = pos[0], pos[1]
    xs = _dispatch(h2_p, p0[:n_p], p1[:n_p], None, n_slots)
    xs = _dispatch(h2_s, p0[n_p:], p1[n_p:], xs, n_slots)
    y = _experts(xs, tmap[0, :n_tiles], tmap[1, :n_tiles], w_gate[0], w_up[0], w_down[0])
    gf = final_norm_g[None, :]
    y_prompt = _combine(xres_p, rc_p, p0[:n_p], p1[:n_p], y, gf).reshape(batch, seq, d)
    y_sample = _combine(xres_s, rc_s, p0[n_p:], p1[n_p:], y, gf).reshape(dec_batch, dec_seq, d)

    kv_shape = (1, -1, window, n_kv, head_dim)
    return (y_prompt, y_sample, conv_p[None], k_p.reshape(kv_shape), v_p.reshape(kv_shape),
            conv_s[None], k_s.reshape(kv_shape), v_s.reshape(kv_shape))
```

```python
import functools
import math

import jax
import jax.numpy as jnp
from jax import lax
from jax.experimental import pallas as pl
from jax.experimental.pallas import tpu as pltpu

F32, BF16, I32, U32 = jnp.float32, jnp.bfloat16, jnp.int32, jnp.uint32

CHUNK = 64
N_BUCKETS = 32
MAX_DISTANCE = 128
EPS = 1e-6
NEG_INF = -1e30
TOP_K = 2

V7X_VMEM_BYTES = 64 * 1024 * 1024
SUBLANES = 8
LANES = 128

INPROJ_ROWS = 1024
INPROJ_COLS = 1024
MIX_ROWS = 256
SLOT_ROWS = 256
ROUTE_BLOCK = 256
GROUP_ROW0 = 8
KEY_PAD = 256
KV_VARIANTS = 4


def _vmem_limit(nbytes):
    return int(min(V7X_VMEM_BYTES - (4 << 20), max(nbytes, 32 << 20)))


def _pack_halves(x):
    half = x.shape[1] // 2
    return pltpu.pack_elementwise([x[:, :half], x[:, half:]], packed_dtype=BF16)


def _unpack_halves(w):
    lo = pltpu.unpack_elementwise(w, index=0, packed_dtype=BF16, unpacked_dtype=F32)
    hi = pltpu.unpack_elementwise(w, index=1, packed_dtype=BF16, unpacked_dtype=F32)
    return jnp.concatenate([lo, hi], axis=1)


def _rms_rows(x, g):
    r = lax.rsqrt(jnp.mean(x * x, axis=-1, keepdims=True) + EPS)
    return (x * r) * g


def _inproj_kernel(x_ref, g_ref, wm_ref, wkv_ref, proj_ref, kv_ref, h_sc):
    @pl.when(pl.program_id(1) == 0)
    def _():
        rows = 128
        def body(i, carry):
            r0 = pl.multiple_of(i * rows, rows)
            h_sc[pl.ds(r0, rows), :] = _rms_rows(x_ref[pl.ds(r0, rows), :], g_ref[...]).astype(BF16)
            return carry
        lax.fori_loop(0, x_ref.shape[0] // rows, body, 0)
        kv_ref[...] = jnp.dot(h_sc[...], wkv_ref[...], preferred_element_type=F32)

    proj_ref[...] = jnp.dot(h_sc[...], wm_ref[...], preferred_element_type=F32).astype(BF16)


def _inproj(x2d, g, w_main, w_kv):
    n, d = x2d.shape
    tm = min(INPROJ_ROWS, n)
    tn = INPROJ_COLS
    n_main, n_kv = w_main.shape[1], w_kv.shape[1]
    est = 2 * tm * d * 4 + tm * d * 2 + 2 * d * tn * 2 + 2 * d * n_kv * 2 + 2 * tm * tn * 2 + 2 * tm * n_kv * 4 + tm * tn * 4
    return pl.pallas_call(
        _inproj_kernel,
        grid=(n // tm, n_main // tn),
        in_specs=[
            pl.BlockSpec((tm, d), lambda i, j: (i, 0)),
            pl.BlockSpec((1, d), lambda i, j: (0, 0)),
            pl.BlockSpec((d, tn), lambda i, j: (0, j)),
            pl.BlockSpec((d, n_kv), lambda i, j: (0, 0)),
        ],
        out_specs=[
            pl.BlockSpec((tm, tn), lambda i, j: (i, j)),
            pl.BlockSpec((tm, n_kv), lambda i, j: (i, 0)),
        ],
        out_shape=[jax.ShapeDtypeStruct((n, n_main), BF16), jax.ShapeDtypeStruct((n, n_kv), F32)],
        scratch_shapes=[pltpu.VMEM((tm, d), BF16)],
        compiler_params=pltpu.CompilerParams(
            dimension_semantics=("arbitrary", "arbitrary"), vmem_limit_bytes=_vmem_limit(est + (8 << 20))),
        name="inproj",
    )(x2d, g, w_main, w_kv)


def _conv_rows(u, prev2, prev1, w):
    row = lax.broadcasted_iota(I32, u.shape, 0)
    u1 = jnp.where(row == 0, prev1, pltpu.roll(u, 1, axis=0))
    u2 = jnp.where(row == 0, prev2, jnp.where(row == 1, prev1, pltpu.roll(u, 2, axis=0)))
    return (w[0:1] * u2 + w[1:2] * u1) + w[2:3] * u


def _mix_kernel(*refs, sample, n_heads, n_kv, head_dim, window, n_groups, n_exp):
    if sample:
        (x_ref, ga_ref, gb_ref, b_ref, c_ref, xc_ref, q_ref, kv_ref, cconv_ref, ck_ref, cv_ref,
         convw_ref, wco_ref, wao_ref, wo_ref, bucket_ref, table_ref, sinks_ref, g2_ref, wr_ref, br_ref,
         xp_ref, h2_ref, re_ref, rc_ref, sconv_ref, sk_ref, sv_ref,
         kvar, o_sc, ya_sc, bias_sc, carry_u) = refs
    else:
        (x_ref, ga_ref, gb_ref, b_ref, c_ref, xc_ref, q_ref, kv_ref,
         convw_ref, wco_ref, wao_ref, wo_ref, bucket_ref, table_ref, sinks_ref, g2_ref, wr_ref, br_ref,
         xp_ref, h2_ref, re_ref, rc_ref, sconv_ref, sk_ref, sv_ref,
         kvar, o_sc, ya_sc, bias_sc, carry_u) = refs
    t_rows, d_model = x_ref.shape
    n_chunks = t_rows // CHUNK
    span = window + CHUNK
    q_per_kv = n_heads // n_kv
    d_kv = n_kv * head_dim
    scale = 1.0 / math.sqrt(head_dim)
    epg = n_exp // n_groups
    first_step = jnp.logical_and(pl.program_id(0) == 0, pl.program_id(1) == 0)
    seq_start = pl.program_id(1) == 0

    @pl.when(first_step)
    def _():
        bk = bucket_ref[...]
        for h in range(n_heads):
            acc = jnp.where(bk < 0, NEG_INF, 0.0)
            for b in range(N_BUCKETS):
                acc = jnp.where(bk == b, table_ref[b, h], acc)
            n, g = divmod(h, q_per_kv)
            pair, half = divmod(g, 2)
            bias_sc[n, pair * CHUNK:(pair + 1) * CHUNK, half * KEY_PAD:(half + 1) * KEY_PAD] = acc

    w_conv = convw_ref[...]
    if sample:
        for s in range(n_chunks):
            rows = slice(s * CHUNK, (s + 1) * CHUNK)
            u = c_ref[rows, :].astype(F32) * xc_ref[rows, :].astype(F32)
            y = _conv_rows(u, cconv_ref[s, 0:1, :], cconv_ref[s, 1:2, :], w_conv)
            ya_sc[rows, :] = (b_ref[rows, :].astype(F32) * y).astype(BF16)
            sconv_ref[s] = u[CHUNK - 2:CHUNK, :]
    else:
        u = c_ref[...].astype(F32) * xc_ref[...].astype(F32)
        prev = jnp.where(seq_start, 0.0, carry_u[...])
        y = _conv_rows(u, prev[SUBLANES - 2:SUBLANES - 1], prev[SUBLANES - 1:SUBLANES], w_conv)
        ya_sc[...] = (b_ref[...].astype(F32) * y).astype(BF16)
        carry_u[...] = u[t_rows - SUBLANES:t_rows, :]
        sconv_ref[0] = u[t_rows - 2:t_rows, :]
    y_a = jnp.dot(ya_sc[...], wco_ref[...], preferred_element_type=F32)

    def store_kv(row0, k_rows, v_rows):
        n_rows = k_rows.shape[0]
        low = lax.broadcasted_iota(I32, k_rows.shape, 1) < head_dim
        for base, a in ((0, k_rows), (KV_VARIANTS, v_rows)):
            b = pltpu.roll(a, head_dim, axis=1)
            kvar[base + 0, row0:row0 + n_rows, :] = jnp.where(low, a, 0.0).astype(BF16)
            kvar[base + 1, row0:row0 + n_rows, :] = jnp.where(low, 0.0, a).astype(BF16)
            kvar[base + 2, row0:row0 + n_rows, :] = jnp.where(low, b, 0.0).astype(BF16)
            kvar[base + 3, row0:row0 + n_rows, :] = jnp.where(low, 0.0, b).astype(BF16)

    if sample:
        for s in range(n_chunks):
            rows = slice(s * CHUNK, (s + 1) * CHUNK)
            store_kv(s * span, ck_ref[s], cv_ref[s])
            store_kv(s * span + window, kv_ref[rows, 0:d_kv], kv_ref[rows, d_kv:2 * d_kv])
            sk_ref[s, 0:window - CHUNK, :] = ck_ref[s, CHUNK:window, :]
            sv_ref[s, 0:window - CHUNK, :] = cv_ref[s, CHUNK:window, :]
            sk_ref[s, window - CHUNK:window, :] = kv_ref[rows, 0:d_kv]
            sv_ref[s, window - CHUNK:window, :] = kv_ref[rows, d_kv:2 * d_kv]
        k_stride = span
    else:
        @pl.when(seq_start)
        def _():
            kvar[:, 0:window, :] = jnp.zeros((2 * KV_VARIANTS, window, d_kv), BF16)
        store_kv(window, kv_ref[:, 0:d_kv], kv_ref[:, d_kv:2 * d_kv])
        sk_ref[0] = kv_ref[t_rows - window:t_rows, 0:d_kv]
        sv_ref[0] = kv_ref[t_rows - window:t_rows, d_kv:2 * d_kv]
        k_stride = CHUNK

    n_pairs = q_per_kv // 2
    key_pad_rows = jnp.zeros((KEY_PAD - span, d_kv), BF16)
    for c in range(n_chunks):
        q_rows = slice(c * CHUNK, (c + 1) * CHUNK)
        k_rows = slice(c * k_stride, c * k_stride + span)
        masked = (not sample) and c * CHUNK < window
        if masked:
            first_key = (pl.program_id(1) * n_chunks + c) * CHUNK - window
            valid = lax.broadcasted_iota(I32, (CHUNK, KEY_PAD), 1) + first_key >= 0
        for n in range(n_kv):
            top, bot = (0, 3) if n == 0 else (2, 1)
            kd = jnp.concatenate([kvar[top, k_rows, :], key_pad_rows, kvar[bot, k_rows, :], key_pad_rows], axis=0)
            vd = jnp.concatenate([kvar[KV_VARIANTS + top, k_rows, :], key_pad_rows,
                                  kvar[KV_VARIANTS + bot, k_rows, :], key_pad_rows], axis=0)
            q4 = jnp.concatenate(
                [q_ref[q_rows, (n * n_pairs + j) * LANES:(n * n_pairs + j + 1) * LANES] for j in range(n_pairs)], axis=0)
            s4 = lax.dot_general(q4, kd, (((1,), (1,)), ((), ())), preferred_element_type=F32)
            p_rows = []
            for j in range(n_pairs):
                p_halves = []
                for half in range(2):
                    blk = (slice(j * CHUNK, (j + 1) * CHUNK), slice(half * KEY_PAD, (half + 1) * KEY_PAD))
                    s = s4[blk] * scale + bias_sc[n, blk[0], blk[1]]
                    if masked:
                        s = jnp.where(valid, s, NEG_INF)
                    sink = sinks_ref[0, n * q_per_kv + 2 * j + half]
                    mx = jnp.maximum(jnp.max(s, axis=-1, keepdims=True), sink)
                    e = jnp.exp(s - mx)
                    den = jnp.sum(e, axis=-1, keepdims=True) + jnp.exp(sink - mx)
                    p_halves.append((e * (1.0 / den)).astype(BF16))
                p_rows.append(jnp.concatenate(p_halves, axis=1))
            o4 = jnp.dot(jnp.concatenate(p_rows, axis=0), vd, preferred_element_type=F32)
            for j in range(n_pairs):
                o_sc[q_rows, (n * n_pairs + j) * LANES:(n * n_pairs + j + 1) * LANES] = (
                    o4[j * CHUNK:(j + 1) * CHUNK].astype(BF16))
    if not sample:
        kvar[:, 0:window, :] = kvar[:, t_rows:t_rows + window, :]
    y_b = jnp.dot(o_sc[...], wao_ref[...], preferred_element_type=F32)

    m = jax.nn.sigmoid(ga_ref[...].astype(F32)) * y_a + jax.nn.sigmoid(gb_ref[...].astype(F32)) * y_b
    xp = x_ref[...] + jnp.dot(m.astype(BF16), wo_ref[...], preferred_element_type=F32)
    xp_ref[...] = xp

    h2 = _rms_rows(xp, g2_ref[...])
    h2_ref[...] = _pack_halves(h2).reshape(t_rows, 1, d_model // 2)
    lt = lax.dot_general(wr_ref[...], h2.astype(BF16), (((1,), (1,)), ((), ())),
                         preferred_element_type=F32) + br_ref[...]
    lg = lt[0:n_groups]
    eg = jnp.exp(lg - jnp.max(lg, axis=0, keepdims=True))
    gp = eg / jnp.sum(eg, axis=0, keepdims=True)
    gw = jnp.max(gp, axis=0, keepdims=True)
    gi = lax.broadcasted_iota(I32, gp.shape, 0).astype(F32)
    gsel = jnp.min(jnp.where(gp == gw, gi, float(n_groups)), axis=0, keepdims=True)
    el = jnp.zeros((epg, t_rows), F32)
    for g in range(n_groups):
        el = jnp.where(gsel == float(g), lt[GROUP_ROW0 + g * epg:GROUP_ROW0 + (g + 1) * epg], el)
    ei = lax.broadcasted_iota(I32, el.shape, 0).astype(F32)
    v1 = jnp.max(el, axis=0, keepdims=True)
    i1 = jnp.min(jnp.where(el == v1, ei, float(epg)), axis=0, keepdims=True)
    el2 = jnp.where(ei == i1, -jnp.inf, el)
    v2 = jnp.max(el2, axis=0, keepdims=True)
    i2 = jnp.min(jnp.where(el2 == v2, ei, float(epg)), axis=0, keepdims=True)
    a1 = jnp.exp(v1 - v1)
    a2 = jnp.exp(v2 - v1)
    den = a1 + a2
    c1 = gw * (a1 / den)
    c2 = gw * (a2 / den)
    e1 = (gsel * float(epg) + i1).astype(I32)
    e2 = (gsel * float(epg) + i2).astype(I32)
    row8 = lax.broadcasted_iota(I32, (SUBLANES, t_rows), 0)
    re_ref[...] = jnp.where(row8 == 0, e1, jnp.where(row8 == 1, e2, 0))
    rc_ref[...] = jnp.where(row8 == 0, c1, jnp.where(row8 == 1, c2, 0.0))


def _mix(x2d, proj, kv, caches, weights, *, n_seq, sample, dims):
    n_heads, n_kv, head_dim, window, n_groups, n_exp, d_conv = dims
    n, d = x2d.shape
    t = MIX_ROWS
    n_chunks = t // CHUNK
    span = window + CHUNK
    d_attn = n_heads * head_dim
    d_kv = n_kv * head_dim
    convw, wco, wao, wo, bucket, table, sinks, g2, wr, br = weights
    if sample:
        n_t = n // t
        grid = (n_t, 1)
        tok = lambda i, j: (i, 0)
        n_state = n // CHUNK
        state_blk = n_chunks
        st = lambda i, j: (i, 0, 0)
    else:
        n_t = (n // n_seq) // t
        grid = (n_seq, n_t)
        tok = lambda i, j: (i * n_t + j, 0)
        n_state = n_seq
        state_blk = 1
        st = lambda i, j: (i, 0, 0)
    const2 = lambda i, j: (0, 0)

    def col(width, idx):
        return pl.BlockSpec((t, width), lambda i, j: (tok(i, j)[0], idx))

    def resident(shape):
        return pl.BlockSpec(shape, const2, pipeline_mode=pl.Buffered(1))

    in_specs = [
        pl.BlockSpec((t, d), tok),
        col(d, 2), col(d, 3),
        col(d_conv, 0), col(d_conv, 1), col(d_conv, 2), col(d_attn, 3),
        pl.BlockSpec((t, 2 * d_kv), tok),
    ]
    args = [x2d, proj, proj, proj, proj, proj, proj, kv]
    if sample:
        cconv, ck, cv = caches
        in_specs += [
            pl.BlockSpec((n_chunks, cconv.shape[1], d_conv), st),
            pl.BlockSpec((n_chunks, window, d_kv), st),
            pl.BlockSpec((n_chunks, window, d_kv), st),
        ]
        args += [cconv, ck, cv]
    in_specs += [
        resident(convw.shape), resident(wco.shape), resident(wao.shape), resident(wo.shape),
        resident(bucket.shape),
        pl.BlockSpec(memory_space=pltpu.SMEM), pl.BlockSpec(memory_space=pltpu.SMEM),
        resident(g2.shape), resident(wr.shape), resident(br.shape),
    ]
    args += [convw, wco, wao, wo, bucket, table, sinks, g2, wr, br]
    out_specs = [
        pl.BlockSpec((t, d), tok),
        pl.BlockSpec((t, 1, d // 2), lambda i, j: (tok(i, j)[0], 0, 0)),
        pl.BlockSpec((SUBLANES, t), lambda i, j: (0, tok(i, j)[0])),
        pl.BlockSpec((SUBLANES, t), lambda i, j: (0, tok(i, j)[0])),
        pl.BlockSpec((state_blk, 2, d_conv), st),
        pl.BlockSpec((state_blk, window, d_kv), st),
        pl.BlockSpec((state_blk, window, d_kv), st),
    ]
    out_shape = [
        jax.ShapeDtypeStruct((n, d), F32),
        jax.ShapeDtypeStruct((n, 1, d // 2), U32),
        jax.ShapeDtypeStruct((SUBLANES, n), I32),
        jax.ShapeDtypeStruct((SUBLANES, n), F32),
        jax.ShapeDtypeStruct((n_state, 2, d_conv), F32),
        jax.ShapeDtypeStruct((n_state, window, d_kv), F32),
        jax.ShapeDtypeStruct((n_state, window, d_kv), F32),
    ]
    kv_rows = n_chunks * span if sample else window + t
    scratch = [
        pltpu.VMEM((2 * KV_VARIANTS, kv_rows, d_kv), BF16),
        pltpu.VMEM((t, d_attn), BF16), pltpu.VMEM((t, d_conv), BF16),
        pltpu.VMEM((n_kv, (n_heads // n_kv // 2) * CHUNK, 2 * KEY_PAD), F32), pltpu.VMEM((SUBLANES, d_conv), F32),
    ]
    est = (2 * t * d * 4 * 3 + 2 * 2 * t * d * 2 + 2 * 4 * t * d_conv * 2
           + (2 * d_conv * d + d * d) * 2 + 12 * t * d * 4)
    kern = functools.partial(_mix_kernel, sample=sample, n_heads=n_heads, n_kv=n_kv, head_dim=head_dim,
                             window=window, n_groups=n_groups, n_exp=n_exp)
    return pl.pallas_call(
        kern, grid=grid, in_specs=in_specs, out_specs=out_specs, out_shape=out_shape,
        scratch_shapes=scratch,
        compiler_params=pltpu.CompilerParams(
            dimension_semantics=("arbitrary", "arbitrary"), vmem_limit_bytes=_vmem_limit(est)),
        name="mix_sample" if sample else "mix_prompt",
    )(*args)


def _route_kernel(re_ref, pos_ref, tmap_ref, *, n_exp, tile_rows, blk):
    n = re_ref.shape[1]
    n_blk = n // blk
    erow = lax.broadcasted_iota(I32, (n_exp, blk), 0)

    def onehots(j):
        c0 = pl.multiple_of(j * blk, blk)
        oh0 = (erow == re_ref[0:1, pl.ds(c0, blk)]).astype(F32)
        oh1 = (erow == re_ref[1:2, pl.ds(c0, blk)]).astype(F32)
        return c0, oh0, oh1

    def count_body(j, cnt):
        _, oh0, oh1 = onehots(j)
        return cnt + jnp.sum(oh0 + oh1, axis=1, keepdims=True)

    cnt = lax.fori_loop(0, n_blk, count_body, jnp.zeros((n_exp, 1), F32))
    cnt = jnp.broadcast_to(cnt, (n_exp, LANES))
    padded = jnp.ceil(cnt / tile_rows) * tile_rows
    ends = padded
    prow = lax.broadcasted_iota(I32, ends.shape, 0)
    step = 1
    while step < n_exp:
        ends = ends + jnp.where(prow >= step, pltpu.roll(ends, step, axis=0), 0.0)
        step *= 2
    offs = ends - padded
    off1 = offs[:, 0:1]

    tri = (lax.broadcasted_iota(I32, (blk, blk), 0) <= lax.broadcasted_iota(I32, (blk, blk), 1)).astype(BF16)
    row8 = lax.broadcasted_iota(I32, (SUBLANES, blk), 0)

    def pos_body(j, run):
        c0, oh0, oh1 = onehots(j)
        both = oh0 + oh1
        csum = jnp.dot(both.astype(BF16), tri, preferred_element_type=F32) + run
        slot = off1 + csum - 1.0
        p0 = jnp.sum(oh0 * slot, axis=0, keepdims=True).astype(I32)
        p1 = jnp.sum(oh1 * slot, axis=0, keepdims=True).astype(I32)
        pos_ref[:, pl.ds(c0, blk)] = jnp.where(row8 == 0, p0, jnp.where(row8 == 1, p1, 0))
        return run + jnp.sum(both, axis=1, keepdims=True)

    lax.fori_loop(0, n_blk, pos_body, jnp.zeros((n_exp, 1), F32))

    n_tiles_pad = tmap_ref.shape[1]
    start = (lax.broadcasted_iota(I32, (1, n_tiles_pad), 1) * tile_rows).astype(F32)
    end1 = ends[:, 0:1]
    te = jnp.sum((end1 <= start).astype(F32), axis=0, keepdims=True)
    trow = lax.broadcasted_iota(I32, (n_exp, n_tiles_pad), 0).astype(F32)
    used_end = jnp.sum(jnp.where(trow == te, off1 + cnt[:, 0:1], 0.0), axis=0, keepdims=True)
    n_rows = jnp.clip(used_end - start, 0.0, float(tile_rows))
    te = jnp.minimum(te, float(n_exp - 1))
    r8 = lax.broadcasted_iota(I32, (SUBLANES, n_tiles_pad), 0)
    tmap_ref[...] = jnp.where(r8 == 0, te.astype(I32), jnp.where(r8 == 1, n_rows.astype(I32), 0))


def _route(re, n_exp, n_tiles):
    n = re.shape[1]
    n_tiles_pad = pl.cdiv(n_tiles, LANES) * LANES
    kern = functools.partial(_route_kernel, n_exp=n_exp, tile_rows=SLOT_ROWS, blk=ROUTE_BLOCK)
    return pl.pallas_call(
        kern,
        out_shape=[jax.ShapeDtypeStruct((SUBLANES, n), I32), jax.ShapeDtypeStruct((SUBLANES, n_tiles_pad), I32)],
        name="route",
    )(re)


def _dispatch_kernel(*refs, aliased):
    if aliased:
        p0_ref, p1_ref, h_ref, _, xs_ref, sem = refs
    else:
        p0_ref, p1_ref, h_ref, xs_ref, sem = refs
    rows = h_ref.shape[0]

    def start(t, carry):
        pltpu.make_async_copy(h_ref.at[t], xs_ref.at[p0_ref[t]], sem).start()
        pltpu.make_async_copy(h_ref.at[t], xs_ref.at[p1_ref[t]], sem).start()
        return carry

    lax.fori_loop(0, rows, start, 0)
    for _ in range(TOP_K):
        pltpu.make_async_copy(h_ref, xs_ref.at[pl.ds(0, rows)], sem).wait()


def _dispatch(h2, p0, p1, xs, n_slots):
    n, _, d = h2.shape
    t = MIX_ROWS
    aliased = xs is not None
    in_specs = [
        pl.BlockSpec((t,), lambda i: (i,), memory_space=pltpu.SMEM),
        pl.BlockSpec((t,), lambda i: (i,), memory_space=pltpu.SMEM),
        pl.BlockSpec((t, 1, d), lambda i: (i, 0, 0)),
    ]
    args = [p0, p1, h2]
    if aliased:
        in_specs.append(pl.BlockSpec(memory_space=pl.ANY))
        args.append(xs)
    return pl.pallas_call(
        functools.partial(_dispatch_kernel, aliased=aliased),
        grid=(n // t,),
        in_specs=in_specs,
        out_specs=pl.BlockSpec(memory_space=pl.ANY),
        out_shape=jax.ShapeDtypeStruct((n_slots, 1, d), h2.dtype),
        scratch_shapes=[pltpu.SemaphoreType.DMA(())],
        input_output_aliases={3: 0} if aliased else {},
        compiler_params=pltpu.CompilerParams(dimension_semantics=("arbitrary",), has_side_effects=True),
        name="dispatch",
    )(*args)


def _experts_kernel(te_ref, nr_ref, xs_ref, wg_ref, wu_ref, wd_ref, y_ref, wg_sc, wu_sc, wd_sc):
    i = pl.program_id(0)
    rows, _, dh = xs_ref.shape
    changed = jnp.logical_or(i == 0, te_ref[i] != te_ref[jnp.maximum(i - 1, 0)])

    @pl.when(changed)
    def _():
        wg_sc[...] = wg_ref[0].astype(BF16)
        wu_sc[...] = wu_ref[0].astype(BF16)
        wd_sc[...] = wd_ref[0].astype(BF16)

    n_rows = nr_ref[i]

    @pl.when(n_rows > 0)
    def _():
        live = lax.broadcasted_iota(I32, (rows, dh), 0) < n_rows
        words = jnp.where(live, xs_ref[...].reshape(rows, dh), jnp.uint32(0))
        x = _unpack_halves(words).astype(BF16)
        gate = jnp.dot(x, wg_sc[...], preferred_element_type=F32)
        up = jnp.dot(x, wu_sc[...], preferred_element_type=F32)
        hid = (jax.nn.silu(gate) * up).astype(BF16)
        y = jnp.dot(hid, wd_sc[...], preferred_element_type=F32)
        y_ref[...] = _pack_halves(y).reshape(rows, 1, dh)

    @pl.when(n_rows <= 0)
    def _():
        y_ref[...] = jnp.zeros(y_ref.shape, U32)


def _experts(xs, te, nr, w_gate, w_up, w_down):
    n_slots, _, dh = xs.shape
    n_exp, d, d_e = w_gate.shape
    t = SLOT_ROWS
    est = 2 * 2 * t * d * 4 + 2 * 3 * d * d_e * 4 + 3 * d * d_e * 2 + 6 * t * d * 4
    return pl.pallas_call(
        _experts_kernel,
        grid_spec=pltpu.PrefetchScalarGridSpec(
            num_scalar_prefetch=2,
            grid=(n_slots // t,),
            in_specs=[
                pl.BlockSpec((t, 1, dh), lambda i, te, nr: (i, 0, 0)),
                pl.BlockSpec((1, d, d_e), lambda i, te, nr: (te[i], 0, 0)),
                pl.BlockSpec((1, d, d_e), lambda i, te, nr: (te[i], 0, 0)),
                pl.BlockSpec((1, d_e, d), lambda i, te, nr: (te[i], 0, 0)),
            ],
            out_specs=pl.BlockSpec((t, 1, dh), lambda i, te, nr: (i, 0, 0)),
            scratch_shapes=[pltpu.VMEM((d, d_e), BF16), pltpu.VMEM((d, d_e), BF16), pltpu.VMEM((d_e, d), BF16)],
        ),
        out_shape=jax.ShapeDtypeStruct((n_slots, 1, dh), U32),
        compiler_params=pltpu.CompilerParams(
            dimension_semantics=("arbitrary",), vmem_limit_bytes=_vmem_limit(est)),
        name="experts",
    )(te, nr, xs, w_gate, w_up, w_down)


def _combine_kernel(p0_ref, p1_ref, p0n_ref, p1n_ref, xp_ref, rc_ref, g_ref, y_hbm, out_ref, y0_buf, y1_buf, sems,
                    *, n_tiles):
    i = pl.program_id(0)
    rows, d = xp_ref.shape

    def gather(pa_ref, pb_ref, slot):
        base = slot * rows
        def body(t, carry):
            pltpu.make_async_copy(y_hbm.at[pa_ref[t]], y0_buf.at[base + t], sems.at[slot]).start()
            pltpu.make_async_copy(y_hbm.at[pb_ref[t]], y1_buf.at[base + t], sems.at[slot]).start()
            return carry
        lax.fori_loop(0, rows, body, 0)

    slot = lax.rem(i, 2)

    @pl.when(i == 0)
    def _():
        gather(p0_ref, p1_ref, 0)

    @pl.when(i + 1 < n_tiles)
    def _():
        gather(p0n_ref, p1n_ref, 1 - slot)

    cur = pl.ds(pl.multiple_of(slot * rows, rows), rows)
    for buf in (y0_buf, y1_buf):
        pltpu.make_async_copy(y_hbm.at[pl.ds(0, rows)], buf.at[cur], sems.at[slot]).wait()
    ct = rc_ref[...].T
    y0 = _unpack_halves(y0_buf[cur].reshape(rows, d // 2))
    y1 = _unpack_halves(y1_buf[cur].reshape(rows, d // 2))
    moe = ct[:, 0:1] * y0 + ct[:, 1:2] * y1
    out_ref[...] = _rms_rows(xp_ref[...] + moe, g_ref[...])


def _combine(xp, rc, p0, p1, y, g):
    n, d = xp.shape
    t = MIX_ROWS
    n_t = n // t
    est = 2 * 2 * t * d * 4 + 2 * 2 * t * d * 2 + 8 * t * d * 4
    nxt = lambda i: (jnp.minimum(i + 1, n_t - 1),)
    return pl.pallas_call(
        functools.partial(_combine_kernel, n_tiles=n_t),
        grid=(n_t,),
        in_specs=[
            pl.BlockSpec((t,), lambda i: (i,), memory_space=pltpu.SMEM),
            pl.BlockSpec((t,), lambda i: (i,), memory_space=pltpu.SMEM),
            pl.BlockSpec((t,), nxt, memory_space=pltpu.SMEM),
            pl.BlockSpec((t,), nxt, memory_space=pltpu.SMEM),
            pl.BlockSpec((t, d), lambda i: (i, 0)),
            pl.BlockSpec((SUBLANES, t), lambda i: (0, i)),
            pl.BlockSpec((1, d), lambda i: (0, 0)),
            pl.BlockSpec(memory_space=pl.ANY),
        ],
        out_specs=pl.BlockSpec((t, d), lambda i: (i, 0)),
        out_shape=jax.ShapeDtypeStruct((n, d), F32),
        scratch_shapes=[pltpu.VMEM((2 * t, 1, d // 2), U32), pltpu.VMEM((2 * t, 1, d // 2), U32),
                        pltpu.SemaphoreType.DMA((2,))],
        compiler_params=pltpu.CompilerParams(
            dimension_semantics=("arbitrary",), vmem_limit_bytes=_vmem_limit(est)),
        name="combine",
    )(p0, p1, p0, p1, xp, rc, g, y)


def _rel_buckets(window):
    rel = (jnp.arange(window + CHUNK, dtype=I32) - window)[None, :] - jnp.arange(CHUNK, dtype=I32)[:, None]
    nb = N_BUCKETS // 2
    ret = (rel > 0).astype(I32) * nb
    n = jnp.abs(rel)
    max_exact = nb // 2
    nf = jnp.maximum(n, 1).astype(F32)
    large = max_exact + (jnp.log(nf / max_exact) / math.log(MAX_DISTANCE / max_exact)
                         * (nb - max_exact)).astype(I32)
    large = jnp.minimum(large, nb - 1)
    bucket = ret + jnp.where(n < max_exact, n, large)
    return jnp.pad(bucket, ((0, 0), (0, KEY_PAD - bucket.shape[1])), constant_values=-1)


def kernel(x_prompt, x_sample, cache_conv, cache_k, cache_v, rel_bias_table, norm_mix_g, w_in, conv_w, w_conv_out, attn_sinks, w_attn_out, w_o, norm_ffn_g, w_group, b_group, w_expert_router, b_expert_router, w_gate, w_up, w_down, final_norm_g):
    assert w_in.shape[0] == 1, "single-layer step"
    batch, seq, d = x_prompt.shape
    dec_batch, dec_seq, _ = x_sample.shape
    assert dec_seq == CHUNK and seq % MIX_ROWS == 0 and (dec_batch * dec_seq) % MIX_ROWS == 0
    d_conv = conv_w.shape[-1]
    window, n_kv, head_dim = cache_k.shape[2], cache_k.shape[3], cache_k.shape[4]
    n_heads = attn_sinks.shape[-1]
    d_attn, d_kv = n_heads * head_dim, n_kv * head_dim
    n_groups, n_exp = w_group.shape[-1], w_expert_router.shape[-1]
    epg = n_exp // n_groups
    assert n_groups <= GROUP_ROW0 and d_conv == d_attn and 2 * d_conv == d
    assert n_kv == 2 and d_kv == LANES and (n_heads // n_kv) % 2 == 0 and window + CHUNK <= KEY_PAD
    dims = (n_heads, n_kv, head_dim, window, n_groups, n_exp, d_conv)

    w = w_in[0]
    kv0 = 3 * d_conv + d_attn
    w_main = jnp.concatenate([w[:, :kv0], w[:, kv0 + 2 * d_kv:]], axis=1).astype(BF16)
    w_kv = w[:, kv0:kv0 + 2 * d_kv].astype(BF16)
    g1 = norm_mix_g[0][None, :]
    wr = jnp.zeros((GROUP_ROW0 + n_exp, d), F32)
    wr = wr.at[:n_groups].set(w_group[0].T).at[GROUP_ROW0:].set(w_expert_router[0].T).astype(BF16)
    br = jnp.zeros((GROUP_ROW0 + n_exp, 1), F32)
    br = br.at[:n_groups, 0].set(b_group[0]).at[GROUP_ROW0:, 0].set(b_expert_router[0])
    weights = (conv_w[0], w_conv_out[0].astype(BF16), w_attn_out[0].astype(BF16), w_o[0].astype(BF16),
               _rel_buckets(window), rel_bias_table, attn_sinks, norm_ffn_g[0][None, :], wr, br)

    xp2d = x_prompt.reshape(batch * seq, d)
    xs2d = x_sample.reshape(dec_batch * dec_seq, d)
    n_p, n_s = xp2d.shape[0], xs2d.shape[0]
    n_tok = n_p + n_s

    proj_p, kv_p = _inproj(xp2d, g1, w_main, w_kv)
    proj_s, kv_s = _inproj(xs2d, g1, w_main, w_kv)
    xres_p, h2_p, re_p, rc_p, conv_p, k_p, v_p = _mix(
        xp2d, proj_p, kv_p, None, weights, n_seq=batch, sample=False, dims=dims)
    caches = (cache_conv[0], cache_k[0].reshape(dec_batch, window, d_kv), cache_v[0].reshape(dec_batch, window, d_kv))
    xres_s, h2_s, re_s, rc_s, conv_s, k_s, v_s = _mix(
        xs2d, proj_s, kv_s, caches, weights, n_seq=dec_batch, sample=True, dims=dims)

    n_tiles = (TOP_K * n_tok) // SLOT_ROWS + n_exp
    n_slots = n_tiles * SLOT_ROWS
    pos, tmap = _route(jnp.concatenate([re_p, re_s], axis=1), n_exp, n_tiles)
    p0, p1 = pos[0], pos[1]
    xs = _dispatch(h2_p, p0[:n_p], p1[:n_p], None, n_slots)
    xs = _dispatch(h2_s, p0[n_p:], p1[n_p:], xs, n_slots)
    y = _experts(xs, tmap[0, :n_tiles], tmap[1, :n_tiles], w_gate[0], w_up[0], w_down[0])
    gf = final_norm_g[None, :]
    y_prompt = _combine(xres_p, rc_p, p0[:n_p], p1[:n_p], y, gf).reshape(batch, seq, d)
    y_sample = _combine(xres_s, rc_s, p0[n_p:], p1[n_p:], y, gf).reshape(dec_batch, dec_seq, d)

    kv_shape = (1, -1, window, n_kv, head_dim)
    return (y_prompt, y_sample, conv_p[None], k_p.reshape(kv_shape), v_p.reshape(kv_shape),
            conv_s[None], k_s.reshape(kv_shape), v_s.reshape(kv_shape))
```

```python
import functools
import math

import jax
import jax.numpy as jnp
from jax import lax
from jax.experimental import pallas as pl
from jax.experimental.pallas import tpu as pltpu

F32, BF16, I32, U32 = jnp.float32, jnp.bfloat16, jnp.int32, jnp.uint32

CHUNK = 64
N_BUCKETS = 32
MAX_DISTANCE = 128
EPS = 1e-6
NEG_INF = -1e30
TOP_K = 2

V7X_VMEM_BYTES = 64 * 1024 * 1024
SUBLANES = 8
LANES = 128

INPROJ_ROWS = 1024
INPROJ_COLS = 1024
MIX_ROWS = 256
SLOT_ROWS = 256
ROUTE_BLOCK = 256
GROUP_ROW0 = 8
KEY_PAD = 256
KV_VARIANTS = 4


def _vmem_limit(nbytes):
    return int(min(V7X_VMEM_BYTES - (4 << 20), max(nbytes, 32 << 20)))


def _pack_halves(x):
    half = x.shape[1] // 2
    return pltpu.pack_elementwise([x[:, :half], x[:, half:]], packed_dtype=BF16)


def _unpack_halves(w):
    lo = pltpu.unpack_elementwise(w, index=0, packed_dtype=BF16, unpacked_dtype=F32)
    hi = pltpu.unpack_elementwise(w, index=1, packed_dtype=BF16, unpacked_dtype=F32)
    return jnp.concatenate([lo, hi], axis=1)


def _rms_rows(x, g):
    r = lax.rsqrt(jnp.mean(x * x, axis=-1, keepdims=True) + EPS)
    return (x * r) * g


def _inproj_kernel(x_ref, g_ref, wm_ref, wkv_ref, proj_ref, kv_ref, h_sc):
    @pl.when(pl.program_id(1) == 0)
    def _():
        rows = 128
        def body(i, carry):
            r0 = pl.multiple_of(i * rows, rows)
            h_sc[pl.ds(r0, rows), :] = _rms_rows(x_ref[pl.ds(r0, rows), :], g_ref[...]).astype(BF16)
            return carry
        lax.fori_loop(0, x_ref.shape[0] // rows, body, 0)
        kv_ref[...] = jnp.dot(h_sc[...], wkv_ref[...], preferred_element_type=F32)

    proj_ref[...] = jnp.dot(h_sc[...], wm_ref[...], preferred_element_type=F32).astype(BF16)


def _inproj(x2d, g, w_main, w_kv):
    n, d = x2d.shape
    tm = min(INPROJ_ROWS, n)
    tn = INPROJ_COLS
    n_main, n_kv = w_main.shape[1], w_kv.shape[1]
    est = 2 * tm * d * 4 + tm * d * 2 + 2 * d * tn * 2 + 2 * d * n_kv * 2 + 2 * tm * tn * 2 + 2 * tm * n_kv * 4 + tm * tn * 4
    return pl.pallas_call(
        _inproj_kernel,
        grid=(n // tm, n_main // tn),
        in_specs=[
            pl.BlockSpec((tm, d), lambda i, j: (i, 0)),
            pl.BlockSpec((1, d), lambda i, j: (0, 0)),
            pl.BlockSpec((d, tn), lambda i, j: (0, j)),
            pl.BlockSpec((d, n_kv), lambda i, j: (0, 0)),
        ],
        out_specs=[
            pl.BlockSpec((tm, tn), lambda i, j: (i, j)),
            pl.BlockSpec((tm, n_kv), lambda i, j: (i, 0)),
        ],
        out_shape=[jax.ShapeDtypeStruct((n, n_main), BF16), jax.ShapeDtypeStruct((n, n_kv), F32)],
        scratch_shapes=[pltpu.VMEM((tm, d), BF16)],
        compiler_params=pltpu.CompilerParams(
            dimension_semantics=("arbitrary", "arbitrary"), vmem_limit_bytes=_vmem_limit(est + (8 << 20))),
        name="inproj",
    )(x2d, g, w_main, w_kv)


def _conv_rows(u, prev2, prev1, w):
    row = lax.broadcasted_iota(I32, u.shape, 0)
    u1 = jnp.where(row == 0, prev1, pltpu.roll(u, 1, axis=0))
    u2 = jnp.where(row == 0, prev2, jnp.where(row == 1, prev1, pltpu.roll(u, 2, axis=0)))
    return (w[0:1] * u2 + w[1:2] * u1) + w[2:3] * u


def _mix_kernel(*refs, sample, n_heads, n_kv, head_dim, window, n_groups, n_exp):
    if sample:
        (x_ref, ga_ref, gb_ref, b_ref, c_ref, xc_ref, q_ref, kv_ref, cconv_ref, ck_ref, cv_ref,
         convw_ref, wco_ref, wao_ref, wo_ref, bucket_ref, table_ref, sinks_ref, g2_ref, wr_ref, br_ref,
         xp_ref, h2_ref, re_ref, rc_ref, sconv_ref, sk_ref, sv_ref,
         kvar, o_sc, ya_sc, bias_sc, carry_u) = refs
    else:
        (x_ref, ga_ref, gb_ref, b_ref, c_ref, xc_ref, q_ref, kv_ref,
         convw_ref, wco_ref, wao_ref, wo_ref, bucket_ref, table_ref, sinks_ref, g2_ref, wr_ref, br_ref,
         xp_ref, h2_ref, re_ref, rc_ref, sconv_ref, sk_ref, sv_ref,
         kvar, o_sc, ya_sc, bias_sc, carry_u) = refs
    t_rows, d_model = x_ref.shape
    n_chunks = t_rows // CHUNK
    span = window + CHUNK
    q_per_kv = n_heads // n_kv
    d_kv = n_kv * head_dim
    scale = 1.0 / math.sqrt(head_dim)
    scale_is_pow2 = math.frexp(scale)[0] == 0.5
    epg = n_exp // n_groups
    first_step = jnp.logical_and(pl.program_id(0) == 0, pl.program_id(1) == 0)
    seq_start = pl.program_id(1) == 0

    @pl.when(first_step)
    def _():
        bk = bucket_ref[...]
        for h in range(n_heads):
            acc = jnp.where(bk < 0, NEG_INF, 0.0)
            for b in range(N_BUCKETS):
                acc = jnp.where(bk == b, table_ref[b, h], acc)
            n, g = divmod(h, q_per_kv)
            pair, half = divmod(g, 2)
            bias_sc[n, pair * CHUNK:(pair + 1) * CHUNK, half * KEY_PAD:(half + 1) * KEY_PAD] = acc

    w_conv = convw_ref[...]
    if sample:
        for s in range(n_chunks):
            rows = slice(s * CHUNK, (s + 1) * CHUNK)
            u = c_ref[rows, :].astype(F32) * xc_ref[rows, :].astype(F32)
            y = _conv_rows(u, cconv_ref[s, 0:1, :], cconv_ref[s, 1:2, :], w_conv)
            ya_sc[rows, :] = (b_ref[rows, :].astype(F32) * y).astype(BF16)
            sconv_ref[s] = u[CHUNK - 2:CHUNK, :]
    else:
        u = c_ref[...].astype(F32) * xc_ref[...].astype(F32)
        prev = jnp.where(seq_start, 0.0, carry_u[...])
        y = _conv_rows(u, prev[SUBLANES - 2:SUBLANES - 1], prev[SUBLANES - 1:SUBLANES], w_conv)
        ya_sc[...] = (b_ref[...].astype(F32) * y).astype(BF16)
        carry_u[...] = u[t_rows - SUBLANES:t_rows, :]
        sconv_ref[0] = u[t_rows - 2:t_rows, :]
    y_a = jnp.dot(ya_sc[...], wco_ref[...], preferred_element_type=F32)

    def store_kv(row0, k_rows, v_rows):
        n_rows = k_rows.shape[0]
        low = lax.broadcasted_iota(I32, k_rows.shape, 1) < head_dim
        for base, a in ((0, k_rows), (KV_VARIANTS, v_rows)):
            b = pltpu.roll(a, head_dim, axis=1)
            kvar[base + 0, row0:row0 + n_rows, :] = jnp.where(low, a, 0.0).astype(BF16)
            kvar[base + 1, row0:row0 + n_rows, :] = jnp.where(low, 0.0, a).astype(BF16)
            kvar[base + 2, row0:row0 + n_rows, :] = jnp.where(low, b, 0.0).astype(BF16)
            kvar[base + 3, row0:row0 + n_rows, :] = jnp.where(low, 0.0, b).astype(BF16)

    if sample:
        for s in range(n_chunks):
            rows = slice(s * CHUNK, (s + 1) * CHUNK)
            store_kv(s * span, ck_ref[s], cv_ref[s])
            store_kv(s * span + window, kv_ref[rows, 0:d_kv], kv_ref[rows, d_kv:2 * d_kv])
            sk_ref[s, 0:window - CHUNK, :] = ck_ref[s, CHUNK:window, :]
            sv_ref[s, 0:window - CHUNK, :] = cv_ref[s, CHUNK:window, :]
            sk_ref[s, window - CHUNK:window, :] = kv_ref[rows, 0:d_kv]
            sv_ref[s, window - CHUNK:window, :] = kv_ref[rows, d_kv:2 * d_kv]
        k_stride = span
    else:
        @pl.when(seq_start)
        def _():
            kvar[:, 0:window, :] = jnp.zeros((2 * KV_VARIANTS, window, d_kv), BF16)
        store_kv(window, kv_ref[:, 0:d_kv], kv_ref[:, d_kv:2 * d_kv])
        sk_ref[0] = kv_ref[t_rows - window:t_rows, 0:d_kv]
        sv_ref[0] = kv_ref[t_rows - window:t_rows, d_kv:2 * d_kv]
        k_stride = CHUNK

    n_pairs = q_per_kv // 2
    key_pad_rows = jnp.zeros((KEY_PAD - span, d_kv), BF16)
    for c in range(n_chunks):
        q_rows = slice(c * CHUNK, (c + 1) * CHUNK)
        k_rows = slice(c * k_stride, c * k_stride + span)
        masked = (not sample) and c * CHUNK < window
        if masked:
            first_key = (pl.program_id(1) * n_chunks + c) * CHUNK - window
            valid = lax.broadcasted_iota(I32, (CHUNK, KEY_PAD), 1) + first_key >= 0
        for n in range(n_kv):
            top, bot = (0, 3) if n == 0 else (2, 1)
            kd = jnp.concatenate([kvar[top, k_rows, :], key_pad_rows, kvar[bot, k_rows, :], key_pad_rows], axis=0)
            vd = jnp.concatenate([kvar[KV_VARIANTS + top, k_rows, :], key_pad_rows,
                                  kvar[KV_VARIANTS + bot, k_rows, :], key_pad_rows], axis=0)
            q4 = jnp.concatenate(
                [q_ref[q_rows, (n * n_pairs + j) * LANES:(n * n_pairs + j + 1) * LANES] for j in range(n_pairs)], axis=0)
            if scale_is_pow2:
                q4 = q4 * scale
            s4 = lax.dot_general(q4, kd, (((1,), (1,)), ((), ())), preferred_element_type=F32)
            if not scale_is_pow2:
                s4 = s4 * scale
            p_rows = []
            for j in range(n_pairs):
                p_halves = []
                for half in range(2):
                    blk = (slice(j * CHUNK, (j + 1) * CHUNK), slice(half * KEY_PAD, (half + 1) * KEY_PAD))
                    s = s4[blk] + bias_sc[n, blk[0], blk[1]]
                    if masked:
                        s = jnp.where(valid, s, NEG_INF)
                    sink = sinks_ref[0, n * q_per_kv + 2 * j + half]
                    mx = jnp.maximum(jnp.max(s, axis=-1, keepdims=True), sink)
                    e = jnp.exp(s - mx)
                    den = jnp.sum(e, axis=-1, keepdims=True) + jnp.exp(sink - mx)
                    p_halves.append((e * (1.0 / den)).astype(BF16))
                p_rows.append(jnp.concatenate(p_halves, axis=1))
            o4 = jnp.dot(jnp.concatenate(p_rows, axis=0), vd, preferred_element_type=F32)
            for j in range(n_pairs):
                o_sc[q_rows, (n * n_pairs + j) * LANES:(n * n_pairs + j + 1) * LANES] = (
                    o4[j * CHUNK:(j + 1) * CHUNK].astype(BF16))
    if not sample:
        kvar[:, 0:window, :] = kvar[:, t_rows:t_rows + window, :]
    y_b = jnp.dot(o_sc[...], wao_ref[...], preferred_element_type=F32)

    m = jax.nn.sigmoid(ga_ref[...].astype(F32)) * y_a + jax.nn.sigmoid(gb_ref[...].astype(F32)) * y_b
    xp = x_ref[...] + jnp.dot(m.astype(BF16), wo_ref[...], preferred_element_type=F32)
    xp_ref[...] = xp

    h2 = _rms_rows(xp, g2_ref[...])
    h2_ref[...] = _pack_halves(h2).reshape(t_rows, 1, d_model // 2)
    lt = lax.dot_general(wr_ref[...], h2.astype(BF16), (((1,), (1,)), ((), ())),
                         preferred_element_type=F32) + br_ref[...]
    lg = lt[0:n_groups]
    eg = jnp.exp(lg - jnp.max(lg, axis=0, keepdims=True))
    gp = eg / jnp.sum(eg, axis=0, keepdims=True)
    gw = jnp.max(gp, axis=0, keepdims=True)
    gi = lax.broadcasted_iota(I32, gp.shape, 0).astype(F32)
    gsel = jnp.min(jnp.where(gp == gw, gi, float(n_groups)), axis=0, keepdims=True)
    el = jnp.zeros((epg, t_rows), F32)
    for g in range(n_groups):
        el = jnp.where(gsel == float(g), lt[GROUP_ROW0 + g * epg:GROUP_ROW0 + (g + 1) * epg], el)
    ei = lax.broadcasted_iota(I32, el.shape, 0).astype(F32)
    v1 = jnp.max(el, axis=0, keepdims=True)
    i1 = jnp.min(jnp.where(el == v1, ei, float(epg)), axis=0, keepdims=True)
    el2 = jnp.where(ei == i1, -jnp.inf, el)
    v2 = jnp.max(el2, axis=0, keepdims=True)
    i2 = jnp.min(jnp.where(el2 == v2, ei, float(epg)), axis=0, keepdims=True)
    a1 = jnp.exp(v1 - v1)
    a2 = jnp.exp(v2 - v1)
    den = a1 + a2
    c1 = gw * (a1 / den)
    c2 = gw * (a2 / den)
    e1 = (gsel * float(epg) + i1).astype(I32)
    e2 = (gsel * float(epg) + i2).astype(I32)
    row8 = lax.broadcasted_iota(I32, (SUBLANES, t_rows), 0)
    re_ref[...] = jnp.where(row8 == 0, e1, jnp.where(row8 == 1, e2, 0))
    rc_ref[...] = jnp.where(row8 == 0, c1, jnp.where(row8 == 1, c2, 0.0))


def _mix(x2d, proj, kv, caches, weights, *, n_seq, sample, dims):
    n_heads, n_kv, head_dim, window, n_groups, n_exp, d_conv = dims
    n, d = x2d.shape
    t = MIX_ROWS
    n_chunks = t // CHUNK
    span = window + CHUNK
    d_attn = n_heads * head_dim
    d_kv = n_kv * head_dim
    convw, wco, wao, wo, bucket, table, sinks, g2, wr, br = weights
    if sample:
        n_t = n // t
        grid = (n_t, 1)
        tok = lambda i, j: (i, 0)
        n_state = n // CHUNK
        state_blk = n_chunks
        st = lambda i, j: (i, 0, 0)
    else:
        n_t = (n // n_seq) // t
        grid = (n_seq, n_t)
        tok = lambda i, j: (i * n_t + j, 0)
        n_state = n_seq
        state_blk = 1
        st = lambda i, j: (i, 0, 0)
    const2 = lambda i, j: (0, 0)

    def col(width, idx):
        return pl.BlockSpec((t, width), lambda i, j: (tok(i, j)[0], idx))

    def resident(shape):
        return pl.BlockSpec(shape, const2, pipeline_mode=pl.Buffered(1))

    in_specs = [
        pl.BlockSpec((t, d), tok),
        col(d, 2), col(d, 3),
        col(d_conv, 0), col(d_conv, 1), col(d_conv, 2), col(d_attn, 3),
        pl.BlockSpec((t, 2 * d_kv), tok),
    ]
    args = [x2d, proj, proj, proj, proj, proj, proj, kv]
    if sample:
        cconv, ck, cv = caches
        in_specs += [
            pl.BlockSpec((n_chunks, cconv.shape[1], d_conv), st),
            pl.BlockSpec((n_chunks, window, d_kv), st),
            pl.BlockSpec((n_chunks, window, d_kv), st),
        ]
        args += [cconv, ck, cv]
    in_specs += [
        resident(convw.shape), resident(wco.shape), resident(wao.shape), resident(wo.shape),
        resident(bucket.shape),
        pl.BlockSpec(memory_space=pltpu.SMEM), pl.BlockSpec(memory_space=pltpu.SMEM),
        resident(g2.shape), resident(wr.shape), resident(br.shape),
    ]
    args += [convw, wco, wao, wo, bucket, table, sinks, g2, wr, br]
    out_specs = [
        pl.BlockSpec((t, d), tok),
        pl.BlockSpec((t, 1, d // 2), lambda i, j: (tok(i, j)[0], 0, 0)),
        pl.BlockSpec((SUBLANES, t), lambda i, j: (0, tok(i, j)[0])),
        pl.BlockSpec((SUBLANES, t), lambda i, j: (0, tok(i, j)[0])),
        pl.BlockSpec((state_blk, 2, d_conv), st),
        pl.BlockSpec((state_blk, window, d_kv), st),
        pl.BlockSpec((state_blk, window, d_kv), st),
    ]
    out_shape = [
        jax.ShapeDtypeStruct((n, d), F32),
        jax.ShapeDtypeStruct((n, 1, d // 2), U32),
        jax.ShapeDtypeStruct((SUBLANES, n), I32),
        jax.ShapeDtypeStruct((SUBLANES, n), F32),
        jax.ShapeDtypeStruct((n_state, 2, d_conv), F32),
        jax.ShapeDtypeStruct((n_state, window, d_kv), F32),
        jax.ShapeDtypeStruct((n_state, window, d_kv), F32),
    ]
    kv_rows = n_chunks * span if sample else window + t
    scratch = [
        pltpu.VMEM((2 * KV_VARIANTS, kv_rows, d_kv), BF16),
        pltpu.VMEM((t, d_attn), BF16), pltpu.VMEM((t, d_conv), BF16),
        pltpu.VMEM((n_kv, (n_heads // n_kv // 2) * CHUNK, 2 * KEY_PAD), F32), pltpu.VMEM((SUBLANES, d_conv), F32),
    ]
    est = (2 * t * d * 4 * 3 + 2 * 2 * t * d * 2 + 2 * 4 * t * d_conv * 2
           + (2 * d_conv * d + d * d) * 2 + 12 * t * d * 4)
    kern = functools.partial(_mix_kernel, sample=sample, n_heads=n_heads, n_kv=n_kv, head_dim=head_dim,
                             window=window, n_groups=n_groups, n_exp=n_exp)
    return pl.pallas_call(
        kern, grid=grid, in_specs=in_specs, out_specs=out_specs, out_shape=out_shape,
        scratch_shapes=scratch,
        compiler_params=pltpu.CompilerParams(
            dimension_semantics=("arbitrary", "arbitrary"), vmem_limit_bytes=_vmem_limit(est)),
        name="mix_sample" if sample else "mix_prompt",
    )(*args)


def _route_kernel(re_ref, pos_ref, tmap_ref, *, n_exp, tile_rows, blk):
    n = re_ref.shape[1]
    n_blk = n // blk
    erow = lax.broadcasted_iota(I32, (n_exp, blk), 0)

    def onehots(j):
        c0 = pl.multiple_of(j * blk, blk)
        oh0 = (erow == re_ref[0:1, pl.ds(c0, blk)]).astype(F32)
        oh1 = (erow == re_ref[1:2, pl.ds(c0, blk)]).astype(F32)
        return c0, oh0, oh1

    def count_body(j, cnt):
        _, oh0, oh1 = onehots(j)
        return cnt + jnp.sum(oh0 + oh1, axis=1, keepdims=True)

    cnt = lax.fori_loop(0, n_blk, count_body, jnp.zeros((n_exp, 1), F32))
    cnt = jnp.broadcast_to(cnt, (n_exp, LANES))
    padded = jnp.ceil(cnt / tile_rows) * tile_rows
    ends = padded
    prow = lax.broadcasted_iota(I32, ends.shape, 0)
    step = 1
    while step < n_exp:
        ends = ends + jnp.where(prow >= step, pltpu.roll(ends, step, axis=0), 0.0)
        step *= 2
    offs = ends - padded
    off1 = offs[:, 0:1]

    tri = (lax.broadcasted_iota(I32, (blk, blk), 0) <= lax.broadcasted_iota(I32, (blk, blk), 1)).astype(BF16)
    row8 = lax.broadcasted_iota(I32, (SUBLANES, blk), 0)

    def pos_body(j, run):
        c0, oh0, oh1 = onehots(j)
        both = oh0 + oh1
        csum = jnp.dot(both.astype(BF16), tri, preferred_element_type=F32) + run
        slot = off1 + csum - 1.0
        p0 = jnp.sum(oh0 * slot, axis=0, keepdims=True).astype(I32)
        p1 = jnp.sum(oh1 * slot, axis=0, keepdims=True).astype(I32)
        pos_ref[:, pl.ds(c0, blk)] = jnp.where(row8 == 0, p0, jnp.where(row8 == 1, p1, 0))
        return run + jnp.sum(both, axis=1, keepdims=True)

    lax.fori_loop(0, n_blk, pos_body, jnp.zeros((n_exp, 1), F32))

    n_tiles_pad = tmap_ref.shape[1]
    start = (lax.broadcasted_iota(I32, (1, n_tiles_pad), 1) * tile_rows).astype(F32)
    end1 = ends[:, 0:1]
    te = jnp.sum((end1 <= start).astype(F32), axis=0, keepdims=True)
    trow = lax.broadcasted_iota(I32, (n_exp, n_tiles_pad), 0).astype(F32)
    used_end = jnp.sum(jnp.where(trow == te, off1 + cnt[:, 0:1], 0.0), axis=0, keepdims=True)
    n_rows = jnp.clip(used_end - start, 0.0, float(tile_rows))
    has_tokens = cnt[:, 0:1] > 0.0
    te = jnp.minimum(te, jnp.max(jnp.where(has_tokens, trow, 0.0), axis=0, keepdims=True))
    nxt = jnp.min(jnp.where(trow > te, jnp.where(has_tokens, trow, float(n_exp)), float(n_exp)), axis=0, keepdims=True)
    r8 = lax.broadcasted_iota(I32, (SUBLANES, n_tiles_pad), 0)
    tmap_ref[...] = jnp.where(r8 == 0, te.astype(I32),
                              jnp.where(r8 == 1, n_rows.astype(I32), jnp.where(r8 == 2, nxt.astype(I32), 0)))


def _route(re, n_exp, n_tiles):
    n = re.shape[1]
    n_tiles_pad = pl.cdiv(n_tiles, LANES) * LANES
    kern = functools.partial(_route_kernel, n_exp=n_exp, tile_rows=SLOT_ROWS, blk=ROUTE_BLOCK)
    return pl.pallas_call(
        kern,
        out_shape=[jax.ShapeDtypeStruct((SUBLANES, n), I32), jax.ShapeDtypeStruct((SUBLANES, n_tiles_pad), I32)],
        name="route",
    )(re)


def _dispatch_kernel(*refs, aliased):
    if aliased:
        p0_ref, p1_ref, h_ref, _, xs_ref, sem = refs
    else:
        p0_ref, p1_ref, h_ref, xs_ref, sem = refs
    rows = h_ref.shape[0]

    def start(t, carry):
        pltpu.make_async_copy(h_ref.at[t], xs_ref.at[p0_ref[t]], sem).start(priority=0)
        pltpu.make_async_copy(h_ref.at[t], xs_ref.at[p1_ref[t]], sem).start(priority=1)
        return carry

    lax.fori_loop(0, rows, start, 0)
    for _ in range(TOP_K):
        pltpu.make_async_copy(h_ref, xs_ref.at[pl.ds(0, rows)], sem).wait()


def _dispatch(h2, p0, p1, xs, n_slots):
    n, _, d = h2.shape
    t = MIX_ROWS
    aliased = xs is not None
    in_specs = [
        pl.BlockSpec((t,), lambda i: (i,), memory_space=pltpu.SMEM),
        pl.BlockSpec((t,), lambda i: (i,), memory_space=pltpu.SMEM),
        pl.BlockSpec((t, 1, d), lambda i: (i, 0, 0)),
    ]
    args = [p0, p1, h2]
    if aliased:
        in_specs.append(pl.BlockSpec(memory_space=pl.ANY))
        args.append(xs)
    return pl.pallas_call(
        functools.partial(_dispatch_kernel, aliased=aliased),
        grid=(n // t,),
        in_specs=in_specs,
        out_specs=pl.BlockSpec(memory_space=pl.ANY),
        out_shape=jax.ShapeDtypeStruct((n_slots, 1, d), h2.dtype),
        scratch_shapes=[pltpu.SemaphoreType.DMA(())],
        input_output_aliases={3: 0} if aliased else {},
        compiler_params=pltpu.CompilerParams(dimension_semantics=("arbitrary",), has_side_effects=True),
        name="dispatch",
    )(*args)


def _experts_kernel(te_ref, nr_ref, nx_ref, xs_ref, wg_hbm, wu_hbm, wd_hbm, y_ref,
                    wg_st, wu_st, wd_st, wg_sc, wu_sc, wd_sc, sems, n_changes, *, n_exp):
    i = pl.program_id(0)
    rows, _, dh = xs_ref.shape
    expert = te_ref[i]

    def copies(e, slot):
        return (pltpu.make_async_copy(wg_hbm.at[e], wg_st.at[slot], sems.at[slot]),
                pltpu.make_async_copy(wu_hbm.at[e], wu_st.at[slot], sems.at[slot]),
                pltpu.make_async_copy(wd_hbm.at[e], wd_st.at[slot], sems.at[slot]))

    @pl.when(i == 0)
    def _():
        n_changes[0] = 0
        for cp in copies(expert, 0):
            cp.start()

    @pl.when(jnp.logical_or(i == 0, expert != te_ref[jnp.maximum(i - 1, 0)]))
    def _():
        slot = lax.rem(n_changes[0], 2)
        n_changes[0] = n_changes[0] + 1
        for cp in copies(expert, slot):
            cp.wait()
        nxt = nx_ref[i]

        @pl.when(nxt < n_exp)
        def _():
            for cp in copies(nxt, 1 - slot):
                cp.start()

        wg_sc[...] = wg_st[slot].astype(BF16)
        wu_sc[...] = wu_st[slot].astype(BF16)
        wd_sc[...] = wd_st[slot].astype(BF16)

    n_rows = nr_ref[i]

    @pl.when(n_rows > 0)
    def _():
        live = lax.broadcasted_iota(I32, (rows, dh), 0) < n_rows
        words = jnp.where(live, xs_ref[...].reshape(rows, dh), jnp.uint32(0))
        x = _unpack_halves(words).astype(BF16)
        gate = jnp.dot(x, wg_sc[...], preferred_element_type=F32)
        up = jnp.dot(x, wu_sc[...], preferred_element_type=F32)
        hid = (jax.nn.silu(gate) * up).astype(BF16)
        y = jnp.dot(hid, wd_sc[...], preferred_element_type=F32)
        y_ref[...] = _pack_halves(y).reshape(rows, 1, dh)

    @pl.when(n_rows <= 0)
    def _():
        y_ref[...] = jnp.zeros(y_ref.shape, U32)


def _experts(xs, te, nr, nx, w_gate, w_up, w_down):
    n_slots, _, dh = xs.shape
    n_exp, d, d_e = w_gate.shape
    t = SLOT_ROWS
    est = 2 * 2 * t * d * 2 + 2 * 3 * d * d_e * 4 + 3 * d * d_e * 2 + 4 * t * d * 4
    row_spec = pl.BlockSpec((t, 1, dh), lambda i, te, nr, nx: (i, 0, 0))
    return pl.pallas_call(
        functools.partial(_experts_kernel, n_exp=n_exp),
        grid_spec=pltpu.PrefetchScalarGridSpec(
            num_scalar_prefetch=3,
            grid=(n_slots // t,),
            in_specs=[row_spec, pl.BlockSpec(memory_space=pl.ANY), pl.BlockSpec(memory_space=pl.ANY),
                      pl.BlockSpec(memory_space=pl.ANY)],
            out_specs=row_spec,
            scratch_shapes=[
                pltpu.VMEM((2, d, d_e), F32), pltpu.VMEM((2, d, d_e), F32), pltpu.VMEM((2, d_e, d), F32),
                pltpu.VMEM((d, d_e), BF16), pltpu.VMEM((d, d_e), BF16), pltpu.VMEM((d_e, d), BF16),
                pltpu.SemaphoreType.DMA((2,)), pltpu.SMEM((1,), I32)],
        ),
        out_shape=jax.ShapeDtypeStruct((n_slots, 1, dh), U32),
        compiler_params=pltpu.CompilerParams(
            dimension_semantics=("arbitrary",), vmem_limit_bytes=_vmem_limit(est)),
        name="experts",
    )(te, nr, nx, xs, w_gate, w_up, w_down)


def _combine_kernel(p0_ref, p1_ref, p0n_ref, p1n_ref, xp_ref, rc_ref, g_ref, y_hbm, out_ref, y0_buf, y1_buf, sems,
                    *, n_tiles):
    i = pl.program_id(0)
    rows, d = xp_ref.shape

    def gather(pa_ref, pb_ref, slot):
        base = slot * rows
        def body(t, carry):
            pltpu.make_async_copy(y_hbm.at[pa_ref[t]], y0_buf.at[base + t], sems.at[slot]).start(priority=0)
            pltpu.make_async_copy(y_hbm.at[pb_ref[t]], y1_buf.at[base + t], sems.at[slot]).start(priority=1)
            return carry
        lax.fori_loop(0, rows, body, 0)

    slot = lax.rem(i, 2)

    @pl.when(i == 0)
    def _():
        gather(p0_ref, p1_ref, 0)

    @pl.when(i + 1 < n_tiles)
    def _():
        gather(p0n_ref, p1n_ref, 1 - slot)

    cur = pl.ds(pl.multiple_of(slot * rows, rows), rows)
    for buf in (y0_buf, y1_buf):
        pltpu.make_async_copy(y_hbm.at[pl.ds(0, rows)], buf.at[cur], sems.at[slot]).wait()
    ct = rc_ref[...].T
    y0 = _unpack_halves(y0_buf[cur].reshape(rows, d // 2))
    y1 = _unpack_halves(y1_buf[cur].reshape(rows, d // 2))
    moe = ct[:, 0:1] * y0 + ct[:, 1:2] * y1
    out_ref[...] = _rms_rows(xp_ref[...] + moe, g_ref[...])


def _combine(xp, rc, p0, p1, y, g):
    n, d = xp.shape
    t = MIX_ROWS
    n_t = n // t
    est = 2 * 2 * t * d * 4 + 2 * 2 * t * d * 2 + 8 * t * d * 4
    nxt = lambda i: (jnp.minimum(i + 1, n_t - 1),)
    return pl.pallas_call(
        functools.partial(_combine_kernel, n_tiles=n_t),
        grid=(n_t,),
        in_specs=[
            pl.BlockSpec((t,), lambda i: (i,), memory_space=pltpu.SMEM),
            pl.BlockSpec((t,), lambda i: (i,), memory_space=pltpu.SMEM),
            pl.BlockSpec((t,), nxt, memory_space=pltpu.SMEM),
            pl.BlockSpec((t,), nxt, memory_space=pltpu.SMEM),
            pl.BlockSpec((t, d), lambda i: (i, 0)),
            pl.BlockSpec((SUBLANES, t), lambda i: (0, i)),
            pl.BlockSpec((1, d), lambda i: (0, 0)),
            pl.BlockSpec(memory_space=pl.ANY),
        ],
        out_specs=pl.BlockSpec((t, d), lambda i: (i, 0)),
        out_shape=jax.ShapeDtypeStruct((n, d), F32),
        scratch_shapes=[pltpu.VMEM((2 * t, 1, d // 2), U32), pltpu.VMEM((2 * t, 1, d // 2), U32),
                        pltpu.SemaphoreType.DMA((2,))],
        compiler_params=pltpu.CompilerParams(
            dimension_semantics=("arbitrary",), vmem_limit_bytes=_vmem_limit(est)),
        name="combine",
    )(p0, p1, p0, p1, xp, rc, g, y)


def _rel_buckets(window):
    rel = (jnp.arange(window + CHUNK, dtype=I32) - window)[None, :] - jnp.arange(CHUNK, dtype=I32)[:, None]
    nb = N_BUCKETS // 2
    ret = (rel > 0).astype(I32) * nb
    n = jnp.abs(rel)
    max_exact = nb // 2
    nf = jnp.maximum(n, 1).astype(F32)
    large = max_exact + (jnp.log(nf / max_exact) / math.log(MAX_DISTANCE / max_exact)
                         * (nb - max_exact)).astype(I32)
    large = jnp.minimum(large, nb - 1)
    bucket = ret + jnp.where(n < max_exact, n, large)
    return jnp.pad(bucket, ((0, 0), (0, KEY_PAD - bucket.shape[1])), constant_values=-1)


def kernel(x_prompt, x_sample, cache_conv, cache_k, cache_v, rel_bias_table, norm_mix_g, w_in, conv_w, w_conv_out, attn_sinks, w_attn_out, w_o, norm_ffn_g, w_group, b_group, w_expert_router, b_expert_router, w_gate, w_up, w_down, final_norm_g):
    assert w_in.shape[0] == 1, "single-layer step"
    batch, seq, d = x_prompt.shape
    dec_batch, dec_seq, _ = x_sample.shape
    assert dec_seq == CHUNK and seq % MIX_ROWS == 0 and (dec_batch * dec_seq) % MIX_ROWS == 0
    d_conv = conv_w.shape[-1]
    window, n_kv, head_dim = cache_k.shape[2], cache_k.shape[3], cache_k.shape[4]
    n_heads = attn_sinks.shape[-1]
    d_attn, d_kv = n_heads * head_dim, n_kv * head_dim
    n_groups, n_exp = w_group.shape[-1], w_expert_router.shape[-1]
    epg = n_exp // n_groups
    assert n_groups <= GROUP_ROW0 and d_conv == d_attn and 2 * d_conv == d
    assert n_kv == 2 and d_kv == LANES and (n_heads // n_kv) % 2 == 0 and window + CHUNK <= KEY_PAD
    dims = (n_heads, n_kv, head_dim, window, n_groups, n_exp, d_conv)

    w = w_in[0]
    kv0 = 3 * d_conv + d_attn
    w_main = jnp.concatenate([w[:, :kv0], w[:, kv0 + 2 * d_kv:]], axis=1).astype(BF16)
    w_kv = w[:, kv0:kv0 + 2 * d_kv].astype(BF16)
    g1 = norm_mix_g[0][None, :]
    wr = jnp.zeros((GROUP_ROW0 + n_exp, d), F32)
    wr = wr.at[:n_groups].set(w_group[0].T).at[GROUP_ROW0:].set(w_expert_router[0].T).astype(BF16)
    br = jnp.zeros((GROUP_ROW0 + n_exp, 1), F32)
    br = br.at[:n_groups, 0].set(b_group[0]).at[GROUP_ROW0:, 0].set(b_expert_router[0])
    weights = (conv_w[0], w_conv_out[0].astype(BF16), w_attn_out[0].astype(BF16), w_o[0].astype(BF16),
               _rel_buckets(window), rel_bias_table, attn_sinks, norm_ffn_g[0][None, :], wr, br)

    xp2d = x_prompt.reshape(batch * seq, d)
    xs2d = x_sample.reshape(dec_batch * dec_seq, d)
    n_p, n_s = xp2d.shape[0], xs2d.shape[0]
    n_tok = n_p + n_s

    proj_p, kv_p = _inproj(xp2d, g1, w_main, w_kv)
    proj_s, kv_s = _inproj(xs2d, g1, w_main, w_kv)
    xres_p, h2_p, re_p, rc_p, conv_p, k_p, v_p = _mix(
        xp2d, proj_p, kv_p, None, weights, n_seq=batch, sample=False, dims=dims)
    caches = (cache_conv[0], cache_k[0].reshape(dec_batch, window, d_kv), cache_v[0].reshape(dec_batch, window, d_kv))
    xres_s, h2_s, re_s, rc_s, conv_s, k_s, v_s = _mix(
        xs2d, proj_s, kv_s, caches, weights, n_seq=dec_batch, sample=True, dims=dims)

    n_tiles = (TOP_K * n_tok) // SLOT_ROWS + n_exp
    n_slots = n_tiles * SLOT_ROWS
    pos, tmap = _route(jnp.concatenate([re_p, re_s], axis=1), n_exp, n_tiles)
    p0, p1 = pos[0], pos[1]
    xs = _dispatch(h2_p, p0[:n_p], p1[:n_p], None, n_slots)
    xs = _dispatch(h2_s, p0[n_p:], p1[n_p:], xs, n_slots)
    y = _experts(xs, tmap[0, :n_tiles], tmap[1, :n_tiles], tmap[2, :n_tiles], w_gate[0], w_up[0], w_down[0])
    gf = final_norm_g[None, :]
    y_prompt = _combine(xres_p, rc_p, p0[:n_p], p1[:n_p], y, gf).reshape(batch, seq, d)
    y_sample = _combine(xres_s, rc_s, p0[n_p:], p1[n_p:], y, gf).reshape(dec_batch, dec_seq, d)

    kv_shape = (1, -1, window, n_kv, head_dim)
    return (y_prompt, y_sample, conv_p[None], k_p.reshape(kv_shape), v_p.reshape(kv_shape),
            conv_s[None], k_s.reshape(kv_shape), v_s.reshape(kv_shape))
```

```python
import functools
import math

import jax
import jax.numpy as jnp
from jax import lax
from jax.experimental import pallas as pl
from jax.experimental.pallas import tpu as pltpu

F32, BF16, I32, U32 = jnp.float32, jnp.bfloat16, jnp.int32, jnp.uint32

CHUNK = 64
N_BUCKETS = 32
MAX_DISTANCE = 128
EPS = 1e-6
NEG_INF = -1e30
TOP_K = 2

V7X_VMEM_BYTES = 64 * 1024 * 1024
SUBLANES = 8
LANES = 128

INPROJ_ROWS = 1024
INPROJ_COLS = 1024
MIX_ROWS = 256
SLOT_ROWS = 256
ROUTE_BLOCK = 256
GROUP_ROW0 = 8
KEY_PAD = 256
KV_VARIANTS = 4


def _vmem_limit(nbytes):
    return int(min(V7X_VMEM_BYTES - (4 << 20), max(nbytes, 32 << 20)))


def _pack_halves(x):
    half = x.shape[1] // 2
    return pltpu.pack_elementwise([x[:, :half], x[:, half:]], packed_dtype=BF16)


def _unpack_halves(w):
    lo = pltpu.unpack_elementwise(w, index=0, packed_dtype=BF16, unpacked_dtype=F32)
    hi = pltpu.unpack_elementwise(w, index=1, packed_dtype=BF16, unpacked_dtype=F32)
    return jnp.concatenate([lo, hi], axis=1)


def _rms_rows(x, g):
    r = lax.rsqrt(jnp.mean(x * x, axis=-1, keepdims=True) + EPS)
    return (x * r) * g


def _inproj_kernel(x_ref, g_ref, wm_ref, wkv_ref, proj_ref, kv_ref, h_sc):
    @pl.when(pl.program_id(1) == 0)
    def _():
        rows = 128
        def body(i, carry):
            r0 = pl.multiple_of(i * rows, rows)
            h_sc[pl.ds(r0, rows), :] = _rms_rows(x_ref[pl.ds(r0, rows), :], g_ref[...]).astype(BF16)
            return carry
        lax.fori_loop(0, x_ref.shape[0] // rows, body, 0)
        kv_ref[...] = jnp.dot(h_sc[...], wkv_ref[...], preferred_element_type=F32)

    proj_ref[...] = jnp.dot(h_sc[...], wm_ref[...], preferred_element_type=F32).astype(BF16)


def _inproj(x2d, g, w_all, kv0, n_kv):
    n, d = x2d.shape
    tm = min(INPROJ_ROWS, n)
    tn = INPROJ_COLS
    n_main = w_all.shape[1] - n_kv
    assert kv0 % tn == 0 and n_main % tn == 0 and n_kv % LANES == 0
    est = 2 * tm * d * 4 + tm * d * 2 + 2 * d * tn * 2 + 2 * d * n_kv * 2 + 2 * tm * tn * 2 + 2 * tm * n_kv * 4 + tm * tn * 4
    return pl.pallas_call(
        _inproj_kernel,
        grid=(n // tm, n_main // tn),
        in_specs=[
            pl.BlockSpec((tm, d), lambda i, j: (i, 0)),
            pl.BlockSpec((1, d), lambda i, j: (0, 0)),
            pl.BlockSpec((pl.Element(d), pl.Element(tn)),
                         lambda i, j: (0, pl.multiple_of(jnp.where(j * tn < kv0, j * tn, j * tn + n_kv), LANES))),
            pl.BlockSpec((pl.Element(d), pl.Element(n_kv)), lambda i, j: (0, kv0)),
        ],
        out_specs=[
            pl.BlockSpec((tm, tn), lambda i, j: (i, j)),
            pl.BlockSpec((tm, n_kv), lambda i, j: (i, 0)),
        ],
        out_shape=[jax.ShapeDtypeStruct((n, n_main), BF16), jax.ShapeDtypeStruct((n, n_kv), F32)],
        scratch_shapes=[pltpu.VMEM((tm, d), BF16)],
        compiler_params=pltpu.CompilerParams(
            dimension_semantics=("arbitrary", "arbitrary"), vmem_limit_bytes=_vmem_limit(est + (8 << 20))),
        name="inproj",
    )(x2d, g, w_all, w_all)


def _conv_rows(u, prev2, prev1, w):
    row = lax.broadcasted_iota(I32, u.shape, 0)
    u1 = jnp.where(row == 0, prev1, pltpu.roll(u, 1, axis=0))
    u2 = jnp.where(row == 0, prev2, jnp.where(row == 1, prev1, pltpu.roll(u, 2, axis=0)))
    return (w[0:1] * u2 + w[1:2] * u1) + w[2:3] * u


def _mix_kernel(*refs, sample, n_heads, n_kv, head_dim, window, n_groups, n_exp):
    if sample:
        (x_ref, ga_ref, gb_ref, b_ref, c_ref, xc_ref, q_ref, kv_ref, cconv_ref, ck_ref, cv_ref,
         convw_ref, wco_ref, wao_ref, wo_ref, bucket_ref, table_ref, sinks_ref, g2_ref, wr_ref, br_ref,
         xp_ref, h2_ref, re_ref, rc_ref, sconv_ref, sk_ref, sv_ref,
         kvar, o_sc, ya_sc, bias_sc, carry_u) = refs
    else:
        (x_ref, ga_ref, gb_ref, b_ref, c_ref, xc_ref, q_ref, kv_ref,
         convw_ref, wco_ref, wao_ref, wo_ref, bucket_ref, table_ref, sinks_ref, g2_ref, wr_ref, br_ref,
         xp_ref, h2_ref, re_ref, rc_ref, sconv_ref, sk_ref, sv_ref,
         kvar, o_sc, ya_sc, bias_sc, carry_u) = refs
    t_rows, d_model = x_ref.shape
    n_chunks = t_rows // CHUNK
    span = window + CHUNK
    q_per_kv = n_heads // n_kv
    d_kv = n_kv * head_dim
    scale = 1.0 / math.sqrt(head_dim)
    scale_is_pow2 = math.frexp(scale)[0] == 0.5
    epg = n_exp // n_groups
    first_step = jnp.logical_and(pl.program_id(0) == 0, pl.program_id(1) == 0)
    seq_start = pl.program_id(1) == 0

    @pl.when(first_step)
    def _():
        bk = bucket_ref[...]
        for h in range(n_heads):
            acc = jnp.where(bk < 0, NEG_INF, 0.0)
            for b in range(N_BUCKETS):
                acc = jnp.where(bk == b, table_ref[b, h], acc)
            n, g = divmod(h, q_per_kv)
            pair, half = divmod(g, 2)
            bias_sc[n, pair * CHUNK:(pair + 1) * CHUNK, half * KEY_PAD:(half + 1) * KEY_PAD] = acc

    w_conv = convw_ref[...]
    if sample:
        for s in range(n_chunks):
            rows = slice(s * CHUNK, (s + 1) * CHUNK)
            u = c_ref[rows, :].astype(F32) * xc_ref[rows, :].astype(F32)
            y = _conv_rows(u, cconv_ref[s, 0:1, :], cconv_ref[s, 1:2, :], w_conv)
            ya_sc[rows, :] = (b_ref[rows, :].astype(F32) * y).astype(BF16)
            sconv_ref[s] = u[CHUNK - 2:CHUNK, :]
    else:
        u = c_ref[...].astype(F32) * xc_ref[...].astype(F32)
        prev = jnp.where(seq_start, 0.0, carry_u[...])
        y = _conv_rows(u, prev[SUBLANES - 2:SUBLANES - 1], prev[SUBLANES - 1:SUBLANES], w_conv)
        ya_sc[...] = (b_ref[...].astype(F32) * y).astype(BF16)
        carry_u[...] = u[t_rows - SUBLANES:t_rows, :]
        sconv_ref[0] = u[t_rows - 2:t_rows, :]

    def store_kv(row0, k_rows, v_rows):
        n_rows = k_rows.shape[0]
        low = lax.broadcasted_iota(I32, k_rows.shape, 1) < head_dim
        for base, a in ((0, k_rows), (KV_VARIANTS, v_rows)):
            b = pltpu.roll(a, head_dim, axis=1)
            kvar[base + 0, row0:row0 + n_rows, :] = jnp.where(low, a, 0.0).astype(BF16)
            kvar[base + 1, row0:row0 + n_rows, :] = jnp.where(low, 0.0, a).astype(BF16)
            kvar[base + 2, row0:row0 + n_rows, :] = jnp.where(low, b, 0.0).astype(BF16)
            kvar[base + 3, row0:row0 + n_rows, :] = jnp.where(low, 0.0, b).astype(BF16)

    if sample:
        for s in range(n_chunks):
            rows = slice(s * CHUNK, (s + 1) * CHUNK)
            store_kv(s * span, ck_ref[s], cv_ref[s])
            store_kv(s * span + window, kv_ref[rows, 0:d_kv], kv_ref[rows, d_kv:2 * d_kv])
            sk_ref[s, 0:window - CHUNK, :] = ck_ref[s, CHUNK:window, :]
            sv_ref[s, 0:window - CHUNK, :] = cv_ref[s, CHUNK:window, :]
            sk_ref[s, window - CHUNK:window, :] = kv_ref[rows, 0:d_kv]
            sv_ref[s, window - CHUNK:window, :] = kv_ref[rows, d_kv:2 * d_kv]
        k_stride = span
    else:
        @pl.when(seq_start)
        def _():
            kvar[:, 0:window, :] = jnp.zeros((2 * KV_VARIANTS, window, d_kv), BF16)
        store_kv(window, kv_ref[:, 0:d_kv], kv_ref[:, d_kv:2 * d_kv])
        sk_ref[0] = kv_ref[t_rows - window:t_rows, 0:d_kv]
        sv_ref[0] = kv_ref[t_rows - window:t_rows, d_kv:2 * d_kv]
        k_stride = CHUNK

    n_pairs = q_per_kv // 2
    key_pad_rows = jnp.zeros((KEY_PAD - span, d_kv), BF16)
    ones_d = ((lax.broadcasted_iota(I32, (2 * KEY_PAD, LANES), 0) < KEY_PAD)
              == (lax.broadcasted_iota(I32, (2 * KEY_PAD, LANES), 1) < head_dim)).astype(BF16)
    low_half = lax.broadcasted_iota(I32, (CHUNK, LANES), 1) < head_dim
    ya_parts = []
    ya_cols = d_model // (n_chunks * n_kv)
    for c in range(n_chunks):
        q_rows = slice(c * CHUNK, (c + 1) * CHUNK)
        k_rows = slice(c * k_stride, c * k_stride + span)
        masked = (not sample) and c * CHUNK < window
        if masked:
            first_key = (pl.program_id(1) * n_chunks + c) * CHUNK - window
            valid = lax.broadcasted_iota(I32, (CHUNK, KEY_PAD), 1) + first_key >= 0
        for n in range(n_kv):
            top, bot = (0, 3) if n == 0 else (2, 1)
            kd = jnp.concatenate([kvar[top, k_rows, :], key_pad_rows, kvar[bot, k_rows, :], key_pad_rows], axis=0)
            vd = jnp.concatenate([kvar[KV_VARIANTS + top, k_rows, :], key_pad_rows,
                                  kvar[KV_VARIANTS + bot, k_rows, :], key_pad_rows], axis=0)
            q4 = jnp.concatenate(
                [q_ref[q_rows, (n * n_pairs + j) * LANES:(n * n_pairs + j + 1) * LANES] for j in range(n_pairs)], axis=0)
            if scale_is_pow2:
                q4 = q4 * scale
            s4 = lax.dot_general(q4, kd, (((1,), (1,)), ((), ())), preferred_element_type=F32)
            if not scale_is_pow2:
                s4 = s4 * scale
            e_rows, sink_rows = [], []
            for j in range(n_pairs):
                e_halves, sink_halves = [], []
                for half in range(2):
                    blk = (slice(j * CHUNK, (j + 1) * CHUNK), slice(half * KEY_PAD, (half + 1) * KEY_PAD))
                    s = s4[blk] + bias_sc[n, blk[0], blk[1]]
                    if masked:
                        s = jnp.where(valid, s, NEG_INF)
                    sink = sinks_ref[0, n * q_per_kv + 2 * j + half]
                    mx = jnp.maximum(jnp.max(s, axis=-1, keepdims=True), sink)
                    e_halves.append(jnp.exp(s - mx).astype(BF16))
                    sink_halves.append(jnp.exp(sink - mx))
                e_rows.append(jnp.concatenate(e_halves, axis=1))
                sink_rows.append(jnp.where(low_half, sink_halves[0], sink_halves[1]))
            od = jnp.dot(jnp.concatenate(e_rows, axis=0), jnp.concatenate([vd, ones_d], axis=1),
                         preferred_element_type=F32)
            o4 = od[:, 0:LANES] / (od[:, LANES:2 * LANES] + jnp.concatenate(sink_rows, axis=0))
            for j in range(n_pairs):
                o_sc[q_rows, (n * n_pairs + j) * LANES:(n * n_pairs + j + 1) * LANES] = (
                    o4[j * CHUNK:(j + 1) * CHUNK].astype(BF16))
            ya_parts.append(jnp.dot(ya_sc[...], wco_ref[:, len(ya_parts) * ya_cols:(len(ya_parts) + 1) * ya_cols],
                                    preferred_element_type=F32))
    if not sample:
        kvar[:, 0:window, :] = kvar[:, t_rows:t_rows + window, :]
    y_a = jnp.concatenate(ya_parts, axis=1)
    y_b = jnp.dot(o_sc[...], wao_ref[...], preferred_element_type=F32)

    m = jax.nn.sigmoid(ga_ref[...].astype(F32)) * y_a + jax.nn.sigmoid(gb_ref[...].astype(F32)) * y_b
    xp = x_ref[...] + jnp.dot(m.astype(BF16), wo_ref[...], preferred_element_type=F32)
    xp_ref[...] = xp

    h2 = _rms_rows(xp, g2_ref[...])
    h2_ref[...] = _pack_halves(h2).reshape(t_rows, 1, d_model // 2)
    lt = lax.dot_general(wr_ref[...], h2.astype(BF16), (((1,), (1,)), ((), ())),
                         preferred_element_type=F32) + br_ref[...]
    lg = lt[0:n_groups]
    eg = jnp.exp(lg - jnp.max(lg, axis=0, keepdims=True))
    gp = eg / jnp.sum(eg, axis=0, keepdims=True)
    gw = jnp.max(gp, axis=0, keepdims=True)
    gi = lax.broadcasted_iota(I32, gp.shape, 0).astype(F32)
    gsel = jnp.min(jnp.where(gp == gw, gi, float(n_groups)), axis=0, keepdims=True)
    el = jnp.zeros((epg, t_rows), F32)
    for g in range(n_groups):
        el = jnp.where(gsel == float(g), lt[GROUP_ROW0 + g * epg:GROUP_ROW0 + (g + 1) * epg], el)
    ei = lax.broadcasted_iota(I32, el.shape, 0).astype(F32)
    v1 = jnp.max(el, axis=0, keepdims=True)
    i1 = jnp.min(jnp.where(el == v1, ei, float(epg)), axis=0, keepdims=True)
    el2 = jnp.where(ei == i1, -jnp.inf, el)
    v2 = jnp.max(el2, axis=0, keepdims=True)
    i2 = jnp.min(jnp.where(el2 == v2, ei, float(epg)), axis=0, keepdims=True)
    a1 = jnp.exp(v1 - v1)
    a2 = jnp.exp(v2 - v1)
    den = a1 + a2
    c1 = gw * (a1 / den)
    c2 = gw * (a2 / den)
    e1 = (gsel * float(epg) + i1).astype(I32)
    e2 = (gsel * float(epg) + i2).astype(I32)
    row8 = lax.broadcasted_iota(I32, (SUBLANES, t_rows), 0)
    re_ref[...] = jnp.where(row8 == 0, e1, jnp.where(row8 == 1, e2, 0))
    rc_ref[...] = jnp.where(row8 == 0, c1, jnp.where(row8 == 1, c2, 0.0))


def _mix(x2d, proj, kv, caches, weights, *, n_seq, sample, dims):
    n_heads, n_kv, head_dim, window, n_groups, n_exp, d_conv = dims
    n, d = x2d.shape
    t = MIX_ROWS
    n_chunks = t // CHUNK
    span = window + CHUNK
    d_attn = n_heads * head_dim
    d_kv = n_kv * head_dim
    convw, wco, wao, wo, bucket, table, sinks, g2, wr, br = weights
    if sample:
        n_t = n // t
        grid = (n_t, 1)
        tok = lambda i, j: (i, 0)
        n_state = n // CHUNK
        state_blk = n_chunks
        st = lambda i, j: (i, 0, 0)
    else:
        n_t = (n // n_seq) // t
        grid = (n_seq, n_t)
        tok = lambda i, j: (i * n_t + j, 0)
        n_state = n_seq
        state_blk = 1
        st = lambda i, j: (i, 0, 0)
    const2 = lambda i, j: (0, 0)

    def col(width, idx):
        return pl.BlockSpec((t, width), lambda i, j: (tok(i, j)[0], idx))

    def resident(shape):
        return pl.BlockSpec(shape, const2, pipeline_mode=pl.Buffered(1))

    in_specs = [
        pl.BlockSpec((t, d), tok),
        col(d, 2), col(d, 3),
        col(d_conv, 0), col(d_conv, 1), col(d_conv, 2), col(d_attn, 3),
        pl.BlockSpec((t, 2 * d_kv), tok),
    ]
    args = [x2d, proj, proj, proj, proj, proj, proj, kv]
    if sample:
        cconv, ck, cv = caches
        in_specs += [
            pl.BlockSpec((n_chunks, cconv.shape[1], d_conv), st),
            pl.BlockSpec((n_chunks, window, d_kv), st),
            pl.BlockSpec((n_chunks, window, d_kv), st),
        ]
        args += [cconv, ck, cv]
    in_specs += [
        resident(convw.shape), resident(wco.shape), resident(wao.shape), resident(wo.shape),
        resident(bucket.shape),
        pl.BlockSpec(memory_space=pltpu.SMEM), pl.BlockSpec(memory_space=pltpu.SMEM),
        resident(g2.shape), resident(wr.shape), resident(br.shape),
    ]
    args += [convw, wco, wao, wo, bucket, table, sinks, g2, wr, br]
    out_specs = [
        pl.BlockSpec((t, d), tok),
        pl.BlockSpec((t, 1, d // 2), lambda i, j: (tok(i, j)[0], 0, 0)),
        pl.BlockSpec((SUBLANES, t), lambda i, j: (0, tok(i, j)[0])),
        pl.BlockSpec((SUBLANES, t), lambda i, j: (0, tok(i, j)[0])),
        pl.BlockSpec((state_blk, 2, d_conv), st),
        pl.BlockSpec((state_blk, window, d_kv), st),
        pl.BlockSpec((state_blk, window, d_kv), st),
    ]
    out_shape = [
        jax.ShapeDtypeStruct((n, d), F32),
        jax.ShapeDtypeStruct((n, 1, d // 2), U32),
        jax.ShapeDtypeStruct((SUBLANES, n), I32),
        jax.ShapeDtypeStruct((SUBLANES, n), F32),
        jax.ShapeDtypeStruct((n_state, 2, d_conv), F32),
        jax.ShapeDtypeStruct((n_state, window, d_kv), F32),
        jax.ShapeDtypeStruct((n_state, window, d_kv), F32),
    ]
    kv_rows = n_chunks * span if sample else window + t
    scratch = [
        pltpu.VMEM((2 * KV_VARIANTS, kv_rows, d_kv), BF16),
        pltpu.VMEM((t, d_attn), BF16), pltpu.VMEM((t, d_conv), BF16),
        pltpu.VMEM((n_kv, (n_heads // n_kv // 2) * CHUNK, 2 * KEY_PAD), F32), pltpu.VMEM((SUBLANES, d_conv), F32),
    ]
    est = (2 * t * d * 4 * 3 + 2 * 2 * t * d * 2 + 2 * 4 * t * d_conv * 2
           + (2 * d_conv * d + d * d) * 2 + 12 * t * d * 4)
    kern = functools.partial(_mix_kernel, sample=sample, n_heads=n_heads, n_kv=n_kv, head_dim=head_dim,
                             window=window, n_groups=n_groups, n_exp=n_exp)
    return pl.pallas_call(
        kern, grid=grid, in_specs=in_specs, out_specs=out_specs, out_shape=out_shape,
        scratch_shapes=scratch,
        compiler_params=pltpu.CompilerParams(
            dimension_semantics=("arbitrary", "arbitrary"), vmem_limit_bytes=_vmem_limit(est)),
        name="mix_sample" if sample else "mix_prompt",
    )(*args)


def _route_kernel(re_ref, pos_ref, tmap_ref, *, n_exp, tile_rows, blk):
    n = re_ref.shape[1]
    n_blk = n // blk
    erow = lax.broadcasted_iota(I32, (n_exp, blk), 0)

    def onehots(j):
        c0 = pl.multiple_of(j * blk, blk)
        oh0 = (erow == re_ref[0:1, pl.ds(c0, blk)]).astype(F32)
        oh1 = (erow == re_ref[1:2, pl.ds(c0, blk)]).astype(F32)
        return c0, oh0, oh1

    def count_body(j, cnt):
        _, oh0, oh1 = onehots(j)
        return cnt + jnp.sum(oh0 + oh1, axis=1, keepdims=True)

    cnt = lax.fori_loop(0, n_blk, count_body, jnp.zeros((n_exp, 1), F32))
    cnt = jnp.broadcast_to(cnt, (n_exp, LANES))
    padded = jnp.ceil(cnt / tile_rows) * tile_rows
    ends = padded
    prow = lax.broadcasted_iota(I32, ends.shape, 0)
    step = 1
    while step < n_exp:
        ends = ends + jnp.where(prow >= step, pltpu.roll(ends, step, axis=0), 0.0)
        step *= 2
    offs = ends - padded
    off1 = offs[:, 0:1]

    tri = (lax.broadcasted_iota(I32, (blk, blk), 0) <= lax.broadcasted_iota(I32, (blk, blk), 1)).astype(BF16)
    row8 = lax.broadcasted_iota(I32, (SUBLANES, blk), 0)

    def pos_body(j, run):
        c0, oh0, oh1 = onehots(j)
        both = oh0 + oh1
        csum = jnp.dot(both.astype(BF16), tri, preferred_element_type=F32) + run
        slot = off1 + csum - 1.0
        p0 = jnp.sum(oh0 * slot, axis=0, keepdims=True).astype(I32)
        p1 = jnp.sum(oh1 * slot, axis=0, keepdims=True).astype(I32)
        pos_ref[:, pl.ds(c0, blk)] = jnp.where(row8 == 0, p0, jnp.where(row8 == 1, p1, 0))
        return run + jnp.sum(both, axis=1, keepdims=True)

    lax.fori_loop(0, n_blk, pos_body, jnp.zeros((n_exp, 1), F32))

    n_tiles_pad = tmap_ref.shape[1]
    start = (lax.broadcasted_iota(I32, (1, n_tiles_pad), 1) * tile_rows).astype(F32)
    end1 = ends[:, 0:1]
    te = jnp.sum((end1 <= start).astype(F32), axis=0, keepdims=True)
    trow = lax.broadcasted_iota(I32, (n_exp, n_tiles_pad), 0).astype(F32)
    used_end = jnp.sum(jnp.where(trow == te, off1 + cnt[:, 0:1], 0.0), axis=0, keepdims=True)
    n_rows = jnp.clip(used_end - start, 0.0, float(tile_rows))
    has_tokens = cnt[:, 0:1] > 0.0
    te = jnp.minimum(te, jnp.max(jnp.where(has_tokens, trow, 0.0), axis=0, keepdims=True))
    nxt = jnp.min(jnp.where(trow > te, jnp.where(has_tokens, trow, float(n_exp)), float(n_exp)), axis=0, keepdims=True)
    r8 = lax.broadcasted_iota(I32, (SUBLANES, n_tiles_pad), 0)
    tmap_ref[...] = jnp.where(r8 == 0, te.astype(I32),
                              jnp.where(r8 == 1, n_rows.astype(I32), jnp.where(r8 == 2, nxt.astype(I32), 0)))


def _route(re, n_exp, n_tiles):
    n = re.shape[1]
    n_tiles_pad = pl.cdiv(n_tiles, LANES) * LANES
    kern = functools.partial(_route_kernel, n_exp=n_exp, tile_rows=SLOT_ROWS, blk=ROUTE_BLOCK)
    return pl.pallas_call(
        kern,
        out_shape=[jax.ShapeDtypeStruct((SUBLANES, n), I32), jax.ShapeDtypeStruct((SUBLANES, n_tiles_pad), I32)],
        name="route",
    )(re)


def _dispatch_kernel(*refs, aliased):
    if aliased:
        p0_ref, p1_ref, h_ref, _, xs_ref, sem = refs
    else:
        p0_ref, p1_ref, h_ref, xs_ref, sem = refs
    rows = h_ref.shape[0]

    def start(t, carry):
        pltpu.make_async_copy(h_ref.at[t], xs_ref.at[p0_ref[t]], sem).start(priority=0)
        pltpu.make_async_copy(h_ref.at[t], xs_ref.at[p1_ref[t]], sem).start(priority=1)
        return carry

    lax.fori_loop(0, rows, start, 0)
    for _ in range(TOP_K):
        pltpu.make_async_copy(h_ref, xs_ref.at[pl.ds(0, rows)], sem).wait()


def _dispatch(h2, p0, p1, xs, n_slots):
    n, _, d = h2.shape
    t = MIX_ROWS
    aliased = xs is not None
    in_specs = [
        pl.BlockSpec((t,), lambda i: (i,), memory_space=pltpu.SMEM),
        pl.BlockSpec((t,), lambda i: (i,), memory_space=pltpu.SMEM),
        pl.BlockSpec((t, 1, d), lambda i: (i, 0, 0)),
    ]
    args = [p0, p1, h2]
    if aliased:
        in_specs.append(pl.BlockSpec(memory_space=pl.ANY))
        args.append(xs)
    return pl.pallas_call(
        functools.partial(_dispatch_kernel, aliased=aliased),
        grid=(n // t,),
        in_specs=in_specs,
        out_specs=pl.BlockSpec(memory_space=pl.ANY),
        out_shape=jax.ShapeDtypeStruct((n_slots, 1, d), h2.dtype),
        scratch_shapes=[pltpu.SemaphoreType.DMA(())],
        input_output_aliases={3: 0} if aliased else {},
        compiler_params=pltpu.CompilerParams(dimension_semantics=("arbitrary",), has_side_effects=True),
        name="dispatch",
    )(*args)


def _experts_kernel(te_ref, nr_ref, nx_ref, xs_ref, wg_hbm, wu_hbm, wd_hbm, y_ref,
                    wg_st, wu_st, wd_st, wg_sc, wu_sc, wd_sc, sems, n_changes, *, n_exp):
    i = pl.program_id(0)
    rows, _, dh = xs_ref.shape
    expert = te_ref[i]

    def copies(e, slot):
        return (pltpu.make_async_copy(wg_hbm.at[e], wg_st.at[slot], sems.at[slot]),
                pltpu.make_async_copy(wu_hbm.at[e], wu_st.at[slot], sems.at[slot]),
                pltpu.make_async_copy(wd_hbm.at[e], wd_st.at[slot], sems.at[slot]))

    @pl.when(i == 0)
    def _():
        n_changes[0] = 0
        for cp in copies(expert, 0):
            cp.start()

    @pl.when(jnp.logical_or(i == 0, expert != te_ref[jnp.maximum(i - 1, 0)]))
    def _():
        slot = lax.rem(n_changes[0], 2)
        n_changes[0] = n_changes[0] + 1
        for cp in copies(expert, slot):
            cp.wait()
        nxt = nx_ref[i]

        @pl.when(nxt < n_exp)
        def _():
            for cp in copies(nxt, 1 - slot):
                cp.start()

        wg_sc[...] = wg_st[slot].astype(BF16)
        wu_sc[...] = wu_st[slot].astype(BF16)
        wd_sc[...] = wd_st[slot].astype(BF16)

    n_rows = nr_ref[i]

    @pl.when(n_rows > 0)
    def _():
        live = lax.broadcasted_iota(I32, (rows, dh), 0) < n_rows
        words = jnp.where(live, xs_ref[...].reshape(rows, dh), jnp.uint32(0))
        x = _unpack_halves(words).astype(BF16)
        gate = jnp.dot(x, wg_sc[...], preferred_element_type=F32)
        up = jnp.dot(x, wu_sc[...], preferred_element_type=F32)
        hid = (jax.nn.silu(gate) * up).astype(BF16)
        y = jnp.dot(hid, wd_sc[...], preferred_element_type=F32)
        y_ref[...] = _pack_halves(y).reshape(rows, 1, dh)

    @pl.when(n_rows <= 0)
    def _():
        y_ref[...] = jnp.zeros(y_ref.shape, U32)


def _experts(xs, te, nr, nx, w_gate, w_up, w_down):
    n_slots, _, dh = xs.shape
    n_exp, d, d_e = w_gate.shape
    t = SLOT_ROWS
    est = 2 * 2 * t * d * 2 + 2 * 3 * d * d_e * 4 + 3 * d * d_e * 2 + 4 * t * d * 4
    row_spec = pl.BlockSpec((t, 1, dh), lambda i, te, nr, nx: (i, 0, 0))
    return pl.pallas_call(
        functools.partial(_experts_kernel, n_exp=n_exp),
        grid_spec=pltpu.PrefetchScalarGridSpec(
            num_scalar_prefetch=3,
            grid=(n_slots // t,),
            in_specs=[row_spec, pl.BlockSpec(memory_space=pl.ANY), pl.BlockSpec(memory_space=pl.ANY),
                      pl.BlockSpec(memory_space=pl.ANY)],
            out_specs=row_spec,
            scratch_shapes=[
                pltpu.VMEM((2, d, d_e), F32), pltpu.VMEM((2, d, d_e), F32), pltpu.VMEM((2, d_e, d), F32),
                pltpu.VMEM((d, d_e), BF16), pltpu.VMEM((d, d_e), BF16), pltpu.VMEM((d_e, d), BF16),
                pltpu.SemaphoreType.DMA((2,)), pltpu.SMEM((1,), I32)],
        ),
        out_shape=jax.ShapeDtypeStruct((n_slots, 1, dh), U32),
        compiler_params=pltpu.CompilerParams(
            dimension_semantics=("arbitrary",), vmem_limit_bytes=_vmem_limit(est)),
        name="experts",
    )(te, nr, nx, xs, w_gate, w_up, w_down)


def _combine_kernel(p0_ref, p1_ref, p0n_ref, p1n_ref, xp_ref, rc_ref, g_ref, y_hbm, out_ref, y0_buf, y1_buf, sems,
                    *, n_tiles):
    i = pl.program_id(0)
    rows, d = xp_ref.shape

    def gather(pa_ref, pb_ref, slot):
        base = slot * rows
        def body(t, carry):
            pltpu.make_async_copy(y_hbm.at[pa_ref[t]], y0_buf.at[base + t], sems.at[slot]).start(priority=0)
            pltpu.make_async_copy(y_hbm.at[pb_ref[t]], y1_buf.at[base + t], sems.at[slot]).start(priority=1)
            return carry
        lax.fori_loop(0, rows, body, 0)

    slot = lax.rem(i, 2)

    @pl.when(i == 0)
    def _():
        gather(p0_ref, p1_ref, 0)

    @pl.when(i + 1 < n_tiles)
    def _():
        gather(p0n_ref, p1n_ref, 1 - slot)

    cur = pl.ds(pl.multiple_of(slot * rows, rows), rows)
    for buf in (y0_buf, y1_buf):
        pltpu.make_async_copy(y_hbm.at[pl.ds(0, rows)], buf.at[cur], sems.at[slot]).wait()
    ct = rc_ref[...].T
    y0 = _unpack_halves(y0_buf[cur].reshape(rows, d // 2))
    y1 = _unpack_halves(y1_buf[cur].reshape(rows, d // 2))
    moe = ct[:, 0:1] * y0 + ct[:, 1:2] * y1
    out_ref[...] = _rms_rows(xp_ref[...] + moe, g_ref[...])


def _combine(xp, rc, p0, p1, y, g):
    n, d = xp.shape
    t = MIX_ROWS
    n_t = n // t
    est = 2 * 2 * t * d * 4 + 2 * 2 * t * d * 2 + 8 * t * d * 4
    nxt = lambda i: (jnp.minimum(i + 1, n_t - 1),)
    return pl.pallas_call(
        functools.partial(_combine_kernel, n_tiles=n_t),
        grid=(n_t,),
        in_specs=[
            pl.BlockSpec((t,), lambda i: (i,), memory_space=pltpu.SMEM),
            pl.BlockSpec((t,), lambda i: (i,), memory_space=pltpu.SMEM),
            pl.BlockSpec((t,), nxt, memory_space=pltpu.SMEM),
            pl.BlockSpec((t,), nxt, memory_space=pltpu.SMEM),
            pl.BlockSpec((t, d), lambda i: (i, 0)),
            pl.BlockSpec((SUBLANES, t), lambda i: (0, i)),
            pl.BlockSpec((1, d), lambda i: (0, 0)),
            pl.BlockSpec(memory_space=pl.ANY),
        ],
        out_specs=pl.BlockSpec((t, d), lambda i: (i, 0)),
        out_shape=jax.ShapeDtypeStruct((n, d), F32),
        scratch_shapes=[pltpu.VMEM((2 * t, 1, d // 2), U32), pltpu.VMEM((2 * t, 1, d // 2), U32),
                        pltpu.SemaphoreType.DMA((2,))],
        compiler_params=pltpu.CompilerParams(
            dimension_semantics=("arbitrary",), vmem_limit_bytes=_vmem_limit(est)),
        name="combine",
    )(p0, p1, p0, p1, xp, rc, g, y)


def _rel_buckets(window):
    rel = (jnp.arange(window + CHUNK, dtype=I32) - window)[None, :] - jnp.arange(CHUNK, dtype=I32)[:, None]
    nb = N_BUCKETS // 2
    ret = (rel > 0).astype(I32) * nb
    n = jnp.abs(rel)
    max_exact = nb // 2
    nf = jnp.maximum(n, 1).astype(F32)
    large = max_exact + (jnp.log(nf / max_exact) / math.log(MAX_DISTANCE / max_exact)
                         * (nb - max_exact)).astype(I32)
    large = jnp.minimum(large, nb - 1)
    bucket = ret + jnp.where(n < max_exact, n, large)
    return jnp.pad(bucket, ((0, 0), (0, KEY_PAD - bucket.shape[1])), constant_values=-1)


def kernel(x_prompt, x_sample, cache_conv, cache_k, cache_v, rel_bias_table, norm_mix_g, w_in, conv_w, w_conv_out, attn_sinks, w_attn_out, w_o, norm_ffn_g, w_group, b_group, w_expert_router, b_expert_router, w_gate, w_up, w_down, final_norm_g):
    assert w_in.shape[0] == 1, "single-layer step"
    batch, seq, d = x_prompt.shape
    dec_batch, dec_seq, _ = x_sample.shape
    assert dec_seq == CHUNK and seq % MIX_ROWS == 0 and (dec_batch * dec_seq) % MIX_ROWS == 0
    d_conv = conv_w.shape[-1]
    window, n_kv, head_dim = cache_k.shape[2], cache_k.shape[3], cache_k.shape[4]
    n_heads = attn_sinks.shape[-1]
    d_attn, d_kv = n_heads * head_dim, n_kv * head_dim
    n_groups, n_exp = w_group.shape[-1], w_expert_router.shape[-1]
    epg = n_exp // n_groups
    assert n_groups <= GROUP_ROW0 and d_conv == d_attn and 2 * d_conv == d
    assert n_kv == 2 and d_kv == LANES and (n_heads // n_kv) % 2 == 0 and window + CHUNK <= KEY_PAD
    dims = (n_heads, n_kv, head_dim, window, n_groups, n_exp, d_conv)

    w_all = w_in[0].astype(BF16)
    kv0 = 3 * d_conv + d_attn
    g1 = norm_mix_g[0][None, :]
    wr = jnp.zeros((GROUP_ROW0 + n_exp, d), F32)
    wr = wr.at[:n_groups].set(w_group[0].T).at[GROUP_ROW0:].set(w_expert_router[0].T).astype(BF16)
    br = jnp.zeros((GROUP_ROW0 + n_exp, 1), F32)
    br = br.at[:n_groups, 0].set(b_group[0]).at[GROUP_ROW0:, 0].set(b_expert_router[0])
    weights = (conv_w[0], w_conv_out[0].astype(BF16), w_attn_out[0].astype(BF16), w_o[0].astype(BF16),
               _rel_buckets(window), rel_bias_table, attn_sinks, norm_ffn_g[0][None, :], wr, br)

    xp2d = x_prompt.reshape(batch * seq, d)
    xs2d = x_sample.reshape(dec_batch * dec_seq, d)
    n_p, n_s = xp2d.shape[0], xs2d.shape[0]
    n_tok = n_p + n_s

    proj_p, kv_p = _inproj(xp2d, g1, w_all, kv0, 2 * d_kv)
    proj_s, kv_s = _inproj(xs2d, g1, w_all, kv0, 2 * d_kv)
    xres_p, h2_p, re_p, rc_p, conv_p, k_p, v_p = _mix(
        xp2d, proj_p, kv_p, None, weights, n_seq=batch, sample=False, dims=dims)
    caches = (cache_conv[0], cache_k[0].reshape(dec_batch, window, d_kv), cache_v[0].reshape(dec_batch, window, d_kv))
    xres_s, h2_s, re_s, rc_s, conv_s, k_s, v_s = _mix(
        xs2d, proj_s, kv_s, caches, weights, n_seq=dec_batch, sample=True, dims=dims)

    n_tiles = (TOP_K * n_tok) // SLOT_ROWS + n_exp
    n_slots = n_tiles * SLOT_ROWS
    pos, tmap = _route(jnp.concatenate([re_p, re_s], axis=1), n_exp, n_tiles)
    p0, p1 = pos[0], pos[1]
    xs = _dispatch(h2_p, p0[:n_p], p1[:n_p], None, n_slots)
    xs = _dispatch(h2_s, p0[n_p:], p1[n_p:], xs, n_slots)
    y = _experts(xs, tmap[0, :n_tiles], tmap[1, :n_tiles], tmap[2, :n_tiles], w_gate[0], w_up[0], w_down[0])
    gf = final_norm_g[None, :]
    y_prompt = _combine(xres_p, rc_p, p0[:n_p], p1[:n_p], y, gf).reshape(batch, seq, d)
    y_sample = _combine(xres_s, rc_s, p0[n_p:], p1[n_p:], y, gf).reshape(dec_batch, dec_seq, d)

    kv_shape = (1, -1, window, n_kv, head_dim)
    return (y_prompt, y_sample, conv_p[None], k_p.reshape(kv_shape), v_p.reshape(kv_shape),
            conv_s[None], k_s.reshape(kv_shape), v_s.reshape(kv_shape))
```

```python
import functools
import math

import jax
import jax.numpy as jnp
from jax import lax
from jax.experimental import pallas as pl
from jax.experimental.pallas import tpu as pltpu

F32, BF16, I32, U32 = jnp.float32, jnp.bfloat16, jnp.int32, jnp.uint32

CHUNK = 64
N_BUCKETS = 32
MAX_DISTANCE = 128
EPS = 1e-6
NEG_INF = -1e30
TOP_K = 2

V7X_VMEM_BYTES = 64 * 1024 * 1024
SUBLANES = 8
LANES = 128

INPROJ_ROWS = 1024
INPROJ_COLS = 1024
MIX_ROWS = 256
SLOT_ROWS = 256
EXPERT_STEP_TILES = 2
ROUTE_BLOCK = 256
GROUP_ROW0 = 8
KEY_PAD = 256
KV_VARIANTS = 4


def _vmem_limit(nbytes):
    return int(min(V7X_VMEM_BYTES - (4 << 20), max(nbytes, 32 << 20)))


def _pack_halves(x):
    half = x.shape[1] // 2
    return pltpu.pack_elementwise([x[:, :half], x[:, half:]], packed_dtype=BF16)


def _unpack_halves(w):
    lo = pltpu.unpack_elementwise(w, index=0, packed_dtype=BF16, unpacked_dtype=F32)
    hi = pltpu.unpack_elementwise(w, index=1, packed_dtype=BF16, unpacked_dtype=F32)
    return jnp.concatenate([lo, hi], axis=1)


def _rms_rows(x, g):
    r = lax.rsqrt(jnp.mean(x * x, axis=-1, keepdims=True) + EPS)
    return (x * r) * g


def _inproj_kernel(x_ref, g_ref, wm_ref, wkv_ref, proj_ref, kv_ref, h_sc):
    @pl.when(pl.program_id(1) == 0)
    def _():
        rows = 128
        def body(i, carry):
            r0 = pl.multiple_of(i * rows, rows)
            h_sc[pl.ds(r0, rows), :] = _rms_rows(x_ref[pl.ds(r0, rows), :], g_ref[...]).astype(BF16)
            return carry
        lax.fori_loop(0, x_ref.shape[0] // rows, body, 0)
        kv_ref[...] = jnp.dot(h_sc[...], wkv_ref[...], preferred_element_type=F32)

    proj_ref[...] = jnp.dot(h_sc[...], wm_ref[...], preferred_element_type=F32).astype(BF16)


def _inproj(x2d, g, w_all, kv0, n_kv):
    n, d = x2d.shape
    tm = min(INPROJ_ROWS, n)
    tn = INPROJ_COLS
    n_main = w_all.shape[1] - n_kv
    assert kv0 % tn == 0 and n_main % tn == 0 and n_kv % LANES == 0
    est = 2 * tm * d * 4 + tm * d * 2 + 2 * d * tn * 2 + 2 * d * n_kv * 2 + 2 * tm * tn * 2 + 2 * tm * n_kv * 4 + tm * tn * 4
    return pl.pallas_call(
        _inproj_kernel,
        grid=(n // tm, n_main // tn),
        in_specs=[
            pl.BlockSpec((tm, d), lambda i, j: (i, 0)),
            pl.BlockSpec((1, d), lambda i, j: (0, 0)),
            pl.BlockSpec((pl.Element(d), pl.Element(tn)),
                         lambda i, j: (0, pl.multiple_of(jnp.where(j * tn < kv0, j * tn, j * tn + n_kv), LANES))),
            pl.BlockSpec((pl.Element(d), pl.Element(n_kv)), lambda i, j: (0, kv0)),
        ],
        out_specs=[
            pl.BlockSpec((tm, tn), lambda i, j: (i, j)),
            pl.BlockSpec((tm, n_kv), lambda i, j: (i, 0)),
        ],
        out_shape=[jax.ShapeDtypeStruct((n, n_main), BF16), jax.ShapeDtypeStruct((n, n_kv), F32)],
        scratch_shapes=[pltpu.VMEM((tm, d), BF16)],
        compiler_params=pltpu.CompilerParams(
            dimension_semantics=("arbitrary", "arbitrary"), vmem_limit_bytes=_vmem_limit(est + (8 << 20))),
        name="inproj",
    )(x2d, g, w_all, w_all)


def _conv_rows(u, prev2, prev1, w):
    row = lax.broadcasted_iota(I32, u.shape, 0)
    u1 = jnp.where(row == 0, prev1, pltpu.roll(u, 1, axis=0))
    u2 = jnp.where(row == 0, prev2, jnp.where(row == 1, prev1, pltpu.roll(u, 2, axis=0)))
    return (w[0:1] * u2 + w[1:2] * u1) + w[2:3] * u


def _mix_kernel(*refs, sample, n_heads, n_kv, head_dim, window, n_groups, n_exp):
    if sample:
        (x_ref, ga_ref, gb_ref, b_ref, c_ref, xc_ref, q_ref, kv_ref, cconv_ref, ck_ref, cv_ref,
         convw_ref, wco_ref, wao_ref, wo_ref, bucket_ref, table_ref, sinks_ref, g2_ref, wr_ref, br_ref,
         xp_ref, h2_ref, re_ref, rc_ref, sconv_ref, sk_ref, sv_ref,
         kvar, o_sc, ya_sc, bias_sc, carry_u) = refs
    else:
        (x_ref, ga_ref, gb_ref, b_ref, c_ref, xc_ref, q_ref, kv_ref,
         convw_ref, wco_ref, wao_ref, wo_ref, bucket_ref, table_ref, sinks_ref, g2_ref, wr_ref, br_ref,
         xp_ref, h2_ref, re_ref, rc_ref, sconv_ref, sk_ref, sv_ref,
         kvar, o_sc, ya_sc, bias_sc, carry_u) = refs
    t_rows, d_model = x_ref.shape
    n_chunks = t_rows // CHUNK
    span = window + CHUNK
    q_per_kv = n_heads // n_kv
    d_kv = n_kv * head_dim
    scale = 1.0 / math.sqrt(head_dim)
    scale_is_pow2 = math.frexp(scale)[0] == 0.5
    epg = n_exp // n_groups
    first_step = jnp.logical_and(pl.program_id(0) == 0, pl.program_id(1) == 0)
    seq_start = pl.program_id(1) == 0

    @pl.when(first_step)
    def _():
        bk = bucket_ref[...]
        for h in range(n_heads):
            acc = jnp.where(bk < 0, NEG_INF, 0.0)
            for b in range(N_BUCKETS):
                acc = jnp.where(bk == b, table_ref[b, h], acc)
            n, g = divmod(h, q_per_kv)
            pair, half = divmod(g, 2)
            bias_sc[n, pair * CHUNK:(pair + 1) * CHUNK, half * KEY_PAD:(half + 1) * KEY_PAD] = acc

    w_conv = convw_ref[...]
    if sample:
        for s in range(n_chunks):
            rows = slice(s * CHUNK, (s + 1) * CHUNK)
            u = c_ref[rows, :].astype(F32) * xc_ref[rows, :].astype(F32)
            y = _conv_rows(u, cconv_ref[s, 0:1, :], cconv_ref[s, 1:2, :], w_conv)
            ya_sc[rows, :] = (b_ref[rows, :].astype(F32) * y).astype(BF16)
            sconv_ref[s] = u[CHUNK - 2:CHUNK, :]
    else:
        u = c_ref[...].astype(F32) * xc_ref[...].astype(F32)
        prev = jnp.where(seq_start, 0.0, carry_u[...])
        y = _conv_rows(u, prev[SUBLANES - 2:SUBLANES - 1], prev[SUBLANES - 1:SUBLANES], w_conv)
        ya_sc[...] = (b_ref[...].astype(F32) * y).astype(BF16)
        carry_u[...] = u[t_rows - SUBLANES:t_rows, :]
        sconv_ref[0] = u[t_rows - 2:t_rows, :]

    def store_kv(row0, k_rows, v_rows):
        n_rows = k_rows.shape[0]
        low = lax.broadcasted_iota(I32, k_rows.shape, 1) < head_dim
        for base, a in ((0, k_rows), (KV_VARIANTS, v_rows)):
            b = pltpu.roll(a, head_dim, axis=1)
            kvar[base + 0, row0:row0 + n_rows, :] = jnp.where(low, a, 0.0).astype(BF16)
            kvar[base + 1, row0:row0 + n_rows, :] = jnp.where(low, 0.0, a).astype(BF16)
            kvar[base + 2, row0:row0 + n_rows, :] = jnp.where(low, b, 0.0).astype(BF16)
            kvar[base + 3, row0:row0 + n_rows, :] = jnp.where(low, 0.0, b).astype(BF16)

    if sample:
        for s in range(n_chunks):
            rows = slice(s * CHUNK, (s + 1) * CHUNK)
            store_kv(s * span, ck_ref[s], cv_ref[s])
            store_kv(s * span + window, kv_ref[rows, 0:d_kv], kv_ref[rows, d_kv:2 * d_kv])
            sk_ref[s, 0:window - CHUNK, :] = ck_ref[s, CHUNK:window, :]
            sv_ref[s, 0:window - CHUNK, :] = cv_ref[s, CHUNK:window, :]
            sk_ref[s, window - CHUNK:window, :] = kv_ref[rows, 0:d_kv]
            sv_ref[s, window - CHUNK:window, :] = kv_ref[rows, d_kv:2 * d_kv]
        k_stride = span
    else:
        @pl.when(seq_start)
        def _():
            kvar[:, 0:window, :] = jnp.zeros((2 * KV_VARIANTS, window, d_kv), BF16)
        store_kv(window, kv_ref[:, 0:d_kv], kv_ref[:, d_kv:2 * d_kv])
        sk_ref[0] = kv_ref[t_rows - window:t_rows, 0:d_kv]
        sv_ref[0] = kv_ref[t_rows - window:t_rows, d_kv:2 * d_kv]
        k_stride = CHUNK

    n_pairs = q_per_kv // 2
    key_pad_rows = jnp.zeros((KEY_PAD - span, d_kv), BF16)
    ones_d = ((lax.broadcasted_iota(I32, (2 * KEY_PAD, LANES), 0) < KEY_PAD)
              == (lax.broadcasted_iota(I32, (2 * KEY_PAD, LANES), 1) < head_dim)).astype(BF16)
    low_half = lax.broadcasted_iota(I32, (CHUNK, LANES), 1) < head_dim
    ya_parts = []
    ya_cols = d_model // (n_chunks * n_kv)
    for c in range(n_chunks):
        q_rows = slice(c * CHUNK, (c + 1) * CHUNK)
        k_rows = slice(c * k_stride, c * k_stride + span)
        masked = (not sample) and c * CHUNK < window
        if masked:
            first_key = (pl.program_id(1) * n_chunks + c) * CHUNK - window
            valid = lax.broadcasted_iota(I32, (CHUNK, KEY_PAD), 1) + first_key >= 0
        for n in range(n_kv):
            top, bot = (0, 3) if n == 0 else (2, 1)
            kd = jnp.concatenate([kvar[top, k_rows, :], key_pad_rows, kvar[bot, k_rows, :], key_pad_rows], axis=0)
            vd = jnp.concatenate([kvar[KV_VARIANTS + top, k_rows, :], key_pad_rows,
                                  kvar[KV_VARIANTS + bot, k_rows, :], key_pad_rows], axis=0)
            q4 = jnp.concatenate(
                [q_ref[q_rows, (n * n_pairs + j) * LANES:(n * n_pairs + j + 1) * LANES] for j in range(n_pairs)], axis=0)
            if scale_is_pow2:
                q4 = q4 * scale
            s4 = lax.dot_general(q4, kd, (((1,), (1,)), ((), ())), preferred_element_type=F32)
            if not scale_is_pow2:
                s4 = s4 * scale
            e_rows, sink_rows = [], []
            for j in range(n_pairs):
                e_halves, sink_halves = [], []
                for half in range(2):
                    blk = (slice(j * CHUNK, (j + 1) * CHUNK), slice(half * KEY_PAD, (half + 1) * KEY_PAD))
                    s = s4[blk] + bias_sc[n, blk[0], blk[1]]
                    if masked:
                        s = jnp.where(valid, s, NEG_INF)
                    sink = sinks_ref[0, n * q_per_kv + 2 * j + half]
                    mx = jnp.maximum(jnp.max(s, axis=-1, keepdims=True), sink)
                    e_halves.append(jnp.exp(s - mx).astype(BF16))
                    sink_halves.append(jnp.exp(sink - mx))
                e_rows.append(jnp.concatenate(e_halves, axis=1))
                sink_rows.append(jnp.where(low_half, sink_halves[0], sink_halves[1]))
            od = jnp.dot(jnp.concatenate(e_rows, axis=0), jnp.concatenate([vd, ones_d], axis=1),
                         preferred_element_type=F32)
            o4 = od[:, 0:LANES] / (od[:, LANES:2 * LANES] + jnp.concatenate(sink_rows, axis=0))
            for j in range(n_pairs):
                o_sc[q_rows, (n * n_pairs + j) * LANES:(n * n_pairs + j + 1) * LANES] = (
                    o4[j * CHUNK:(j + 1) * CHUNK].astype(BF16))
            ya_parts.append(jnp.dot(ya_sc[...], wco_ref[:, len(ya_parts) * ya_cols:(len(ya_parts) + 1) * ya_cols],
                                    preferred_element_type=F32))
    if not sample:
        kvar[:, 0:window, :] = kvar[:, t_rows:t_rows + window, :]
    y_a = jnp.concatenate(ya_parts, axis=1)
    y_b = jnp.dot(o_sc[...], wao_ref[...], preferred_element_type=F32)

    m = jax.nn.sigmoid(ga_ref[...].astype(F32)) * y_a + jax.nn.sigmoid(gb_ref[...].astype(F32)) * y_b
    xp = x_ref[...] + jnp.dot(m.astype(BF16), wo_ref[...], preferred_element_type=F32)
    xp_ref[...] = xp

    h2 = _rms_rows(xp, g2_ref[...])
    h2_ref[...] = _pack_halves(h2).reshape(t_rows, 1, d_model // 2)
    lt = lax.dot_general(wr_ref[...], h2.astype(BF16), (((1,), (1,)), ((), ())),
                         preferred_element_type=F32) + br_ref[...]
    lg = lt[0:n_groups]
    eg = jnp.exp(lg - jnp.max(lg, axis=0, keepdims=True))
    gp = eg / jnp.sum(eg, axis=0, keepdims=True)
    gw = jnp.max(gp, axis=0, keepdims=True)
    gi = lax.broadcasted_iota(I32, gp.shape, 0).astype(F32)
    gsel = jnp.min(jnp.where(gp == gw, gi, float(n_groups)), axis=0, keepdims=True)
    el = jnp.zeros((epg, t_rows), F32)
    for g in range(n_groups):
        el = jnp.where(gsel == float(g), lt[GROUP_ROW0 + g * epg:GROUP_ROW0 + (g + 1) * epg], el)
    ei = lax.broadcasted_iota(I32, el.shape, 0).astype(F32)
    v1 = jnp.max(el, axis=0, keepdims=True)
    i1 = jnp.min(jnp.where(el == v1, ei, float(epg)), axis=0, keepdims=True)
    el2 = jnp.where(ei == i1, -jnp.inf, el)
    v2 = jnp.max(el2, axis=0, keepdims=True)
    i2 = jnp.min(jnp.where(el2 == v2, ei, float(epg)), axis=0, keepdims=True)
    a1 = jnp.exp(v1 - v1)
    a2 = jnp.exp(v2 - v1)
    den = a1 + a2
    c1 = gw * (a1 / den)
    c2 = gw * (a2 / den)
    e1 = (gsel * float(epg) + i1).astype(I32)
    e2 = (gsel * float(epg) + i2).astype(I32)
    row8 = lax.broadcasted_iota(I32, (SUBLANES, t_rows), 0)
    re_ref[...] = jnp.where(row8 == 0, e1, jnp.where(row8 == 1, e2, 0))
    rc_ref[...] = jnp.where(row8 == 0, c1, jnp.where(row8 == 1, c2, 0.0))


def _mix(x2d, proj, kv, caches, weights, *, n_seq, sample, dims):
    n_heads, n_kv, head_dim, window, n_groups, n_exp, d_conv = dims
    n, d = x2d.shape
    t = MIX_ROWS
    n_chunks = t // CHUNK
    span = window + CHUNK
    d_attn = n_heads * head_dim
    d_kv = n_kv * head_dim
    convw, wco, wao, wo, bucket, table, sinks, g2, wr, br = weights
    if sample:
        n_t = n // t
        grid = (n_t, 1)
        tok = lambda i, j: (i, 0)
        n_state = n // CHUNK
        state_blk = n_chunks
        st = lambda i, j: (i, 0, 0)
    else:
        n_t = (n // n_seq) // t
        grid = (n_seq, n_t)
        tok = lambda i, j: (i * n_t + j, 0)
        n_state = n_seq
        state_blk = 1
        st = lambda i, j: (i, 0, 0)
    const2 = lambda i, j: (0, 0)

    def col(width, idx):
        return pl.BlockSpec((t, width), lambda i, j: (tok(i, j)[0], idx))

    def resident(shape):
        return pl.BlockSpec(shape, const2, pipeline_mode=pl.Buffered(1))

    in_specs = [
        pl.BlockSpec((t, d), tok),
        col(d, 2), col(d, 3),
        col(d_conv, 0), col(d_conv, 1), col(d_conv, 2), col(d_attn, 3),
        pl.BlockSpec((t, 2 * d_kv), tok),
    ]
    args = [x2d, proj, proj, proj, proj, proj, proj, kv]
    if sample:
        cconv, ck, cv = caches
        in_specs += [
            pl.BlockSpec((n_chunks, cconv.shape[1], d_conv), st),
            pl.BlockSpec((n_chunks, window, d_kv), st),
            pl.BlockSpec((n_chunks, window, d_kv), st),
        ]
        args += [cconv, ck, cv]
    in_specs += [
        resident(convw.shape), resident(wco.shape), resident(wao.shape), resident(wo.shape),
        resident(bucket.shape),
        pl.BlockSpec(memory_space=pltpu.SMEM), pl.BlockSpec(memory_space=pltpu.SMEM),
        resident(g2.shape), resident(wr.shape), resident(br.shape),
    ]
    args += [convw, wco, wao, wo, bucket, table, sinks, g2, wr, br]
    out_specs = [
        pl.BlockSpec((t, d), tok),
        pl.BlockSpec((t, 1, d // 2), lambda i, j: (tok(i, j)[0], 0, 0)),
        pl.BlockSpec((SUBLANES, t), lambda i, j: (0, tok(i, j)[0])),
        pl.BlockSpec((SUBLANES, t), lambda i, j: (0, tok(i, j)[0])),
        pl.BlockSpec((state_blk, 2, d_conv), st),
        pl.BlockSpec((state_blk, window, d_kv), st),
        pl.BlockSpec((state_blk, window, d_kv), st),
    ]
    out_shape = [
        jax.ShapeDtypeStruct((n, d), F32),
        jax.ShapeDtypeStruct((n, 1, d // 2), U32),
        jax.ShapeDtypeStruct((SUBLANES, n), I32),
        jax.ShapeDtypeStruct((SUBLANES, n), F32),
        jax.ShapeDtypeStruct((n_state, 2, d_conv), F32),
        jax.ShapeDtypeStruct((n_state, window, d_kv), F32),
        jax.ShapeDtypeStruct((n_state, window, d_kv), F32),
    ]
    kv_rows = n_chunks * span if sample else window + t
    scratch = [
        pltpu.VMEM((2 * KV_VARIANTS, kv_rows, d_kv), BF16),
        pltpu.VMEM((t, d_attn), BF16), pltpu.VMEM((t, d_conv), BF16),
        pltpu.VMEM((n_kv, (n_heads // n_kv // 2) * CHUNK, 2 * KEY_PAD), F32), pltpu.VMEM((SUBLANES, d_conv), F32),
    ]
    est = (2 * t * d * 4 * 3 + 2 * 2 * t * d * 2 + 2 * 4 * t * d_conv * 2
           + (2 * d_conv * d + d * d) * 2 + 12 * t * d * 4)
    kern = functools.partial(_mix_kernel, sample=sample, n_heads=n_heads, n_kv=n_kv, head_dim=head_dim,
                             window=window, n_groups=n_groups, n_exp=n_exp)
    return pl.pallas_call(
        kern, grid=grid, in_specs=in_specs, out_specs=out_specs, out_shape=out_shape,
        scratch_shapes=scratch,
        compiler_params=pltpu.CompilerParams(
            dimension_semantics=("arbitrary", "arbitrary"), vmem_limit_bytes=_vmem_limit(est)),
        name="mix_sample" if sample else "mix_prompt",
    )(*args)


def _route_kernel(re_ref, pos_ref, tmap_ref, *, n_exp, tile_rows, blk):
    n = re_ref.shape[1]
    n_blk = n // blk
    erow = lax.broadcasted_iota(I32, (n_exp, blk), 0)

    def onehots(j):
        c0 = pl.multiple_of(j * blk, blk)
        oh0 = (erow == re_ref[0:1, pl.ds(c0, blk)]).astype(F32)
        oh1 = (erow == re_ref[1:2, pl.ds(c0, blk)]).astype(F32)
        return c0, oh0, oh1

    def count_body(j, cnt):
        _, oh0, oh1 = onehots(j)
        return cnt + jnp.sum(oh0 + oh1, axis=1, keepdims=True)

    cnt = lax.fori_loop(0, n_blk, count_body, jnp.zeros((n_exp, 1), F32))
    cnt = jnp.broadcast_to(cnt, (n_exp, LANES))
    padded = jnp.ceil(cnt / tile_rows) * tile_rows
    ends = padded
    prow = lax.broadcasted_iota(I32, ends.shape, 0)
    step = 1
    while step < n_exp:
        ends = ends + jnp.where(prow >= step, pltpu.roll(ends, step, axis=0), 0.0)
        step *= 2
    offs = ends - padded
    off1 = offs[:, 0:1]

    tri = (lax.broadcasted_iota(I32, (blk, blk), 0) <= lax.broadcasted_iota(I32, (blk, blk), 1)).astype(BF16)
    row8 = lax.broadcasted_iota(I32, (SUBLANES, blk), 0)

    def pos_body(j, run):
        c0, oh0, oh1 = onehots(j)
        both = oh0 + oh1
        csum = jnp.dot(both.astype(BF16), tri, preferred_element_type=F32) + run
        slot = off1 + csum - 1.0
        p0 = jnp.sum(oh0 * slot, axis=0, keepdims=True).astype(I32)
        p1 = jnp.sum(oh1 * slot, axis=0, keepdims=True).astype(I32)
        pos_ref[:, pl.ds(c0, blk)] = jnp.where(row8 == 0, p0, jnp.where(row8 == 1, p1, 0))
        return run + jnp.sum(both, axis=1, keepdims=True)

    lax.fori_loop(0, n_blk, pos_body, jnp.zeros((n_exp, 1), F32))

    n_tiles_pad = tmap_ref.shape[1]
    start = (lax.broadcasted_iota(I32, (1, n_tiles_pad), 1) * tile_rows).astype(F32)
    end1 = ends[:, 0:1]
    te = jnp.sum((end1 <= start).astype(F32), axis=0, keepdims=True)
    trow = lax.broadcasted_iota(I32, (n_exp, n_tiles_pad), 0).astype(F32)
    used_end = jnp.sum(jnp.where(trow == te, off1 + cnt[:, 0:1], 0.0), axis=0, keepdims=True)
    n_rows = jnp.clip(used_end - start, 0.0, float(tile_rows))
    has_tokens = cnt[:, 0:1] > 0.0
    te = jnp.minimum(te, jnp.max(jnp.where(has_tokens, trow, 0.0), axis=0, keepdims=True))
    nxt = jnp.min(jnp.where(trow > te, jnp.where(has_tokens, trow, float(n_exp)), float(n_exp)), axis=0, keepdims=True)
    r8 = lax.broadcasted_iota(I32, (SUBLANES, n_tiles_pad), 0)
    tmap_ref[...] = jnp.where(r8 == 0, te.astype(I32),
                              jnp.where(r8 == 1, n_rows.astype(I32), jnp.where(r8 == 2, nxt.astype(I32), 0)))


def _route(re, n_exp, n_tiles):
    n = re.shape[1]
    n_tiles_pad = pl.cdiv(n_tiles, LANES) * LANES
    kern = functools.partial(_route_kernel, n_exp=n_exp, tile_rows=SLOT_ROWS, blk=ROUTE_BLOCK)
    return pl.pallas_call(
        kern,
        out_shape=[jax.ShapeDtypeStruct((SUBLANES, n), I32), jax.ShapeDtypeStruct((SUBLANES, n_tiles_pad), I32)],
        name="route",
    )(re)


def _dispatch_kernel(*refs, aliased):
    if aliased:
        p0_ref, p1_ref, h_ref, _, xs_ref, sem = refs
    else:
        p0_ref, p1_ref, h_ref, xs_ref, sem = refs
    rows = h_ref.shape[0]

    def start(t, carry):
        pltpu.make_async_copy(h_ref.at[t], xs_ref.at[p0_ref[t]], sem).start(priority=0)
        pltpu.make_async_copy(h_ref.at[t], xs_ref.at[p1_ref[t]], sem).start(priority=1)
        return carry

    lax.fori_loop(0, rows, start, 0)
    for _ in range(TOP_K):
        pltpu.make_async_copy(h_ref, xs_ref.at[pl.ds(0, rows)], sem).wait()


def _dispatch(h2, p0, p1, xs, n_slots):
    n, _, d = h2.shape
    t = MIX_ROWS
    aliased = xs is not None
    in_specs = [
        pl.BlockSpec((t,), lambda i: (i,), memory_space=pltpu.SMEM),
        pl.BlockSpec((t,), lambda i: (i,), memory_space=pltpu.SMEM),
        pl.BlockSpec((t, 1, d), lambda i: (i, 0, 0)),
    ]
    args = [p0, p1, h2]
    if aliased:
        in_specs.append(pl.BlockSpec(memory_space=pl.ANY))
        args.append(xs)
    return pl.pallas_call(
        functools.partial(_dispatch_kernel, aliased=aliased),
        grid=(n // t,),
        in_specs=in_specs,
        out_specs=pl.BlockSpec(memory_space=pl.ANY),
        out_shape=jax.ShapeDtypeStruct((n_slots, 1, d), h2.dtype),
        scratch_shapes=[pltpu.SemaphoreType.DMA(())],
        input_output_aliases={3: 0} if aliased else {},
        compiler_params=pltpu.CompilerParams(dimension_semantics=("arbitrary",), has_side_effects=True),
        name="dispatch",
    )(*args)


def _experts_kernel(te_ref, nr_ref, nx_ref, xs_ref, wg_hbm, wu_hbm, wd_hbm, y_ref,
                    wg_st, wu_st, wd_st, wg_sc, wu_sc, wd_sc, sems, n_changes, *, n_exp, tile_rows):
    dh = xs_ref.shape[2]
    tiles_per_step = xs_ref.shape[0] // tile_rows

    def copies(e, slot):
        return (pltpu.make_async_copy(wg_hbm.at[e], wg_st.at[slot], sems.at[slot]),
                pltpu.make_async_copy(wu_hbm.at[e], wu_st.at[slot], sems.at[slot]),
                pltpu.make_async_copy(wd_hbm.at[e], wd_st.at[slot], sems.at[slot]))

    def one_tile(sub):
        tile = pl.program_id(0) * tiles_per_step + sub
        rows = pl.ds(sub * tile_rows, tile_rows)
        expert = te_ref[tile]

        @pl.when(tile == 0)
        def _():
            n_changes[0] = 0
            for cp in copies(expert, 0):
                cp.start(priority=1)

        @pl.when(jnp.logical_or(tile == 0, expert != te_ref[jnp.maximum(tile - 1, 0)]))
        def _():
            slot = lax.rem(n_changes[0], 2)
            n_changes[0] = n_changes[0] + 1
            for cp in copies(expert, slot):
                cp.wait()
            nxt = nx_ref[tile]

            @pl.when(nxt < n_exp)
            def _():
                for cp in copies(nxt, 1 - slot):
                    cp.start(priority=1)

            wg_sc[...] = wg_st[slot].astype(BF16)
            wu_sc[...] = wu_st[slot].astype(BF16)
            wd_sc[...] = wd_st[slot].astype(BF16)

        n_rows = nr_ref[tile]

        @pl.when(n_rows > 0)
        def _():
            live = lax.broadcasted_iota(I32, (tile_rows, dh), 0) < n_rows
            words = jnp.where(live, xs_ref[rows].reshape(tile_rows, dh), jnp.uint32(0))
            x = _unpack_halves(words).astype(BF16)
            gate = jnp.dot(x, wg_sc[...], preferred_element_type=F32)
            up = jnp.dot(x, wu_sc[...], preferred_element_type=F32)
            hid = (jax.nn.silu(gate) * up).astype(BF16)
            y = jnp.dot(hid, wd_sc[...], preferred_element_type=F32)
            y_ref[rows] = _pack_halves(y).reshape(tile_rows, 1, dh)

        @pl.when(n_rows <= 0)
        def _():
            y_ref[rows] = jnp.zeros((tile_rows, 1, dh), U32)

    for sub in range(tiles_per_step):
        one_tile(sub)


def _experts(xs, te, nr, nx, w_gate, w_up, w_down):
    n_slots, _, dh = xs.shape
    n_exp, d, d_e = w_gate.shape
    t = SLOT_ROWS * EXPERT_STEP_TILES
    est = 2 * 2 * t * d * 2 + 2 * 3 * d * d_e * 4 + 3 * d * d_e * 2 + 4 * SLOT_ROWS * d * 4
    row_spec = pl.BlockSpec((t, 1, dh), lambda i, te, nr, nx: (i, 0, 0))
    return pl.pallas_call(
        functools.partial(_experts_kernel, n_exp=n_exp, tile_rows=SLOT_ROWS),
        grid_spec=pltpu.PrefetchScalarGridSpec(
            num_scalar_prefetch=3,
            grid=(n_slots // t,),
            in_specs=[row_spec, pl.BlockSpec(memory_space=pl.ANY), pl.BlockSpec(memory_space=pl.ANY),
                      pl.BlockSpec(memory_space=pl.ANY)],
            out_specs=row_spec,
            scratch_shapes=[
                pltpu.VMEM((2, d, d_e), F32), pltpu.VMEM((2, d, d_e), F32), pltpu.VMEM((2, d_e, d), F32),
                pltpu.VMEM((d, d_e), BF16), pltpu.VMEM((d, d_e), BF16), pltpu.VMEM((d_e, d), BF16),
                pltpu.SemaphoreType.DMA((2,)), pltpu.SMEM((1,), I32)],
        ),
        out_shape=jax.ShapeDtypeStruct((n_slots, 1, dh), U32),
        compiler_params=pltpu.CompilerParams(
            dimension_semantics=("arbitrary",), vmem_limit_bytes=_vmem_limit(est)),
        name="experts",
    )(te, nr, nx, xs, w_gate, w_up, w_down)


def _combine_kernel(p0_ref, p1_ref, p0n_ref, p1n_ref, xp_ref, rc_ref, g_ref, y_hbm, out_ref, y0_buf, y1_buf, sems,
                    *, n_tiles):
    i = pl.program_id(0)
    rows, d = xp_ref.shape

    def gather(pa_ref, pb_ref, slot):
        base = slot * rows
        def body(t, carry):
            pltpu.make_async_copy(y_hbm.at[pa_ref[t]], y0_buf.at[base + t], sems.at[slot]).start(priority=0)
            pltpu.make_async_copy(y_hbm.at[pb_ref[t]], y1_buf.at[base + t], sems.at[slot]).start(priority=1)
            return carry
        lax.fori_loop(0, rows, body, 0)

    slot = lax.rem(i, 2)

    @pl.when(i == 0)
    def _():
        gather(p0_ref, p1_ref, 0)

    @pl.when(i + 1 < n_tiles)
    def _():
        gather(p0n_ref, p1n_ref, 1 - slot)

    cur = pl.ds(pl.multiple_of(slot * rows, rows), rows)
    for buf in (y0_buf, y1_buf):
        pltpu.make_async_copy(y_hbm.at[pl.ds(0, rows)], buf.at[cur], sems.at[slot]).wait()
    ct = rc_ref[...].T
    y0 = _unpack_halves(y0_buf[cur].reshape(rows, d // 2))
    y1 = _unpack_halves(y1_buf[cur].reshape(rows, d // 2))
    moe = ct[:, 0:1] * y0 + ct[:, 1:2] * y1
    out_ref[...] = _rms_rows(xp_ref[...] + moe, g_ref[...])


def _combine(xp, rc, p0, p1, y, g):
    n, d = xp.shape
    t = MIX_ROWS
    n_t = n // t
    est = 2 * 2 * t * d * 4 + 2 * 2 * t * d * 2 + 8 * t * d * 4
    nxt = lambda i: (jnp.minimum(i + 1, n_t - 1),)
    return pl.pallas_call(
        functools.partial(_combine_kernel, n_tiles=n_t),
        grid=(n_t,),
        in_specs=[
            pl.BlockSpec((t,), lambda i: (i,), memory_space=pltpu.SMEM),
            pl.BlockSpec((t,), lambda i: (i,), memory_space=pltpu.SMEM),
            pl.BlockSpec((t,), nxt, memory_space=pltpu.SMEM),
            pl.BlockSpec((t,), nxt, memory_space=pltpu.SMEM),
            pl.BlockSpec((t, d), lambda i: (i, 0)),
            pl.BlockSpec((SUBLANES, t), lambda i: (0, i)),
            pl.BlockSpec((1, d), lambda i: (0, 0)),
            pl.BlockSpec(memory_space=pl.ANY),
        ],
        out_specs=pl.BlockSpec((t, d), lambda i: (i, 0)),
        out_shape=jax.ShapeDtypeStruct((n, d), F32),
        scratch_shapes=[pltpu.VMEM((2 * t, 1, d // 2), U32), pltpu.VMEM((2 * t, 1, d // 2), U32),
                        pltpu.SemaphoreType.DMA((2,))],
        compiler_params=pltpu.CompilerParams(
            dimension_semantics=("arbitrary",), vmem_limit_bytes=_vmem_limit(est)),
        name="combine",
    )(p0, p1, p0, p1, xp, rc, g, y)


def _rel_buckets(window):
    rel = (jnp.arange(window + CHUNK, dtype=I32) - window)[None, :] - jnp.arange(CHUNK, dtype=I32)[:, None]
    nb = N_BUCKETS // 2
    ret = (rel > 0).astype(I32) * nb
    n = jnp.abs(rel)
    max_exact = nb // 2
    nf = jnp.maximum(n, 1).astype(F32)
    large = max_exact + (jnp.log(nf / max_exact) / math.log(MAX_DISTANCE / max_exact)
                         * (nb - max_exact)).astype(I32)
    large = jnp.minimum(large, nb - 1)
    bucket = ret + jnp.where(n < max_exact, n, large)
    return jnp.pad(bucket, ((0, 0), (0, KEY_PAD - bucket.shape[1])), constant_values=-1)


def kernel(x_prompt, x_sample, cache_conv, cache_k, cache_v, rel_bias_table, norm_mix_g, w_in, conv_w, w_conv_out, attn_sinks, w_attn_out, w_o, norm_ffn_g, w_group, b_group, w_expert_router, b_expert_router, w_gate, w_up, w_down, final_norm_g):
    assert w_in.shape[0] == 1, "single-layer step"
    batch, seq, d = x_prompt.shape
    dec_batch, dec_seq, _ = x_sample.shape
    assert dec_seq == CHUNK and seq % MIX_ROWS == 0 and (dec_batch * dec_seq) % MIX_ROWS == 0
    d_conv = conv_w.shape[-1]
    window, n_kv, head_dim = cache_k.shape[2], cache_k.shape[3], cache_k.shape[4]
    n_heads = attn_sinks.shape[-1]
    d_attn, d_kv = n_heads * head_dim, n_kv * head_dim
    n_groups, n_exp = w_group.shape[-1], w_expert_router.shape[-1]
    epg = n_exp // n_groups
    assert n_groups <= GROUP_ROW0 and d_conv == d_attn and 2 * d_conv == d
    assert n_kv == 2 and d_kv == LANES and (n_heads // n_kv) % 2 == 0 and window + CHUNK <= KEY_PAD
    dims = (n_heads, n_kv, head_dim, window, n_groups, n_exp, d_conv)

    w_all = w_in[0].astype(BF16)
    kv0 = 3 * d_conv + d_attn
    g1 = norm_mix_g[0][None, :]
    wr = jnp.zeros((GROUP_ROW0 + n_exp, d), F32)
    wr = wr.at[:n_groups].set(w_group[0].T).at[GROUP_ROW0:].set(w_expert_router[0].T).astype(BF16)
    br = jnp.zeros((GROUP_ROW0 + n_exp, 1), F32)
    br = br.at[:n_groups, 0].set(b_group[0]).at[GROUP_ROW0:, 0].set(b_expert_router[0])
    weights = (conv_w[0], w_conv_out[0].astype(BF16), w_attn_out[0].astype(BF16), w_o[0].astype(BF16),
               _rel_buckets(window), rel_bias_table, attn_sinks, norm_ffn_g[0][None, :], wr, br)

    xp2d = x_prompt.reshape(batch * seq, d)
    xs2d = x_sample.reshape(dec_batch * dec_seq, d)
    n_p, n_s = xp2d.shape[0], xs2d.shape[0]
    n_tok = n_p + n_s

    proj_p, kv_p = _inproj(xp2d, g1, w_all, kv0, 2 * d_kv)
    proj_s, kv_s = _inproj(xs2d, g1, w_all, kv0, 2 * d_kv)
    xres_p, h2_p, re_p, rc_p, conv_p, k_p, v_p = _mix(
        xp2d, proj_p, kv_p, None, weights, n_seq=batch, sample=False, dims=dims)
    caches = (cache_conv[0], cache_k[0].reshape(dec_batch, window, d_kv), cache_v[0].reshape(dec_batch, window, d_kv))
    xres_s, h2_s, re_s, rc_s, conv_s, k_s, v_s = _mix(
        xs2d, proj_s, kv_s, caches, weights, n_seq=dec_batch, sample=True, dims=dims)

    n_tiles = pl.cdiv((TOP_K * n_tok) // SLOT_ROWS + n_exp, EXPERT_STEP_TILES) * EXPERT_STEP_TILES
    n_slots = n_tiles * SLOT_ROWS
    pos, tmap = _route(jnp.concatenate([re_p, re_s], axis=1), n_exp, n_tiles)
    p0, p1 = pos[0], pos[1]
    xs = _dispatch(h2_p, p0[:n_p], p1[:n_p], None, n_slots)
    xs = _dispatch(h2_s, p0[n_p:], p1[n_p:], xs, n_slots)
    y = _experts(xs, tmap[0, :n_tiles], tmap[1, :n_tiles], tmap[2, :n_tiles], w_gate[0], w_up[0], w_down[0])
    gf = final_norm_g[None, :]
    y_prompt = _combine(xres_p, rc_p, p0[:n_p], p1[:n_p], y, gf).reshape(batch, seq, d)
    y_sample = _combine(xres_s, rc_s, p0[n_p:], p1[n_p:], y, gf).reshape(dec_batch, dec_seq, d)

    kv_shape = (1, -1, window, n_kv, head_dim)
    return (y_prompt, y_sample, conv_p[None], k_p.reshape(kv_shape), v_p.reshape(kv_shape),
            conv_s[None], k_s.reshape(kv_shape), v_s.reshape(kv_shape))
```

```python
import functools
import math

import jax
import jax.numpy as jnp
from jax import lax
from jax.experimental import pallas as pl
from jax.experimental.pallas import tpu as pltpu

F32, BF16, I32, U32 = jnp.float32, jnp.bfloat16, jnp.int32, jnp.uint32

CHUNK = 64
N_BUCKETS = 32
MAX_DISTANCE = 128
EPS = 1e-6
NEG_INF = -1e30
TOP_K = 2

V7X_VMEM_BYTES = 64 * 1024 * 1024
SUBLANES = 8
LANES = 128

INPROJ_ROWS = 1024
INPROJ_COLS = 2048
MIX_ROWS = 256
MOVE_ROWS = 512
SLOT_ROWS = 256
EXPERT_STEP_TILES = 2
ROUTE_BLOCK = 256
GROUP_ROW0 = 8
KEY_PAD = 256
KV_VARIANTS = 4


def _vmem_limit(nbytes):
    return int(min(V7X_VMEM_BYTES - (4 << 20), max(nbytes, 32 << 20)))


def _pack_halves(x):
    half = x.shape[1] // 2
    return pltpu.pack_elementwise([x[:, :half], x[:, half:]], packed_dtype=BF16)


def _unpack_halves(w):
    lo = pltpu.unpack_elementwise(w, index=0, packed_dtype=BF16, unpacked_dtype=F32)
    hi = pltpu.unpack_elementwise(w, index=1, packed_dtype=BF16, unpacked_dtype=F32)
    return jnp.concatenate([lo, hi], axis=1)


def _rms_rows(x, g):
    r = lax.rsqrt(jnp.mean(x * x, axis=-1, keepdims=True) + EPS)
    return (x * r) * g


def _inproj_kernel(x_ref, g_ref, wm_ref, wkv_ref, proj_ref, kv_ref, h_sc):
    @pl.when(pl.program_id(1) == 0)
    def _():
        rows = 128
        def body(i, carry):
            r0 = pl.multiple_of(i * rows, rows)
            h_sc[pl.ds(r0, rows), :] = _rms_rows(x_ref[pl.ds(r0, rows), :], g_ref[...]).astype(BF16)
            return carry
        lax.fori_loop(0, x_ref.shape[0] // rows, body, 0)
        kv_ref[...] = jnp.dot(h_sc[...], wkv_ref[...], preferred_element_type=F32)

    proj_ref[...] = jnp.dot(h_sc[...], wm_ref[...], preferred_element_type=F32).astype(BF16)


def _inproj(x2d, g, w_all, kv0, n_kv):
    n, d = x2d.shape
    tm = min(INPROJ_ROWS, n)
    tn = INPROJ_COLS
    n_main = w_all.shape[1] - n_kv
    assert kv0 % tn == 0 and n_main % tn == 0 and n_kv % LANES == 0
    est = 2 * tm * d * 4 + tm * d * 2 + 2 * d * tn * 2 + 2 * d * n_kv * 2 + 2 * tm * tn * 2 + 2 * tm * n_kv * 4 + tm * tn * 4
    return pl.pallas_call(
        _inproj_kernel,
        grid=(n // tm, n_main // tn),
        in_specs=[
            pl.BlockSpec((tm, d), lambda i, j: (i, 0)),
            pl.BlockSpec((1, d), lambda i, j: (0, 0)),
            pl.BlockSpec((pl.Element(d), pl.Element(tn)),
                         lambda i, j: (0, pl.multiple_of(jnp.where(j * tn < kv0, j * tn, j * tn + n_kv), LANES))),
            pl.BlockSpec((pl.Element(d), pl.Element(n_kv)), lambda i, j: (0, kv0)),
        ],
        out_specs=[
            pl.BlockSpec((tm, tn), lambda i, j: (i, j)),
            pl.BlockSpec((tm, n_kv), lambda i, j: (i, 0)),
        ],
        out_shape=[jax.ShapeDtypeStruct((n, n_main), BF16), jax.ShapeDtypeStruct((n, n_kv), F32)],
        scratch_shapes=[pltpu.VMEM((tm, d), BF16)],
        compiler_params=pltpu.CompilerParams(
            dimension_semantics=("arbitrary", "arbitrary"), vmem_limit_bytes=_vmem_limit(est + (8 << 20))),
        name="inproj",
    )(x2d, g, w_all, w_all)


def _conv_rows(u, prev2, prev1, w):
    row = lax.broadcasted_iota(I32, u.shape, 0)
    u1 = jnp.where(row == 0, prev1, pltpu.roll(u, 1, axis=0))
    u2 = jnp.where(row == 0, prev2, jnp.where(row == 1, prev1, pltpu.roll(u, 2, axis=0)))
    return (w[0:1] * u2 + w[1:2] * u1) + w[2:3] * u


def _mix_kernel(*refs, sample, n_heads, n_kv, head_dim, window, n_groups, n_exp):
    if sample:
        (x_ref, ga_ref, gb_ref, b_ref, c_ref, xc_ref, q_ref, kv_ref, cconv_ref, ck_ref, cv_ref,
         convw_ref, wco_ref, wao_ref, wo_ref, bias_ref, sinks_ref, g2_ref, wr_ref, br_ref,
         xp_ref, h2_ref, re_ref, rc_ref, sconv_ref, sk_ref, sv_ref,
         kvar, o_sc, ya_sc, carry_u) = refs
    else:
        (x_ref, ga_ref, gb_ref, b_ref, c_ref, xc_ref, q_ref, kv_ref,
         convw_ref, wco_ref, wao_ref, wo_ref, bias_ref, sinks_ref, g2_ref, wr_ref, br_ref,
         xp_ref, h2_ref, re_ref, rc_ref, sconv_ref, sk_ref, sv_ref,
         kvar, o_sc, ya_sc, carry_u) = refs
    t_rows, d_model = x_ref.shape
    n_chunks = t_rows // CHUNK
    span = window + CHUNK
    q_per_kv = n_heads // n_kv
    d_kv = n_kv * head_dim
    scale = 1.0 / math.sqrt(head_dim)
    scale_is_pow2 = math.frexp(scale)[0] == 0.5
    epg = n_exp // n_groups
    seq_start = pl.program_id(1) == 0

    w_conv = convw_ref[...]
    if sample:
        for s in range(n_chunks):
            rows = slice(s * CHUNK, (s + 1) * CHUNK)
            u = c_ref[rows, :].astype(F32) * xc_ref[rows, :].astype(F32)
            y = _conv_rows(u, cconv_ref[s, 0:1, :], cconv_ref[s, 1:2, :], w_conv)
            ya_sc[rows, :] = (b_ref[rows, :].astype(F32) * y).astype(BF16)
            sconv_ref[s] = u[CHUNK - 2:CHUNK, :]
    else:
        u = c_ref[...].astype(F32) * xc_ref[...].astype(F32)
        prev = jnp.where(seq_start, 0.0, carry_u[...])
        y = _conv_rows(u, prev[SUBLANES - 2:SUBLANES - 1], prev[SUBLANES - 1:SUBLANES], w_conv)
        ya_sc[...] = (b_ref[...].astype(F32) * y).astype(BF16)
        carry_u[...] = u[t_rows - SUBLANES:t_rows, :]
        sconv_ref[0] = u[t_rows - 2:t_rows, :]

    def store_kv(row0, k_rows, v_rows):
        n_rows = k_rows.shape[0]
        low = lax.broadcasted_iota(I32, k_rows.shape, 1) < head_dim
        for base, a in ((0, k_rows), (KV_VARIANTS, v_rows)):
            b = pltpu.roll(a, head_dim, axis=1)
            kvar[base + 0, row0:row0 + n_rows, :] = jnp.where(low, a, 0.0).astype(BF16)
            kvar[base + 1, row0:row0 + n_rows, :] = jnp.where(low, 0.0, a).astype(BF16)
            kvar[base + 2, row0:row0 + n_rows, :] = jnp.where(low, b, 0.0).astype(BF16)
            kvar[base + 3, row0:row0 + n_rows, :] = jnp.where(low, 0.0, b).astype(BF16)

    if sample:
        for s in range(n_chunks):
            rows = slice(s * CHUNK, (s + 1) * CHUNK)
            store_kv(s * span, ck_ref[s], cv_ref[s])
            store_kv(s * span + window, kv_ref[rows, 0:d_kv], kv_ref[rows, d_kv:2 * d_kv])
            sk_ref[s, 0:window - CHUNK, :] = ck_ref[s, CHUNK:window, :]
            sv_ref[s, 0:window - CHUNK, :] = cv_ref[s, CHUNK:window, :]
            sk_ref[s, window - CHUNK:window, :] = kv_ref[rows, 0:d_kv]
            sv_ref[s, window - CHUNK:window, :] = kv_ref[rows, d_kv:2 * d_kv]
        k_stride = span
    else:
        @pl.when(seq_start)
        def _():
            kvar[:, 0:window, :] = jnp.zeros((2 * KV_VARIANTS, window, d_kv), BF16)
        store_kv(window, kv_ref[:, 0:d_kv], kv_ref[:, d_kv:2 * d_kv])
        sk_ref[0] = kv_ref[t_rows - window:t_rows, 0:d_kv]
        sv_ref[0] = kv_ref[t_rows - window:t_rows, d_kv:2 * d_kv]
        k_stride = CHUNK

    n_pairs = q_per_kv // 2
    key_pad_rows = jnp.zeros((KEY_PAD - span, d_kv), BF16)
    ones_d = ((lax.broadcasted_iota(I32, (2 * KEY_PAD, LANES), 0) < KEY_PAD)
              == (lax.broadcasted_iota(I32, (2 * KEY_PAD, LANES), 1) < head_dim)).astype(BF16)
    low_half = lax.broadcasted_iota(I32, (CHUNK, LANES), 1) < head_dim
    ya_parts = []
    ya_cols = d_model // (n_chunks * n_kv)
    for c in range(n_chunks):
        q_rows = slice(c * CHUNK, (c + 1) * CHUNK)
        k_rows = slice(c * k_stride, c * k_stride + span)
        masked = (not sample) and c * CHUNK < window
        if masked:
            first_key = (pl.program_id(1) * n_chunks + c) * CHUNK - window
            valid = lax.broadcasted_iota(I32, (CHUNK, KEY_PAD), 1) + first_key >= 0
        for n in range(n_kv):
            top, bot = (0, 3) if n == 0 else (2, 1)
            kd = jnp.concatenate([kvar[top, k_rows, :], key_pad_rows, kvar[bot, k_rows, :], key_pad_rows], axis=0)
            vd = jnp.concatenate([kvar[KV_VARIANTS + top, k_rows, :], key_pad_rows,
                                  kvar[KV_VARIANTS + bot, k_rows, :], key_pad_rows], axis=0)
            q4 = jnp.concatenate(
                [q_ref[q_rows, (n * n_pairs + j) * LANES:(n * n_pairs + j + 1) * LANES] for j in range(n_pairs)], axis=0)
            if scale_is_pow2:
                q4 = q4 * scale
            s4 = lax.dot_general(q4, kd, (((1,), (1,)), ((), ())), preferred_element_type=F32)
            if not scale_is_pow2:
                s4 = s4 * scale
            e_rows, sink_rows = [], []
            for j in range(n_pairs):
                e_halves, sink_halves = [], []
                for half in range(2):
                    blk = (slice(j * CHUNK, (j + 1) * CHUNK), slice(half * KEY_PAD, (half + 1) * KEY_PAD))
                    s = s4[blk] + bias_ref[n, blk[0], blk[1]]
                    if masked:
                        s = jnp.where(valid, s, NEG_INF)
                    sink = sinks_ref[0, n * q_per_kv + 2 * j + half]
                    mx = jnp.maximum(jnp.max(s, axis=-1, keepdims=True), sink)
                    e_halves.append(jnp.exp(s - mx).astype(BF16))
                    sink_halves.append(jnp.exp(sink - mx))
                e_rows.append(jnp.concatenate(e_halves, axis=1))
                sink_rows.append(jnp.where(low_half, sink_halves[0], sink_halves[1]))
            od = jnp.dot(jnp.concatenate(e_rows, axis=0), jnp.concatenate([vd, ones_d], axis=1),
                         preferred_element_type=F32)
            o4 = od[:, 0:LANES] / (od[:, LANES:2 * LANES] + jnp.concatenate(sink_rows, axis=0))
            for j in range(n_pairs):
                o_sc[q_rows, (n * n_pairs + j) * LANES:(n * n_pairs + j + 1) * LANES] = (
                    o4[j * CHUNK:(j + 1) * CHUNK].astype(BF16))
            ya_parts.append(jnp.dot(ya_sc[...], wco_ref[:, len(ya_parts) * ya_cols:(len(ya_parts) + 1) * ya_cols],
                                    preferred_element_type=F32))
    if not sample:
        kvar[:, 0:window, :] = kvar[:, t_rows:t_rows + window, :]
    y_a = jnp.concatenate(ya_parts, axis=1)
    y_b = jnp.dot(o_sc[...], wao_ref[...], preferred_element_type=F32)

    m = jax.nn.sigmoid(ga_ref[...].astype(F32)) * y_a + jax.nn.sigmoid(gb_ref[...].astype(F32)) * y_b
    xp = x_ref[...] + jnp.dot(m.astype(BF16), wo_ref[...], preferred_element_type=F32)
    xp_ref[...] = xp

    h2 = _rms_rows(xp, g2_ref[...])
    h2_ref[...] = _pack_halves(h2).reshape(t_rows, 1, d_model // 2)
    lt = lax.dot_general(wr_ref[...], h2.astype(BF16), (((1,), (1,)), ((), ())),
                         preferred_element_type=F32) + br_ref[...]
    lg = lt[0:n_groups]
    eg = jnp.exp(lg - jnp.max(lg, axis=0, keepdims=True))
    gp = eg / jnp.sum(eg, axis=0, keepdims=True)
    gw = jnp.max(gp, axis=0, keepdims=True)
    gi = lax.broadcasted_iota(I32, gp.shape, 0).astype(F32)
    gsel = jnp.min(jnp.where(gp == gw, gi, float(n_groups)), axis=0, keepdims=True)
    el = jnp.zeros((epg, t_rows), F32)
    for g in range(n_groups):
        el = jnp.where(gsel == float(g), lt[GROUP_ROW0 + g * epg:GROUP_ROW0 + (g + 1) * epg], el)
    ei = lax.broadcasted_iota(I32, el.shape, 0).astype(F32)
    v1 = jnp.max(el, axis=0, keepdims=True)
    i1 = jnp.min(jnp.where(el == v1, ei, float(epg)), axis=0, keepdims=True)
    el2 = jnp.where(ei == i1, -jnp.inf, el)
    v2 = jnp.max(el2, axis=0, keepdims=True)
    i2 = jnp.min(jnp.where(el2 == v2, ei, float(epg)), axis=0, keepdims=True)
    a1 = jnp.exp(v1 - v1)
    a2 = jnp.exp(v2 - v1)
    den = a1 + a2
    c1 = gw * (a1 / den)
    c2 = gw * (a2 / den)
    e1 = (gsel * float(epg) + i1).astype(I32)
    e2 = (gsel * float(epg) + i2).astype(I32)
    row8 = lax.broadcasted_iota(I32, (SUBLANES, t_rows), 0)
    re_ref[...] = jnp.where(row8 == 0, e1, jnp.where(row8 == 1, e2, 0))
    rc_ref[...] = jnp.where(row8 == 0, c1, jnp.where(row8 == 1, c2, 0.0))


def _mix(x2d, proj, kv, caches, weights, *, n_seq, sample, dims):
    n_heads, n_kv, head_dim, window, n_groups, n_exp, d_conv = dims
    n, d = x2d.shape
    t = MIX_ROWS
    n_chunks = t // CHUNK
    span = window + CHUNK
    d_attn = n_heads * head_dim
    d_kv = n_kv * head_dim
    convw, wco, wao, wo, bias, sinks, g2, wr, br = weights
    if sample:
        n_t = n // t
        grid = (n_t, 1)
        tok = lambda i, j: (i, 0)
        n_state = n // CHUNK
        state_blk = n_chunks
        st = lambda i, j: (i, 0, 0)
    else:
        n_t = (n // n_seq) // t
        grid = (n_seq, n_t)
        tok = lambda i, j: (i * n_t + j, 0)
        n_state = n_seq
        state_blk = 1
        st = lambda i, j: (i, 0, 0)
    const2 = lambda i, j: (0, 0)

    def col(width, idx):
        return pl.BlockSpec((t, width), lambda i, j: (tok(i, j)[0], idx))

    def resident(shape):
        return pl.BlockSpec(shape, const2, pipeline_mode=pl.Buffered(1))

    in_specs = [
        pl.BlockSpec((t, d), tok),
        col(d, 2), col(d, 3),
        col(d_conv, 0), col(d_conv, 1), col(d_conv, 2), col(d_attn, 3),
        pl.BlockSpec((t, 2 * d_kv), tok),
    ]
    args = [x2d, proj, proj, proj, proj, proj, proj, kv]
    if sample:
        cconv, ck, cv = caches
        in_specs += [
            pl.BlockSpec((n_chunks, cconv.shape[1], d_conv), st),
            pl.BlockSpec((n_chunks, window, d_kv), st),
            pl.BlockSpec((n_chunks, window, d_kv), st),
        ]
        args += [cconv, ck, cv]
    in_specs += [
        resident(convw.shape), resident(wco.shape), resident(wao.shape), resident(wo.shape),
        pl.BlockSpec(bias.shape, lambda i, j: (0, 0, 0), pipeline_mode=pl.Buffered(1)),
        pl.BlockSpec(memory_space=pltpu.SMEM),
        resident(g2.shape), resident(wr.shape), resident(br.shape),
    ]
    args += [convw, wco, wao, wo, bias, sinks, g2, wr, br]
    out_specs = [
        pl.BlockSpec((t, d), tok),
        pl.BlockSpec((t, 1, d // 2), lambda i, j: (tok(i, j)[0], 0, 0)),
        pl.BlockSpec((SUBLANES, t), lambda i, j: (0, tok(i, j)[0])),
        pl.BlockSpec((SUBLANES, t), lambda i, j: (0, tok(i, j)[0])),
        pl.BlockSpec((state_blk, 2, d_conv), st),
        pl.BlockSpec((state_blk, window, d_kv), st),
        pl.BlockSpec((state_blk, window, d_kv), st),
    ]
    out_shape = [
        jax.ShapeDtypeStruct((n, d), F32),
        jax.ShapeDtypeStruct((n, 1, d // 2), U32),
        jax.ShapeDtypeStruct((SUBLANES, n), I32),
        jax.ShapeDtypeStruct((SUBLANES, n), F32),
        jax.ShapeDtypeStruct((n_state, 2, d_conv), F32),
        jax.ShapeDtypeStruct((n_state, window, d_kv), F32),
        jax.ShapeDtypeStruct((n_state, window, d_kv), F32),
    ]
    kv_rows = n_chunks * span if sample else window + t
    scratch = [
        pltpu.VMEM((2 * KV_VARIANTS, kv_rows, d_kv), BF16),
        pltpu.VMEM((t, d_attn), BF16), pltpu.VMEM((t, d_conv), BF16),
        pltpu.VMEM((SUBLANES, d_conv), F32),
    ]
    est = (2 * t * d * 4 * 3 + 2 * 2 * t * d * 2 + 2 * 4 * t * d_conv * 2
           + (2 * d_conv * d + d * d) * 2 + 12 * t * d * 4)
    kern = functools.partial(_mix_kernel, sample=sample, n_heads=n_heads, n_kv=n_kv, head_dim=head_dim,
                             window=window, n_groups=n_groups, n_exp=n_exp)
    return pl.pallas_call(
        kern, grid=grid, in_specs=in_specs, out_specs=out_specs, out_shape=out_shape,
        scratch_shapes=scratch,
        compiler_params=pltpu.CompilerParams(
            dimension_semantics=("arbitrary", "arbitrary"), vmem_limit_bytes=_vmem_limit(est)),
        name="mix_sample" if sample else "mix_prompt",
    )(*args)


def _route_kernel(re_ref, pos_ref, tmap_ref, *, n_exp, tile_rows, blk):
    n = re_ref.shape[1]
    n_blk = n // blk
    erow = lax.broadcasted_iota(I32, (n_exp, blk), 0)

    def onehots(j):
        c0 = pl.multiple_of(j * blk, blk)
        oh0 = (erow == re_ref[0:1, pl.ds(c0, blk)]).astype(F32)
        oh1 = (erow == re_ref[1:2, pl.ds(c0, blk)]).astype(F32)
        return c0, oh0, oh1

    def count_body(j, cnt):
        _, oh0, oh1 = onehots(j)
        return cnt + jnp.sum(oh0 + oh1, axis=1, keepdims=True)

    cnt = lax.fori_loop(0, n_blk, count_body, jnp.zeros((n_exp, 1), F32))
    cnt = jnp.broadcast_to(cnt, (n_exp, LANES))
    padded = jnp.ceil(cnt / tile_rows) * tile_rows
    ends = padded
    prow = lax.broadcasted_iota(I32, ends.shape, 0)
    step = 1
    while step < n_exp:
        ends = ends + jnp.where(prow >= step, pltpu.roll(ends, step, axis=0), 0.0)
        step *= 2
    offs = ends - padded
    off1 = offs[:, 0:1]

    tri = (lax.broadcasted_iota(I32, (blk, blk), 0) <= lax.broadcasted_iota(I32, (blk, blk), 1)).astype(BF16)
    row8 = lax.broadcasted_iota(I32, (SUBLANES, blk), 0)

    def pos_body(j, run):
        c0, oh0, oh1 = onehots(j)
        both = oh0 + oh1
        csum = jnp.dot(both.astype(BF16), tri, preferred_element_type=F32) + run
        slot = off1 + csum - 1.0
        p0 = jnp.sum(oh0 * slot, axis=0, keepdims=True).astype(I32)
        p1 = jnp.sum(oh1 * slot, axis=0, keepdims=True).astype(I32)
        pos_ref[:, pl.ds(c0, blk)] = jnp.where(row8 == 0, p0, jnp.where(row8 == 1, p1, 0))
        return run + jnp.sum(both, axis=1, keepdims=True)

    lax.fori_loop(0, n_blk, pos_body, jnp.zeros((n_exp, 1), F32))

    n_tiles_pad = tmap_ref.shape[1]
    start = (lax.broadcasted_iota(I32, (1, n_tiles_pad), 1) * tile_rows).astype(F32)
    end1 = ends[:, 0:1]
    te = jnp.sum((end1 <= start).astype(F32), axis=0, keepdims=True)
    trow = lax.broadcasted_iota(I32, (n_exp, n_tiles_pad), 0).astype(F32)
    used_end = jnp.sum(jnp.where(trow == te, off1 + cnt[:, 0:1], 0.0), axis=0, keepdims=True)
    n_rows = jnp.clip(used_end - start, 0.0, float(tile_rows))
    has_tokens = cnt[:, 0:1] > 0.0
    te = jnp.minimum(te, jnp.max(jnp.where(has_tokens, trow, 0.0), axis=0, keepdims=True))
    nxt = jnp.min(jnp.where(trow > te, jnp.where(has_tokens, trow, float(n_exp)), float(n_exp)), axis=0, keepdims=True)
    r8 = lax.broadcasted_iota(I32, (SUBLANES, n_tiles_pad), 0)
    tmap_ref[...] = jnp.where(r8 == 0, te.astype(I32),
                              jnp.where(r8 == 1, n_rows.astype(I32), jnp.where(r8 == 2, nxt.astype(I32), 0)))


def _route(re, n_exp, n_tiles):
    n = re.shape[1]
    n_tiles_pad = pl.cdiv(n_tiles, LANES) * LANES
    kern = functools.partial(_route_kernel, n_exp=n_exp, tile_rows=SLOT_ROWS, blk=ROUTE_BLOCK)
    return pl.pallas_call(
        kern,
        out_shape=[jax.ShapeDtypeStruct((SUBLANES, n), I32), jax.ShapeDtypeStruct((SUBLANES, n_tiles_pad), I32)],
        name="route",
    )(re)


def _dispatch_kernel(*refs, aliased):
    if aliased:
        p0_ref, p1_ref, h_ref, _, xs_ref, sem = refs
    else:
        p0_ref, p1_ref, h_ref, xs_ref, sem = refs
    rows = h_ref.shape[0]

    def start(t, carry):
        pltpu.make_async_copy(h_ref.at[t], xs_ref.at[p0_ref[t]], sem).start(priority=0)
        pltpu.make_async_copy(h_ref.at[t], xs_ref.at[p1_ref[t]], sem).start(priority=1)
        return carry

    lax.fori_loop(0, rows, start, 0)
    for _ in range(TOP_K):
        pltpu.make_async_copy(h_ref, xs_ref.at[pl.ds(0, rows)], sem).wait()


def _dispatch(h2, p0, p1, xs, n_slots):
    n, _, d = h2.shape
    t = min(MOVE_ROWS, n)
    assert n % t == 0
    aliased = xs is not None
    in_specs = [
        pl.BlockSpec((t,), lambda i: (i,), memory_space=pltpu.SMEM),
        pl.BlockSpec((t,), lambda i: (i,), memory_space=pltpu.SMEM),
        pl.BlockSpec((t, 1, d), lambda i: (i, 0, 0)),
    ]
    args = [p0, p1, h2]
    if aliased:
        in_specs.append(pl.BlockSpec(memory_space=pl.ANY))
        args.append(xs)
    return pl.pallas_call(
        functools.partial(_dispatch_kernel, aliased=aliased),
        grid=(n // t,),
        in_specs=in_specs,
        out_specs=pl.BlockSpec(memory_space=pl.ANY),
        out_shape=jax.ShapeDtypeStruct((n_slots, 1, d), h2.dtype),
        scratch_shapes=[pltpu.SemaphoreType.DMA(())],
        input_output_aliases={3: 0} if aliased else {},
        compiler_params=pltpu.CompilerParams(dimension_semantics=("arbitrary",), has_side_effects=True),
        name="dispatch",
    )(*args)


def _experts_kernel(te_ref, nr_ref, nx_ref, xs_ref, wg_hbm, wu_hbm, wd_hbm, y_ref,
                    wg_st, wu_st, wd_st, wg_sc, wu_sc, wd_sc, sems, n_changes, *, n_exp, tile_rows):
    dh = xs_ref.shape[2]
    tiles_per_step = xs_ref.shape[0] // tile_rows

    def copies(e, slot):
        return (pltpu.make_async_copy(wg_hbm.at[e], wg_st.at[slot], sems.at[slot]),
                pltpu.make_async_copy(wu_hbm.at[e], wu_st.at[slot], sems.at[slot]),
                pltpu.make_async_copy(wd_hbm.at[e], wd_st.at[slot], sems.at[slot]))

    def one_tile(sub):
        tile = pl.program_id(0) * tiles_per_step + sub
        rows = pl.ds(sub * tile_rows, tile_rows)
        expert = te_ref[tile]

        @pl.when(tile == 0)
        def _():
            n_changes[0] = 0
            for cp in copies(expert, 0):
                cp.start(priority=1)

        @pl.when(jnp.logical_or(tile == 0, expert != te_ref[jnp.maximum(tile - 1, 0)]))
        def _():
            slot = lax.rem(n_changes[0], 2)
            n_changes[0] = n_changes[0] + 1
            for cp in copies(expert, slot):
                cp.wait()
            nxt = nx_ref[tile]

            @pl.when(nxt < n_exp)
            def _():
                for cp in copies(nxt, 1 - slot):
                    cp.start(priority=1)

            wg_sc[...] = wg_st[slot].astype(BF16)
            wu_sc[...] = wu_st[slot].astype(BF16)
            wd_sc[...] = wd_st[slot].astype(BF16)

        n_rows = nr_ref[tile]

        @pl.when(n_rows > 0)
        def _():
            live = lax.broadcasted_iota(I32, (tile_rows, 2 * dh), 0) < n_rows
            x = jnp.where(live, _unpack_halves(xs_ref[rows].reshape(tile_rows, dh)), 0.0).astype(BF16)
            gate = jnp.dot(x, wg_sc[...], preferred_element_type=F32)
            up = jnp.dot(x, wu_sc[...], preferred_element_type=F32)
            hid = (jax.nn.silu(gate) * up).astype(BF16)
            y = jnp.dot(hid, wd_sc[...], preferred_element_type=F32)
            y_ref[rows] = _pack_halves(y).reshape(tile_rows, 1, dh)

        @pl.when(n_rows <= 0)
        def _():
            y_ref[rows] = _pack_halves(jnp.zeros((tile_rows, 2 * dh), F32)).reshape(tile_rows, 1, dh)

    for sub in range(tiles_per_step):
        one_tile(sub)


def _experts(xs, te, nr, nx, w_gate, w_up, w_down):
    n_slots, _, dh = xs.shape
    n_exp, d, d_e = w_gate.shape
    t = SLOT_ROWS * EXPERT_STEP_TILES
    est = 2 * 2 * t * d * 2 + 2 * 3 * d * d_e * 4 + 3 * d * d_e * 2 + 4 * SLOT_ROWS * d * 4
    row_spec = pl.BlockSpec((t, 1, dh), lambda i, te, nr, nx: (i, 0, 0))
    return pl.pallas_call(
        functools.partial(_experts_kernel, n_exp=n_exp, tile_rows=SLOT_ROWS),
        grid_spec=pltpu.PrefetchScalarGridSpec(
            num_scalar_prefetch=3,
            grid=(n_slots // t,),
            in_specs=[row_spec, pl.BlockSpec(memory_space=pl.ANY), pl.BlockSpec(memory_space=pl.ANY),
                      pl.BlockSpec(memory_space=pl.ANY)],
            out_specs=row_spec,
            scratch_shapes=[
                pltpu.VMEM((2, d, d_e), F32), pltpu.VMEM((2, d, d_e), F32), pltpu.VMEM((2, d_e, d), F32),
                pltpu.VMEM((d, d_e), BF16), pltpu.VMEM((d, d_e), BF16), pltpu.VMEM((d_e, d), BF16),
                pltpu.SemaphoreType.DMA((2,)), pltpu.SMEM((1,), I32)],
        ),
        out_shape=jax.ShapeDtypeStruct((n_slots, 1, dh), U32),
        compiler_params=pltpu.CompilerParams(
            dimension_semantics=("arbitrary",), vmem_limit_bytes=_vmem_limit(est)),
        name="experts",
    )(te, nr, nx, xs, w_gate, w_up, w_down)


def _combine_kernel(p0_ref, p1_ref, p0n_ref, p1n_ref, xp_ref, rc_ref, g_ref, y_hbm, out_ref, y0_buf, y1_buf, sems,
                    *, n_tiles):
    i = pl.program_id(0)
    rows, d = xp_ref.shape

    def gather(pa_ref, pb_ref, slot):
        base = slot * rows
        def body(t, carry):
            pltpu.make_async_copy(y_hbm.at[pa_ref[t]], y0_buf.at[base + t], sems.at[slot]).start(priority=0)
            pltpu.make_async_copy(y_hbm.at[pb_ref[t]], y1_buf.at[base + t], sems.at[slot]).start(priority=1)
            return carry
        lax.fori_loop(0, rows, body, 0)

    slot = lax.rem(i, 2)

    @pl.when(i == 0)
    def _():
        gather(p0_ref, p1_ref, 0)

    @pl.when(i + 1 < n_tiles)
    def _():
        gather(p0n_ref, p1n_ref, 1 - slot)

    cur = pl.ds(pl.multiple_of(slot * rows, rows), rows)
    for buf in (y0_buf, y1_buf):
        pltpu.make_async_copy(y_hbm.at[pl.ds(0, rows)], buf.at[cur], sems.at[slot]).wait()
    ct = rc_ref[...].T
    y0 = _unpack_halves(y0_buf[cur].reshape(rows, d // 2))
    y1 = _unpack_halves(y1_buf[cur].reshape(rows, d // 2))
    moe = ct[:, 0:1] * y0 + ct[:, 1:2] * y1
    out_ref[...] = _rms_rows(xp_ref[...] + moe, g_ref[...])


def _combine(xp, rc, p0, p1, y, g):
    n, d = xp.shape
    t = min(MOVE_ROWS, n)
    assert n % t == 0
    n_t = n // t
    est = 2 * 2 * t * d * 4 + 2 * 2 * t * d * 2 + 8 * t * d * 4
    nxt = lambda i: (jnp.minimum(i + 1, n_t - 1),)
    return pl.pallas_call(
        functools.partial(_combine_kernel, n_tiles=n_t),
        grid=(n_t,),
        in_specs=[
            pl.BlockSpec((t,), lambda i: (i,), memory_space=pltpu.SMEM),
            pl.BlockSpec((t,), lambda i: (i,), memory_space=pltpu.SMEM),
            pl.BlockSpec((t,), nxt, memory_space=pltpu.SMEM),
            pl.BlockSpec((t,), nxt, memory_space=pltpu.SMEM),
            pl.BlockSpec((t, d), lambda i: (i, 0)),
            pl.BlockSpec((SUBLANES, t), lambda i: (0, i)),
            pl.BlockSpec((1, d), lambda i: (0, 0)),
            pl.BlockSpec(memory_space=pl.ANY),
        ],
        out_specs=pl.BlockSpec((t, d), lambda i: (i, 0)),
        out_shape=jax.ShapeDtypeStruct((n, d), F32),
        scratch_shapes=[pltpu.VMEM((2 * t, 1, d // 2), U32), pltpu.VMEM((2 * t, 1, d // 2), U32),
                        pltpu.SemaphoreType.DMA((2,))],
        compiler_params=pltpu.CompilerParams(
            dimension_semantics=("arbitrary",), vmem_limit_bytes=_vmem_limit(est)),
        name="combine",
    )(p0, p1, p0, p1, xp, rc, g, y)


def _rel_buckets(window):
    rel = (jnp.arange(window + CHUNK, dtype=I32) - window)[None, :] - jnp.arange(CHUNK, dtype=I32)[:, None]
    nb = N_BUCKETS // 2
    ret = (rel > 0).astype(I32) * nb
    n = jnp.abs(rel)
    max_exact = nb // 2
    nf = jnp.maximum(n, 1).astype(F32)
    large = max_exact + (jnp.log(nf / max_exact) / math.log(MAX_DISTANCE / max_exact)
                         * (nb - max_exact)).astype(I32)
    large = jnp.minimum(large, nb - 1)
    return ret + jnp.where(n < max_exact, n, large)


def _rel_bias(table, window, n_kv):
    n_heads = table.shape[1]
    bias = jnp.transpose(table[_rel_buckets(window)].astype(F32), (2, 0, 1))
    bias = jnp.pad(bias, ((0, 0), (0, 0), (0, KEY_PAD - bias.shape[2])), constant_values=NEG_INF)
    bias = bias.reshape(n_kv, n_heads // n_kv // 2, 2, CHUNK, KEY_PAD)
    return jnp.transpose(bias, (0, 1, 3, 2, 4)).reshape(n_kv, (n_heads // n_kv // 2) * CHUNK, 2 * KEY_PAD)


def kernel(x_prompt, x_sample, cache_conv, cache_k, cache_v, rel_bias_table, norm_mix_g, w_in, conv_w, w_conv_out, attn_sinks, w_attn_out, w_o, norm_ffn_g, w_group, b_group, w_expert_router, b_expert_router, w_gate, w_up, w_down, final_norm_g):
    assert w_in.shape[0] == 1, "single-layer step"
    batch, seq, d = x_prompt.shape
    dec_batch, dec_seq, _ = x_sample.shape
    assert dec_seq == CHUNK and seq % MIX_ROWS == 0 and (dec_batch * dec_seq) % MIX_ROWS == 0
    d_conv = conv_w.shape[-1]
    window, n_kv, head_dim = cache_k.shape[2], cache_k.shape[3], cache_k.shape[4]
    n_heads = attn_sinks.shape[-1]
    d_attn, d_kv = n_heads * head_dim, n_kv * head_dim
    n_groups, n_exp = w_group.shape[-1], w_expert_router.shape[-1]
    assert n_groups <= GROUP_ROW0 and d_conv == d_attn and 2 * d_conv == d
    assert n_kv == 2 and d_kv == LANES and (n_heads // n_kv) % 2 == 0 and window + CHUNK <= KEY_PAD
    dims = (n_heads, n_kv, head_dim, window, n_groups, n_exp, d_conv)

    w_all = w_in[0].astype(BF16)
    kv0 = 3 * d_conv + d_attn
    g1 = norm_mix_g[0][None, :]
    wr = jnp.zeros((GROUP_ROW0 + n_exp, d), F32)
    wr = wr.at[:n_groups].set(w_group[0].T).at[GROUP_ROW0:].set(w_expert_router[0].T).astype(BF16)
    br = jnp.zeros((GROUP_ROW0 + n_exp, 1), F32)
    br = br.at[:n_groups, 0].set(b_group[0]).at[GROUP_ROW0:, 0].set(b_expert_router[0])
    weights = (conv_w[0], w_conv_out[0].astype(BF16), w_attn_out[0].astype(BF16), w_o[0].astype(BF16),
               _rel_bias(rel_bias_table, window, n_kv), attn_sinks, norm_ffn_g[0][None, :], wr, br)

    xp2d = x_prompt.reshape(batch * seq, d)
    xs2d = x_sample.reshape(dec_batch * dec_seq, d)
    n_p, n_s = xp2d.shape[0], xs2d.shape[0]
    n_tok = n_p + n_s

    proj_p, kv_p = _inproj(xp2d, g1, w_all, kv0, 2 * d_kv)
    proj_s, kv_s = _inproj(xs2d, g1, w_all, kv0, 2 * d_kv)
    xres_p, h2_p, re_p, rc_p, conv_p, k_p, v_p = _mix(
        xp2d, proj_p, kv_p, None, weights, n_seq=batch, sample=False, dims=dims)
    caches = (cache_conv[0], cache_k[0].reshape(dec_batch, window, d_kv), cache_v[0].reshape(dec_batch, window, d_kv))
    xres_s, h2_s, re_s, rc_s, conv_s, k_s, v_s = _mix(
        xs2d, proj_s, kv_s, caches, weights, n_seq=dec_batch, sample=True, dims=dims)

    n_tiles = pl.cdiv((TOP_K * n_tok) // SLOT_ROWS + n_exp, EXPERT_STEP_TILES) * EXPERT_STEP_TILES
    n_slots = n_tiles * SLOT_ROWS
    pos, tmap = _route(jnp.concatenate([re_p, re_s], axis=1), n_exp, n_tiles)
    p0, p1 = pos[0], pos[1]
    xs = _dispatch(h2_p, p0[:n_p], p1[:n_p], None, n_slots)
    xs = _dispatch(h2_s, p0[n_p:], p1[n_p:], xs, n_slots)
    y = _experts(xs, tmap[0, :n_tiles], tmap[1, :n_tiles], tmap[2, :n_tiles], w_gate[0], w_up[0], w_down[0])
    gf = final_norm_g[None, :]
    y_prompt = _combine(xres_p, rc_p, p0[:n_p], p1[:n_p], y, gf).reshape(batch, seq, d)
    y_sample = _combine(xres_s, rc_s, p0[n_p:], p1[n_p:], y, gf).reshape(dec_batch, dec_seq, d)

    kv_shape = (1, -1, window, n_kv, head_dim)
    return (y_prompt, y_sample, conv_p[None], k_p.reshape(kv_shape), v_p.reshape(kv_shape),
            conv_s[None], k_s.reshape(kv_shape), v_s.reshape(kv_shape))
```

```python
import functools
import math

import jax
import jax.numpy as jnp
from jax import lax
from jax.experimental import pallas as pl
from jax.experimental.pallas import tpu as pltpu

F32, BF16, I32, U32 = jnp.float32, jnp.bfloat16, jnp.int32, jnp.uint32

CHUNK = 64
N_BUCKETS = 32
MAX_DISTANCE = 128
EPS = 1e-6
NEG_INF = -1e30
TOP_K = 2

V7X_VMEM_BYTES = 64 * 1024 * 1024
SUBLANES = 8
LANES = 128

INPROJ_ROWS = 1024
INPROJ_COLS = 2048
MIX_ROWS = 256
MOVE_ROWS = 512
SLOT_ROWS = 256
EXPERT_STEP_TILES = 2
ROUTE_BLOCK = 256
GROUP_ROW0 = 8
KEY_PAD = 256
KV_VARIANTS = 4


def _vmem_limit(nbytes):
    return int(min(V7X_VMEM_BYTES - (4 << 20), max(nbytes, 32 << 20)))


def _pack_halves(x):
    half = x.shape[1] // 2
    return pltpu.pack_elementwise([x[:, :half], x[:, half:]], packed_dtype=BF16)


def _unpack_halves(w):
    lo = pltpu.unpack_elementwise(w, index=0, packed_dtype=BF16, unpacked_dtype=F32)
    hi = pltpu.unpack_elementwise(w, index=1, packed_dtype=BF16, unpacked_dtype=F32)
    return jnp.concatenate([lo, hi], axis=1)


def _rms_rows(x, g):
    r = lax.rsqrt(jnp.mean(x * x, axis=-1, keepdims=True) + EPS)
    return (x * r) * g


def _inproj_kernel(x_ref, g_ref, wm_ref, wkv_ref, proj_ref, kv_ref, h_sc):
    @pl.when(pl.program_id(1) == 0)
    def _():
        rows = 128
        def body(i, carry):
            r0 = pl.multiple_of(i * rows, rows)
            h_sc[pl.ds(r0, rows), :] = _rms_rows(x_ref[pl.ds(r0, rows), :], g_ref[...]).astype(BF16)
            return carry
        lax.fori_loop(0, x_ref.shape[0] // rows, body, 0)
        kv_ref[...] = jnp.dot(h_sc[...], wkv_ref[...], preferred_element_type=F32)

    proj_ref[...] = jnp.dot(h_sc[...], wm_ref[...], preferred_element_type=F32).astype(BF16)


def _inproj(x2d, g, w_all, kv0, n_kv):
    n, d = x2d.shape
    tm = min(INPROJ_ROWS, n)
    tn = INPROJ_COLS
    n_main = w_all.shape[1] - n_kv
    assert kv0 % tn == 0 and n_main % tn == 0 and n_kv % LANES == 0
    est = 2 * tm * d * 4 + tm * d * 2 + 2 * d * tn * 2 + 2 * d * n_kv * 2 + 2 * tm * tn * 2 + 2 * tm * n_kv * 4 + tm * tn * 4
    return pl.pallas_call(
        _inproj_kernel,
        grid=(n // tm, n_main // tn),
        in_specs=[
            pl.BlockSpec((tm, d), lambda i, j: (i, 0)),
            pl.BlockSpec((1, d), lambda i, j: (0, 0)),
            pl.BlockSpec((pl.Element(d), pl.Element(tn)),
                         lambda i, j: (0, pl.multiple_of(jnp.where(j * tn < kv0, j * tn, j * tn + n_kv), LANES))),
            pl.BlockSpec((pl.Element(d), pl.Element(n_kv)), lambda i, j: (0, kv0)),
        ],
        out_specs=[
            pl.BlockSpec((tm, tn), lambda i, j: (i, j)),
            pl.BlockSpec((tm, n_kv), lambda i, j: (i, 0)),
        ],
        out_shape=[jax.ShapeDtypeStruct((n, n_main), BF16), jax.ShapeDtypeStruct((n, n_kv), F32)],
        scratch_shapes=[pltpu.VMEM((tm, d), BF16)],
        compiler_params=pltpu.CompilerParams(
            dimension_semantics=("arbitrary", "arbitrary"), vmem_limit_bytes=_vmem_limit(est + (8 << 20))),
        name="inproj",
    )(x2d, g, w_all, w_all)


def _conv_rows(u, prev2, prev1, w):
    row = lax.broadcasted_iota(I32, u.shape, 0)
    u1 = jnp.where(row == 0, prev1, pltpu.roll(u, 1, axis=0))
    u2 = jnp.where(row == 0, prev2, jnp.where(row == 1, prev1, pltpu.roll(u, 2, axis=0)))
    return (w[0:1] * u2 + w[1:2] * u1) + w[2:3] * u


def _mix_kernel(*refs, sample, n_heads, n_kv, head_dim, window, n_groups, n_exp):
    if sample:
        (x_ref, ga_ref, gb_ref, b_ref, c_ref, xc_ref, q_ref, kv_ref, cconv_ref, ck_ref, cv_ref,
         convw_ref, wco_ref, wao_ref, wo_ref, bias_ref, sinks_ref, g2_ref, wr_ref, br_ref,
         xp_ref, h2_ref, re_ref, rc_ref, sconv_ref, sk_ref, sv_ref,
         kvar, o_sc, ya_sc, carry_u) = refs
    else:
        (x_ref, ga_ref, gb_ref, b_ref, c_ref, xc_ref, q_ref, kv_ref,
         convw_ref, wco_ref, wao_ref, wo_ref, bias_ref, sinks_ref, g2_ref, wr_ref, br_ref,
         xp_ref, h2_ref, re_ref, rc_ref, sconv_ref, sk_ref, sv_ref,
         kvar, o_sc, ya_sc, carry_u) = refs
    t_rows, d_model = x_ref.shape
    n_chunks = t_rows // CHUNK
    span = window + CHUNK
    q_per_kv = n_heads // n_kv
    d_kv = n_kv * head_dim
    scale = 1.0 / math.sqrt(head_dim)
    scale_is_pow2 = math.frexp(scale)[0] == 0.5
    epg = n_exp // n_groups
    seq_start = pl.program_id(1) == 0

    w_conv = convw_ref[...]
    if sample:
        for s in range(n_chunks):
            rows = slice(s * CHUNK, (s + 1) * CHUNK)
            u = c_ref[rows, :].astype(F32) * xc_ref[rows, :].astype(F32)
            y = _conv_rows(u, cconv_ref[s, 0:1, :], cconv_ref[s, 1:2, :], w_conv)
            ya_sc[rows, :] = (b_ref[rows, :].astype(F32) * y).astype(BF16)
            sconv_ref[s] = u[CHUNK - 2:CHUNK, :]
    else:
        u = c_ref[...].astype(F32) * xc_ref[...].astype(F32)
        prev = jnp.where(seq_start, 0.0, carry_u[...])
        y = _conv_rows(u, prev[SUBLANES - 2:SUBLANES - 1], prev[SUBLANES - 1:SUBLANES], w_conv)
        ya_sc[...] = (b_ref[...].astype(F32) * y).astype(BF16)
        carry_u[...] = u[t_rows - SUBLANES:t_rows, :]
        sconv_ref[0] = u[t_rows - 2:t_rows, :]

    def store_kv(row0, k_rows, v_rows):
        n_rows = k_rows.shape[0]
        low = lax.broadcasted_iota(I32, k_rows.shape, 1) < head_dim
        for base, a in ((0, k_rows), (KV_VARIANTS, v_rows)):
            b = pltpu.roll(a, head_dim, axis=1)
            kvar[base + 0, row0:row0 + n_rows, :] = jnp.where(low, a, 0.0).astype(BF16)
            kvar[base + 1, row0:row0 + n_rows, :] = jnp.where(low, 0.0, a).astype(BF16)
            kvar[base + 2, row0:row0 + n_rows, :] = jnp.where(low, b, 0.0).astype(BF16)
            kvar[base + 3, row0:row0 + n_rows, :] = jnp.where(low, 0.0, b).astype(BF16)

    if sample:
        for s in range(n_chunks):
            rows = slice(s * CHUNK, (s + 1) * CHUNK)
            store_kv(s * span, ck_ref[s], cv_ref[s])
            store_kv(s * span + window, kv_ref[rows, 0:d_kv], kv_ref[rows, d_kv:2 * d_kv])
            sk_ref[s, 0:window - CHUNK, :] = ck_ref[s, CHUNK:window, :]
            sv_ref[s, 0:window - CHUNK, :] = cv_ref[s, CHUNK:window, :]
            sk_ref[s, window - CHUNK:window, :] = kv_ref[rows, 0:d_kv]
            sv_ref[s, window - CHUNK:window, :] = kv_ref[rows, d_kv:2 * d_kv]
        k_stride = span
    else:
        @pl.when(seq_start)
        def _():
            kvar[:, 0:window, :] = jnp.zeros((2 * KV_VARIANTS, window, d_kv), BF16)
        store_kv(window, kv_ref[:, 0:d_kv], kv_ref[:, d_kv:2 * d_kv])
        sk_ref[0] = kv_ref[t_rows - window:t_rows, 0:d_kv]
        sv_ref[0] = kv_ref[t_rows - window:t_rows, d_kv:2 * d_kv]
        k_stride = CHUNK

    n_pairs = q_per_kv // 2
    key_pad_rows = jnp.zeros((KEY_PAD - span, d_kv), BF16)
    ones_d = ((lax.broadcasted_iota(I32, (2 * KEY_PAD, LANES), 0) < KEY_PAD)
              == (lax.broadcasted_iota(I32, (2 * KEY_PAD, LANES), 1) < head_dim)).astype(BF16)
    low_half = lax.broadcasted_iota(I32, (CHUNK, LANES), 1) < head_dim
    ya_parts = []
    ya_cols = d_model // (n_chunks * n_kv)
    for c in range(n_chunks):
        q_rows = slice(c * CHUNK, (c + 1) * CHUNK)
        k_rows = slice(c * k_stride, c * k_stride + span)
        masked = (not sample) and c * CHUNK < window
        if masked:
            first_key = (pl.program_id(1) * n_chunks + c) * CHUNK - window
            valid = lax.broadcasted_iota(I32, (CHUNK, KEY_PAD), 1) + first_key >= 0
        for n in range(n_kv):
            top, bot = (0, 3) if n == 0 else (2, 1)
            kd = jnp.concatenate([kvar[top, k_rows, :], key_pad_rows, kvar[bot, k_rows, :], key_pad_rows], axis=0)
            vd = jnp.concatenate([kvar[KV_VARIANTS + top, k_rows, :], key_pad_rows,
                                  kvar[KV_VARIANTS + bot, k_rows, :], key_pad_rows], axis=0)
            q4 = jnp.concatenate(
                [q_ref[q_rows, (n * n_pairs + j) * LANES:(n * n_pairs + j + 1) * LANES] for j in range(n_pairs)], axis=0)
            if scale_is_pow2:
                q4 = q4 * scale
            s4 = lax.dot_general(q4, kd, (((1,), (1,)), ((), ())), preferred_element_type=F32)
            if not scale_is_pow2:
                s4 = s4 * scale
            e_rows, sink_rows = [], []
            for j in range(n_pairs):
                e_halves, sink_halves = [], []
                for half in range(2):
                    blk = (slice(j * CHUNK, (j + 1) * CHUNK), slice(half * KEY_PAD, (half + 1) * KEY_PAD))
                    s = s4[blk] + bias_ref[n, blk[0], blk[1]]
                    if masked:
                        s = jnp.where(valid, s, NEG_INF)
                    sink = sinks_ref[0, n * q_per_kv + 2 * j + half]
                    mx = jnp.maximum(jnp.max(s, axis=-1, keepdims=True), sink)
                    e_halves.append(jnp.exp(s - mx).astype(BF16))
                    sink_halves.append(jnp.exp(sink - mx))
                e_rows.append(jnp.concatenate(e_halves, axis=1))
                sink_rows.append(jnp.where(low_half, sink_halves[0], sink_halves[1]))
            od = jnp.dot(jnp.concatenate(e_rows, axis=0), jnp.concatenate([vd, ones_d], axis=1),
                         preferred_element_type=F32)
            o4 = od[:, 0:LANES] / (od[:, LANES:2 * LANES] + jnp.concatenate(sink_rows, axis=0))
            for j in range(n_pairs):
                o_sc[q_rows, (n * n_pairs + j) * LANES:(n * n_pairs + j + 1) * LANES] = (
                    o4[j * CHUNK:(j + 1) * CHUNK].astype(BF16))
            ya_parts.append(jnp.dot(ya_sc[...], wco_ref[:, len(ya_parts) * ya_cols:(len(ya_parts) + 1) * ya_cols],
                                    preferred_element_type=F32))
    if not sample:
        kvar[:, 0:window, :] = kvar[:, t_rows:t_rows + window, :]
    y_a = jnp.concatenate(ya_parts, axis=1)
    y_b = jnp.dot(o_sc[...], wao_ref[...], preferred_element_type=F32)

    m = jax.nn.sigmoid(ga_ref[...].astype(F32)) * y_a + jax.nn.sigmoid(gb_ref[...].astype(F32)) * y_b
    xp = x_ref[...] + jnp.dot(m.astype(BF16), wo_ref[...], preferred_element_type=F32)
    xp_ref[...] = xp

    h2 = _rms_rows(xp, g2_ref[...])
    h2_ref[...] = _pack_halves(h2).reshape(t_rows, 1, d_model // 2)
    lt = lax.dot_general(wr_ref[...], h2.astype(BF16), (((1,), (1,)), ((), ())),
                         preferred_element_type=F32) + br_ref[...]
    lg = lt[0:n_groups]
    eg = jnp.exp(lg - jnp.max(lg, axis=0, keepdims=True))
    gp = eg / jnp.sum(eg, axis=0, keepdims=True)
    gw = jnp.max(gp, axis=0, keepdims=True)
    gi = lax.broadcasted_iota(I32, gp.shape, 0).astype(F32)
    gsel = jnp.min(jnp.where(gp == gw, gi, float(n_groups)), axis=0, keepdims=True)
    el = jnp.zeros((epg, t_rows), F32)
    for g in range(n_groups):
        el = jnp.where(gsel == float(g), lt[GROUP_ROW0 + g * epg:GROUP_ROW0 + (g + 1) * epg], el)
    ei = lax.broadcasted_iota(I32, el.shape, 0).astype(F32)
    v1 = jnp.max(el, axis=0, keepdims=True)
    i1 = jnp.min(jnp.where(el == v1, ei, float(epg)), axis=0, keepdims=True)
    el2 = jnp.where(ei == i1, -jnp.inf, el)
    v2 = jnp.max(el2, axis=0, keepdims=True)
    i2 = jnp.min(jnp.where(el2 == v2, ei, float(epg)), axis=0, keepdims=True)
    a1 = jnp.exp(v1 - v1)
    a2 = jnp.exp(v2 - v1)
    den = a1 + a2
    c1 = gw * (a1 / den)
    c2 = gw * (a2 / den)
    e1 = (gsel * float(epg) + i1).astype(I32)
    e2 = (gsel * float(epg) + i2).astype(I32)
    row8 = lax.broadcasted_iota(I32, (SUBLANES, t_rows), 0)
    re_ref[...] = jnp.where(row8 == 0, e1, jnp.where(row8 == 1, e2, 0))
    rc_ref[...] = jnp.where(row8 == 0, c1, jnp.where(row8 == 1, c2, 0.0))


def _mix(x2d, proj, kv, caches, weights, *, n_seq, sample, dims):
    n_heads, n_kv, head_dim, window, n_groups, n_exp, d_conv = dims
    n, d = x2d.shape
    t = MIX_ROWS
    n_chunks = t // CHUNK
    span = window + CHUNK
    d_attn = n_heads * head_dim
    d_kv = n_kv * head_dim
    convw, wco, wao, wo, bias, sinks, g2, wr, br = weights
    if sample:
        n_t = n // t
        grid = (n_t, 1)
        tok = lambda i, j: (i, 0)
        n_state = n // CHUNK
        state_blk = n_chunks
        st = lambda i, j: (i, 0, 0)
    else:
        n_t = (n // n_seq) // t
        grid = (n_seq, n_t)
        tok = lambda i, j: (i * n_t + j, 0)
        n_state = n_seq
        state_blk = 1
        st = lambda i, j: (i, 0, 0)
    const2 = lambda i, j: (0, 0)

    def col(width, idx):
        return pl.BlockSpec((t, width), lambda i, j: (tok(i, j)[0], idx))

    def resident(shape):
        return pl.BlockSpec(shape, const2, pipeline_mode=pl.Buffered(1))

    in_specs = [
        pl.BlockSpec((t, d), tok),
        col(d, 2), col(d, 3),
        col(d_conv, 0), col(d_conv, 1), col(d_conv, 2), col(d_attn, 3),
        pl.BlockSpec((t, 2 * d_kv), tok),
    ]
    args = [x2d, proj, proj, proj, proj, proj, proj, kv]
    if sample:
        cconv, ck, cv = caches
        in_specs += [
            pl.BlockSpec((n_chunks, cconv.shape[1], d_conv), st),
            pl.BlockSpec((n_chunks, window, d_kv), st),
            pl.BlockSpec((n_chunks, window, d_kv), st),
        ]
        args += [cconv, ck, cv]
    in_specs += [
        resident(convw.shape), resident(wco.shape), resident(wao.shape), resident(wo.shape),
        pl.BlockSpec(bias.shape, lambda i, j: (0, 0, 0), pipeline_mode=pl.Buffered(1)),
        pl.BlockSpec(memory_space=pltpu.SMEM),
        resident(g2.shape), resident(wr.shape), resident(br.shape),
    ]
    args += [convw, wco, wao, wo, bias, sinks, g2, wr, br]
    out_specs = [
        pl.BlockSpec((t, d), tok),
        pl.BlockSpec((t, 1, d // 2), lambda i, j: (tok(i, j)[0], 0, 0)),
        pl.BlockSpec((SUBLANES, t), lambda i, j: (0, tok(i, j)[0])),
        pl.BlockSpec((SUBLANES, t), lambda i, j: (0, tok(i, j)[0])),
        pl.BlockSpec((state_blk, 2, d_conv), st),
        pl.BlockSpec((state_blk, window, d_kv), st),
        pl.BlockSpec((state_blk, window, d_kv), st),
    ]
    out_shape = [
        jax.ShapeDtypeStruct((n, d), F32),
        jax.ShapeDtypeStruct((n, 1, d // 2), U32),
        jax.ShapeDtypeStruct((SUBLANES, n), I32),
        jax.ShapeDtypeStruct((SUBLANES, n), F32),
        jax.ShapeDtypeStruct((n_state, 2, d_conv), F32),
        jax.ShapeDtypeStruct((n_state, window, d_kv), F32),
        jax.ShapeDtypeStruct((n_state, window, d_kv), F32),
    ]
    kv_rows = n_chunks * span if sample else window + t
    scratch = [
        pltpu.VMEM((2 * KV_VARIANTS, kv_rows, d_kv), BF16),
        pltpu.VMEM((t, d_attn), BF16), pltpu.VMEM((t, d_conv), BF16),
        pltpu.VMEM((SUBLANES, d_conv), F32),
    ]
    est = (2 * t * d * 4 * 3 + 2 * 2 * t * d * 2 + 2 * 4 * t * d_conv * 2
           + (2 * d_conv * d + d * d) * 2 + 12 * t * d * 4)
    kern = functools.partial(_mix_kernel, sample=sample, n_heads=n_heads, n_kv=n_kv, head_dim=head_dim,
                             window=window, n_groups=n_groups, n_exp=n_exp)
    return pl.pallas_call(
        kern, grid=grid, in_specs=in_specs, out_specs=out_specs, out_shape=out_shape,
        scratch_shapes=scratch,
        compiler_params=pltpu.CompilerParams(
            dimension_semantics=("arbitrary", "arbitrary"), vmem_limit_bytes=_vmem_limit(est)),
        name="mix_sample" if sample else "mix_prompt",
    )(*args)


def _route_kernel(re_ref, pos_ref, tmap_ref, *, n_exp, tile_rows, blk):
    n = re_ref.shape[1]
    n_blk = n // blk
    erow = lax.broadcasted_iota(I32, (n_exp, blk), 0)

    def onehots(j):
        c0 = pl.multiple_of(j * blk, blk)
        oh0 = (erow == re_ref[0:1, pl.ds(c0, blk)]).astype(F32)
        oh1 = (erow == re_ref[1:2, pl.ds(c0, blk)]).astype(F32)
        return c0, oh0, oh1

    def count_body(j, cnt):
        _, oh0, oh1 = onehots(j)
        return cnt + jnp.sum(oh0 + oh1, axis=1, keepdims=True)

    cnt = lax.fori_loop(0, n_blk, count_body, jnp.zeros((n_exp, 1), F32))
    cnt = jnp.broadcast_to(cnt, (n_exp, LANES))
    padded = jnp.ceil(cnt / tile_rows) * tile_rows
    ends = padded
    prow = lax.broadcasted_iota(I32, ends.shape, 0)
    step = 1
    while step < n_exp:
        ends = ends + jnp.where(prow >= step, pltpu.roll(ends, step, axis=0), 0.0)
        step *= 2
    offs = ends - padded
    off1 = offs[:, 0:1]

    tri = (lax.broadcasted_iota(I32, (blk, blk), 0) <= lax.broadcasted_iota(I32, (blk, blk), 1)).astype(BF16)
    row8 = lax.broadcasted_iota(I32, (SUBLANES, blk), 0)

    def pos_body(j, run):
        c0, oh0, oh1 = onehots(j)
        both = oh0 + oh1
        csum = jnp.dot(both.astype(BF16), tri, preferred_element_type=F32) + run
        slot = off1 + csum - 1.0
        p0 = jnp.sum(oh0 * slot, axis=0, keepdims=True).astype(I32)
        p1 = jnp.sum(oh1 * slot, axis=0, keepdims=True).astype(I32)
        pos_ref[:, pl.ds(c0, blk)] = jnp.where(row8 == 0, p0, jnp.where(row8 == 1, p1, 0))
        return run + jnp.sum(both, axis=1, keepdims=True)

    lax.fori_loop(0, n_blk, pos_body, jnp.zeros((n_exp, 1), F32))

    n_tiles_pad = tmap_ref.shape[1]
    start = (lax.broadcasted_iota(I32, (1, n_tiles_pad), 1) * tile_rows).astype(F32)
    end1 = ends[:, 0:1]
    te = jnp.sum((end1 <= start).astype(F32), axis=0, keepdims=True)
    trow = lax.broadcasted_iota(I32, (n_exp, n_tiles_pad), 0).astype(F32)
    used_end = jnp.sum(jnp.where(trow == te, off1 + cnt[:, 0:1], 0.0), axis=0, keepdims=True)
    n_rows = jnp.clip(used_end - start, 0.0, float(tile_rows))
    has_tokens = cnt[:, 0:1] > 0.0
    te = jnp.minimum(te, jnp.max(jnp.where(has_tokens, trow, 0.0), axis=0, keepdims=True))
    nxt = jnp.min(jnp.where(trow > te, jnp.where(has_tokens, trow, float(n_exp)), float(n_exp)), axis=0, keepdims=True)
    r8 = lax.broadcasted_iota(I32, (SUBLANES, n_tiles_pad), 0)
    tmap_ref[...] = jnp.where(r8 == 0, te.astype(I32),
                              jnp.where(r8 == 1, n_rows.astype(I32), jnp.where(r8 == 2, nxt.astype(I32), 0)))


def _route(re, n_exp, n_tiles):
    n = re.shape[1]
    n_tiles_pad = pl.cdiv(n_tiles, LANES) * LANES
    kern = functools.partial(_route_kernel, n_exp=n_exp, tile_rows=SLOT_ROWS, blk=ROUTE_BLOCK)
    return pl.pallas_call(
        kern,
        out_shape=[jax.ShapeDtypeStruct((SUBLANES, n), I32), jax.ShapeDtypeStruct((SUBLANES, n_tiles_pad), I32)],
        name="route",
    )(re)


def _dispatch_kernel(*refs, aliased):
    if aliased:
        p0_ref, p1_ref, h_ref, _, xs_ref, sem = refs
    else:
        p0_ref, p1_ref, h_ref, xs_ref, sem = refs
    rows = h_ref.shape[0]

    def start(t, carry):
        pltpu.make_async_copy(h_ref.at[t], xs_ref.at[p0_ref[t]], sem).start(priority=0)
        pltpu.make_async_copy(h_ref.at[t], xs_ref.at[p1_ref[t]], sem).start(priority=1)
        return carry

    lax.fori_loop(0, rows, start, 0)
    for _ in range(TOP_K):
        pltpu.make_async_copy(h_ref, xs_ref.at[pl.ds(0, rows)], sem).wait()


def _dispatch(h2, p0, p1, xs, n_slots):
    n, _, d = h2.shape
    t = min(MOVE_ROWS, n)
    assert n % t == 0
    aliased = xs is not None
    in_specs = [
        pl.BlockSpec((t,), lambda i: (i,), memory_space=pltpu.SMEM),
        pl.BlockSpec((t,), lambda i: (i,), memory_space=pltpu.SMEM),
        pl.BlockSpec((t, 1, d), lambda i: (i, 0, 0)),
    ]
    args = [p0, p1, h2]
    if aliased:
        in_specs.append(pl.BlockSpec(memory_space=pl.ANY))
        args.append(xs)
    return pl.pallas_call(
        functools.partial(_dispatch_kernel, aliased=aliased),
        grid=(n // t,),
        in_specs=in_specs,
        out_specs=pl.BlockSpec(memory_space=pl.ANY),
        out_shape=jax.ShapeDtypeStruct((n_slots, 1, d), h2.dtype),
        scratch_shapes=[pltpu.SemaphoreType.DMA(())],
        input_output_aliases={3: 0} if aliased else {},
        compiler_params=pltpu.CompilerParams(dimension_semantics=("arbitrary",), has_side_effects=True),
        name="dispatch",
    )(*args)


def _experts_kernel(te_ref, nr_ref, nx_ref, xs_ref, wg_hbm, wu_hbm, wd_hbm, y_ref,
                    wg_st, wu_st, wd_st, wg_sc, wu_sc, wd_sc, sems, n_changes, *, n_exp, tile_rows):
    dh = xs_ref.shape[2]
    tiles_per_step = xs_ref.shape[0] // tile_rows

    def copies(e, slot):
        return (pltpu.make_async_copy(wg_hbm.at[e], wg_st.at[slot], sems.at[slot]),
                pltpu.make_async_copy(wu_hbm.at[e], wu_st.at[slot], sems.at[slot]),
                pltpu.make_async_copy(wd_hbm.at[e], wd_st.at[slot], sems.at[slot]))

    def one_tile(sub):
        tile = pl.program_id(0) * tiles_per_step + sub
        rows = pl.ds(sub * tile_rows, tile_rows)
        expert = te_ref[tile]

        @pl.when(tile == 0)
        def _():
            n_changes[0] = 0
            for cp in copies(expert, 0):
                cp.start(priority=1)

        @pl.when(jnp.logical_or(tile == 0, expert != te_ref[jnp.maximum(tile - 1, 0)]))
        def _():
            slot = lax.rem(n_changes[0], 2)
            n_changes[0] = n_changes[0] + 1
            for cp in copies(expert, slot):
                cp.wait()
            nxt = nx_ref[tile]

            @pl.when(nxt < n_exp)
            def _():
                for cp in copies(nxt, 1 - slot):
                    cp.start(priority=1)

            wg_sc[...] = wg_st[slot].astype(BF16)
            wu_sc[...] = wu_st[slot].astype(BF16)
            wd_sc[...] = wd_st[slot].astype(BF16)

        n_rows = nr_ref[tile]

        @pl.when(n_rows > 0)
        def _():
            live = lax.broadcasted_iota(I32, (tile_rows, 2 * dh), 0) < n_rows
            x = jnp.where(live, _unpack_halves(xs_ref[rows].reshape(tile_rows, dh)), 0.0).astype(BF16)
            gate = jnp.dot(x, wg_sc[...], preferred_element_type=F32)
            up = jnp.dot(x, wu_sc[...], preferred_element_type=F32)
            hid = (jax.nn.silu(gate) * up).astype(BF16)
            y = jnp.dot(hid, wd_sc[...], preferred_element_type=F32)
            y_ref[rows] = _pack_halves(y).reshape(tile_rows, 1, dh)

        @pl.when(n_rows <= 0)
        def _():
            y_ref[rows] = _pack_halves(jnp.zeros((tile_rows, 2 * dh), F32)).reshape(tile_rows, 1, dh)

    for sub in range(tiles_per_step):
        one_tile(sub)


def _experts(xs, te, nr, nx, w_gate, w_up, w_down):
    n_slots, _, dh = xs.shape
    n_exp, d, d_e = w_gate.shape
    t = SLOT_ROWS * EXPERT_STEP_TILES
    est = 2 * 2 * t * d * 2 + 2 * 3 * d * d_e * 4 + 3 * d * d_e * 2 + 4 * SLOT_ROWS * d * 4
    row_spec = pl.BlockSpec((t, 1, dh), lambda i, te, nr, nx: (i, 0, 0))
    return pl.pallas_call(
        functools.partial(_experts_kernel, n_exp=n_exp, tile_rows=SLOT_ROWS),
        grid_spec=pltpu.PrefetchScalarGridSpec(
            num_scalar_prefetch=3,
            grid=(n_slots // t,),
            in_specs=[row_spec, pl.BlockSpec(memory_space=pl.ANY), pl.BlockSpec(memory_space=pl.ANY),
                      pl.BlockSpec(memory_space=pl.ANY)],
            out_specs=row_spec,
            scratch_shapes=[
                pltpu.VMEM((2, d, d_e), F32), pltpu.VMEM((2, d, d_e), F32), pltpu.VMEM((2, d_e, d), F32),
                pltpu.VMEM((d, d_e), BF16), pltpu.VMEM((d, d_e), BF16), pltpu.VMEM((d_e, d), BF16),
                pltpu.SemaphoreType.DMA((2,)), pltpu.SMEM((1,), I32)],
        ),
        out_shape=jax.ShapeDtypeStruct((n_slots, 1, dh), U32),
        compiler_params=pltpu.CompilerParams(
            dimension_semantics=("arbitrary",), vmem_limit_bytes=_vmem_limit(est)),
        name="experts",
    )(te, nr, nx, xs, w_gate, w_up, w_down)


def _combine_kernel(p0_ref, p1_ref, p0n_ref, p1n_ref, xp_ref, rc_ref, g_ref, y_hbm, out_ref, y0_buf, y1_buf, sems,
                    *, n_tiles):
    i = pl.program_id(0)
    rows, d = xp_ref.shape

    def gather(pa_ref, pb_ref, slot):
        base = slot * rows
        def body(t, carry):
            pltpu.make_async_copy(y_hbm.at[pa_ref[t]], y0_buf.at[base + t], sems.at[slot]).start(priority=0)
            pltpu.make_async_copy(y_hbm.at[pb_ref[t]], y1_buf.at[base + t], sems.at[slot]).start(priority=1)
            return carry
        lax.fori_loop(0, rows, body, 0)

    slot = lax.rem(i, 2)

    @pl.when(i == 0)
    def _():
        gather(p0_ref, p1_ref, 0)

    @pl.when(i + 1 < n_tiles)
    def _():
        gather(p0n_ref, p1n_ref, 1 - slot)

    cur = pl.ds(pl.multiple_of(slot * rows, rows), rows)
    for buf in (y0_buf, y1_buf):
        pltpu.make_async_copy(y_hbm.at[pl.ds(0, rows)], buf.at[cur], sems.at[slot]).wait()
    ct = rc_ref[...].T
    y0 = _unpack_halves(y0_buf[cur].reshape(rows, d // 2))
    y1 = _unpack_halves(y1_buf[cur].reshape(rows, d // 2))
    moe = ct[:, 0:1] * y0 + ct[:, 1:2] * y1
    out_ref[...] = _rms_rows(xp_ref[...] + moe, g_ref[...])


def _combine(xp, rc, p0, p1, y, g):
    n, d = xp.shape
    t = min(MOVE_ROWS, n)
    assert n % t == 0
    n_t = n // t
    est = 2 * 2 * t * d * 4 + 2 * 2 * t * d * 2 + 8 * t * d * 4
    nxt = lambda i: (jnp.minimum(i + 1, n_t - 1),)
    return pl.pallas_call(
        functools.partial(_combine_kernel, n_tiles=n_t),
        grid=(n_t,),
        in_specs=[
            pl.BlockSpec((t,), lambda i: (i,), memory_space=pltpu.SMEM),
            pl.BlockSpec((t,), lambda i: (i,), memory_space=pltpu.SMEM),
            pl.BlockSpec((t,), nxt, memory_space=pltpu.SMEM),
            pl.BlockSpec((t,), nxt, memory_space=pltpu.SMEM),
            pl.BlockSpec((t, d), lambda i: (i, 0)),
            pl.BlockSpec((SUBLANES, t), lambda i: (0, i)),
            pl.BlockSpec((1, d), lambda i: (0, 0)),
            pl.BlockSpec(memory_space=pl.ANY),
        ],
        out_specs=pl.BlockSpec((t, d), lambda i: (i, 0)),
        out_shape=jax.ShapeDtypeStruct((n, d), F32),
        scratch_shapes=[pltpu.VMEM((2 * t, 1, d // 2), U32), pltpu.VMEM((2 * t, 1, d // 2), U32),
                        pltpu.SemaphoreType.DMA((2,))],
        compiler_params=pltpu.CompilerParams(
            dimension_semantics=("arbitrary",), vmem_limit_bytes=_vmem_limit(est)),
        name="combine",
    )(p0, p1, p0, p1, xp, rc, g, y)


def _rel_buckets(rel):
    nb = N_BUCKETS // 2
    ret = (rel > 0).astype(I32) * nb
    n = jnp.abs(rel)
    max_exact = nb // 2
    nf = jnp.maximum(n, 1).astype(F32)
    large = max_exact + (jnp.log(nf / max_exact) / math.log(MAX_DISTANCE / max_exact)
                         * (nb - max_exact)).astype(I32)
    large = jnp.minimum(large, nb - 1)
    return ret + jnp.where(n < max_exact, n, large)


def _rel_bias(table, window, n_kv):
    n_heads = table.shape[1]
    span = window + CHUNK
    rows = table[_rel_buckets(jnp.arange(-(span - 1), CHUNK, dtype=I32))].astype(F32)
    bias = jnp.stack([rows[CHUNK - 1 - q:CHUNK - 1 - q + span] for q in range(CHUNK)])
    bias = jnp.transpose(bias, (2, 0, 1))
    bias = jnp.pad(bias, ((0, 0), (0, 0), (0, KEY_PAD - bias.shape[2])), constant_values=NEG_INF)
    bias = bias.reshape(n_kv, n_heads // n_kv // 2, 2, CHUNK, KEY_PAD)
    return jnp.transpose(bias, (0, 1, 3, 2, 4)).reshape(n_kv, (n_heads // n_kv // 2) * CHUNK, 2 * KEY_PAD)


def kernel(x_prompt, x_sample, cache_conv, cache_k, cache_v, rel_bias_table, norm_mix_g, w_in, conv_w, w_conv_out, attn_sinks, w_attn_out, w_o, norm_ffn_g, w_group, b_group, w_expert_router, b_expert_router, w_gate, w_up, w_down, final_norm_g):
    assert w_in.shape[0] == 1, "single-layer step"
    batch, seq, d = x_prompt.shape
    dec_batch, dec_seq, _ = x_sample.shape
    assert dec_seq == CHUNK and seq % MIX_ROWS == 0 and (dec_batch * dec_seq) % MIX_ROWS == 0
    d_conv = conv_w.shape[-1]
    window, n_kv, head_dim = cache_k.shape[2], cache_k.shape[3], cache_k.shape[4]
    n_heads = attn_sinks.shape[-1]
    d_attn, d_kv = n_heads * head_dim, n_kv * head_dim
    n_groups, n_exp = w_group.shape[-1], w_expert_router.shape[-1]
    assert n_groups <= GROUP_ROW0 and d_conv == d_attn and 2 * d_conv == d
    assert n_kv == 2 and d_kv == LANES and (n_heads // n_kv) % 2 == 0 and window + CHUNK <= KEY_PAD
    dims = (n_heads, n_kv, head_dim, window, n_groups, n_exp, d_conv)

    w_all = w_in[0].astype(BF16)
    kv0 = 3 * d_conv + d_attn
    g1 = norm_mix_g[0][None, :]
    wr = jnp.zeros((GROUP_ROW0 + n_exp, d), F32)
    wr = wr.at[:n_groups].set(w_group[0].T).at[GROUP_ROW0:].set(w_expert_router[0].T).astype(BF16)
    br = jnp.zeros((GROUP_ROW0 + n_exp, 1), F32)
    br = br.at[:n_groups, 0].set(b_group[0]).at[GROUP_ROW0:, 0].set(b_expert_router[0])
    weights = (conv_w[0], w_conv_out[0].astype(BF16), w_attn_out[0].astype(BF16), w_o[0].astype(BF16),
               _rel_bias(rel_bias_table, window, n_kv), attn_sinks, norm_ffn_g[0][None, :], wr, br)

    xp2d = x_prompt.reshape(batch * seq, d)
    xs2d = x_sample.reshape(dec_batch * dec_seq, d)
    n_p, n_s = xp2d.shape[0], xs2d.shape[0]
    n_tok = n_p + n_s

    proj_p, kv_p = _inproj(xp2d, g1, w_all, kv0, 2 * d_kv)
    proj_s, kv_s = _inproj(xs2d, g1, w_all, kv0, 2 * d_kv)
    xres_p, h2_p, re_p, rc_p, conv_p, k_p, v_p = _mix(
        xp2d, proj_p, kv_p, None, weights, n_seq=batch, sample=False, dims=dims)
    caches = (cache_conv[0], cache_k[0].reshape(dec_batch, window, d_kv), cache_v[0].reshape(dec_batch, window, d_kv))
    xres_s, h2_s, re_s, rc_s, conv_s, k_s, v_s = _mix(
        xs2d, proj_s, kv_s, caches, weights, n_seq=dec_batch, sample=True, dims=dims)

    n_tiles = pl.cdiv((TOP_K * n_tok) // SLOT_ROWS + n_exp, EXPERT_STEP_TILES) * EXPERT_STEP_TILES
    n_slots = n_tiles * SLOT_ROWS
    pos, tmap = _route(jnp.concatenate([re_p, re_s], axis=1), n_exp, n_tiles)
    p0, p1 = pos[0], pos[1]
    xs = _dispatch(h2_p, p0[:n_p], p1[:n_p], None, n_slots)
    xs = _dispatch(h2_s, p0[n_p:], p1[n_p:], xs, n_slots)
    y = _experts(xs, tmap[0, :n_tiles], tmap[1, :n_tiles], tmap[2, :n_tiles], w_gate[0], w_up[0], w_down[0])
    gf = final_norm_g[None, :]
    y_prompt = _combine(xres_p, rc_p, p0[:n_p], p1[:n_p], y, gf).reshape(batch, seq, d)
    y_sample = _combine(xres_s, rc_s, p0[n_p:], p1[n_p:], y, gf).reshape(dec_batch, dec_seq, d)

    kv_shape = (1, -1, window, n_kv, head_dim)
    return (y_prompt, y_sample, conv_p[None], k_p.reshape(kv_shape), v_p.reshape(kv_shape),
            conv_s[None], k_s.reshape(kv_shape), v_s.reshape(kv_shape))
```

```python
import functools
import math

import jax
import jax.numpy as jnp
from jax import lax
from jax.experimental import pallas as pl
from jax.experimental.pallas import tpu as pltpu

F32, BF16, I32, U32 = jnp.float32, jnp.bfloat16, jnp.int32, jnp.uint32

CHUNK = 64
N_BUCKETS = 32
MAX_DISTANCE = 128
EPS = 1e-6
NEG_INF = -1e30
TOP_K = 2

V7X_VMEM_BYTES = 64 * 1024 * 1024
SUBLANES = 8
LANES = 128

INPROJ_ROWS = 1024
INPROJ_COLS = 2048
MIX_ROWS = 256
MOVE_ROWS = 512
SLOT_ROWS = 256
EXPERT_STEP_TILES = 2
ROUTE_BLOCK = 512
GROUP_ROW0 = 8
KEY_PAD = 256
KV_VARIANTS = 4


def _vmem_limit(nbytes):
    return int(min(V7X_VMEM_BYTES - (4 << 20), max(nbytes, 32 << 20)))


def _pack_halves(x):
    half = x.shape[1] // 2
    return pltpu.pack_elementwise([x[:, :half], x[:, half:]], packed_dtype=BF16)


def _unpack_halves(w):
    lo = pltpu.unpack_elementwise(w, index=0, packed_dtype=BF16, unpacked_dtype=F32)
    hi = pltpu.unpack_elementwise(w, index=1, packed_dtype=BF16, unpacked_dtype=F32)
    return jnp.concatenate([lo, hi], axis=1)


def _rms_rows(x, g):
    r = lax.rsqrt(jnp.mean(x * x, axis=-1, keepdims=True) + EPS)
    return (x * r) * g


def _inproj_kernel(x_ref, g_ref, wm_ref, wkv_ref, proj_ref, kv_ref, h_sc):
    @pl.when(pl.program_id(1) == 0)
    def _():
        rows = 128
        def body(i, carry):
            r0 = pl.multiple_of(i * rows, rows)
            h_sc[pl.ds(r0, rows), :] = _rms_rows(x_ref[pl.ds(r0, rows), :], g_ref[...]).astype(BF16)
            return carry
        lax.fori_loop(0, x_ref.shape[0] // rows, body, 0)
        kv_ref[...] = jnp.dot(h_sc[...], wkv_ref[...], preferred_element_type=F32)

    proj_ref[...] = jnp.dot(h_sc[...], wm_ref[...], preferred_element_type=F32).astype(BF16)


def _inproj(x2d, g, w_all, kv0, n_kv):
    n, d = x2d.shape
    tm = min(INPROJ_ROWS, n)
    tn = INPROJ_COLS
    n_main = w_all.shape[1] - n_kv
    assert kv0 % tn == 0 and n_main % tn == 0 and n_kv % LANES == 0
    est = 2 * tm * d * 4 + tm * d * 2 + 2 * d * tn * 2 + 2 * d * n_kv * 2 + 2 * tm * tn * 2 + 2 * tm * n_kv * 4 + tm * tn * 4
    return pl.pallas_call(
        _inproj_kernel,
        grid=(n // tm, n_main // tn),
        in_specs=[
            pl.BlockSpec((tm, d), lambda i, j: (i, 0)),
            pl.BlockSpec((1, d), lambda i, j: (0, 0)),
            pl.BlockSpec((pl.Element(d), pl.Element(tn)),
                         lambda i, j: (0, pl.multiple_of(jnp.where(j * tn < kv0, j * tn, j * tn + n_kv), LANES))),
            pl.BlockSpec((pl.Element(d), pl.Element(n_kv)), lambda i, j: (0, kv0)),
        ],
        out_specs=[
            pl.BlockSpec((tm, tn), lambda i, j: (i, j)),
            pl.BlockSpec((tm, n_kv), lambda i, j: (i, 0)),
        ],
        out_shape=[jax.ShapeDtypeStruct((n, n_main), BF16), jax.ShapeDtypeStruct((n, n_kv), F32)],
        scratch_shapes=[pltpu.VMEM((tm, d), BF16)],
        compiler_params=pltpu.CompilerParams(
            dimension_semantics=("arbitrary", "arbitrary"), vmem_limit_bytes=_vmem_limit(est + (8 << 20))),
        name="inproj",
    )(x2d, g, w_all, w_all)


def _conv_rows(u, prev2, prev1, w):
    row = lax.broadcasted_iota(I32, u.shape, 0)
    u1 = jnp.where(row == 0, prev1, pltpu.roll(u, 1, axis=0))
    u2 = jnp.where(row == 0, prev2, jnp.where(row == 1, prev1, pltpu.roll(u, 2, axis=0)))
    return (w[0:1] * u2 + w[1:2] * u1) + w[2:3] * u


def _mix_kernel(*refs, sample, n_heads, n_kv, head_dim, window, n_groups, n_exp):
    if sample:
        (x_ref, ga_ref, gb_ref, b_ref, c_ref, xc_ref, q_ref, kv_ref, cconv_ref, ck_ref, cv_ref,
         convw_ref, wco_ref, wao_ref, wo_ref, bias_ref, sinks_ref, g2_ref, wr_ref, br_ref,
         xp_ref, h2_ref, re_ref, rc_ref, sconv_ref, sk_ref, sv_ref,
         kvar, o_sc, ya_sc, carry_u) = refs
    else:
        (x_ref, ga_ref, gb_ref, b_ref, c_ref, xc_ref, q_ref, kv_ref,
         convw_ref, wco_ref, wao_ref, wo_ref, bias_ref, sinks_ref, g2_ref, wr_ref, br_ref,
         xp_ref, h2_ref, re_ref, rc_ref, sconv_ref, sk_ref, sv_ref,
         kvar, o_sc, ya_sc, carry_u) = refs
    t_rows, d_model = x_ref.shape
    n_chunks = t_rows // CHUNK
    span = window + CHUNK
    q_per_kv = n_heads // n_kv
    d_kv = n_kv * head_dim
    scale = 1.0 / math.sqrt(head_dim)
    scale_is_pow2 = math.frexp(scale)[0] == 0.5
    epg = n_exp // n_groups
    seq_start = pl.program_id(1) == 0

    w_conv = convw_ref[...]
    if sample:
        for s in range(n_chunks):
            rows = slice(s * CHUNK, (s + 1) * CHUNK)
            u = c_ref[rows, :].astype(F32) * xc_ref[rows, :].astype(F32)
            y = _conv_rows(u, cconv_ref[s, 0:1, :], cconv_ref[s, 1:2, :], w_conv)
            ya_sc[rows, :] = (b_ref[rows, :].astype(F32) * y).astype(BF16)
            sconv_ref[s] = u[CHUNK - 2:CHUNK, :]
    else:
        u = c_ref[...].astype(F32) * xc_ref[...].astype(F32)
        prev = jnp.where(seq_start, 0.0, carry_u[...])
        y = _conv_rows(u, prev[SUBLANES - 2:SUBLANES - 1], prev[SUBLANES - 1:SUBLANES], w_conv)
        ya_sc[...] = (b_ref[...].astype(F32) * y).astype(BF16)
        carry_u[...] = u[t_rows - SUBLANES:t_rows, :]
        sconv_ref[0] = u[t_rows - 2:t_rows, :]

    def store_kv(row0, k_rows, v_rows):
        n_rows = k_rows.shape[0]
        low = lax.broadcasted_iota(I32, k_rows.shape, 1) < head_dim
        for base, a in ((0, k_rows), (KV_VARIANTS, v_rows)):
            b = pltpu.roll(a, head_dim, axis=1)
            kvar[base + 0, row0:row0 + n_rows, :] = jnp.where(low, a, 0.0).astype(BF16)
            kvar[base + 1, row0:row0 + n_rows, :] = jnp.where(low, 0.0, a).astype(BF16)
            kvar[base + 2, row0:row0 + n_rows, :] = jnp.where(low, b, 0.0).astype(BF16)
            kvar[base + 3, row0:row0 + n_rows, :] = jnp.where(low, 0.0, b).astype(BF16)

    if sample:
        for s in range(n_chunks):
            rows = slice(s * CHUNK, (s + 1) * CHUNK)
            store_kv(s * span, ck_ref[s], cv_ref[s])
            store_kv(s * span + window, kv_ref[rows, 0:d_kv], kv_ref[rows, d_kv:2 * d_kv])
            sk_ref[s, 0:window - CHUNK, :] = ck_ref[s, CHUNK:window, :]
            sv_ref[s, 0:window - CHUNK, :] = cv_ref[s, CHUNK:window, :]
            sk_ref[s, window - CHUNK:window, :] = kv_ref[rows, 0:d_kv]
            sv_ref[s, window - CHUNK:window, :] = kv_ref[rows, d_kv:2 * d_kv]
        k_stride = span
    else:
        @pl.when(seq_start)
        def _():
            kvar[:, 0:window, :] = jnp.zeros((2 * KV_VARIANTS, window, d_kv), BF16)
        store_kv(window, kv_ref[:, 0:d_kv], kv_ref[:, d_kv:2 * d_kv])
        sk_ref[0] = kv_ref[t_rows - window:t_rows, 0:d_kv]
        sv_ref[0] = kv_ref[t_rows - window:t_rows, d_kv:2 * d_kv]
        k_stride = CHUNK

    n_pairs = q_per_kv // 2
    key_pad_rows = jnp.zeros((KEY_PAD - span, d_kv), BF16)
    ones_d = ((lax.broadcasted_iota(I32, (2 * KEY_PAD, LANES), 0) < KEY_PAD)
              == (lax.broadcasted_iota(I32, (2 * KEY_PAD, LANES), 1) < head_dim)).astype(BF16)
    low_half = lax.broadcasted_iota(I32, (CHUNK, LANES), 1) < head_dim
    ya_parts = []
    ya_cols = d_model // (n_chunks * n_kv)
    for c in range(n_chunks):
        q_rows = slice(c * CHUNK, (c + 1) * CHUNK)
        k_rows = slice(c * k_stride, c * k_stride + span)
        masked = (not sample) and c * CHUNK < window
        if masked:
            first_key = (pl.program_id(1) * n_chunks + c) * CHUNK - window
            valid = lax.broadcasted_iota(I32, (CHUNK, KEY_PAD), 1) + first_key >= 0
        for n in range(n_kv):
            top, bot = (0, 3) if n == 0 else (2, 1)
            kd = jnp.concatenate([kvar[top, k_rows, :], key_pad_rows, kvar[bot, k_rows, :], key_pad_rows], axis=0)
            vd = jnp.concatenate([kvar[KV_VARIANTS + top, k_rows, :], key_pad_rows,
                                  kvar[KV_VARIANTS + bot, k_rows, :], key_pad_rows], axis=0)
            q4 = jnp.concatenate(
                [q_ref[q_rows, (n * n_pairs + j) * LANES:(n * n_pairs + j + 1) * LANES] for j in range(n_pairs)], axis=0)
            if scale_is_pow2:
                q4 = q4 * scale
            s4 = lax.dot_general(q4, kd, (((1,), (1,)), ((), ())), preferred_element_type=F32)
            if not scale_is_pow2:
                s4 = s4 * scale
            e_rows, sink_rows = [], []
            for j in range(n_pairs):
                e_halves, sink_halves = [], []
                for half in range(2):
                    blk = (slice(j * CHUNK, (j + 1) * CHUNK), slice(half * KEY_PAD, (half + 1) * KEY_PAD))
                    s = s4[blk] + bias_ref[n, blk[0], blk[1]]
                    if masked:
                        s = jnp.where(valid, s, NEG_INF)
                    sink = sinks_ref[0, n * q_per_kv + 2 * j + half]
                    mx = jnp.maximum(jnp.max(s, axis=-1, keepdims=True), sink)
                    e_halves.append(jnp.exp(s - mx).astype(BF16))
                    sink_halves.append(jnp.exp(sink - mx))
                e_rows.append(jnp.concatenate(e_halves, axis=1))
                sink_rows.append(jnp.where(low_half, sink_halves[0], sink_halves[1]))
            od = jnp.dot(jnp.concatenate(e_rows, axis=0), jnp.concatenate([vd, ones_d], axis=1),
                         preferred_element_type=F32)
            o4 = od[:, 0:LANES] / (od[:, LANES:2 * LANES] + jnp.concatenate(sink_rows, axis=0))
            for j in range(n_pairs):
                o_sc[q_rows, (n * n_pairs + j) * LANES:(n * n_pairs + j + 1) * LANES] = (
                    o4[j * CHUNK:(j + 1) * CHUNK].astype(BF16))
            ya_parts.append(jnp.dot(ya_sc[...], wco_ref[:, len(ya_parts) * ya_cols:(len(ya_parts) + 1) * ya_cols],
                                    preferred_element_type=F32))
    if not sample:
        kvar[:, 0:window, :] = kvar[:, t_rows:t_rows + window, :]
    y_a = jnp.concatenate(ya_parts, axis=1)
    y_b = jnp.dot(o_sc[...], wao_ref[...], preferred_element_type=F32)

    m = jax.nn.sigmoid(ga_ref[...].astype(F32)) * y_a + jax.nn.sigmoid(gb_ref[...].astype(F32)) * y_b
    xp = x_ref[...] + jnp.dot(m.astype(BF16), wo_ref[...], preferred_element_type=F32)
    xp_ref[...] = xp

    h2 = _rms_rows(xp, g2_ref[...])
    h2_ref[...] = _pack_halves(h2).reshape(t_rows, 1, d_model // 2)
    lt = lax.dot_general(wr_ref[...], h2.astype(BF16), (((1,), (1,)), ((), ())),
                         preferred_element_type=F32) + br_ref[...]
    lg = lt[0:n_groups]
    eg = jnp.exp(lg - jnp.max(lg, axis=0, keepdims=True))
    gp = eg / jnp.sum(eg, axis=0, keepdims=True)
    gw = jnp.max(gp, axis=0, keepdims=True)
    gi = lax.broadcasted_iota(I32, gp.shape, 0).astype(F32)
    gsel = jnp.min(jnp.where(gp == gw, gi, float(n_groups)), axis=0, keepdims=True)
    el = jnp.zeros((epg, t_rows), F32)
    for g in range(n_groups):
        el = jnp.where(gsel == float(g), lt[GROUP_ROW0 + g * epg:GROUP_ROW0 + (g + 1) * epg], el)
    ei = lax.broadcasted_iota(I32, el.shape, 0).astype(F32)
    v1 = jnp.max(el, axis=0, keepdims=True)
    i1 = jnp.min(jnp.where(el == v1, ei, float(epg)), axis=0, keepdims=True)
    el2 = jnp.where(ei == i1, -jnp.inf, el)
    v2 = jnp.max(el2, axis=0, keepdims=True)
    i2 = jnp.min(jnp.where(el2 == v2, ei, float(epg)), axis=0, keepdims=True)
    a1 = jnp.exp(v1 - v1)
    a2 = jnp.exp(v2 - v1)
    den = a1 + a2
    c1 = gw * (a1 / den)
    c2 = gw * (a2 / den)
    e1 = (gsel * float(epg) + i1).astype(I32)
    e2 = (gsel * float(epg) + i2).astype(I32)
    row8 = lax.broadcasted_iota(I32, (SUBLANES, t_rows), 0)
    re_ref[...] = jnp.where(row8 == 0, e1, jnp.where(row8 == 1, e2, 0))
    rc_ref[...] = jnp.where(row8 == 0, c1, jnp.where(row8 == 1, c2, 0.0))


def _mix(x2d, proj, kv, caches, weights, *, n_seq, sample, dims):
    n_heads, n_kv, head_dim, window, n_groups, n_exp, d_conv = dims
    n, d = x2d.shape
    t = MIX_ROWS
    n_chunks = t // CHUNK
    span = window + CHUNK
    d_attn = n_heads * head_dim
    d_kv = n_kv * head_dim
    convw, wco, wao, wo, bias, sinks, g2, wr, br = weights
    if sample:
        n_t = n // t
        grid = (n_t, 1)
        tok = lambda i, j: (i, 0)
        n_state = n // CHUNK
        state_blk = n_chunks
        st = lambda i, j: (i, 0, 0)
    else:
        n_t = (n // n_seq) // t
        grid = (n_seq, n_t)
        tok = lambda i, j: (i * n_t + j, 0)
        n_state = n_seq
        state_blk = 1
        st = lambda i, j: (i, 0, 0)
    const2 = lambda i, j: (0, 0)

    def col(width, idx):
        return pl.BlockSpec((t, width), lambda i, j: (tok(i, j)[0], idx))

    def resident(shape):
        return pl.BlockSpec(shape, const2, pipeline_mode=pl.Buffered(1))

    in_specs = [
        pl.BlockSpec((t, d), tok),
        col(d, 2), col(d, 3),
        col(d_conv, 0), col(d_conv, 1), col(d_conv, 2), col(d_attn, 3),
        pl.BlockSpec((t, 2 * d_kv), tok),
    ]
    args = [x2d, proj, proj, proj, proj, proj, proj, kv]
    if sample:
        cconv, ck, cv = caches
        in_specs += [
            pl.BlockSpec((n_chunks, cconv.shape[1], d_conv), st),
            pl.BlockSpec((n_chunks, window, d_kv), st),
            pl.BlockSpec((n_chunks, window, d_kv), st),
        ]
        args += [cconv, ck, cv]
    in_specs += [
        resident(convw.shape), resident(wco.shape), resident(wao.shape), resident(wo.shape),
        pl.BlockSpec(bias.shape, lambda i, j: (0, 0, 0), pipeline_mode=pl.Buffered(1)),
        pl.BlockSpec(memory_space=pltpu.SMEM),
        resident(g2.shape), resident(wr.shape), resident(br.shape),
    ]
    args += [convw, wco, wao, wo, bias, sinks, g2, wr, br]
    out_specs = [
        pl.BlockSpec((t, d), tok),
        pl.BlockSpec((t, 1, d // 2), lambda i, j: (tok(i, j)[0], 0, 0)),
        pl.BlockSpec((SUBLANES, t), lambda i, j: (0, tok(i, j)[0])),
        pl.BlockSpec((SUBLANES, t), lambda i, j: (0, tok(i, j)[0])),
        pl.BlockSpec((state_blk, 2, d_conv), st),
        pl.BlockSpec((state_blk, window, d_kv), st),
        pl.BlockSpec((state_blk, window, d_kv), st),
    ]
    out_shape = [
        jax.ShapeDtypeStruct((n, d), F32),
        jax.ShapeDtypeStruct((n, 1, d // 2), U32),
        jax.ShapeDtypeStruct((SUBLANES, n), I32),
        jax.ShapeDtypeStruct((SUBLANES, n), F32),
        jax.ShapeDtypeStruct((n_state, 2, d_conv), F32),
        jax.ShapeDtypeStruct((n_state, window, d_kv), F32),
        jax.ShapeDtypeStruct((n_state, window, d_kv), F32),
    ]
    kv_rows = n_chunks * span if sample else window + t
    scratch = [
        pltpu.VMEM((2 * KV_VARIANTS, kv_rows, d_kv), BF16),
        pltpu.VMEM((t, d_attn), BF16), pltpu.VMEM((t, d_conv), BF16),
        pltpu.VMEM((SUBLANES, d_conv), F32),
    ]
    est = (2 * t * d * 4 * 3 + 2 * 2 * t * d * 2 + 2 * 4 * t * d_conv * 2
           + (2 * d_conv * d + d * d) * 2 + 12 * t * d * 4)
    kern = functools.partial(_mix_kernel, sample=sample, n_heads=n_heads, n_kv=n_kv, head_dim=head_dim,
                             window=window, n_groups=n_groups, n_exp=n_exp)
    return pl.pallas_call(
        kern, grid=grid, in_specs=in_specs, out_specs=out_specs, out_shape=out_shape,
        scratch_shapes=scratch,
        compiler_params=pltpu.CompilerParams(
            dimension_semantics=("arbitrary", "arbitrary"), vmem_limit_bytes=_vmem_limit(est)),
        name="mix_sample" if sample else "mix_prompt",
    )(*args)


def _route_kernel(re_ref, pos_ref, tmap_ref, *, n_exp, tile_rows, blk):
    n = re_ref.shape[1]
    n_blk = n // blk
    erow = lax.broadcasted_iota(I32, (n_exp, blk), 0)

    def onehots(j):
        c0 = pl.multiple_of(j * blk, blk)
        oh0 = (erow == re_ref[0:1, pl.ds(c0, blk)]).astype(F32)
        oh1 = (erow == re_ref[1:2, pl.ds(c0, blk)]).astype(F32)
        return c0, oh0, oh1

    def count_body(j, cnt):
        _, oh0, oh1 = onehots(j)
        return cnt + jnp.sum(oh0 + oh1, axis=1, keepdims=True)

    cnt = lax.fori_loop(0, n_blk, count_body, jnp.zeros((n_exp, 1), F32))
    cnt = jnp.broadcast_to(cnt, (n_exp, LANES))
    padded = jnp.ceil(cnt / tile_rows) * tile_rows
    ends = padded
    prow = lax.broadcasted_iota(I32, ends.shape, 0)
    step = 1
    while step < n_exp:
        ends = ends + jnp.where(prow >= step, pltpu.roll(ends, step, axis=0), 0.0)
        step *= 2
    offs = ends - padded
    off1 = offs[:, 0:1]

    tri = (lax.broadcasted_iota(I32, (blk, blk), 0) <= lax.broadcasted_iota(I32, (blk, blk), 1)).astype(BF16)
    row8 = lax.broadcasted_iota(I32, (SUBLANES, blk), 0)

    def pos_body(j, run):
        c0, oh0, oh1 = onehots(j)
        both = oh0 + oh1
        csum = jnp.dot(both.astype(BF16), tri, preferred_element_type=F32) + run
        slot = off1 + csum - 1.0
        p0 = jnp.sum(oh0 * slot, axis=0, keepdims=True).astype(I32)
        p1 = jnp.sum(oh1 * slot, axis=0, keepdims=True).astype(I32)
        pos_ref[:, pl.ds(c0, blk)] = jnp.where(row8 == 0, p0, jnp.where(row8 == 1, p1, 0))
        return run + jnp.sum(both, axis=1, keepdims=True)

    lax.fori_loop(0, n_blk, pos_body, jnp.zeros((n_exp, 1), F32))

    n_tiles_pad = tmap_ref.shape[1]
    start = (lax.broadcasted_iota(I32, (1, n_tiles_pad), 1) * tile_rows).astype(F32)
    end1 = ends[:, 0:1]
    te = jnp.sum((end1 <= start).astype(F32), axis=0, keepdims=True)
    trow = lax.broadcasted_iota(I32, (n_exp, n_tiles_pad), 0).astype(F32)
    used_end = jnp.sum(jnp.where(trow == te, off1 + cnt[:, 0:1], 0.0), axis=0, keepdims=True)
    n_rows = jnp.clip(used_end - start, 0.0, float(tile_rows))
    has_tokens = cnt[:, 0:1] > 0.0
    te = jnp.minimum(te, jnp.max(jnp.where(has_tokens, trow, 0.0), axis=0, keepdims=True))
    nxt = jnp.min(jnp.where(trow > te, jnp.where(has_tokens, trow, float(n_exp)), float(n_exp)), axis=0, keepdims=True)
    r8 = lax.broadcasted_iota(I32, (SUBLANES, n_tiles_pad), 0)
    tmap_ref[...] = jnp.where(r8 == 0, te.astype(I32),
                              jnp.where(r8 == 1, n_rows.astype(I32), jnp.where(r8 == 2, nxt.astype(I32), 0)))


def _route(re, n_exp, n_tiles):
    n = re.shape[1]
    assert n % ROUTE_BLOCK == 0
    n_tiles_pad = pl.cdiv(n_tiles, LANES) * LANES
    kern = functools.partial(_route_kernel, n_exp=n_exp, tile_rows=SLOT_ROWS, blk=ROUTE_BLOCK)
    return pl.pallas_call(
        kern,
        out_shape=[jax.ShapeDtypeStruct((SUBLANES, n), I32), jax.ShapeDtypeStruct((SUBLANES, n_tiles_pad), I32)],
        name="route",
    )(re)


def _dispatch_kernel(*refs, aliased):
    if aliased:
        p0_ref, p1_ref, h_ref, _, xs_ref, sem = refs
    else:
        p0_ref, p1_ref, h_ref, xs_ref, sem = refs
    rows = h_ref.shape[0]

    def start(t, carry):
        pltpu.make_async_copy(h_ref.at[t], xs_ref.at[p0_ref[t]], sem).start(priority=0)
        pltpu.make_async_copy(h_ref.at[t], xs_ref.at[p1_ref[t]], sem).start(priority=1)
        return carry

    lax.fori_loop(0, rows, start, 0)
    for _ in range(TOP_K):
        pltpu.make_async_copy(h_ref, xs_ref.at[pl.ds(0, rows)], sem).wait()


def _dispatch(h2, p0, p1, xs, n_slots):
    n, _, d = h2.shape
    t = min(MOVE_ROWS, n)
    assert n % t == 0
    aliased = xs is not None
    in_specs = [
        pl.BlockSpec((t,), lambda i: (i,), memory_space=pltpu.SMEM),
        pl.BlockSpec((t,), lambda i: (i,), memory_space=pltpu.SMEM),
        pl.BlockSpec((t, 1, d), lambda i: (i, 0, 0)),
    ]
    args = [p0, p1, h2]
    if aliased:
        in_specs.append(pl.BlockSpec(memory_space=pl.ANY))
        args.append(xs)
    return pl.pallas_call(
        functools.partial(_dispatch_kernel, aliased=aliased),
        grid=(n // t,),
        in_specs=in_specs,
        out_specs=pl.BlockSpec(memory_space=pl.ANY),
        out_shape=jax.ShapeDtypeStruct((n_slots, 1, d), h2.dtype),
        scratch_shapes=[pltpu.SemaphoreType.DMA(())],
        input_output_aliases={3: 0} if aliased else {},
        compiler_params=pltpu.CompilerParams(dimension_semantics=("arbitrary",), has_side_effects=True),
        name="dispatch",
    )(*args)


def _experts_kernel(te_ref, nr_ref, nx_ref, xs_ref, wg_hbm, wu_hbm, wd_hbm, y_ref,
                    wg_st, wu_st, wd_st, wg_sc, wu_sc, wd_sc, x_sc, sems, n_changes, *, n_exp, tile_rows):
    dh = xs_ref.shape[2]
    tiles_per_step = xs_ref.shape[0] // tile_rows

    def copies(e, slot):
        return (pltpu.make_async_copy(wg_hbm.at[e], wg_st.at[slot], sems.at[slot]),
                pltpu.make_async_copy(wu_hbm.at[e], wu_st.at[slot], sems.at[slot]),
                pltpu.make_async_copy(wd_hbm.at[e], wd_st.at[slot], sems.at[slot]))

    def one_tile(sub):
        tile = pl.program_id(0) * tiles_per_step + sub
        rows = pl.ds(sub * tile_rows, tile_rows)
        expert = te_ref[tile]

        @pl.when(tile == 0)
        def _():
            n_changes[0] = 0
            for cp in copies(expert, 0):
                cp.start(priority=1)

        @pl.when(jnp.logical_or(tile == 0, expert != te_ref[jnp.maximum(tile - 1, 0)]))
        def _():
            slot = lax.rem(n_changes[0], 2)
            n_changes[0] = n_changes[0] + 1
            for cp in copies(expert, slot):
                cp.wait()
            nxt = nx_ref[tile]

            @pl.when(nxt < n_exp)
            def _():
                for cp in copies(nxt, 1 - slot):
                    cp.start(priority=1)

            wg_sc[...] = wg_st[slot].astype(BF16)
            wu_sc[...] = wu_st[slot].astype(BF16)
            wd_sc[...] = wd_st[slot].astype(BF16)

        n_rows = nr_ref[tile]

        @pl.when(n_rows == tile_rows)
        def _():
            x_sc[...] = _unpack_halves(xs_ref[rows].reshape(tile_rows, dh)).astype(BF16)

        @pl.when(jnp.logical_and(n_rows > 0, n_rows < tile_rows))
        def _():
            live = lax.broadcasted_iota(I32, (tile_rows, 2 * dh), 0) < n_rows
            x_sc[...] = jnp.where(live, _unpack_halves(xs_ref[rows].reshape(tile_rows, dh)), 0.0).astype(BF16)

        @pl.when(n_rows > 0)
        def _():
            x = x_sc[...]
            gate = jnp.dot(x, wg_sc[...], preferred_element_type=F32)
            up = jnp.dot(x, wu_sc[...], preferred_element_type=F32)
            hid = (jax.nn.silu(gate) * up).astype(BF16)
            y = jnp.dot(hid, wd_sc[...], preferred_element_type=F32)
            y_ref[rows] = _pack_halves(y).reshape(tile_rows, 1, dh)

        @pl.when(n_rows <= 0)
        def _():
            y_ref[rows] = _pack_halves(jnp.zeros((tile_rows, 2 * dh), F32)).reshape(tile_rows, 1, dh)

    for sub in range(tiles_per_step):
        one_tile(sub)


def _experts(xs, te, nr, nx, w_gate, w_up, w_down):
    n_slots, _, dh = xs.shape
    n_exp, d, d_e = w_gate.shape
    t = SLOT_ROWS * EXPERT_STEP_TILES
    est = 2 * 2 * t * d * 2 + 2 * 3 * d * d_e * 4 + 3 * d * d_e * 2 + 4 * SLOT_ROWS * d * 4
    row_spec = pl.BlockSpec((t, 1, dh), lambda i, te, nr, nx: (i, 0, 0))
    return pl.pallas_call(
        functools.partial(_experts_kernel, n_exp=n_exp, tile_rows=SLOT_ROWS),
        grid_spec=pltpu.PrefetchScalarGridSpec(
            num_scalar_prefetch=3,
            grid=(n_slots // t,),
            in_specs=[row_spec, pl.BlockSpec(memory_space=pl.ANY), pl.BlockSpec(memory_space=pl.ANY),
                      pl.BlockSpec(memory_space=pl.ANY)],
            out_specs=row_spec,
            scratch_shapes=[
                pltpu.VMEM((2, d, d_e), F32), pltpu.VMEM((2, d, d_e), F32), pltpu.VMEM((2, d_e, d), F32),
                pltpu.VMEM((d, d_e), BF16), pltpu.VMEM((d, d_e), BF16), pltpu.VMEM((d_e, d), BF16),
                pltpu.VMEM((SLOT_ROWS, d), BF16),
                pltpu.SemaphoreType.DMA((2,)), pltpu.SMEM((1,), I32)],
        ),
        out_shape=jax.ShapeDtypeStruct((n_slots, 1, dh), U32),
        compiler_params=pltpu.CompilerParams(
            dimension_semantics=("arbitrary",), vmem_limit_bytes=_vmem_limit(est)),
        name="experts",
    )(te, nr, nx, xs, w_gate, w_up, w_down)


def _combine_kernel(p0_ref, p1_ref, p0n_ref, p1n_ref, xp_ref, rc_ref, g_ref, y_hbm, out_ref, y0_buf, y1_buf, sems,
                    *, n_tiles):
    i = pl.program_id(0)
    rows, d = xp_ref.shape

    def gather(pa_ref, pb_ref, slot):
        base = slot * rows
        def body(t, carry):
            pltpu.make_async_copy(y_hbm.at[pa_ref[t]], y0_buf.at[base + t], sems.at[slot]).start(priority=0)
            pltpu.make_async_copy(y_hbm.at[pb_ref[t]], y1_buf.at[base + t], sems.at[slot]).start(priority=1)
            return carry
        lax.fori_loop(0, rows, body, 0)

    slot = lax.rem(i, 2)

    @pl.when(i == 0)
    def _():
        gather(p0_ref, p1_ref, 0)

    @pl.when(i + 1 < n_tiles)
    def _():
        gather(p0n_ref, p1n_ref, 1 - slot)

    cur = pl.ds(pl.multiple_of(slot * rows, rows), rows)
    for buf in (y0_buf, y1_buf):
        pltpu.make_async_copy(y_hbm.at[pl.ds(0, rows)], buf.at[cur], sems.at[slot]).wait()
    ct = rc_ref[...].T
    y0 = _unpack_halves(y0_buf[cur].reshape(rows, d // 2))
    y1 = _unpack_halves(y1_buf[cur].reshape(rows, d // 2))
    moe = ct[:, 0:1] * y0 + ct[:, 1:2] * y1
    out_ref[...] = _rms_rows(xp_ref[...] + moe, g_ref[...])


def _combine(xp, rc, p0, p1, y, g):
    n, d = xp.shape
    t = min(MOVE_ROWS, n)
    assert n % t == 0
    n_t = n // t
    est = 2 * 2 * t * d * 4 + 2 * 2 * t * d * 2 + 8 * t * d * 4
    nxt = lambda i: (jnp.minimum(i + 1, n_t - 1),)
    return pl.pallas_call(
        functools.partial(_combine_kernel, n_tiles=n_t),
        grid=(n_t,),
        in_specs=[
            pl.BlockSpec((t,), lambda i: (i,), memory_space=pltpu.SMEM),
            pl.BlockSpec((t,), lambda i: (i,), memory_space=pltpu.SMEM),
            pl.BlockSpec((t,), nxt, memory_space=pltpu.SMEM),
            pl.BlockSpec((t,), nxt, memory_space=pltpu.SMEM),
            pl.BlockSpec((t, d), lambda i: (i, 0)),
            pl.BlockSpec((SUBLANES, t), lambda i: (0, i)),
            pl.BlockSpec((1, d), lambda i: (0, 0)),
            pl.BlockSpec(memory_space=pl.ANY),
        ],
        out_specs=pl.BlockSpec((t, d), lambda i: (i, 0)),
        out_shape=jax.ShapeDtypeStruct((n, d), F32),
        scratch_shapes=[pltpu.VMEM((2 * t, 1, d // 2), U32), pltpu.VMEM((2 * t, 1, d // 2), U32),
                        pltpu.SemaphoreType.DMA((2,))],
        compiler_params=pltpu.CompilerParams(
            dimension_semantics=("arbitrary",), vmem_limit_bytes=_vmem_limit(est)),
        name="combine",
    )(p0, p1, p0, p1, xp, rc, g, y)


def _rel_buckets(rel):
    nb = N_BUCKETS // 2
    ret = (rel > 0).astype(I32) * nb
    n = jnp.abs(rel)
    max_exact = nb // 2
    nf = jnp.maximum(n, 1).astype(F32)
    large = max_exact + (jnp.log(nf / max_exact) / math.log(MAX_DISTANCE / max_exact)
                         * (nb - max_exact)).astype(I32)
    large = jnp.minimum(large, nb - 1)
    return ret + jnp.where(n < max_exact, n, large)


def _rel_bias(table, window, n_kv):
    n_heads = table.shape[1]
    span = window + CHUNK
    rows = table[_rel_buckets(jnp.arange(-(span - 1), CHUNK, dtype=I32))].astype(F32)
    bias = jnp.stack([rows[CHUNK - 1 - q:CHUNK - 1 - q + span] for q in range(CHUNK)])
    bias = jnp.transpose(bias, (2, 0, 1))
    bias = jnp.pad(bias, ((0, 0), (0, 0), (0, KEY_PAD - bias.shape[2])), constant_values=NEG_INF)
    bias = bias.reshape(n_kv, n_heads // n_kv // 2, 2, CHUNK, KEY_PAD)
    return jnp.transpose(bias, (0, 1, 3, 2, 4)).reshape(n_kv, (n_heads // n_kv // 2) * CHUNK, 2 * KEY_PAD)


def kernel(x_prompt, x_sample, cache_conv, cache_k, cache_v, rel_bias_table, norm_mix_g, w_in, conv_w, w_conv_out, attn_sinks, w_attn_out, w_o, norm_ffn_g, w_group, b_group, w_expert_router, b_expert_router, w_gate, w_up, w_down, final_norm_g):
    assert w_in.shape[0] == 1, "single-layer step"
    batch, seq, d = x_prompt.shape
    dec_batch, dec_seq, _ = x_sample.shape
    assert dec_seq == CHUNK and seq % MIX_ROWS == 0 and (dec_batch * dec_seq) % MIX_ROWS == 0
    d_conv = conv_w.shape[-1]
    window, n_kv, head_dim = cache_k.shape[2], cache_k.shape[3], cache_k.shape[4]
    n_heads = attn_sinks.shape[-1]
    d_attn, d_kv = n_heads * head_dim, n_kv * head_dim
    n_groups, n_exp = w_group.shape[-1], w_expert_router.shape[-1]
    assert n_groups <= GROUP_ROW0 and d_conv == d_attn and 2 * d_conv == d
    assert n_kv == 2 and d_kv == LANES and (n_heads // n_kv) % 2 == 0 and window + CHUNK <= KEY_PAD
    dims = (n_heads, n_kv, head_dim, window, n_groups, n_exp, d_conv)

    w_all = w_in[0].astype(BF16)
    kv0 = 3 * d_conv + d_attn
    g1 = norm_mix_g[0][None, :]
    wr = jnp.zeros((GROUP_ROW0 + n_exp, d), F32)
    wr = wr.at[:n_groups].set(w_group[0].T).at[GROUP_ROW0:].set(w_expert_router[0].T).astype(BF16)
    br = jnp.zeros((GROUP_ROW0 + n_exp, 1), F32)
    br = br.at[:n_groups, 0].set(b_group[0]).at[GROUP_ROW0:, 0].set(b_expert_router[0])
    weights = (conv_w[0], w_conv_out[0].astype(BF16), w_attn_out[0].astype(BF16), w_o[0].astype(BF16),
               _rel_bias(rel_bias_table, window, n_kv), attn_sinks, norm_ffn_g[0][None, :], wr, br)

    xp2d = x_prompt.reshape(batch * seq, d)
    xs2d = x_sample.reshape(dec_batch * dec_seq, d)
    n_p, n_s = xp2d.shape[0], xs2d.shape[0]
    n_tok = n_p + n_s

    proj_p, kv_p = _inproj(xp2d, g1, w_all, kv0, 2 * d_kv)
    proj_s, kv_s = _inproj(xs2d, g1, w_all, kv0, 2 * d_kv)
    xres_p, h2_p, re_p, rc_p, conv_p, k_p, v_p = _mix(
        xp2d, proj_p, kv_p, None, weights, n_seq=batch, sample=False, dims=dims)
    caches = (cache_conv[0], cache_k[0].reshape(dec_batch, window, d_kv), cache_v[0].reshape(dec_batch, window, d_kv))
    xres_s, h2_s, re_s, rc_s, conv_s, k_s, v_s = _mix(
        xs2d, proj_s, kv_s, caches, weights, n_seq=dec_batch, sample=True, dims=dims)

    n_tiles = pl.cdiv((TOP_K * n_tok) // SLOT_ROWS + n_exp, EXPERT_STEP_TILES) * EXPERT_STEP_TILES
    n_slots = n_tiles * SLOT_ROWS
    pos, tmap = _route(jnp.concatenate([re_p, re_s], axis=1), n_exp, n_tiles)
    p0, p1 = pos[0], pos[1]
    xs = _dispatch(h2_p, p0[:n_p], p1[:n_p], None, n_slots)
    xs = _dispatch(h2_s, p0[n_p:], p1[n_p:], xs, n_slots)
    y = _experts(xs, tmap[0, :n_tiles], tmap[1, :n_tiles], tmap[2, :n_tiles], w_gate[0], w_up[0], w_down[0])
    gf = final_norm_g[None, :]
    y_prompt = _combine(xres_p, rc_p, p0[:n_p], p1[:n_p], y, gf).reshape(batch, seq, d)
    y_sample = _combine(xres_s, rc_s, p0[n_p:], p1[n_p:], y, gf).reshape(dec_batch, dec_seq, d)

    kv_shape = (1, -1, window, n_kv, head_dim)
    return (y_prompt, y_sample, conv_p[None], k_p.reshape(kv_shape), v_p.reshape(kv_shape),
            conv_s[None], k_s.reshape(kv_shape), v_s.reshape(kv_shape))
```

```python
import functools
import math

import jax
import jax.numpy as jnp
from jax import lax
from jax.experimental import pallas as pl
from jax.experimental.pallas import tpu as pltpu

F32, BF16, I32, U32 = jnp.float32, jnp.bfloat16, jnp.int32, jnp.uint32

CHUNK = 64
N_BUCKETS = 32
MAX_DISTANCE = 128
EPS = 1e-6
NEG_INF = -1e30
TOP_K = 2

V7X_VMEM_BYTES = 64 * 1024 * 1024
SUBLANES = 8
LANES = 128

INPROJ_ROWS = 1024
INPROJ_COLS = 2048
MIX_ROWS = 256
MOVE_ROWS = 512
SLOT_ROWS = 256
EXPERT_STEP_TILES = 4
ROUTE_BLOCK = 512
GROUP_ROW0 = 8
KEY_PAD = 256
KV_VARIANTS = 4


def _vmem_limit(nbytes):
    return int(min(V7X_VMEM_BYTES - (4 << 20), max(nbytes, 32 << 20)))


def _pack_halves(x):
    half = x.shape[1] // 2
    return pltpu.pack_elementwise([x[:, :half], x[:, half:]], packed_dtype=BF16)


def _unpack_halves(w):
    lo = pltpu.unpack_elementwise(w, index=0, packed_dtype=BF16, unpacked_dtype=F32)
    hi = pltpu.unpack_elementwise(w, index=1, packed_dtype=BF16, unpacked_dtype=F32)
    return jnp.concatenate([lo, hi], axis=1)


def _rms_rows(x, g):
    r = lax.rsqrt(jnp.mean(x * x, axis=-1, keepdims=True) + EPS)
    return (x * r) * g


def _inproj_kernel(x_ref, g_ref, wm_ref, wkv_ref, proj_ref, kv_ref, h_sc):
    @pl.when(pl.program_id(1) == 0)
    def _():
        rows = 128
        def body(i, carry):
            r0 = pl.multiple_of(i * rows, rows)
            h_sc[pl.ds(r0, rows), :] = _rms_rows(x_ref[pl.ds(r0, rows), :], g_ref[...]).astype(BF16)
            return carry
        lax.fori_loop(0, x_ref.shape[0] // rows, body, 0)
        kv_ref[...] = jnp.dot(h_sc[...], wkv_ref[...], preferred_element_type=F32)

    proj_ref[...] = jnp.dot(h_sc[...], wm_ref[...], preferred_element_type=F32).astype(BF16)


def _inproj(x2d, g, w_all, kv0, n_kv):
    n, d = x2d.shape
    tm = min(INPROJ_ROWS, n)
    tn = INPROJ_COLS
    n_main = w_all.shape[1] - n_kv
    assert kv0 % tn == 0 and n_main % tn == 0 and n_kv % LANES == 0
    est = 2 * tm * d * 4 + tm * d * 2 + 2 * d * tn * 2 + 2 * d * n_kv * 2 + 2 * tm * tn * 2 + 2 * tm * n_kv * 4 + tm * tn * 4
    return pl.pallas_call(
        _inproj_kernel,
        grid=(n // tm, n_main // tn),
        in_specs=[
            pl.BlockSpec((tm, d), lambda i, j: (i, 0)),
            pl.BlockSpec((1, d), lambda i, j: (0, 0)),
            pl.BlockSpec((pl.Element(d), pl.Element(tn)),
                         lambda i, j: (0, pl.multiple_of(jnp.where(j * tn < kv0, j * tn, j * tn + n_kv), LANES))),
            pl.BlockSpec((pl.Element(d), pl.Element(n_kv)), lambda i, j: (0, kv0)),
        ],
        out_specs=[
            pl.BlockSpec((tm, tn), lambda i, j: (i, j)),
            pl.BlockSpec((tm, n_kv), lambda i, j: (i, 0)),
        ],
        out_shape=[jax.ShapeDtypeStruct((n, n_main), BF16), jax.ShapeDtypeStruct((n, n_kv), F32)],
        scratch_shapes=[pltpu.VMEM((tm, d), BF16)],
        compiler_params=pltpu.CompilerParams(
            dimension_semantics=("arbitrary", "arbitrary"), vmem_limit_bytes=_vmem_limit(est + (8 << 20))),
        name="inproj",
    )(x2d, g, w_all, w_all)


def _conv_rows(u, prev2, prev1, w):
    row = lax.broadcasted_iota(I32, u.shape, 0)
    u1 = jnp.where(row == 0, prev1, pltpu.roll(u, 1, axis=0))
    u2 = jnp.where(row == 0, prev2, jnp.where(row == 1, prev1, pltpu.roll(u, 2, axis=0)))
    return (w[0:1] * u2 + w[1:2] * u1) + w[2:3] * u


def _mix_kernel(*refs, sample, n_heads, n_kv, head_dim, window, n_groups, n_exp):
    if sample:
        (x_ref, ga_ref, gb_ref, b_ref, c_ref, xc_ref, q_ref, kv_ref, cconv_ref, ck_ref, cv_ref,
         convw_ref, wco_ref, wao_ref, wo_ref, bias_ref, sinks_ref, g2_ref, wr_ref, br_ref,
         xp_ref, h2_ref, re_ref, rc_ref, sconv_ref, sk_ref, sv_ref,
         kvar, o_sc, ya_sc, carry_u) = refs
    else:
        (x_ref, ga_ref, gb_ref, b_ref, c_ref, xc_ref, q_ref, kv_ref,
         convw_ref, wco_ref, wao_ref, wo_ref, bias_ref, sinks_ref, g2_ref, wr_ref, br_ref,
         xp_ref, h2_ref, re_ref, rc_ref, sconv_ref, sk_ref, sv_ref,
         kvar, o_sc, ya_sc, carry_u) = refs
    t_rows, d_model = x_ref.shape
    n_chunks = t_rows // CHUNK
    span = window + CHUNK
    q_per_kv = n_heads // n_kv
    d_kv = n_kv * head_dim
    scale = 1.0 / math.sqrt(head_dim)
    scale_is_pow2 = math.frexp(scale)[0] == 0.5
    epg = n_exp // n_groups
    seq_start = pl.program_id(1) == 0

    w_conv = convw_ref[...]
    if sample:
        for s in range(n_chunks):
            rows = slice(s * CHUNK, (s + 1) * CHUNK)
            u = c_ref[rows, :].astype(F32) * xc_ref[rows, :].astype(F32)
            y = _conv_rows(u, cconv_ref[s, 0:1, :], cconv_ref[s, 1:2, :], w_conv)
            ya_sc[rows, :] = (b_ref[rows, :].astype(F32) * y).astype(BF16)
            sconv_ref[s] = u[CHUNK - 2:CHUNK, :]
    else:
        u = c_ref[...].astype(F32) * xc_ref[...].astype(F32)
        prev = jnp.where(seq_start, 0.0, carry_u[...])
        y = _conv_rows(u, prev[SUBLANES - 2:SUBLANES - 1], prev[SUBLANES - 1:SUBLANES], w_conv)
        ya_sc[...] = (b_ref[...].astype(F32) * y).astype(BF16)
        carry_u[...] = u[t_rows - SUBLANES:t_rows, :]
        sconv_ref[0] = u[t_rows - 2:t_rows, :]

    def store_kv(row0, k_rows, v_rows):
        n_rows = k_rows.shape[0]
        low = lax.broadcasted_iota(I32, k_rows.shape, 1) < head_dim
        for base, a in ((0, k_rows), (KV_VARIANTS, v_rows)):
            b = pltpu.roll(a, head_dim, axis=1)
            kvar[base + 0, row0:row0 + n_rows, :] = jnp.where(low, a, 0.0).astype(BF16)
            kvar[base + 1, row0:row0 + n_rows, :] = jnp.where(low, 0.0, a).astype(BF16)
            kvar[base + 2, row0:row0 + n_rows, :] = jnp.where(low, b, 0.0).astype(BF16)
            kvar[base + 3, row0:row0 + n_rows, :] = jnp.where(low, 0.0, b).astype(BF16)

    if sample:
        for s in range(n_chunks):
            rows = slice(s * CHUNK, (s + 1) * CHUNK)
            store_kv(s * span, ck_ref[s], cv_ref[s])
            store_kv(s * span + window, kv_ref[rows, 0:d_kv], kv_ref[rows, d_kv:2 * d_kv])
            sk_ref[s, 0:window - CHUNK, :] = ck_ref[s, CHUNK:window, :]
            sv_ref[s, 0:window - CHUNK, :] = cv_ref[s, CHUNK:window, :]
            sk_ref[s, window - CHUNK:window, :] = kv_ref[rows, 0:d_kv]
            sv_ref[s, window - CHUNK:window, :] = kv_ref[rows, d_kv:2 * d_kv]
        k_stride = span
    else:
        @pl.when(seq_start)
        def _():
            kvar[:, 0:window, :] = jnp.zeros((2 * KV_VARIANTS, window, d_kv), BF16)
        store_kv(window, kv_ref[:, 0:d_kv], kv_ref[:, d_kv:2 * d_kv])
        sk_ref[0] = kv_ref[t_rows - window:t_rows, 0:d_kv]
        sv_ref[0] = kv_ref[t_rows - window:t_rows, d_kv:2 * d_kv]
        k_stride = CHUNK

    n_pairs = q_per_kv // 2
    key_pad_rows = jnp.zeros((KEY_PAD - span, d_kv), BF16)
    ones_d = ((lax.broadcasted_iota(I32, (2 * KEY_PAD, LANES), 0) < KEY_PAD)
              == (lax.broadcasted_iota(I32, (2 * KEY_PAD, LANES), 1) < head_dim)).astype(BF16)
    low_half = lax.broadcasted_iota(I32, (CHUNK, LANES), 1) < head_dim
    ya_parts = []
    ya_cols = d_model // (n_chunks * n_kv)
    for c in range(n_chunks):
        q_rows = slice(c * CHUNK, (c + 1) * CHUNK)
        k_rows = slice(c * k_stride, c * k_stride + span)
        masked = (not sample) and c * CHUNK < window
        if masked:
            first_key = (pl.program_id(1) * n_chunks + c) * CHUNK - window
            valid = lax.broadcasted_iota(I32, (CHUNK, KEY_PAD), 1) + first_key >= 0
        for n in range(n_kv):
            top, bot = (0, 3) if n == 0 else (2, 1)
            kd = jnp.concatenate([kvar[top, k_rows, :], key_pad_rows, kvar[bot, k_rows, :], key_pad_rows], axis=0)
            vd = jnp.concatenate([kvar[KV_VARIANTS + top, k_rows, :], key_pad_rows,
                                  kvar[KV_VARIANTS + bot, k_rows, :], key_pad_rows], axis=0)
            q4 = jnp.concatenate(
                [q_ref[q_rows, (n * n_pairs + j) * LANES:(n * n_pairs + j + 1) * LANES] for j in range(n_pairs)], axis=0)
            if scale_is_pow2:
                q4 = q4 * scale
            s4 = lax.dot_general(q4, kd, (((1,), (1,)), ((), ())), preferred_element_type=F32)
            if not scale_is_pow2:
                s4 = s4 * scale
            e_rows, sink_rows = [], []
            for j in range(n_pairs):
                e_halves, sink_halves = [], []
                for half in range(2):
                    blk = (slice(j * CHUNK, (j + 1) * CHUNK), slice(half * KEY_PAD, (half + 1) * KEY_PAD))
                    s = s4[blk] + bias_ref[n, blk[0], blk[1]]
                    if masked:
                        s = jnp.where(valid, s, NEG_INF)
                    sink = sinks_ref[0, n * q_per_kv + 2 * j + half]
                    mx = jnp.maximum(jnp.max(s, axis=-1, keepdims=True), sink)
                    e_halves.append(jnp.exp(s - mx).astype(BF16))
                    sink_halves.append(jnp.exp(sink - mx))
                e_rows.append(jnp.concatenate(e_halves, axis=1))
                sink_rows.append(jnp.where(low_half, sink_halves[0], sink_halves[1]))
            od = jnp.dot(jnp.concatenate(e_rows, axis=0), jnp.concatenate([vd, ones_d], axis=1),
                         preferred_element_type=F32)
            o4 = od[:, 0:LANES] / (od[:, LANES:2 * LANES] + jnp.concatenate(sink_rows, axis=0))
            for j in range(n_pairs):
                o_sc[q_rows, (n * n_pairs + j) * LANES:(n * n_pairs + j + 1) * LANES] = (
                    o4[j * CHUNK:(j + 1) * CHUNK].astype(BF16))
            ya_parts.append(jnp.dot(ya_sc[...], wco_ref[:, len(ya_parts) * ya_cols:(len(ya_parts) + 1) * ya_cols],
                                    preferred_element_type=F32))
    if not sample:
        kvar[:, 0:window, :] = kvar[:, t_rows:t_rows + window, :]
    y_a = jnp.concatenate(ya_parts, axis=1)
    y_b = jnp.dot(o_sc[...], wao_ref[...], preferred_element_type=F32)

    m = jax.nn.sigmoid(ga_ref[...].astype(F32)) * y_a + jax.nn.sigmoid(gb_ref[...].astype(F32)) * y_b
    xp = x_ref[...] + jnp.dot(m.astype(BF16), wo_ref[...], preferred_element_type=F32)
    xp_ref[...] = xp

    h2 = _rms_rows(xp, g2_ref[...])
    h2_ref[...] = _pack_halves(h2).reshape(t_rows, 1, d_model // 2)
    lt = lax.dot_general(wr_ref[...], h2.astype(BF16), (((1,), (1,)), ((), ())),
                         preferred_element_type=F32) + br_ref[...]
    lg = lt[0:n_groups]
    eg = jnp.exp(lg - jnp.max(lg, axis=0, keepdims=True))
    gp = eg / jnp.sum(eg, axis=0, keepdims=True)
    gw = jnp.max(gp, axis=0, keepdims=True)
    gi = lax.broadcasted_iota(I32, gp.shape, 0).astype(F32)
    gsel = jnp.min(jnp.where(gp == gw, gi, float(n_groups)), axis=0, keepdims=True)
    el = jnp.zeros((epg, t_rows), F32)
    for g in range(n_groups):
        el = jnp.where(gsel == float(g), lt[GROUP_ROW0 + g * epg:GROUP_ROW0 + (g + 1) * epg], el)
    ei = lax.broadcasted_iota(I32, el.shape, 0).astype(F32)
    v1 = jnp.max(el, axis=0, keepdims=True)
    i1 = jnp.min(jnp.where(el == v1, ei, float(epg)), axis=0, keepdims=True)
    el2 = jnp.where(ei == i1, -jnp.inf, el)
    v2 = jnp.max(el2, axis=0, keepdims=True)
    i2 = jnp.min(jnp.where(el2 == v2, ei, float(epg)), axis=0, keepdims=True)
    a1 = jnp.exp(v1 - v1)
    a2 = jnp.exp(v2 - v1)
    den = a1 + a2
    c1 = gw * (a1 / den)
    c2 = gw * (a2 / den)
    e1 = (gsel * float(epg) + i1).astype(I32)
    e2 = (gsel * float(epg) + i2).astype(I32)
    row8 = lax.broadcasted_iota(I32, (SUBLANES, t_rows), 0)
    re_ref[...] = jnp.where(row8 == 0, e1, jnp.where(row8 == 1, e2, 0))
    rc_ref[...] = jnp.where(row8 == 0, c1, jnp.where(row8 == 1, c2, 0.0))


def _mix(x2d, proj, kv, caches, weights, *, n_seq, sample, dims):
    n_heads, n_kv, head_dim, window, n_groups, n_exp, d_conv = dims
    n, d = x2d.shape
    t = MIX_ROWS
    n_chunks = t // CHUNK
    span = window + CHUNK
    d_attn = n_heads * head_dim
    d_kv = n_kv * head_dim
    convw, wco, wao, wo, bias, sinks, g2, wr, br = weights
    if sample:
        n_t = n // t
        grid = (n_t, 1)
        tok = lambda i, j: (i, 0)
        n_state = n // CHUNK
        state_blk = n_chunks
        st = lambda i, j: (i, 0, 0)
    else:
        n_t = (n // n_seq) // t
        grid = (n_seq, n_t)
        tok = lambda i, j: (i * n_t + j, 0)
        n_state = n_seq
        state_blk = 1
        st = lambda i, j: (i, 0, 0)
    const2 = lambda i, j: (0, 0)

    def col(width, idx):
        return pl.BlockSpec((t, width), lambda i, j: (tok(i, j)[0], idx))

    def resident(shape):
        return pl.BlockSpec(shape, const2, pipeline_mode=pl.Buffered(1))

    in_specs = [
        pl.BlockSpec((t, d), tok),
        col(d, 2), col(d, 3),
        col(d_conv, 0), col(d_conv, 1), col(d_conv, 2), col(d_attn, 3),
        pl.BlockSpec((t, 2 * d_kv), tok),
    ]
    args = [x2d, proj, proj, proj, proj, proj, proj, kv]
    if sample:
        cconv, ck, cv = caches
        in_specs += [
            pl.BlockSpec((n_chunks, cconv.shape[1], d_conv), st),
            pl.BlockSpec((n_chunks, window, d_kv), st),
            pl.BlockSpec((n_chunks, window, d_kv), st),
        ]
        args += [cconv, ck, cv]
    in_specs += [
        resident(convw.shape), resident(wco.shape), resident(wao.shape), resident(wo.shape),
        pl.BlockSpec(bias.shape, lambda i, j: (0, 0, 0), pipeline_mode=pl.Buffered(1)),
        pl.BlockSpec(memory_space=pltpu.SMEM),
        resident(g2.shape), resident(wr.shape), resident(br.shape),
    ]
    args += [convw, wco, wao, wo, bias, sinks, g2, wr, br]
    out_specs = [
        pl.BlockSpec((t, d), tok),
        pl.BlockSpec((t, 1, d // 2), lambda i, j: (tok(i, j)[0], 0, 0)),
        pl.BlockSpec((SUBLANES, t), lambda i, j: (0, tok(i, j)[0])),
        pl.BlockSpec((SUBLANES, t), lambda i, j: (0, tok(i, j)[0])),
        pl.BlockSpec((state_blk, 2, d_conv), st),
        pl.BlockSpec((state_blk, window, d_kv), st),
        pl.BlockSpec((state_blk, window, d_kv), st),
    ]
    out_shape = [
        jax.ShapeDtypeStruct((n, d), F32),
        jax.ShapeDtypeStruct((n, 1, d // 2), U32),
        jax.ShapeDtypeStruct((SUBLANES, n), I32),
        jax.ShapeDtypeStruct((SUBLANES, n), F32),
        jax.ShapeDtypeStruct((n_state, 2, d_conv), F32),
        jax.ShapeDtypeStruct((n_state, window, d_kv), F32),
        jax.ShapeDtypeStruct((n_state, window, d_kv), F32),
    ]
    kv_rows = n_chunks * span if sample else window + t
    scratch = [
        pltpu.VMEM((2 * KV_VARIANTS, kv_rows, d_kv), BF16),
        pltpu.VMEM((t, d_attn), BF16), pltpu.VMEM((t, d_conv), BF16),
        pltpu.VMEM((SUBLANES, d_conv), F32),
    ]
    est = (2 * t * d * 4 * 3 + 2 * 2 * t * d * 2 + 2 * 4 * t * d_conv * 2
           + (2 * d_conv * d + d * d) * 2 + 12 * t * d * 4)
    kern = functools.partial(_mix_kernel, sample=sample, n_heads=n_heads, n_kv=n_kv, head_dim=head_dim,
                             window=window, n_groups=n_groups, n_exp=n_exp)
    return pl.pallas_call(
        kern, grid=grid, in_specs=in_specs, out_specs=out_specs, out_shape=out_shape,
        scratch_shapes=scratch,
        compiler_params=pltpu.CompilerParams(
            dimension_semantics=("arbitrary", "arbitrary"), vmem_limit_bytes=_vmem_limit(est)),
        name="mix_sample" if sample else "mix_prompt",
    )(*args)


def _route_kernel(re_ref, pos_ref, tmap_ref, *, n_exp, tile_rows, blk):
    n = re_ref.shape[1]
    n_blk = n // blk
    erow = lax.broadcasted_iota(I32, (n_exp, blk), 0)

    def onehots(j):
        c0 = pl.multiple_of(j * blk, blk)
        oh0 = (erow == re_ref[0:1, pl.ds(c0, blk)]).astype(F32)
        oh1 = (erow == re_ref[1:2, pl.ds(c0, blk)]).astype(F32)
        return c0, oh0, oh1

    def count_body(j, cnt):
        _, oh0, oh1 = onehots(j)
        return cnt + jnp.sum(oh0 + oh1, axis=1, keepdims=True)

    cnt = lax.fori_loop(0, n_blk, count_body, jnp.zeros((n_exp, 1), F32))
    cnt = jnp.broadcast_to(cnt, (n_exp, LANES))
    padded = jnp.ceil(cnt / tile_rows) * tile_rows
    ends = padded
    prow = lax.broadcasted_iota(I32, ends.shape, 0)
    step = 1
    while step < n_exp:
        ends = ends + jnp.where(prow >= step, pltpu.roll(ends, step, axis=0), 0.0)
        step *= 2
    offs = ends - padded
    off1 = offs[:, 0:1]

    tri = (lax.broadcasted_iota(I32, (blk, blk), 0) <= lax.broadcasted_iota(I32, (blk, blk), 1)).astype(BF16)
    row8 = lax.broadcasted_iota(I32, (SUBLANES, blk), 0)

    def pos_body(j, run):
        c0, oh0, oh1 = onehots(j)
        both = oh0 + oh1
        csum = jnp.dot(both.astype(BF16), tri, preferred_element_type=F32) + run
        slot = off1 + csum - 1.0
        p0 = jnp.sum(oh0 * slot, axis=0, keepdims=True).astype(I32)
        p1 = jnp.sum(oh1 * slot, axis=0, keepdims=True).astype(I32)
        pos_ref[:, pl.ds(c0, blk)] = jnp.where(row8 == 0, p0, jnp.where(row8 == 1, p1, 0))
        return run + jnp.sum(both, axis=1, keepdims=True)

    lax.fori_loop(0, n_blk, pos_body, jnp.zeros((n_exp, 1), F32))

    n_tiles_pad = tmap_ref.shape[1]
    start = (lax.broadcasted_iota(I32, (1, n_tiles_pad), 1) * tile_rows).astype(F32)
    end1 = ends[:, 0:1]
    te = jnp.sum((end1 <= start).astype(F32), axis=0, keepdims=True)
    trow = lax.broadcasted_iota(I32, (n_exp, n_tiles_pad), 0).astype(F32)
    used_end = jnp.sum(jnp.where(trow == te, off1 + cnt[:, 0:1], 0.0), axis=0, keepdims=True)
    n_rows = jnp.clip(used_end - start, 0.0, float(tile_rows))
    has_tokens = cnt[:, 0:1] > 0.0
    te = jnp.minimum(te, jnp.max(jnp.where(has_tokens, trow, 0.0), axis=0, keepdims=True))
    nxt = jnp.min(jnp.where(trow > te, jnp.where(has_tokens, trow, float(n_exp)), float(n_exp)), axis=0, keepdims=True)
    r8 = lax.broadcasted_iota(I32, (SUBLANES, n_tiles_pad), 0)
    tmap_ref[...] = jnp.where(r8 == 0, te.astype(I32),
                              jnp.where(r8 == 1, n_rows.astype(I32), jnp.where(r8 == 2, nxt.astype(I32), 0)))


def _route(re, n_exp, n_tiles):
    n = re.shape[1]
    assert n % ROUTE_BLOCK == 0
    n_tiles_pad = pl.cdiv(n_tiles, LANES) * LANES
    kern = functools.partial(_route_kernel, n_exp=n_exp, tile_rows=SLOT_ROWS, blk=ROUTE_BLOCK)
    return pl.pallas_call(
        kern,
        out_shape=[jax.ShapeDtypeStruct((SUBLANES, n), I32), jax.ShapeDtypeStruct((SUBLANES, n_tiles_pad), I32)],
        name="route",
    )(re)


def _dispatch_kernel(*refs, aliased):
    if aliased:
        p0_ref, p1_ref, h_ref, _, xs_ref, sem = refs
    else:
        p0_ref, p1_ref, h_ref, xs_ref, sem = refs
    rows = h_ref.shape[0]

    def start(t, carry):
        pltpu.make_async_copy(h_ref.at[t], xs_ref.at[p0_ref[t]], sem).start(priority=0)
        pltpu.make_async_copy(h_ref.at[t], xs_ref.at[p1_ref[t]], sem).start(priority=1)
        return carry

    lax.fori_loop(0, rows, start, 0)
    for _ in range(TOP_K):
        pltpu.make_async_copy(h_ref, xs_ref.at[pl.ds(0, rows)], sem).wait()


def _dispatch(h2, p0, p1, xs, n_slots):
    n, _, d = h2.shape
    t = min(MOVE_ROWS, n)
    assert n % t == 0
    aliased = xs is not None
    in_specs = [
        pl.BlockSpec((t,), lambda i: (i,), memory_space=pltpu.SMEM),
        pl.BlockSpec((t,), lambda i: (i,), memory_space=pltpu.SMEM),
        pl.BlockSpec((t, 1, d), lambda i: (i, 0, 0)),
    ]
    args = [p0, p1, h2]
    if aliased:
        in_specs.append(pl.BlockSpec(memory_space=pl.ANY))
        args.append(xs)
    return pl.pallas_call(
        functools.partial(_dispatch_kernel, aliased=aliased),
        grid=(n // t,),
        in_specs=in_specs,
        out_specs=pl.BlockSpec(memory_space=pl.ANY),
        out_shape=jax.ShapeDtypeStruct((n_slots, 1, d), h2.dtype),
        scratch_shapes=[pltpu.SemaphoreType.DMA(())],
        input_output_aliases={3: 0} if aliased else {},
        compiler_params=pltpu.CompilerParams(dimension_semantics=("arbitrary",), has_side_effects=True),
        name="dispatch",
    )(*args)


def _experts_kernel(te_ref, nr_ref, nx_ref, xs_ref, wg_hbm, wu_hbm, wd_hbm, y_ref,
                    wg_st, wu_st, wd_st, wg_sc, wu_sc, wd_sc, sems, n_changes, *, n_exp, tile_rows):
    dh = xs_ref.shape[2]
    tiles_per_step = xs_ref.shape[0] // tile_rows

    def copies(e, slot):
        return (pltpu.make_async_copy(wg_hbm.at[e], wg_st.at[slot], sems.at[slot]),
                pltpu.make_async_copy(wu_hbm.at[e], wu_st.at[slot], sems.at[slot]),
                pltpu.make_async_copy(wd_hbm.at[e], wd_st.at[slot], sems.at[slot]))

    def one_tile(sub):
        tile = pl.program_id(0) * tiles_per_step + sub
        rows = pl.ds(sub * tile_rows, tile_rows)
        expert = te_ref[tile]

        @pl.when(tile == 0)
        def _():
            n_changes[0] = 0
            for cp in copies(expert, 0):
                cp.start(priority=1)

        @pl.when(jnp.logical_or(tile == 0, expert != te_ref[jnp.maximum(tile - 1, 0)]))
        def _():
            slot = lax.rem(n_changes[0], 2)
            n_changes[0] = n_changes[0] + 1
            for cp in copies(expert, slot):
                cp.wait()
            nxt = nx_ref[tile]

            @pl.when(nxt < n_exp)
            def _():
                for cp in copies(nxt, 1 - slot):
                    cp.start(priority=1)

            wg_sc[...] = wg_st[slot].astype(BF16)
            wu_sc[...] = wu_st[slot].astype(BF16)
            wd_sc[...] = wd_st[slot].astype(BF16)

        n_rows = nr_ref[tile]

        @pl.when(n_rows > 0)
        def _():
            live = lax.broadcasted_iota(I32, (tile_rows, 2 * dh), 0) < n_rows
            x = jnp.where(live, _unpack_halves(xs_ref[rows].reshape(tile_rows, dh)), 0.0).astype(BF16)
            gate = jnp.dot(x, wg_sc[...], preferred_element_type=F32)
            up = jnp.dot(x, wu_sc[...], preferred_element_type=F32)
            hid = (jax.nn.silu(gate) * up).astype(BF16)
            y = jnp.dot(hid, wd_sc[...], preferred_element_type=F32)
            y_ref[rows] = _pack_halves(y).reshape(tile_rows, 1, dh)

        @pl.when(n_rows <= 0)
        def _():
            y_ref[rows] = _pack_halves(jnp.zeros((tile_rows, 2 * dh), F32)).reshape(tile_rows, 1, dh)

    for sub in range(tiles_per_step):
        one_tile(sub)


def _experts(xs, te, nr, nx, w_gate, w_up, w_down):
    n_slots, _, dh = xs.shape
    n_exp, d, d_e = w_gate.shape
    t = SLOT_ROWS * EXPERT_STEP_TILES
    est = 2 * 2 * t * d * 2 + 2 * 3 * d * d_e * 4 + 3 * d * d_e * 2 + 4 * SLOT_ROWS * d * 4
    row_spec = pl.BlockSpec((t, 1, dh), lambda i, te, nr, nx: (i, 0, 0))
    return pl.pallas_call(
        functools.partial(_experts_kernel, n_exp=n_exp, tile_rows=SLOT_ROWS),
        grid_spec=pltpu.PrefetchScalarGridSpec(
            num_scalar_prefetch=3,
            grid=(n_slots // t,),
            in_specs=[row_spec, pl.BlockSpec(memory_space=pl.ANY), pl.BlockSpec(memory_space=pl.ANY),
                      pl.BlockSpec(memory_space=pl.ANY)],
            out_specs=row_spec,
            scratch_shapes=[
                pltpu.VMEM((2, d, d_e), F32), pltpu.VMEM((2, d, d_e), F32), pltpu.VMEM((2, d_e, d), F32),
                pltpu.VMEM((d, d_e), BF16), pltpu.VMEM((d, d_e), BF16), pltpu.VMEM((d_e, d), BF16),
                pltpu.SemaphoreType.DMA((2,)), pltpu.SMEM((1,), I32)],
        ),
        out_shape=jax.ShapeDtypeStruct((n_slots, 1, dh), U32),
        compiler_params=pltpu.CompilerParams(
            dimension_semantics=("arbitrary",), vmem_limit_bytes=_vmem_limit(est)),
        name="experts",
    )(te, nr, nx, xs, w_gate, w_up, w_down)


def _combine_kernel(p0_ref, p1_ref, p0n_ref, p1n_ref, xp_ref, rc_ref, g_ref, y_hbm, out_ref, y0_buf, y1_buf, sems,
                    *, n_tiles):
    i = pl.program_id(0)
    rows, d = xp_ref.shape

    def gather(pa_ref, pb_ref, slot):
        base = slot * rows
        def body(t, carry):
            pltpu.make_async_copy(y_hbm.at[pa_ref[t]], y0_buf.at[base + t], sems.at[slot]).start(priority=0)
            pltpu.make_async_copy(y_hbm.at[pb_ref[t]], y1_buf.at[base + t], sems.at[slot]).start(priority=1)
            return carry
        lax.fori_loop(0, rows, body, 0)

    slot = lax.rem(i, 2)

    @pl.when(i == 0)
    def _():
        gather(p0_ref, p1_ref, 0)

    @pl.when(i + 1 < n_tiles)
    def _():
        gather(p0n_ref, p1n_ref, 1 - slot)

    cur = pl.ds(pl.multiple_of(slot * rows, rows), rows)
    for buf in (y0_buf, y1_buf):
        pltpu.make_async_copy(y_hbm.at[pl.ds(0, rows)], buf.at[cur], sems.at[slot]).wait()
    ct = rc_ref[...].T
    y0 = _unpack_halves(y0_buf[cur].reshape(rows, d // 2))
    y1 = _unpack_halves(y1_buf[cur].reshape(rows, d // 2))
    moe = ct[:, 0:1] * y0 + ct[:, 1:2] * y1
    out_ref[...] = _rms_rows(xp_ref[...] + moe, g_ref[...])


def _combine(xp, rc, p0, p1, y, g):
    n, d = xp.shape
    t = min(MOVE_ROWS, n)
    assert n % t == 0
    n_t = n // t
    est = 2 * 2 * t * d * 4 + 2 * 2 * t * d * 2 + 8 * t * d * 4
    nxt = lambda i: (jnp.minimum(i + 1, n_t - 1),)
    return pl.pallas_call(
        functools.partial(_combine_kernel, n_tiles=n_t),
        grid=(n_t,),
        in_specs=[
            pl.BlockSpec((t,), lambda i: (i,), memory_space=pltpu.SMEM),
            pl.BlockSpec((t,), lambda i: (i,), memory_space=pltpu.SMEM),
            pl.BlockSpec((t,), nxt, memory_space=pltpu.SMEM),
            pl.BlockSpec((t,), nxt, memory_space=pltpu.SMEM),
            pl.BlockSpec((t, d), lambda i: (i, 0)),
            pl.BlockSpec((SUBLANES, t), lambda i: (0, i)),
            pl.BlockSpec((1, d), lambda i: (0, 0)),
            pl.BlockSpec(memory_space=pl.ANY),
        ],
        out_specs=pl.BlockSpec((t, d), lambda i: (i, 0)),
        out_shape=jax.ShapeDtypeStruct((n, d), F32),
        scratch_shapes=[pltpu.VMEM((2 * t, 1, d // 2), U32), pltpu.VMEM((2 * t, 1, d // 2), U32),
                        pltpu.SemaphoreType.DMA((2,))],
        compiler_params=pltpu.CompilerParams(
            dimension_semantics=("arbitrary",), vmem_limit_bytes=_vmem_limit(est)),
        name="combine",
    )(p0, p1, p0, p1, xp, rc, g, y)


def _rel_buckets(rel):
    nb = N_BUCKETS // 2
    ret = (rel > 0).astype(I32) * nb
    n = jnp.abs(rel)
    max_exact = nb // 2
    nf = jnp.maximum(n, 1).astype(F32)
    large = max_exact + (jnp.log(nf / max_exact) / math.log(MAX_DISTANCE / max_exact)
                         * (nb - max_exact)).astype(I32)
    large = jnp.minimum(large, nb - 1)
    return ret + jnp.where(n < max_exact, n, large)


def _rel_bias(table, window, n_kv):
    n_heads = table.shape[1]
    span = window + CHUNK
    rows = table[_rel_buckets(jnp.arange(-(span - 1), CHUNK, dtype=I32))].astype(F32)
    bias = jnp.stack([rows[CHUNK - 1 - q:CHUNK - 1 - q + span] for q in range(CHUNK)])
    bias = jnp.transpose(bias, (2, 0, 1))
    bias = jnp.pad(bias, ((0, 0), (0, 0), (0, KEY_PAD - bias.shape[2])), constant_values=NEG_INF)
    bias = bias.reshape(n_kv, n_heads // n_kv // 2, 2, CHUNK, KEY_PAD)
    return jnp.transpose(bias, (0, 1, 3, 2, 4)).reshape(n_kv, (n_heads // n_kv // 2) * CHUNK, 2 * KEY_PAD)


def kernel(x_prompt, x_sample, cache_conv, cache_k, cache_v, rel_bias_table, norm_mix_g, w_in, conv_w, w_conv_out, attn_sinks, w_attn_out, w_o, norm_ffn_g, w_group, b_group, w_expert_router, b_expert_router, w_gate, w_up, w_down, final_norm_g):
    assert w_in.shape[0] == 1, "single-layer step"
    batch, seq, d = x_prompt.shape
    dec_batch, dec_seq, _ = x_sample.shape
    assert dec_seq == CHUNK and seq % MIX_ROWS == 0 and (dec_batch * dec_seq) % MIX_ROWS == 0
    d_conv = conv_w.shape[-1]
    window, n_kv, head_dim = cache_k.shape[2], cache_k.shape[3], cache_k.shape[4]
    n_heads = attn_sinks.shape[-1]
    d_attn, d_kv = n_heads * head_dim, n_kv * head_dim
    n_groups, n_exp = w_group.shape[-1], w_expert_router.shape[-1]
    assert n_groups <= GROUP_ROW0 and d_conv == d_attn and 2 * d_conv == d
    assert n_kv == 2 and d_kv == LANES and (n_heads // n_kv) % 2 == 0 and window + CHUNK <= KEY_PAD
    dims = (n_heads, n_kv, head_dim, window, n_groups, n_exp, d_conv)

    w_all = w_in[0].astype(BF16)
    kv0 = 3 * d_conv + d_attn
    g1 = norm_mix_g[0][None, :]
    wr = jnp.zeros((GROUP_ROW0 + n_exp, d), F32)
    wr = wr.at[:n_groups].set(w_group[0].T).at[GROUP_ROW0:].set(w_expert_router[0].T).astype(BF16)
    br = jnp.zeros((GROUP_ROW0 + n_exp, 1), F32)
    br = br.at[:n_groups, 0].set(b_group[0]).at[GROUP_ROW0:, 0].set(b_expert_router[0])
    weights = (conv_w[0], w_conv_out[0].astype(BF16), w_attn_out[0].astype(BF16), w_o[0].astype(BF16),
               _rel_bias(rel_bias_table, window, n_kv), attn_sinks, norm_ffn_g[0][None, :], wr, br)

    xp2d = x_prompt.reshape(batch * seq, d)
    xs2d = x_sample.reshape(dec_batch * dec_seq, d)
    n_p, n_s = xp2d.shape[0], xs2d.shape[0]
    n_tok = n_p + n_s

    proj_p, kv_p = _inproj(xp2d, g1, w_all, kv0, 2 * d_kv)
    proj_s, kv_s = _inproj(xs2d, g1, w_all, kv0, 2 * d_kv)
    xres_p, h2_p, re_p, rc_p, conv_p, k_p, v_p = _mix(
        xp2d, proj_p, kv_p, None, weights, n_seq=batch, sample=False, dims=dims)
    caches = (cache_conv[0], cache_k[0].reshape(dec_batch, window, d_kv), cache_v[0].reshape(dec_batch, window, d_kv))
    xres_s, h2_s, re_s, rc_s, conv_s, k_s, v_s = _mix(
        xs2d, proj_s, kv_s, caches, weights, n_seq=dec_batch, sample=True, dims=dims)

    n_tiles = pl.cdiv((TOP_K * n_tok) // SLOT_ROWS + n_exp, EXPERT_STEP_TILES) * EXPERT_STEP_TILES
    n_slots = n_tiles * SLOT_ROWS
    pos, tmap = _route(jnp.concatenate([re_p, re_s], axis=1), n_exp, n_tiles)
    p0, p1 = pos[0], pos[1]
    xs = _dispatch(h2_p, p0[:n_p], p1[:n_p], None, n_slots)
    xs = _dispatch(h2_s, p0[n_p:], p1[n_p:], xs, n_slots)
    y = _experts(xs, tmap[0, :n_tiles], tmap[1, :n_tiles], tmap[2, :n_tiles], w_gate[0], w_up[0], w_down[0])
    gf = final_norm_g[None, :]
    y_prompt = _combine(xres_p, rc_p, p0[:n_p], p1[:n_p], y, gf).reshape(batch, seq, d)
    y_sample = _combine(xres_s, rc_s, p0[n_p:], p1[n_p:], y, gf).reshape(dec_batch, dec_seq, d)

    kv_shape = (1, -1, window, n_kv, head_dim)
    return (y_prompt, y_sample, conv_p[None], k_p.reshape(kv_shape), v_p.reshape(kv_shape),
            conv_s[None], k_s.reshape(kv_shape), v_s.reshape(kv_shape))
```

```python
import functools
import math

import jax
import jax.numpy as jnp
from jax import lax
from jax.experimental import pallas as pl
from jax.experimental.pallas import tpu as pltpu

F32, BF16, I32, U32 = jnp.float32, jnp.bfloat16, jnp.int32, jnp.uint32

CHUNK = 64
N_BUCKETS = 32
MAX_DISTANCE = 128
EPS = 1e-6
NEG_INF = -1e30
TOP_K = 2

V7X_VMEM_BYTES = 64 * 1024 * 1024
SUBLANES = 8
LANES = 128

INPROJ_ROWS = 1024
INPROJ_COLS = 2048
MIX_ROWS = 256
MOVE_ROWS = 512
SLOT_ROWS = 256
EXPERT_STEP_TILES = 4
ROUTE_BLOCK = 512
GROUP_ROW0 = 8
KEY_PAD = 256
KV_VARIANTS = 4


def _vmem_limit(nbytes):
    return int(min(V7X_VMEM_BYTES - (4 << 20), max(nbytes, 32 << 20)))


def _pack_halves(x):
    half = x.shape[1] // 2
    return pltpu.pack_elementwise([x[:, :half], x[:, half:]], packed_dtype=BF16)


def _unpack_halves(w):
    lo = pltpu.unpack_elementwise(w, index=0, packed_dtype=BF16, unpacked_dtype=F32)
    hi = pltpu.unpack_elementwise(w, index=1, packed_dtype=BF16, unpacked_dtype=F32)
    return jnp.concatenate([lo, hi], axis=1)


def _rms_rows(x, g):
    r = lax.rsqrt(jnp.mean(x * x, axis=-1, keepdims=True) + EPS)
    return (x * r) * g


def _inproj_kernel(x_ref, g_ref, wm_ref, wkv_ref, proj_ref, kv_ref, h_sc):
    @pl.when(pl.program_id(1) == 0)
    def _():
        rows = 128
        def body(i, carry):
            r0 = pl.multiple_of(i * rows, rows)
            h_sc[pl.ds(r0, rows), :] = _rms_rows(x_ref[pl.ds(r0, rows), :], g_ref[...]).astype(BF16)
            return carry
        lax.fori_loop(0, x_ref.shape[0] // rows, body, 0)
        kv_ref[...] = jnp.dot(h_sc[...], wkv_ref[...], preferred_element_type=F32)

    proj_ref[...] = jnp.dot(h_sc[...], wm_ref[...], preferred_element_type=F32).astype(BF16)


def _inproj(x2d, g, w_all, kv0, n_kv):
    n, d = x2d.shape
    tm = min(INPROJ_ROWS, n)
    tn = INPROJ_COLS
    n_main = w_all.shape[1] - n_kv
    assert kv0 % tn == 0 and n_main % tn == 0 and n_kv % LANES == 0
    est = 2 * tm * d * 4 + tm * d * 2 + 2 * d * tn * 2 + 2 * d * n_kv * 2 + 2 * tm * tn * 2 + 2 * tm * n_kv * 4 + tm * tn * 4
    return pl.pallas_call(
        _inproj_kernel,
        grid=(n // tm, n_main // tn),
        in_specs=[
            pl.BlockSpec((tm, d), lambda i, j: (i, 0)),
            pl.BlockSpec((1, d), lambda i, j: (0, 0)),
            pl.BlockSpec((pl.Element(d), pl.Element(tn)),
                         lambda i, j: (0, pl.multiple_of(jnp.where(j * tn < kv0, j * tn, j * tn + n_kv), LANES))),
            pl.BlockSpec((pl.Element(d), pl.Element(n_kv)), lambda i, j: (0, kv0)),
        ],
        out_specs=[
            pl.BlockSpec((tm, tn), lambda i, j: (i, j)),
            pl.BlockSpec((tm, n_kv), lambda i, j: (i, 0)),
        ],
        out_shape=[jax.ShapeDtypeStruct((n, n_main), BF16), jax.ShapeDtypeStruct((n, n_kv), F32)],
        scratch_shapes=[pltpu.VMEM((tm, d), BF16)],
        compiler_params=pltpu.CompilerParams(
            dimension_semantics=("arbitrary", "arbitrary"), vmem_limit_bytes=_vmem_limit(est + (8 << 20))),
        name="inproj",
    )(x2d, g, w_all, w_all)


def _conv_rows(u, prev2, prev1, w):
    row = lax.broadcasted_iota(I32, u.shape, 0)
    u1 = jnp.where(row == 0, prev1, pltpu.roll(u, 1, axis=0))
    u2 = jnp.where(row == 0, prev2, jnp.where(row == 1, prev1, pltpu.roll(u, 2, axis=0)))
    return (w[0:1] * u2 + w[1:2] * u1) + w[2:3] * u


def _mix_kernel(*refs, sample, n_heads, n_kv, head_dim, window, n_groups, n_exp):
    if sample:
        (x_ref, ga_ref, gb_ref, b_ref, c_ref, xc_ref, q_ref, kv_ref, cconv_ref, ck_ref, cv_ref,
         convw_ref, wco_ref, wao_ref, wo_ref, bias_ref, sinks_ref, g2_ref, wr_ref, br_ref,
         xp_ref, h2_ref, re_ref, rc_ref, sconv_ref, sk_ref, sv_ref,
         kvar, o_sc, ya_sc, carry_u) = refs
    else:
        (x_ref, ga_ref, gb_ref, b_ref, c_ref, xc_ref, q_ref, kv_ref,
         convw_ref, wco_ref, wao_ref, wo_ref, bias_ref, sinks_ref, g2_ref, wr_ref, br_ref,
         xp_ref, h2_ref, re_ref, rc_ref, sconv_ref, sk_ref, sv_ref,
         kvar, o_sc, ya_sc, carry_u) = refs
    t_rows, d_model = x_ref.shape
    n_chunks = t_rows // CHUNK
    span = window + CHUNK
    q_per_kv = n_heads // n_kv
    d_kv = n_kv * head_dim
    scale = 1.0 / math.sqrt(head_dim)
    scale_is_pow2 = math.frexp(scale)[0] == 0.5
    epg = n_exp // n_groups
    seq_start = pl.program_id(1) == 0

    w_conv = convw_ref[...]
    if sample:
        for s in range(n_chunks):
            rows = slice(s * CHUNK, (s + 1) * CHUNK)
            u = c_ref[rows, :].astype(F32) * xc_ref[rows, :].astype(F32)
            y = _conv_rows(u, cconv_ref[s, 0:1, :], cconv_ref[s, 1:2, :], w_conv)
            ya_sc[rows, :] = (b_ref[rows, :].astype(F32) * y).astype(BF16)
            sconv_ref[s] = u[CHUNK - 2:CHUNK, :]
    else:
        u = c_ref[...].astype(F32) * xc_ref[...].astype(F32)
        prev = jnp.where(seq_start, 0.0, carry_u[...])
        y = _conv_rows(u, prev[SUBLANES - 2:SUBLANES - 1], prev[SUBLANES - 1:SUBLANES], w_conv)
        ya_sc[...] = (b_ref[...].astype(F32) * y).astype(BF16)
        carry_u[...] = u[t_rows - SUBLANES:t_rows, :]
        sconv_ref[0] = u[t_rows - 2:t_rows, :]

    def store_kv(row0, k_rows, v_rows):
        n_rows = k_rows.shape[0]
        low = lax.broadcasted_iota(I32, k_rows.shape, 1) < head_dim
        for base, a in ((0, k_rows), (KV_VARIANTS, v_rows)):
            b = pltpu.roll(a, head_dim, axis=1)
            kvar[base + 0, row0:row0 + n_rows, :] = jnp.where(low, a, 0.0).astype(BF16)
            kvar[base + 1, row0:row0 + n_rows, :] = jnp.where(low, 0.0, a).astype(BF16)
            kvar[base + 2, row0:row0 + n_rows, :] = jnp.where(low, b, 0.0).astype(BF16)
            kvar[base + 3, row0:row0 + n_rows, :] = jnp.where(low, 0.0, b).astype(BF16)

    if sample:
        for s in range(n_chunks):
            rows = slice(s * CHUNK, (s + 1) * CHUNK)
            store_kv(s * span, ck_ref[s], cv_ref[s])
            store_kv(s * span + window, kv_ref[rows, 0:d_kv], kv_ref[rows, d_kv:2 * d_kv])
            sk_ref[s, 0:window - CHUNK, :] = ck_ref[s, CHUNK:window, :]
            sv_ref[s, 0:window - CHUNK, :] = cv_ref[s, CHUNK:window, :]
            sk_ref[s, window - CHUNK:window, :] = kv_ref[rows, 0:d_kv]
            sv_ref[s, window - CHUNK:window, :] = kv_ref[rows, d_kv:2 * d_kv]
        k_stride = span
    else:
        @pl.when(seq_start)
        def _():
            kvar[:, 0:window, :] = jnp.zeros((2 * KV_VARIANTS, window, d_kv), BF16)
        store_kv(window, kv_ref[:, 0:d_kv], kv_ref[:, d_kv:2 * d_kv])
        sk_ref[0] = kv_ref[t_rows - window:t_rows, 0:d_kv]
        sv_ref[0] = kv_ref[t_rows - window:t_rows, d_kv:2 * d_kv]
        k_stride = CHUNK

    n_pairs = q_per_kv // 2
    key_pad_rows = jnp.zeros((KEY_PAD - span, d_kv), BF16)
    ones_d = ((lax.broadcasted_iota(I32, (2 * KEY_PAD, LANES), 0) < KEY_PAD)
              == (lax.broadcasted_iota(I32, (2 * KEY_PAD, LANES), 1) < head_dim)).astype(BF16)
    low_half = lax.broadcasted_iota(I32, (CHUNK, LANES), 1) < head_dim
    ya_parts = []
    ya_cols = d_model // (n_chunks * n_kv)
    for c in range(n_chunks):
        q_rows = slice(c * CHUNK, (c + 1) * CHUNK)
        k_rows = slice(c * k_stride, c * k_stride + span)
        masked = (not sample) and c * CHUNK < window
        if masked:
            first_key = (pl.program_id(1) * n_chunks + c) * CHUNK - window
            valid = lax.broadcasted_iota(I32, (CHUNK, KEY_PAD), 1) + first_key >= 0
        for n in range(n_kv):
            top, bot = (0, 3) if n == 0 else (2, 1)
            kd = jnp.concatenate([kvar[top, k_rows, :], key_pad_rows, kvar[bot, k_rows, :], key_pad_rows], axis=0)
            vd = jnp.concatenate([kvar[KV_VARIANTS + top, k_rows, :], key_pad_rows,
                                  kvar[KV_VARIANTS + bot, k_rows, :], key_pad_rows], axis=0)
            q4 = jnp.concatenate(
                [q_ref[q_rows, (n * n_pairs + j) * LANES:(n * n_pairs + j + 1) * LANES] for j in range(n_pairs)], axis=0)
            if scale_is_pow2:
                q4 = q4 * scale
            s4 = lax.dot_general(q4, kd, (((1,), (1,)), ((), ())), preferred_element_type=F32)
            if not scale_is_pow2:
                s4 = s4 * scale
            e_rows, sink_rows = [], []
            for j in range(n_pairs):
                e_halves, sink_halves = [], []
                for half in range(2):
                    blk = (slice(j * CHUNK, (j + 1) * CHUNK), slice(half * KEY_PAD, (half + 1) * KEY_PAD))
                    s = s4[blk] + bias_ref[n, blk[0], blk[1]]
                    if masked:
                        s = jnp.where(valid, s, NEG_INF)
                    sink = sinks_ref[0, n * q_per_kv + 2 * j + half]
                    mx = jnp.maximum(jnp.max(s, axis=-1, keepdims=True), sink)
                    e_halves.append(jnp.exp(s - mx).astype(BF16))
                    sink_halves.append(jnp.exp(sink - mx))
                e_rows.append(jnp.concatenate(e_halves, axis=1))
                sink_rows.append(jnp.where(low_half, sink_halves[0], sink_halves[1]))
            od = jnp.dot(jnp.concatenate(e_rows, axis=0), jnp.concatenate([vd, ones_d], axis=1),
                         preferred_element_type=F32)
            o4 = od[:, 0:LANES] / (od[:, LANES:2 * LANES] + jnp.concatenate(sink_rows, axis=0))
            for j in range(n_pairs):
                o_sc[q_rows, (n * n_pairs + j) * LANES:(n * n_pairs + j + 1) * LANES] = (
                    o4[j * CHUNK:(j + 1) * CHUNK].astype(BF16))
            ya_parts.append(jnp.dot(ya_sc[...], wco_ref[:, len(ya_parts) * ya_cols:(len(ya_parts) + 1) * ya_cols],
                                    preferred_element_type=F32))
    if not sample:
        kvar[:, 0:window, :] = kvar[:, t_rows:t_rows + window, :]
    y_a = jnp.concatenate(ya_parts, axis=1)
    y_b = jnp.dot(o_sc[...], wao_ref[...], preferred_element_type=F32)

    m = jax.nn.sigmoid(ga_ref[...].astype(F32)) * y_a + jax.nn.sigmoid(gb_ref[...].astype(F32)) * y_b
    xp = x_ref[...] + jnp.dot(m.astype(BF16), wo_ref[...], preferred_element_type=F32)
    xp_ref[...] = xp

    h2 = _rms_rows(xp, g2_ref[...])
    h2_ref[...] = _pack_halves(h2).reshape(t_rows, 1, d_model // 2)
    lt = lax.dot_general(wr_ref[...], h2.astype(BF16), (((1,), (1,)), ((), ())),
                         preferred_element_type=F32) + br_ref[...]
    lg = lt[0:n_groups]
    eg = jnp.exp(lg - jnp.max(lg, axis=0, keepdims=True))
    gp = eg / jnp.sum(eg, axis=0, keepdims=True)
    gw = jnp.max(gp, axis=0, keepdims=True)
    gi = lax.broadcasted_iota(I32, gp.shape, 0).astype(F32)
    gsel = jnp.min(jnp.where(gp == gw, gi, float(n_groups)), axis=0, keepdims=True)
    el = jnp.zeros((epg, t_rows), F32)
    for g in range(n_groups):
        el = jnp.where(gsel == float(g), lt[GROUP_ROW0 + g * epg:GROUP_ROW0 + (g + 1) * epg], el)
    ei = lax.broadcasted_iota(I32, el.shape, 0).astype(F32)
    v1 = jnp.max(el, axis=0, keepdims=True)
    i1 = jnp.min(jnp.where(el == v1, ei, float(epg)), axis=0, keepdims=True)
    el2 = jnp.where(ei == i1, -jnp.inf, el)
    v2 = jnp.max(el2, axis=0, keepdims=True)
    i2 = jnp.min(jnp.where(el2 == v2, ei, float(epg)), axis=0, keepdims=True)
    a1 = jnp.exp(v1 - v1)
    a2 = jnp.exp(v2 - v1)
    den = a1 + a2
    c1 = gw * (a1 / den)
    c2 = gw * (a2 / den)
    e1 = (gsel * float(epg) + i1).astype(I32)
    e2 = (gsel * float(epg) + i2).astype(I32)
    row8 = lax.broadcasted_iota(I32, (SUBLANES, t_rows), 0)
    re_ref[...] = jnp.where(row8 == 0, e1, jnp.where(row8 == 1, e2, 0))
    rc_ref[...] = jnp.where(row8 == 0, c1, jnp.where(row8 == 1, c2, 0.0))


def _mix(x2d, proj, kv, caches, weights, *, n_seq, sample, dims):
    n_heads, n_kv, head_dim, window, n_groups, n_exp, d_conv = dims
    n, d = x2d.shape
    t = MIX_ROWS
    n_chunks = t // CHUNK
    span = window + CHUNK
    d_attn = n_heads * head_dim
    d_kv = n_kv * head_dim
    convw, wco, wao, wo, bias, sinks, g2, wr, br = weights
    if sample:
        n_t = n // t
        grid = (n_t, 1)
        tok = lambda i, j: (i, 0)
        n_state = n // CHUNK
        state_blk = n_chunks
        st = lambda i, j: (i, 0, 0)
    else:
        n_t = (n // n_seq) // t
        grid = (n_seq, n_t)
        tok = lambda i, j: (i * n_t + j, 0)
        n_state = n_seq
        state_blk = 1
        st = lambda i, j: (i, 0, 0)
    const2 = lambda i, j: (0, 0)

    def col(width, idx):
        return pl.BlockSpec((t, width), lambda i, j: (tok(i, j)[0], idx))

    def resident(shape):
        return pl.BlockSpec(shape, const2, pipeline_mode=pl.Buffered(1))

    in_specs = [
        pl.BlockSpec((t, d), tok),
        col(d, 2), col(d, 3),
        col(d_conv, 0), col(d_conv, 1), col(d_conv, 2), col(d_attn, 3),
        pl.BlockSpec((t, 2 * d_kv), tok),
    ]
    args = [x2d, proj, proj, proj, proj, proj, proj, kv]
    if sample:
        cconv, ck, cv = caches
        in_specs += [
            pl.BlockSpec((n_chunks, cconv.shape[1], d_conv), st),
            pl.BlockSpec((n_chunks, window, d_kv), st),
            pl.BlockSpec((n_chunks, window, d_kv), st),
        ]
        args += [cconv, ck, cv]
    in_specs += [
        resident(convw.shape), resident(wco.shape), resident(wao.shape), resident(wo.shape),
        pl.BlockSpec(bias.shape, lambda i, j: (0, 0, 0), pipeline_mode=pl.Buffered(1)),
        pl.BlockSpec(memory_space=pltpu.SMEM),
        resident(g2.shape), resident(wr.shape), resident(br.shape),
    ]
    args += [convw, wco, wao, wo, bias, sinks, g2, wr, br]
    out_specs = [
        pl.BlockSpec((t, d), tok),
        pl.BlockSpec((t, 1, d // 2), lambda i, j: (tok(i, j)[0], 0, 0)),
        pl.BlockSpec((SUBLANES, t), lambda i, j: (0, tok(i, j)[0])),
        pl.BlockSpec((SUBLANES, t), lambda i, j: (0, tok(i, j)[0])),
        pl.BlockSpec((state_blk, 2, d_conv), st),
        pl.BlockSpec((state_blk, window, d_kv), st),
        pl.BlockSpec((state_blk, window, d_kv), st),
    ]
    out_shape = [
        jax.ShapeDtypeStruct((n, d), F32),
        jax.ShapeDtypeStruct((n, 1, d // 2), U32),
        jax.ShapeDtypeStruct((SUBLANES, n), I32),
        jax.ShapeDtypeStruct((SUBLANES, n), F32),
        jax.ShapeDtypeStruct((n_state, 2, d_conv), F32),
        jax.ShapeDtypeStruct((n_state, window, d_kv), F32),
        jax.ShapeDtypeStruct((n_state, window, d_kv), F32),
    ]
    kv_rows = n_chunks * span if sample else window + t
    scratch = [
        pltpu.VMEM((2 * KV_VARIANTS, kv_rows, d_kv), BF16),
        pltpu.VMEM((t, d_attn), BF16), pltpu.VMEM((t, d_conv), BF16),
        pltpu.VMEM((SUBLANES, d_conv), F32),
    ]
    est = (2 * t * d * 4 * 3 + 2 * 2 * t * d * 2 + 2 * 4 * t * d_conv * 2
           + (2 * d_conv * d + d * d) * 2 + 12 * t * d * 4)
    kern = functools.partial(_mix_kernel, sample=sample, n_heads=n_heads, n_kv=n_kv, head_dim=head_dim,
                             window=window, n_groups=n_groups, n_exp=n_exp)
    return pl.pallas_call(
        kern, grid=grid, in_specs=in_specs, out_specs=out_specs, out_shape=out_shape,
        scratch_shapes=scratch,
        compiler_params=pltpu.CompilerParams(
            dimension_semantics=("arbitrary", "arbitrary"), vmem_limit_bytes=_vmem_limit(est)),
        name="mix_sample" if sample else "mix_prompt",
    )(*args)


def _route_kernel(re_ref, pos_ref, tmap_ref, *, n_exp, tile_rows, blk):
    n = re_ref.shape[1]
    n_blk = n // blk
    erow = lax.broadcasted_iota(I32, (n_exp, blk), 0)

    def onehots(j):
        c0 = pl.multiple_of(j * blk, blk)
        oh0 = (erow == re_ref[0:1, pl.ds(c0, blk)]).astype(F32)
        oh1 = (erow == re_ref[1:2, pl.ds(c0, blk)]).astype(F32)
        return c0, oh0, oh1

    def count_body(j, cnt):
        _, oh0, oh1 = onehots(j)
        return cnt + jnp.sum(oh0 + oh1, axis=1, keepdims=True)

    cnt = lax.fori_loop(0, n_blk, count_body, jnp.zeros((n_exp, 1), F32))
    cnt = jnp.broadcast_to(cnt, (n_exp, LANES))
    padded = jnp.ceil(cnt / tile_rows) * tile_rows
    ends = padded
    prow = lax.broadcasted_iota(I32, ends.shape, 0)
    step = 1
    while step < n_exp:
        ends = ends + jnp.where(prow >= step, pltpu.roll(ends, step, axis=0), 0.0)
        step *= 2
    offs = ends - padded
    off1 = offs[:, 0:1]

    tri = (lax.broadcasted_iota(I32, (blk, blk), 0) <= lax.broadcasted_iota(I32, (blk, blk), 1)).astype(BF16)
    row8 = lax.broadcasted_iota(I32, (SUBLANES, blk), 0)

    def pos_body(j, run):
        c0, oh0, oh1 = onehots(j)
        both = oh0 + oh1
        csum = jnp.dot(both.astype(BF16), tri, preferred_element_type=F32) + run
        slot = off1 + csum - 1.0
        p0 = jnp.sum(oh0 * slot, axis=0, keepdims=True).astype(I32)
        p1 = jnp.sum(oh1 * slot, axis=0, keepdims=True).astype(I32)
        pos_ref[:, pl.ds(c0, blk)] = jnp.where(row8 == 0, p0, jnp.where(row8 == 1, p1, 0))
        return run + jnp.sum(both, axis=1, keepdims=True)

    lax.fori_loop(0, n_blk, pos_body, jnp.zeros((n_exp, 1), F32))

    n_tiles_pad = tmap_ref.shape[1]
    start = (lax.broadcasted_iota(I32, (1, n_tiles_pad), 1) * tile_rows).astype(F32)
    end1 = ends[:, 0:1]
    te = jnp.sum((end1 <= start).astype(F32), axis=0, keepdims=True)
    trow = lax.broadcasted_iota(I32, (n_exp, n_tiles_pad), 0).astype(F32)
    used_end = jnp.sum(jnp.where(trow == te, off1 + cnt[:, 0:1], 0.0), axis=0, keepdims=True)
    n_rows = jnp.clip(used_end - start, 0.0, float(tile_rows))
    has_tokens = cnt[:, 0:1] > 0.0
    te = jnp.minimum(te, jnp.max(jnp.where(has_tokens, trow, 0.0), axis=0, keepdims=True))
    nxt = jnp.min(jnp.where(trow > te, jnp.where(has_tokens, trow, float(n_exp)), float(n_exp)), axis=0, keepdims=True)
    r8 = lax.broadcasted_iota(I32, (SUBLANES, n_tiles_pad), 0)
    tmap_ref[...] = jnp.where(r8 == 0, te.astype(I32),
                              jnp.where(r8 == 1, n_rows.astype(I32), jnp.where(r8 == 2, nxt.astype(I32), 0)))


def _route(re, n_exp, n_tiles):
    n = re.shape[1]
    assert n % ROUTE_BLOCK == 0
    n_tiles_pad = pl.cdiv(n_tiles, LANES) * LANES
    kern = functools.partial(_route_kernel, n_exp=n_exp, tile_rows=SLOT_ROWS, blk=ROUTE_BLOCK)
    return pl.pallas_call(
        kern,
        out_shape=[jax.ShapeDtypeStruct((SUBLANES, n), I32), jax.ShapeDtypeStruct((SUBLANES, n_tiles_pad), I32)],
        name="route",
    )(re)


def _dispatch_kernel(p0_ref, p1_ref, hp_ref, hs_ref, xs_ref, src_ref, sem, *, n_prompt_tiles, n_tok):
    i = pl.program_id(0)
    rows = hp_ref.shape[0]

    def scatter(h_ref):
        def start(t, carry):
            tok = i * rows + t
            s0, s1 = p0_ref[t], p1_ref[t]
            src_ref[s0] = tok
            src_ref[s1] = n_tok + tok
            pltpu.make_async_copy(h_ref.at[t], xs_ref.at[s0], sem).start(priority=0)
            pltpu.make_async_copy(h_ref.at[t], xs_ref.at[s1], sem).start(priority=1)
            return carry

        lax.fori_loop(0, rows, start, 0)
        for _ in range(TOP_K):
            pltpu.make_async_copy(h_ref, xs_ref.at[pl.ds(0, rows)], sem).wait()

    @pl.when(i < n_prompt_tiles)
    def _():
        scatter(hp_ref)

    @pl.when(i >= n_prompt_tiles)
    def _():
        scatter(hs_ref)


def _dispatch(h2_p, h2_s, p0, p1, n_slots):
    n_p, _, d = h2_p.shape
    n_s = h2_s.shape[0]
    t = MOVE_ROWS
    assert n_p % t == 0 and n_s % t == 0
    n_pt, n_st = n_p // t, n_s // t
    return pl.pallas_call(
        functools.partial(_dispatch_kernel, n_prompt_tiles=n_pt, n_tok=n_p + n_s),
        grid=(n_pt + n_st,),
        in_specs=[
            pl.BlockSpec((t,), lambda i: (i,), memory_space=pltpu.SMEM),
            pl.BlockSpec((t,), lambda i: (i,), memory_space=pltpu.SMEM),
            pl.BlockSpec((t, 1, d), lambda i: (jnp.minimum(i, n_pt - 1), 0, 0)),
            pl.BlockSpec((t, 1, d), lambda i: (jnp.maximum(i - n_pt, 0), 0, 0)),
        ],
        out_specs=[pl.BlockSpec(memory_space=pl.ANY), pl.BlockSpec(memory_space=pltpu.SMEM)],
        out_shape=[jax.ShapeDtypeStruct((n_slots, 1, d), h2_p.dtype), jax.ShapeDtypeStruct((n_slots,), I32)],
        scratch_shapes=[pltpu.SemaphoreType.DMA(())],
        compiler_params=pltpu.CompilerParams(dimension_semantics=("arbitrary",), has_side_effects=True),
        name="dispatch",
    )(p0, p1, h2_p, h2_s)


def _experts_kernel(te_ref, nr_ref, nx_ref, xs_ref, src_ref, wg_hbm, wu_hbm, wd_hbm, y2_hbm,
                    wg_st, wu_st, wd_st, wg_sc, wu_sc, wd_sc, y_buf, sems, y_sems, n_changes,
                    *, n_exp, tile_rows, n_steps, dump_row0):
    step = pl.program_id(0)
    step_rows, _, dh = xs_ref.shape
    tiles_per_step = step_rows // tile_rows
    y_slot = lax.rem(step, 2)
    y_base = y_slot * step_rows

    def wait_rows(slot):
        pltpu.make_async_copy(y_buf.at[pl.ds(slot * step_rows, step_rows)], y2_hbm.at[pl.ds(0, step_rows)],
                              y_sems.at[slot]).wait()

    @pl.when(step >= 2)
    def _():
        wait_rows(y_slot)

    def copies(e, slot):
        return (pltpu.make_async_copy(wg_hbm.at[e], wg_st.at[slot], sems.at[slot]),
                pltpu.make_async_copy(wu_hbm.at[e], wu_st.at[slot], sems.at[slot]),
                pltpu.make_async_copy(wd_hbm.at[e], wd_st.at[slot], sems.at[slot]))

    def one_tile(sub):
        tile = pl.program_id(0) * tiles_per_step + sub
        rows = pl.ds(sub * tile_rows, tile_rows)
        expert = te_ref[tile]

        @pl.when(tile == 0)
        def _():
            n_changes[0] = 0
            for cp in copies(expert, 0):
                cp.start(priority=1)

        @pl.when(jnp.logical_or(tile == 0, expert != te_ref[jnp.maximum(tile - 1, 0)]))
        def _():
            slot = lax.rem(n_changes[0], 2)
            n_changes[0] = n_changes[0] + 1
            for cp in copies(expert, slot):
                cp.wait()
            nxt = nx_ref[tile]

            @pl.when(nxt < n_exp)
            def _():
                for cp in copies(nxt, 1 - slot):
                    cp.start(priority=1)

            wg_sc[...] = wg_st[slot].astype(BF16)
            wu_sc[...] = wu_st[slot].astype(BF16)
            wd_sc[...] = wd_st[slot].astype(BF16)

        n_rows = nr_ref[tile]

        @pl.when(n_rows > 0)
        def _():
            live = lax.broadcasted_iota(I32, (tile_rows, 2 * dh), 0) < n_rows
            x = jnp.where(live, _unpack_halves(xs_ref[rows].reshape(tile_rows, dh)), 0.0).astype(BF16)
            gate = jnp.dot(x, wg_sc[...], preferred_element_type=F32)
            up = jnp.dot(x, wu_sc[...], preferred_element_type=F32)
            hid = (jax.nn.silu(gate) * up).astype(BF16)
            y = jnp.dot(hid, wd_sc[...], preferred_element_type=F32)
            y_buf[pl.ds(y_base + sub * tile_rows, tile_rows)] = _pack_halves(y).reshape(tile_rows, 1, dh)

        @pl.when(n_rows <= 0)
        def _():
            y_buf[pl.ds(y_base + sub * tile_rows, tile_rows)] = (
                _pack_halves(jnp.zeros((tile_rows, 2 * dh), F32)).reshape(tile_rows, 1, dh))

        def send(dst_of):
            def body(r, carry):
                pltpu.make_async_copy(y_buf.at[y_base + sub * tile_rows + r], y2_hbm.at[dst_of(r)],
                                      y_sems.at[y_slot]).start()
                return carry
            return body

        lax.fori_loop(0, n_rows, send(lambda r: src_ref[sub * tile_rows + r]), 0)
        lax.fori_loop(n_rows, tile_rows, send(lambda r: dump_row0 + y_base + sub * tile_rows + r), 0)

    for sub in range(tiles_per_step):
        one_tile(sub)

    @pl.when(step == n_steps - 1)
    def _():
        if n_steps > 1:
            wait_rows(1 - y_slot)
        wait_rows(y_slot)


def _experts(xs, src, te, nr, nx, w_gate, w_up, w_down, n_tok):
    n_slots, _, dh = xs.shape
    n_exp, d, d_e = w_gate.shape
    t = SLOT_ROWS * EXPERT_STEP_TILES
    n_steps = n_slots // t
    est = 2 * t * d * 2 + 2 * t * d * 2 + 2 * 3 * d * d_e * 4 + 3 * d * d_e * 2 + 4 * SLOT_ROWS * d * 4
    return pl.pallas_call(
        functools.partial(_experts_kernel, n_exp=n_exp, tile_rows=SLOT_ROWS, n_steps=n_steps,
                          dump_row0=TOP_K * n_tok),
        grid_spec=pltpu.PrefetchScalarGridSpec(
            num_scalar_prefetch=3,
            grid=(n_steps,),
            in_specs=[pl.BlockSpec((t, 1, dh), lambda i, te, nr, nx: (i, 0, 0)),
                      pl.BlockSpec((t,), lambda i, te, nr, nx: (i,), memory_space=pltpu.SMEM),
                      pl.BlockSpec(memory_space=pl.ANY), pl.BlockSpec(memory_space=pl.ANY),
                      pl.BlockSpec(memory_space=pl.ANY)],
            out_specs=pl.BlockSpec(memory_space=pl.ANY),
            scratch_shapes=[
                pltpu.VMEM((2, d, d_e), F32), pltpu.VMEM((2, d, d_e), F32), pltpu.VMEM((2, d_e, d), F32),
                pltpu.VMEM((d, d_e), BF16), pltpu.VMEM((d, d_e), BF16), pltpu.VMEM((d_e, d), BF16),
                pltpu.VMEM((2 * t, 1, dh), U32),
                pltpu.SemaphoreType.DMA((2,)), pltpu.SemaphoreType.DMA((2,)), pltpu.SMEM((1,), I32)],
        ),
        out_shape=jax.ShapeDtypeStruct((TOP_K * n_tok + 2 * t, 1, dh), U32),
        compiler_params=pltpu.CompilerParams(
            dimension_semantics=("arbitrary",), vmem_limit_bytes=_vmem_limit(est), has_side_effects=True),
        name="experts",
    )(te, nr, nx, xs, src, w_gate, w_up, w_down)


def _combine_kernel(xp_ref, rc_ref, g_ref, y0_ref, y1_ref, out_ref):
    rows, d = xp_ref.shape
    ct = rc_ref[...].T
    y0 = _unpack_halves(y0_ref[...].reshape(rows, d // 2))
    y1 = _unpack_halves(y1_ref[...].reshape(rows, d // 2))
    moe = ct[:, 0:1] * y0 + ct[:, 1:2] * y1
    out_ref[...] = _rms_rows(xp_ref[...] + moe, g_ref[...])


def _combine(xp, rc, y2, g, tok0, n_tok):
    n, d = xp.shape
    t = min(MOVE_ROWS, n)
    assert n % t == 0 and tok0 % t == 0 and n_tok % t == 0
    est = 2 * 2 * t * d * 4 + 2 * 2 * t * d * 2 + 8 * t * d * 4
    return pl.pallas_call(
        _combine_kernel,
        grid=(n // t,),
        in_specs=[
            pl.BlockSpec((t, d), lambda i: (i, 0)),
            pl.BlockSpec((SUBLANES, t), lambda i: (0, i)),
            pl.BlockSpec((1, d), lambda i: (0, 0)),
            pl.BlockSpec((t, 1, d // 2), lambda i: (tok0 // t + i, 0, 0)),
            pl.BlockSpec((t, 1, d // 2), lambda i: ((n_tok + tok0) // t + i, 0, 0)),
        ],
        out_specs=pl.BlockSpec((t, d), lambda i: (i, 0)),
        out_shape=jax.ShapeDtypeStruct((n, d), F32),
        compiler_params=pltpu.CompilerParams(
            dimension_semantics=("arbitrary",), vmem_limit_bytes=_vmem_limit(est)),
        name="combine",
    )(xp, rc, g, y2, y2)


def _rel_buckets(rel):
    nb = N_BUCKETS // 2
    ret = (rel > 0).astype(I32) * nb
    n = jnp.abs(rel)
    max_exact = nb // 2
    nf = jnp.maximum(n, 1).astype(F32)
    large = max_exact + (jnp.log(nf / max_exact) / math.log(MAX_DISTANCE / max_exact)
                         * (nb - max_exact)).astype(I32)
    large = jnp.minimum(large, nb - 1)
    return ret + jnp.where(n < max_exact, n, large)


def _rel_bias(table, window, n_kv):
    n_heads = table.shape[1]
    span = window + CHUNK
    rows = table[_rel_buckets(jnp.arange(-(span - 1), CHUNK, dtype=I32))].astype(F32)
    bias = jnp.stack([rows[CHUNK - 1 - q:CHUNK - 1 - q + span] for q in range(CHUNK)])
    bias = jnp.transpose(bias, (2, 0, 1))
    bias = jnp.pad(bias, ((0, 0), (0, 0), (0, KEY_PAD - bias.shape[2])), constant_values=NEG_INF)
    bias = bias.reshape(n_kv, n_heads // n_kv // 2, 2, CHUNK, KEY_PAD)
    return jnp.transpose(bias, (0, 1, 3, 2, 4)).reshape(n_kv, (n_heads // n_kv // 2) * CHUNK, 2 * KEY_PAD)


def kernel(x_prompt, x_sample, cache_conv, cache_k, cache_v, rel_bias_table, norm_mix_g, w_in, conv_w, w_conv_out, attn_sinks, w_attn_out, w_o, norm_ffn_g, w_group, b_group, w_expert_router, b_expert_router, w_gate, w_up, w_down, final_norm_g):
    assert w_in.shape[0] == 1, "single-layer step"
    batch, seq, d = x_prompt.shape
    dec_batch, dec_seq, _ = x_sample.shape
    assert dec_seq == CHUNK and seq % MIX_ROWS == 0 and (dec_batch * dec_seq) % MIX_ROWS == 0
    d_conv = conv_w.shape[-1]
    window, n_kv, head_dim = cache_k.shape[2], cache_k.shape[3], cache_k.shape[4]
    n_heads = attn_sinks.shape[-1]
    d_attn, d_kv = n_heads * head_dim, n_kv * head_dim
    n_groups, n_exp = w_group.shape[-1], w_expert_router.shape[-1]
    assert n_groups <= GROUP_ROW0 and d_conv == d_attn and 2 * d_conv == d
    assert n_kv == 2 and d_kv == LANES and (n_heads // n_kv) % 2 == 0 and window + CHUNK <= KEY_PAD
    dims = (n_heads, n_kv, head_dim, window, n_groups, n_exp, d_conv)

    w_all = w_in[0].astype(BF16)
    kv0 = 3 * d_conv + d_attn
    g1 = norm_mix_g[0][None, :]
    wr = jnp.zeros((GROUP_ROW0 + n_exp, d), F32)
    wr = wr.at[:n_groups].set(w_group[0].T).at[GROUP_ROW0:].set(w_expert_router[0].T).astype(BF16)
    br = jnp.zeros((GROUP_ROW0 + n_exp, 1), F32)
    br = br.at[:n_groups, 0].set(b_group[0]).at[GROUP_ROW0:, 0].set(b_expert_router[0])
    weights = (conv_w[0], w_conv_out[0].astype(BF16), w_attn_out[0].astype(BF16), w_o[0].astype(BF16),
               _rel_bias(rel_bias_table, window, n_kv), attn_sinks, norm_ffn_g[0][None, :], wr, br)

    xp2d = x_prompt.reshape(batch * seq, d)
    xs2d = x_sample.reshape(dec_batch * dec_seq, d)
    n_p, n_s = xp2d.shape[0], xs2d.shape[0]
    n_tok = n_p + n_s

    proj_p, kv_p = _inproj(xp2d, g1, w_all, kv0, 2 * d_kv)
    proj_s, kv_s = _inproj(xs2d, g1, w_all, kv0, 2 * d_kv)
    xres_p, h2_p, re_p, rc_p, conv_p, k_p, v_p = _mix(
        xp2d, proj_p, kv_p, None, weights, n_seq=batch, sample=False, dims=dims)
    caches = (cache_conv[0], cache_k[0].reshape(dec_batch, window, d_kv), cache_v[0].reshape(dec_batch, window, d_kv))
    xres_s, h2_s, re_s, rc_s, conv_s, k_s, v_s = _mix(
        xs2d, proj_s, kv_s, caches, weights, n_seq=dec_batch, sample=True, dims=dims)

    n_tiles = pl.cdiv((TOP_K * n_tok) // SLOT_ROWS + n_exp, EXPERT_STEP_TILES) * EXPERT_STEP_TILES
    n_slots = n_tiles * SLOT_ROWS
    pos, tmap = _route(jnp.concatenate([re_p, re_s], axis=1), n_exp, n_tiles)
    xs, src = _dispatch(h2_p, h2_s, pos[0], pos[1], n_slots)
    y2 = _experts(xs, src, tmap[0, :n_tiles], tmap[1, :n_tiles], tmap[2, :n_tiles],
                  w_gate[0], w_up[0], w_down[0], n_tok)
    gf = final_norm_g[None, :]
    y_prompt = _combine(xres_p, rc_p, y2, gf, 0, n_tok).reshape(batch, seq, d)
    y_sample = _combine(xres_s, rc_s, y2, gf, n_p, n_tok).reshape(dec_batch, dec_seq, d)

    kv_shape = (1, -1, window, n_kv, head_dim)
    return (y_prompt, y_sample, conv_p[None], k_p.reshape(kv_shape), v_p.reshape(kv_shape),
            conv_s[None], k_s.reshape(kv_shape), v_s.reshape(kv_shape))
```

```python
import functools
import math

import jax
import jax.numpy as jnp
from jax import lax
from jax.experimental import pallas as pl
from jax.experimental.pallas import tpu as pltpu

F32, BF16, I32, U32 = jnp.float32, jnp.bfloat16, jnp.int32, jnp.uint32

CHUNK = 64
N_BUCKETS = 32
MAX_DISTANCE = 128
EPS = 1e-6
NEG_INF = -1e30
TOP_K = 2

V7X_VMEM_BYTES = 64 * 1024 * 1024
SUBLANES = 8
LANES = 128

INPROJ_ROWS = 1024
INPROJ_COLS = 2048
MIX_ROWS = 256
MOVE_ROWS = 512
SLOT_ROWS = 256
EXPERT_STEP_TILES = 4
ROUTE_BLOCK = 512
GROUP_ROW0 = 8
KEY_PAD = 256
KV_VARIANTS = 4


def _vmem_limit(nbytes):
    return int(min(V7X_VMEM_BYTES - (4 << 20), max(nbytes, 32 << 20)))


def _pack_halves(x):
    half = x.shape[1] // 2
    return pltpu.pack_elementwise([x[:, :half], x[:, half:]], packed_dtype=BF16)


def _unpack_halves(w):
    lo = pltpu.unpack_elementwise(w, index=0, packed_dtype=BF16, unpacked_dtype=F32)
    hi = pltpu.unpack_elementwise(w, index=1, packed_dtype=BF16, unpacked_dtype=F32)
    return jnp.concatenate([lo, hi], axis=1)


def _rms_rows(x, g):
    r = lax.rsqrt(jnp.mean(x * x, axis=-1, keepdims=True) + EPS)
    return (x * r) * g


def _inproj_kernel(x_ref, g_ref, wm_ref, wkv_ref, proj_ref, kv_ref, h_sc):
    @pl.when(pl.program_id(1) == 0)
    def _():
        rows = 128
        def body(i, carry):
            r0 = pl.multiple_of(i * rows, rows)
            h_sc[pl.ds(r0, rows), :] = _rms_rows(x_ref[pl.ds(r0, rows), :], g_ref[...]).astype(BF16)
            return carry
        lax.fori_loop(0, x_ref.shape[0] // rows, body, 0)
        kv_ref[...] = jnp.dot(h_sc[...], wkv_ref[...], preferred_element_type=F32)

    proj_ref[...] = jnp.dot(h_sc[...], wm_ref[...], preferred_element_type=F32).astype(BF16)


def _inproj(x2d, g, w_all, kv0, n_kv):
    n, d = x2d.shape
    tm = min(INPROJ_ROWS, n)
    tn = INPROJ_COLS
    n_main = w_all.shape[1] - n_kv
    assert kv0 % tn == 0 and n_main % tn == 0 and n_kv % LANES == 0
    est = 2 * tm * d * 4 + tm * d * 2 + 2 * d * tn * 2 + 2 * d * n_kv * 2 + 2 * tm * tn * 2 + 2 * tm * n_kv * 4 + tm * tn * 4
    return pl.pallas_call(
        _inproj_kernel,
        grid=(n // tm, n_main // tn),
        in_specs=[
            pl.BlockSpec((tm, d), lambda i, j: (i, 0)),
            pl.BlockSpec((1, d), lambda i, j: (0, 0)),
            pl.BlockSpec((pl.Element(d), pl.Element(tn)),
                         lambda i, j: (0, pl.multiple_of(jnp.where(j * tn < kv0, j * tn, j * tn + n_kv), LANES))),
            pl.BlockSpec((pl.Element(d), pl.Element(n_kv)), lambda i, j: (0, kv0)),
        ],
        out_specs=[
            pl.BlockSpec((tm, tn), lambda i, j: (i, j)),
            pl.BlockSpec((tm, n_kv), lambda i, j: (i, 0)),
        ],
        out_shape=[jax.ShapeDtypeStruct((n, n_main), BF16), jax.ShapeDtypeStruct((n, n_kv), F32)],
        scratch_shapes=[pltpu.VMEM((tm, d), BF16)],
        compiler_params=pltpu.CompilerParams(
            dimension_semantics=("arbitrary", "arbitrary"), vmem_limit_bytes=_vmem_limit(est + (8 << 20))),
        name="inproj",
    )(x2d, g, w_all, w_all)


def _conv_rows(u, prev2, prev1, w):
    row = lax.broadcasted_iota(I32, u.shape, 0)
    u1 = jnp.where(row == 0, prev1, pltpu.roll(u, 1, axis=0))
    u2 = jnp.where(row == 0, prev2, jnp.where(row == 1, prev1, pltpu.roll(u, 2, axis=0)))
    return (w[0:1] * u2 + w[1:2] * u1) + w[2:3] * u


def _mix_kernel(*refs, sample, n_heads, n_kv, head_dim, window, n_groups, n_exp):
    if sample:
        (x_ref, ga_ref, gb_ref, b_ref, c_ref, xc_ref, q_ref, kv_ref, cconv_ref, ck_ref, cv_ref,
         convw_ref, wco_ref, wao_ref, wo_ref, bias_ref, sinks_ref, g2_ref, wr_ref, br_ref,
         xp_ref, h2_ref, re_ref, rc_ref, sconv_ref, sk_ref, sv_ref,
         kvar, o_sc, ya_sc, carry_u, s_sc, e_sc, sink_sc) = refs
    else:
        (x_ref, ga_ref, gb_ref, b_ref, c_ref, xc_ref, q_ref, kv_ref,
         convw_ref, wco_ref, wao_ref, wo_ref, bias_ref, sinks_ref, g2_ref, wr_ref, br_ref,
         xp_ref, h2_ref, re_ref, rc_ref, sconv_ref, sk_ref, sv_ref,
         kvar, o_sc, ya_sc, carry_u, s_sc, e_sc, sink_sc) = refs
    t_rows, d_model = x_ref.shape
    n_chunks = t_rows // CHUNK
    span = window + CHUNK
    q_per_kv = n_heads // n_kv
    d_kv = n_kv * head_dim
    scale = 1.0 / math.sqrt(head_dim)
    scale_is_pow2 = math.frexp(scale)[0] == 0.5
    epg = n_exp // n_groups
    seq_start = pl.program_id(1) == 0

    def conv_branch():
        w_conv = convw_ref[...]
        if sample:
            for s in range(n_chunks):
                rows = slice(s * CHUNK, (s + 1) * CHUNK)
                u = c_ref[rows, :].astype(F32) * xc_ref[rows, :].astype(F32)
                y = _conv_rows(u, cconv_ref[s, 0:1, :], cconv_ref[s, 1:2, :], w_conv)
                ya_sc[rows, :] = (b_ref[rows, :].astype(F32) * y).astype(BF16)
                sconv_ref[s] = u[CHUNK - 2:CHUNK, :]
        else:
            u = c_ref[...].astype(F32) * xc_ref[...].astype(F32)
            prev = jnp.where(seq_start, 0.0, carry_u[...])
            y = _conv_rows(u, prev[SUBLANES - 2:SUBLANES - 1], prev[SUBLANES - 1:SUBLANES], w_conv)
            ya_sc[...] = (b_ref[...].astype(F32) * y).astype(BF16)
            carry_u[...] = u[t_rows - SUBLANES:t_rows, :]
            sconv_ref[0] = u[t_rows - 2:t_rows, :]

    def store_kv(row0, k_rows, v_rows):
        n_rows = k_rows.shape[0]
        low = lax.broadcasted_iota(I32, k_rows.shape, 1) < head_dim
        for base, a in ((0, k_rows), (KV_VARIANTS, v_rows)):
            b = pltpu.roll(a, head_dim, axis=1)
            kvar[base + 0, row0:row0 + n_rows, :] = jnp.where(low, a, 0.0).astype(BF16)
            kvar[base + 1, row0:row0 + n_rows, :] = jnp.where(low, 0.0, a).astype(BF16)
            kvar[base + 2, row0:row0 + n_rows, :] = jnp.where(low, b, 0.0).astype(BF16)
            kvar[base + 3, row0:row0 + n_rows, :] = jnp.where(low, 0.0, b).astype(BF16)

    if sample:
        for s in range(n_chunks):
            rows = slice(s * CHUNK, (s + 1) * CHUNK)
            store_kv(s * span, ck_ref[s], cv_ref[s])
            store_kv(s * span + window, kv_ref[rows, 0:d_kv], kv_ref[rows, d_kv:2 * d_kv])
            sk_ref[s, 0:window - CHUNK, :] = ck_ref[s, CHUNK:window, :]
            sv_ref[s, 0:window - CHUNK, :] = cv_ref[s, CHUNK:window, :]
            sk_ref[s, window - CHUNK:window, :] = kv_ref[rows, 0:d_kv]
            sv_ref[s, window - CHUNK:window, :] = kv_ref[rows, d_kv:2 * d_kv]
        k_stride = span
    else:
        @pl.when(seq_start)
        def _():
            kvar[:, 0:window, :] = jnp.zeros((2 * KV_VARIANTS, window, d_kv), BF16)
        store_kv(window, kv_ref[:, 0:d_kv], kv_ref[:, d_kv:2 * d_kv])
        sk_ref[0] = kv_ref[t_rows - window:t_rows, 0:d_kv]
        sv_ref[0] = kv_ref[t_rows - window:t_rows, d_kv:2 * d_kv]
        k_stride = CHUNK

    n_pairs = q_per_kv // 2
    key_pad_rows = jnp.zeros((KEY_PAD - span, d_kv), BF16)
    ones_d = ((lax.broadcasted_iota(I32, (2 * KEY_PAD, LANES), 0) < KEY_PAD)
              == (lax.broadcasted_iota(I32, (2 * KEY_PAD, LANES), 1) < head_dim)).astype(BF16)
    low_half = lax.broadcasted_iota(I32, (CHUNK, LANES), 1) < head_dim
    ya_parts = []
    ya_cols = d_model // (n_chunks * n_kv)
    blocks = [(c, n) for c in range(n_chunks) for n in range(n_kv)]

    def key_rows(c):
        return slice(c * k_stride, c * k_stride + span)

    for b, (c, n) in enumerate(blocks):
        q_rows = slice(c * CHUNK, (c + 1) * CHUNK)
        top, bot = (0, 3) if n == 0 else (2, 1)
        kd = jnp.concatenate([kvar[top, key_rows(c), :], key_pad_rows, kvar[bot, key_rows(c), :], key_pad_rows],
                             axis=0)
        q4 = jnp.concatenate(
            [q_ref[q_rows, (n * n_pairs + j) * LANES:(n * n_pairs + j + 1) * LANES] for j in range(n_pairs)], axis=0)
        if scale_is_pow2:
            q4 = q4 * scale
        s4 = lax.dot_general(q4, kd, (((1,), (1,)), ((), ())), preferred_element_type=F32)
        s_sc[b] = s4 if scale_is_pow2 else s4 * scale
    conv_branch()
    for b, (c, n) in enumerate(blocks):
        masked = (not sample) and c * CHUNK < window
        if masked:
            first_key = (pl.program_id(1) * n_chunks + c) * CHUNK - window
            valid = lax.broadcasted_iota(I32, (CHUNK, KEY_PAD), 1) + first_key >= 0
        for j in range(n_pairs):
            sink_halves = []
            for half in range(2):
                blk = (slice(j * CHUNK, (j + 1) * CHUNK), slice(half * KEY_PAD, (half + 1) * KEY_PAD))
                s = s_sc[b, blk[0], blk[1]] + bias_ref[n, blk[0], blk[1]]
                if masked:
                    s = jnp.where(valid, s, NEG_INF)
                sink = sinks_ref[0, n * q_per_kv + 2 * j + half]
                mx = jnp.maximum(jnp.max(s, axis=-1, keepdims=True), sink)
                e_sc[b, blk[0], blk[1]] = jnp.exp(s - mx).astype(BF16)
                sink_halves.append(jnp.exp(sink - mx))
            sink_sc[b, j * CHUNK:(j + 1) * CHUNK, :] = jnp.where(low_half, sink_halves[0], sink_halves[1])
        ya_parts.append(jnp.dot(ya_sc[...], wco_ref[:, b * ya_cols:(b + 1) * ya_cols], preferred_element_type=F32))
    for b, (c, n) in enumerate(blocks):
        q_rows = slice(c * CHUNK, (c + 1) * CHUNK)
        top, bot = (0, 3) if n == 0 else (2, 1)
        vd = jnp.concatenate([kvar[KV_VARIANTS + top, key_rows(c), :], key_pad_rows,
                              kvar[KV_VARIANTS + bot, key_rows(c), :], key_pad_rows], axis=0)
        od = jnp.dot(e_sc[b], jnp.concatenate([vd, ones_d], axis=1), preferred_element_type=F32)
        o4 = od[:, 0:LANES] / (od[:, LANES:2 * LANES] + sink_sc[b])
        for j in range(n_pairs):
            o_sc[q_rows, (n * n_pairs + j) * LANES:(n * n_pairs + j + 1) * LANES] = (
                o4[j * CHUNK:(j + 1) * CHUNK].astype(BF16))
    if not sample:
        kvar[:, 0:window, :] = kvar[:, t_rows:t_rows + window, :]
    y_a = jnp.concatenate(ya_parts, axis=1)
    y_b = jnp.dot(o_sc[...], wao_ref[...], preferred_element_type=F32)

    m = jax.nn.sigmoid(ga_ref[...].astype(F32)) * y_a + jax.nn.sigmoid(gb_ref[...].astype(F32)) * y_b
    xp = x_ref[...] + jnp.dot(m.astype(BF16), wo_ref[...], preferred_element_type=F32)
    xp_ref[...] = xp

    h2 = _rms_rows(xp, g2_ref[...])
    h2_ref[...] = _pack_halves(h2).reshape(t_rows, 1, d_model // 2)
    lt = lax.dot_general(wr_ref[...], h2.astype(BF16), (((1,), (1,)), ((), ())),
                         preferred_element_type=F32) + br_ref[...]
    lg = lt[0:n_groups]
    eg = jnp.exp(lg - jnp.max(lg, axis=0, keepdims=True))
    gp = eg / jnp.sum(eg, axis=0, keepdims=True)
    gw = jnp.max(gp, axis=0, keepdims=True)
    gi = lax.broadcasted_iota(I32, gp.shape, 0).astype(F32)
    gsel = jnp.min(jnp.where(gp == gw, gi, float(n_groups)), axis=0, keepdims=True)
    el = jnp.zeros((epg, t_rows), F32)
    for g in range(n_groups):
        el = jnp.where(gsel == float(g), lt[GROUP_ROW0 + g * epg:GROUP_ROW0 + (g + 1) * epg], el)
    ei = lax.broadcasted_iota(I32, el.shape, 0).astype(F32)
    v1 = jnp.max(el, axis=0, keepdims=True)
    i1 = jnp.min(jnp.where(el == v1, ei, float(epg)), axis=0, keepdims=True)
    el2 = jnp.where(ei == i1, -jnp.inf, el)
    v2 = jnp.max(el2, axis=0, keepdims=True)
    i2 = jnp.min(jnp.where(el2 == v2, ei, float(epg)), axis=0, keepdims=True)
    a1 = jnp.exp(v1 - v1)
    a2 = jnp.exp(v2 - v1)
    den = a1 + a2
    c1 = gw * (a1 / den)
    c2 = gw * (a2 / den)
    e1 = (gsel * float(epg) + i1).astype(I32)
    e2 = (gsel * float(epg) + i2).astype(I32)
    row8 = lax.broadcasted_iota(I32, (SUBLANES, t_rows), 0)
    re_ref[...] = jnp.where(row8 == 0, e1, jnp.where(row8 == 1, e2, 0))
    rc_ref[...] = jnp.where(row8 == 0, c1, jnp.where(row8 == 1, c2, 0.0))


def _mix(x2d, proj, kv, caches, weights, *, n_seq, sample, dims):
    n_heads, n_kv, head_dim, window, n_groups, n_exp, d_conv = dims
    n, d = x2d.shape
    t = MIX_ROWS
    n_chunks = t // CHUNK
    span = window + CHUNK
    d_attn = n_heads * head_dim
    d_kv = n_kv * head_dim
    convw, wco, wao, wo, bias, sinks, g2, wr, br = weights
    if sample:
        n_t = n // t
        grid = (n_t, 1)
        tok = lambda i, j: (i, 0)
        n_state = n // CHUNK
        state_blk = n_chunks
        st = lambda i, j: (i, 0, 0)
    else:
        n_t = (n // n_seq) // t
        grid = (n_seq, n_t)
        tok = lambda i, j: (i * n_t + j, 0)
        n_state = n_seq
        state_blk = 1
        st = lambda i, j: (i, 0, 0)
    const2 = lambda i, j: (0, 0)

    def col(width, idx):
        return pl.BlockSpec((t, width), lambda i, j: (tok(i, j)[0], idx))

    def resident(shape):
        return pl.BlockSpec(shape, const2, pipeline_mode=pl.Buffered(1))

    in_specs = [
        pl.BlockSpec((t, d), tok),
        col(d, 2), col(d, 3),
        col(d_conv, 0), col(d_conv, 1), col(d_conv, 2), col(d_attn, 3),
        pl.BlockSpec((t, 2 * d_kv), tok),
    ]
    args = [x2d, proj, proj, proj, proj, proj, proj, kv]
    if sample:
        cconv, ck, cv = caches
        in_specs += [
            pl.BlockSpec((n_chunks, cconv.shape[1], d_conv), st),
            pl.BlockSpec((n_chunks, window, d_kv), st),
            pl.BlockSpec((n_chunks, window, d_kv), st),
        ]
        args += [cconv, ck, cv]
    in_specs += [
        resident(convw.shape), resident(wco.shape), resident(wao.shape), resident(wo.shape),
        pl.BlockSpec(bias.shape, lambda i, j: (0, 0, 0), pipeline_mode=pl.Buffered(1)),
        pl.BlockSpec(memory_space=pltpu.SMEM),
        resident(g2.shape), resident(wr.shape), resident(br.shape),
    ]
    args += [convw, wco, wao, wo, bias, sinks, g2, wr, br]
    out_specs = [
        pl.BlockSpec((t, d), tok),
        pl.BlockSpec((t, 1, d // 2), lambda i, j: (tok(i, j)[0], 0, 0)),
        pl.BlockSpec((SUBLANES, t), lambda i, j: (0, tok(i, j)[0])),
        pl.BlockSpec((SUBLANES, t), lambda i, j: (0, tok(i, j)[0])),
        pl.BlockSpec((state_blk, 2, d_conv), st),
        pl.BlockSpec((state_blk, window, d_kv), st),
        pl.BlockSpec((state_blk, window, d_kv), st),
    ]
    out_shape = [
        jax.ShapeDtypeStruct((n, d), F32),
        jax.ShapeDtypeStruct((n, 1, d // 2), U32),
        jax.ShapeDtypeStruct((SUBLANES, n), I32),
        jax.ShapeDtypeStruct((SUBLANES, n), F32),
        jax.ShapeDtypeStruct((n_state, 2, d_conv), F32),
        jax.ShapeDtypeStruct((n_state, window, d_kv), F32),
        jax.ShapeDtypeStruct((n_state, window, d_kv), F32),
    ]
    kv_rows = n_chunks * span if sample else window + t
    scratch = [
        pltpu.VMEM((2 * KV_VARIANTS, kv_rows, d_kv), BF16),
        pltpu.VMEM((t, d_attn), BF16), pltpu.VMEM((t, d_conv), BF16),
        pltpu.VMEM((SUBLANES, d_conv), F32),
        pltpu.VMEM((n_chunks * n_kv, (n_heads // n_kv // 2) * CHUNK, 2 * KEY_PAD), F32),
        pltpu.VMEM((n_chunks * n_kv, (n_heads // n_kv // 2) * CHUNK, 2 * KEY_PAD), BF16),
        pltpu.VMEM((n_chunks * n_kv, (n_heads // n_kv // 2) * CHUNK, LANES), F32),
    ]
    est = (2 * t * d * 4 * 3 + 2 * 2 * t * d * 2 + 2 * 4 * t * d_conv * 2
           + (2 * d_conv * d + d * d) * 2 + 12 * t * d * 4)
    kern = functools.partial(_mix_kernel, sample=sample, n_heads=n_heads, n_kv=n_kv, head_dim=head_dim,
                             window=window, n_groups=n_groups, n_exp=n_exp)
    return pl.pallas_call(
        kern, grid=grid, in_specs=in_specs, out_specs=out_specs, out_shape=out_shape,
        scratch_shapes=scratch,
        compiler_params=pltpu.CompilerParams(
            dimension_semantics=("arbitrary", "arbitrary"), vmem_limit_bytes=_vmem_limit(est)),
        name="mix_sample" if sample else "mix_prompt",
    )(*args)


def _route_kernel(re_ref, pos_ref, tmap_ref, *, n_exp, tile_rows, blk):
    n = re_ref.shape[1]
    n_blk = n // blk
    erow = lax.broadcasted_iota(I32, (n_exp, blk), 0)

    def onehots(j):
        c0 = pl.multiple_of(j * blk, blk)
        oh0 = (erow == re_ref[0:1, pl.ds(c0, blk)]).astype(F32)
        oh1 = (erow == re_ref[1:2, pl.ds(c0, blk)]).astype(F32)
        return c0, oh0, oh1

    def count_body(j, cnt):
        _, oh0, oh1 = onehots(j)
        return cnt + jnp.sum(oh0 + oh1, axis=1, keepdims=True)

    cnt = lax.fori_loop(0, n_blk, count_body, jnp.zeros((n_exp, 1), F32))
    cnt = jnp.broadcast_to(cnt, (n_exp, LANES))
    padded = jnp.ceil(cnt / tile_rows) * tile_rows
    ends = padded
    prow = lax.broadcasted_iota(I32, ends.shape, 0)
    step = 1
    while step < n_exp:
        ends = ends + jnp.where(prow >= step, pltpu.roll(ends, step, axis=0), 0.0)
        step *= 2
    offs = ends - padded
    off1 = offs[:, 0:1]

    tri = (lax.broadcasted_iota(I32, (blk, blk), 0) <= lax.broadcasted_iota(I32, (blk, blk), 1)).astype(BF16)
    row8 = lax.broadcasted_iota(I32, (SUBLANES, blk), 0)

    def pos_body(j, run):
        c0, oh0, oh1 = onehots(j)
        both = oh0 + oh1
        csum = jnp.dot(both.astype(BF16), tri, preferred_element_type=F32) + run
        slot = off1 + csum - 1.0
        p0 = jnp.sum(oh0 * slot, axis=0, keepdims=True).astype(I32)
        p1 = jnp.sum(oh1 * slot, axis=0, keepdims=True).astype(I32)
        pos_ref[:, pl.ds(c0, blk)] = jnp.where(row8 == 0, p0, jnp.where(row8 == 1, p1, 0))
        return run + jnp.sum(both, axis=1, keepdims=True)

    lax.fori_loop(0, n_blk, pos_body, jnp.zeros((n_exp, 1), F32))

    n_tiles_pad = tmap_ref.shape[1]
    start = (lax.broadcasted_iota(I32, (1, n_tiles_pad), 1) * tile_rows).astype(F32)
    end1 = ends[:, 0:1]
    te = jnp.sum((end1 <= start).astype(F32), axis=0, keepdims=True)
    trow = lax.broadcasted_iota(I32, (n_exp, n_tiles_pad), 0).astype(F32)
    used_end = jnp.sum(jnp.where(trow == te, off1 + cnt[:, 0:1], 0.0), axis=0, keepdims=True)
    n_rows = jnp.clip(used_end - start, 0.0, float(tile_rows))
    has_tokens = cnt[:, 0:1] > 0.0
    te = jnp.minimum(te, jnp.max(jnp.where(has_tokens, trow, 0.0), axis=0, keepdims=True))
    nxt = jnp.min(jnp.where(trow > te, jnp.where(has_tokens, trow, float(n_exp)), float(n_exp)), axis=0, keepdims=True)
    r8 = lax.broadcasted_iota(I32, (SUBLANES, n_tiles_pad), 0)
    tmap_ref[...] = jnp.where(r8 == 0, te.astype(I32),
                              jnp.where(r8 == 1, n_rows.astype(I32), jnp.where(r8 == 2, nxt.astype(I32), 0)))


def _route(re, n_exp, n_tiles):
    n = re.shape[1]
    assert n % ROUTE_BLOCK == 0
    n_tiles_pad = pl.cdiv(n_tiles, LANES) * LANES
    kern = functools.partial(_route_kernel, n_exp=n_exp, tile_rows=SLOT_ROWS, blk=ROUTE_BLOCK)
    return pl.pallas_call(
        kern,
        out_shape=[jax.ShapeDtypeStruct((SUBLANES, n), I32), jax.ShapeDtypeStruct((SUBLANES, n_tiles_pad), I32)],
        name="route",
    )(re)


def _dispatch_kernel(*refs, aliased):
    if aliased:
        p0_ref, p1_ref, h_ref, _, xs_ref, sem = refs
    else:
        p0_ref, p1_ref, h_ref, xs_ref, sem = refs
    rows = h_ref.shape[0]

    def start(t, carry):
        pltpu.make_async_copy(h_ref.at[t], xs_ref.at[p0_ref[t]], sem).start(priority=0)
        pltpu.make_async_copy(h_ref.at[t], xs_ref.at[p1_ref[t]], sem).start(priority=1)
        return carry

    lax.fori_loop(0, rows, start, 0)
    for _ in range(TOP_K):
        pltpu.make_async_copy(h_ref, xs_ref.at[pl.ds(0, rows)], sem).wait()


def _dispatch(h2, p0, p1, xs, n_slots):
    n, _, d = h2.shape
    t = min(MOVE_ROWS, n)
    assert n % t == 0
    aliased = xs is not None
    in_specs = [
        pl.BlockSpec((t,), lambda i: (i,), memory_space=pltpu.SMEM),
        pl.BlockSpec((t,), lambda i: (i,), memory_space=pltpu.SMEM),
        pl.BlockSpec((t, 1, d), lambda i: (i, 0, 0)),
    ]
    args = [p0, p1, h2]
    if aliased:
        in_specs.append(pl.BlockSpec(memory_space=pl.ANY))
        args.append(xs)
    return pl.pallas_call(
        functools.partial(_dispatch_kernel, aliased=aliased),
        grid=(n // t,),
        in_specs=in_specs,
        out_specs=pl.BlockSpec(memory_space=pl.ANY),
        out_shape=jax.ShapeDtypeStruct((n_slots, 1, d), h2.dtype),
        scratch_shapes=[pltpu.SemaphoreType.DMA(())],
        input_output_aliases={3: 0} if aliased else {},
        compiler_params=pltpu.CompilerParams(dimension_semantics=("arbitrary",), has_side_effects=True),
        name="dispatch",
    )(*args)


def _experts_kernel(te_ref, nr_ref, nx_ref, xs_ref, wg_hbm, wu_hbm, wd_hbm, y_ref,
                    wg_st, wu_st, wd_st, wg_sc, wu_sc, wd_sc, sems, n_changes, *, n_exp, tile_rows):
    dh = xs_ref.shape[2]
    tiles_per_step = xs_ref.shape[0] // tile_rows

    def copies(e, slot):
        return (pltpu.make_async_copy(wg_hbm.at[e], wg_st.at[slot], sems.at[slot]),
                pltpu.make_async_copy(wu_hbm.at[e], wu_st.at[slot], sems.at[slot]),
                pltpu.make_async_copy(wd_hbm.at[e], wd_st.at[slot], sems.at[slot]))

    def one_tile(sub):
        tile = pl.program_id(0) * tiles_per_step + sub
        rows = pl.ds(sub * tile_rows, tile_rows)
        expert = te_ref[tile]

        @pl.when(tile == 0)
        def _():
            n_changes[0] = 0
            for cp in copies(expert, 0):
                cp.start(priority=1)

        @pl.when(jnp.logical_or(tile == 0, expert != te_ref[jnp.maximum(tile - 1, 0)]))
        def _():
            slot = lax.rem(n_changes[0], 2)
            n_changes[0] = n_changes[0] + 1
            for cp in copies(expert, slot):
                cp.wait()
            nxt = nx_ref[tile]

            @pl.when(nxt < n_exp)
            def _():
                for cp in copies(nxt, 1 - slot):
                    cp.start(priority=1)

            wg_sc[...] = wg_st[slot].astype(BF16)
            wu_sc[...] = wu_st[slot].astype(BF16)
            wd_sc[...] = wd_st[slot].astype(BF16)

        n_rows = nr_ref[tile]

        @pl.when(n_rows > 0)
        def _():
            live = lax.broadcasted_iota(I32, (tile_rows, 2 * dh), 0) < n_rows
            x = jnp.where(live, _unpack_halves(xs_ref[rows].reshape(tile_rows, dh)), 0.0).astype(BF16)
            gate = jnp.dot(x, wg_sc[...], preferred_element_type=F32)
            up = jnp.dot(x, wu_sc[...], preferred_element_type=F32)
            hid = (jax.nn.silu(gate) * up).astype(BF16)
            y = jnp.dot(hid, wd_sc[...], preferred_element_type=F32)
            y_ref[rows] = _pack_halves(y).reshape(tile_rows, 1, dh)

        @pl.when(n_rows <= 0)
        def _():
            y_ref[rows] = _pack_halves(jnp.zeros((tile_rows, 2 * dh), F32)).reshape(tile_rows, 1, dh)

    for sub in range(tiles_per_step):
        one_tile(sub)


def _experts(xs, te, nr, nx, w_gate, w_up, w_down):
    n_slots, _, dh = xs.shape
    n_exp, d, d_e = w_gate.shape
    t = SLOT_ROWS * EXPERT_STEP_TILES
    est = 2 * 2 * t * d * 2 + 2 * 3 * d * d_e * 4 + 3 * d * d_e * 2 + 4 * SLOT_ROWS * d * 4
    row_spec = pl.BlockSpec((t, 1, dh), lambda i, te, nr, nx: (i, 0, 0))
    return pl.pallas_call(
        functools.partial(_experts_kernel, n_exp=n_exp, tile_rows=SLOT_ROWS),
        grid_spec=pltpu.PrefetchScalarGridSpec(
            num_scalar_prefetch=3,
            grid=(n_slots // t,),
            in_specs=[row_spec, pl.BlockSpec(memory_space=pl.ANY), pl.BlockSpec(memory_space=pl.ANY),
                      pl.BlockSpec(memory_space=pl.ANY)],
            out_specs=row_spec,
            scratch_shapes=[
                pltpu.VMEM((2, d, d_e), F32), pltpu.VMEM((2, d, d_e), F32), pltpu.VMEM((2, d_e, d), F32),
                pltpu.VMEM((d, d_e), BF16), pltpu.VMEM((d, d_e), BF16), pltpu.VMEM((d_e, d), BF16),
                pltpu.SemaphoreType.DMA((2,)), pltpu.SMEM((1,), I32)],
        ),
        out_shape=jax.ShapeDtypeStruct((n_slots, 1, dh), U32),
        compiler_params=pltpu.CompilerParams(
            dimension_semantics=("arbitrary",), vmem_limit_bytes=_vmem_limit(est)),
        name="experts",
    )(te, nr, nx, xs, w_gate, w_up, w_down)


def _combine_kernel(p0_ref, p1_ref, p0n_ref, p1n_ref, xp_ref, rc_ref, g_ref, y_hbm, out_ref, y0_buf, y1_buf, sems,
                    *, n_tiles):
    i = pl.program_id(0)
    rows, d = xp_ref.shape

    def gather(pa_ref, pb_ref, slot):
        base = slot * rows
        def body(t, carry):
            pltpu.make_async_copy(y_hbm.at[pa_ref[t]], y0_buf.at[base + t], sems.at[slot]).start(priority=0)
            pltpu.make_async_copy(y_hbm.at[pb_ref[t]], y1_buf.at[base + t], sems.at[slot]).start(priority=1)
            return carry
        lax.fori_loop(0, rows, body, 0)

    slot = lax.rem(i, 2)

    @pl.when(i == 0)
    def _():
        gather(p0_ref, p1_ref, 0)

    @pl.when(i + 1 < n_tiles)
    def _():
        gather(p0n_ref, p1n_ref, 1 - slot)

    cur = pl.ds(pl.multiple_of(slot * rows, rows), rows)
    for buf in (y0_buf, y1_buf):
        pltpu.make_async_copy(y_hbm.at[pl.ds(0, rows)], buf.at[cur], sems.at[slot]).wait()
    ct = rc_ref[...].T
    y0 = _unpack_halves(y0_buf[cur].reshape(rows, d // 2))
    y1 = _unpack_halves(y1_buf[cur].reshape(rows, d // 2))
    moe = ct[:, 0:1] * y0 + ct[:, 1:2] * y1
    out_ref[...] = _rms_rows(xp_ref[...] + moe, g_ref[...])


def _combine(xp, rc, p0, p1, y, g):
    n, d = xp.shape
    t = min(MOVE_ROWS, n)
    assert n % t == 0
    n_t = n // t
    est = 2 * 2 * t * d * 4 + 2 * 2 * t * d * 2 + 8 * t * d * 4
    nxt = lambda i: (jnp.minimum(i + 1, n_t - 1),)
    return pl.pallas_call(
        functools.partial(_combine_kernel, n_tiles=n_t),
        grid=(n_t,),
        in_specs=[
            pl.BlockSpec((t,), lambda i: (i,), memory_space=pltpu.SMEM),
            pl.BlockSpec((t,), lambda i: (i,), memory_space=pltpu.SMEM),
            pl.BlockSpec((t,), nxt, memory_space=pltpu.SMEM),
            pl.BlockSpec((t,), nxt, memory_space=pltpu.SMEM),
            pl.BlockSpec((t, d), lambda i: (i, 0)),
            pl.BlockSpec((SUBLANES, t), lambda i: (0, i)),
            pl.BlockSpec((1, d), lambda i: (0, 0)),
            pl.BlockSpec(memory_space=pl.ANY),
        ],
        out_specs=pl.BlockSpec((t, d), lambda i: (i, 0)),
        out_shape=jax.ShapeDtypeStruct((n, d), F32),
        scratch_shapes=[pltpu.VMEM((2 * t, 1, d // 2), U32), pltpu.VMEM((2 * t, 1, d // 2), U32),
                        pltpu.SemaphoreType.DMA((2,))],
        compiler_params=pltpu.CompilerParams(
            dimension_semantics=("arbitrary",), vmem_limit_bytes=_vmem_limit(est)),
        name="combine",
    )(p0, p1, p0, p1, xp, rc, g, y)


def _rel_buckets(rel):
    nb = N_BUCKETS // 2
    ret = (rel > 0).astype(I32) * nb
    n = jnp.abs(rel)
    max_exact = nb // 2
    nf = jnp.maximum(n, 1).astype(F32)
    large = max_exact + (jnp.log(nf / max_exact) / math.log(MAX_DISTANCE / max_exact)
                         * (nb - max_exact)).astype(I32)
    large = jnp.minimum(large, nb - 1)
    return ret + jnp.where(n < max_exact, n, large)


def _rel_bias(table, window, n_kv):
    n_heads = table.shape[1]
    span = window + CHUNK
    rows = table[_rel_buckets(jnp.arange(-(span - 1), CHUNK, dtype=I32))].astype(F32)
    bias = jnp.stack([rows[CHUNK - 1 - q:CHUNK - 1 - q + span] for q in range(CHUNK)])
    bias = jnp.transpose(bias, (2, 0, 1))
    bias = jnp.pad(bias, ((0, 0), (0, 0), (0, KEY_PAD - bias.shape[2])), constant_values=NEG_INF)
    bias = bias.reshape(n_kv, n_heads // n_kv // 2, 2, CHUNK, KEY_PAD)
    return jnp.transpose(bias, (0, 1, 3, 2, 4)).reshape(n_kv, (n_heads // n_kv // 2) * CHUNK, 2 * KEY_PAD)


def kernel(x_prompt, x_sample, cache_conv, cache_k, cache_v, rel_bias_table, norm_mix_g, w_in, conv_w, w_conv_out, attn_sinks, w_attn_out, w_o, norm_ffn_g, w_group, b_group, w_expert_router, b_expert_router, w_gate, w_up, w_down, final_norm_g):
    assert w_in.shape[0] == 1, "single-layer step"
    batch, seq, d = x_prompt.shape
    dec_batch, dec_seq, _ = x_sample.shape
    assert dec_seq == CHUNK and seq % MIX_ROWS == 0 and (dec_batch * dec_seq) % MIX_ROWS == 0
    d_conv = conv_w.shape[-1]
    window, n_kv, head_dim = cache_k.shape[2], cache_k.shape[3], cache_k.shape[4]
    n_heads = attn_sinks.shape[-1]
    d_attn, d_kv = n_heads * head_dim, n_kv * head_dim
    n_groups, n_exp = w_group.shape[-1], w_expert_router.shape[-1]
    assert n_groups <= GROUP_ROW0 and d_conv == d_attn and 2 * d_conv == d
    assert n_kv == 2 and d_kv == LANES and (n_heads // n_kv) % 2 == 0 and window + CHUNK <= KEY_PAD
    dims = (n_heads, n_kv, head_dim, window, n_groups, n_exp, d_conv)

    w_all = w_in[0].astype(BF16)
    kv0 = 3 * d_conv + d_attn
    g1 = norm_mix_g[0][None, :]
    wr = jnp.zeros((GROUP_ROW0 + n_exp, d), F32)
    wr = wr.at[:n_groups].set(w_group[0].T).at[GROUP_ROW0:].set(w_expert_router[0].T).astype(BF16)
    br = jnp.zeros((GROUP_ROW0 + n_exp, 1), F32)
    br = br.at[:n_groups, 0].set(b_group[0]).at[GROUP_ROW0:, 0].set(b_expert_router[0])
    weights = (conv_w[0], w_conv_out[0].astype(BF16), w_attn_out[0].astype(BF16), w_o[0].astype(BF16),
               _rel_bias(rel_bias_table, window, n_kv), attn_sinks, norm_ffn_g[0][None, :], wr, br)

    xp2d = x_prompt.reshape(batch * seq, d)
    xs2d = x_sample.reshape(dec_batch * dec_seq, d)
    n_p, n_s = xp2d.shape[0], xs2d.shape[0]
    n_tok = n_p + n_s

    proj_p, kv_p = _inproj(xp2d, g1, w_all, kv0, 2 * d_kv)
    proj_s, kv_s = _inproj(xs2d, g1, w_all, kv0, 2 * d_kv)
    xres_p, h2_p, re_p, rc_p, conv_p, k_p, v_p = _mix(
        xp2d, proj_p, kv_p, None, weights, n_seq=batch, sample=False, dims=dims)
    caches = (cache_conv[0], cache_k[0].reshape(dec_batch, window, d_kv), cache_v[0].reshape(dec_batch, window, d_kv))
    xres_s, h2_s, re_s, rc_s, conv_s, k_s, v_s = _mix(
        xs2d, proj_s, kv_s, caches, weights, n_seq=dec_batch, sample=True, dims=dims)

    n_tiles = pl.cdiv((TOP_K * n_tok) // SLOT_ROWS + n_exp, EXPERT_STEP_TILES) * EXPERT_STEP_TILES
    n_slots = n_tiles * SLOT_ROWS
    pos, tmap = _route(jnp.concatenate([re_p, re_s], axis=1), n_exp, n_tiles)
    p0, p1 = pos[0], pos[1]
    xs = _dispatch(h2_p, p0[:n_p], p1[:n_p], None, n_slots)
    xs = _dispatch(h2_s, p0[n_p:], p1[n_p:], xs, n_slots)
    y = _experts(xs, tmap[0, :n_tiles], tmap[1, :n_tiles], tmap[2, :n_tiles], w_gate[0], w_up[0], w_down[0])
    gf = final_norm_g[None, :]
    y_prompt = _combine(xres_p, rc_p, p0[:n_p], p1[:n_p], y, gf).reshape(batch, seq, d)
    y_sample = _combine(xres_s, rc_s, p0[n_p:], p1[n_p:], y, gf).reshape(dec_batch, dec_seq, d)

    kv_shape = (1, -1, window, n_kv, head_dim)
    return (y_prompt, y_sample, conv_p[None], k_p.reshape(kv_shape), v_p.reshape(kv_shape),
            conv_s[None], k_s.reshape(kv_shape), v_s.reshape(kv_shape))
```

```python
import functools
import math

import jax
import jax.numpy as jnp
from jax import lax
from jax.experimental import pallas as pl
from jax.experimental.pallas import tpu as pltpu

F32, BF16, I32, U32 = jnp.float32, jnp.bfloat16, jnp.int32, jnp.uint32

CHUNK = 64
N_BUCKETS = 32
MAX_DISTANCE = 128
EPS = 1e-6
NEG_INF = -1e30
TOP_K = 2

V7X_VMEM_BYTES = 64 * 1024 * 1024
SUBLANES = 8
LANES = 128

INPROJ_ROWS = 1024
INPROJ_COLS = 2048
MIX_ROWS = 256
MOVE_ROWS = 512
SLOT_ROWS = 256
EXPERT_STEP_TILES = 4
ROUTE_BLOCK = 512
GROUP_ROW0 = 8
KEY_PAD = 256
KV_VARIANTS = 4


def _vmem_limit(nbytes):
    return int(min(V7X_VMEM_BYTES - (4 << 20), max(nbytes, 32 << 20)))


def _pack_halves(x):
    half = x.shape[1] // 2
    return pltpu.pack_elementwise([x[:, :half], x[:, half:]], packed_dtype=BF16)


def _unpack_halves(w):
    lo = pltpu.unpack_elementwise(w, index=0, packed_dtype=BF16, unpacked_dtype=F32)
    hi = pltpu.unpack_elementwise(w, index=1, packed_dtype=BF16, unpacked_dtype=F32)
    return jnp.concatenate([lo, hi], axis=1)


def _rms_rows(x, g):
    r = lax.rsqrt(jnp.mean(x * x, axis=-1, keepdims=True) + EPS)
    return (x * r) * g


def _inproj_kernel(x_ref, g_ref, wm_ref, wkv_ref, proj_ref, kv_ref, h_sc):
    @pl.when(pl.program_id(1) == 0)
    def _():
        rows = 128
        for i in range(x_ref.shape[0] // rows):
            r = slice(i * rows, (i + 1) * rows)
            h = _rms_rows(x_ref[r, :], g_ref[...]).astype(BF16)
            h_sc[r, :] = h
            kv_ref[r, :] = jnp.dot(h, wkv_ref[...], preferred_element_type=F32)

    proj_ref[...] = jnp.dot(h_sc[...], wm_ref[...], preferred_element_type=F32).astype(BF16)


def _inproj(x2d, g, w_all, kv0, n_kv):
    n, d = x2d.shape
    tm = min(INPROJ_ROWS, n)
    tn = INPROJ_COLS
    n_main = w_all.shape[1] - n_kv
    assert kv0 % tn == 0 and n_main % tn == 0 and n_kv % LANES == 0
    est = 2 * tm * d * 4 + tm * d * 2 + 2 * d * tn * 2 + 2 * d * n_kv * 2 + 2 * tm * tn * 2 + 2 * tm * n_kv * 4 + tm * tn * 4
    return pl.pallas_call(
        _inproj_kernel,
        grid=(n // tm, n_main // tn),
        in_specs=[
            pl.BlockSpec((tm, d), lambda i, j: (i, 0)),
            pl.BlockSpec((1, d), lambda i, j: (0, 0)),
            pl.BlockSpec((pl.Element(d), pl.Element(tn)),
                         lambda i, j: (0, pl.multiple_of(jnp.where(j * tn < kv0, j * tn, j * tn + n_kv), LANES))),
            pl.BlockSpec((pl.Element(d), pl.Element(n_kv)), lambda i, j: (0, kv0)),
        ],
        out_specs=[
            pl.BlockSpec((tm, tn), lambda i, j: (i, j)),
            pl.BlockSpec((tm, n_kv), lambda i, j: (i, 0)),
        ],
        out_shape=[jax.ShapeDtypeStruct((n, n_main), BF16), jax.ShapeDtypeStruct((n, n_kv), F32)],
        scratch_shapes=[pltpu.VMEM((tm, d), BF16)],
        compiler_params=pltpu.CompilerParams(
            dimension_semantics=("arbitrary", "arbitrary"), vmem_limit_bytes=_vmem_limit(est + (8 << 20))),
        name="inproj",
    )(x2d, g, w_all, w_all)


def _conv_rows(u, prev2, prev1, w):
    row = lax.broadcasted_iota(I32, u.shape, 0)
    u1 = jnp.where(row == 0, prev1, pltpu.roll(u, 1, axis=0))
    u2 = jnp.where(row == 0, prev2, jnp.where(row == 1, prev1, pltpu.roll(u, 2, axis=0)))
    return (w[0:1] * u2 + w[1:2] * u1) + w[2:3] * u


def _mix_kernel(*refs, sample, n_heads, n_kv, head_dim, window, n_groups, n_exp):
    if sample:
        (x_ref, ga_ref, gb_ref, b_ref, c_ref, xc_ref, q_ref, kv_ref, cconv_ref, ck_ref, cv_ref,
         convw_ref, wco_ref, wao_ref, wo_ref, bias_ref, sinks_ref, g2_ref, wr_ref, br_ref,
         xp_ref, h2_ref, re_ref, rc_ref, sconv_ref, sk_ref, sv_ref,
         kvar, o_sc, ya_sc, carry_u, s_sc, e_sc, sink_sc) = refs
    else:
        (x_ref, ga_ref, gb_ref, b_ref, c_ref, xc_ref, q_ref, kv_ref,
         convw_ref, wco_ref, wao_ref, wo_ref, bias_ref, sinks_ref, g2_ref, wr_ref, br_ref,
         xp_ref, h2_ref, re_ref, rc_ref, sconv_ref, sk_ref, sv_ref,
         kvar, o_sc, ya_sc, carry_u, s_sc, e_sc, sink_sc) = refs
    t_rows, d_model = x_ref.shape
    n_chunks = t_rows // CHUNK
    span = window + CHUNK
    q_per_kv = n_heads // n_kv
    d_kv = n_kv * head_dim
    scale = 1.0 / math.sqrt(head_dim)
    scale_is_pow2 = math.frexp(scale)[0] == 0.5
    epg = n_exp // n_groups
    seq_start = pl.program_id(1) == 0

    def conv_branch():
        w_conv = convw_ref[...]
        if sample:
            for s in range(n_chunks):
                rows = slice(s * CHUNK, (s + 1) * CHUNK)
                u = c_ref[rows, :].astype(F32) * xc_ref[rows, :].astype(F32)
                y = _conv_rows(u, cconv_ref[s, 0:1, :], cconv_ref[s, 1:2, :], w_conv)
                ya_sc[rows, :] = (b_ref[rows, :].astype(F32) * y).astype(BF16)
                sconv_ref[s] = u[CHUNK - 2:CHUNK, :]
        else:
            u = c_ref[...].astype(F32) * xc_ref[...].astype(F32)
            prev = jnp.where(seq_start, 0.0, carry_u[...])
            y = _conv_rows(u, prev[SUBLANES - 2:SUBLANES - 1], prev[SUBLANES - 1:SUBLANES], w_conv)
            ya_sc[...] = (b_ref[...].astype(F32) * y).astype(BF16)
            carry_u[...] = u[t_rows - SUBLANES:t_rows, :]
            sconv_ref[0] = u[t_rows - 2:t_rows, :]

    def store_kv(row0, k_rows, v_rows):
        n_rows = k_rows.shape[0]
        low = lax.broadcasted_iota(I32, k_rows.shape, 1) < head_dim
        for base, a in ((0, k_rows), (KV_VARIANTS, v_rows)):
            b = pltpu.roll(a, head_dim, axis=1)
            kvar[base + 0, row0:row0 + n_rows, :] = jnp.where(low, a, 0.0).astype(BF16)
            kvar[base + 1, row0:row0 + n_rows, :] = jnp.where(low, 0.0, a).astype(BF16)
            kvar[base + 2, row0:row0 + n_rows, :] = jnp.where(low, b, 0.0).astype(BF16)
            kvar[base + 3, row0:row0 + n_rows, :] = jnp.where(low, 0.0, b).astype(BF16)

    if sample:
        for s in range(n_chunks):
            rows = slice(s * CHUNK, (s + 1) * CHUNK)
            store_kv(s * span, ck_ref[s], cv_ref[s])
            store_kv(s * span + window, kv_ref[rows, 0:d_kv], kv_ref[rows, d_kv:2 * d_kv])
            sk_ref[s, 0:window - CHUNK, :] = ck_ref[s, CHUNK:window, :]
            sv_ref[s, 0:window - CHUNK, :] = cv_ref[s, CHUNK:window, :]
            sk_ref[s, window - CHUNK:window, :] = kv_ref[rows, 0:d_kv]
            sv_ref[s, window - CHUNK:window, :] = kv_ref[rows, d_kv:2 * d_kv]
        k_stride = span
    else:
        @pl.when(seq_start)
        def _():
            kvar[:, 0:window, :] = jnp.zeros((2 * KV_VARIANTS, window, d_kv), BF16)
        store_kv(window, kv_ref[:, 0:d_kv], kv_ref[:, d_kv:2 * d_kv])
        sk_ref[0] = kv_ref[t_rows - window:t_rows, 0:d_kv]
        sv_ref[0] = kv_ref[t_rows - window:t_rows, d_kv:2 * d_kv]
        k_stride = CHUNK

    n_pairs = q_per_kv // 2
    key_pad_rows = jnp.zeros((KEY_PAD - span, d_kv), BF16)
    ones_d = ((lax.broadcasted_iota(I32, (2 * KEY_PAD, LANES), 0) < KEY_PAD)
              == (lax.broadcasted_iota(I32, (2 * KEY_PAD, LANES), 1) < head_dim)).astype(BF16)
    low_half = lax.broadcasted_iota(I32, (CHUNK, LANES), 1) < head_dim
    ya_parts = []
    ya_cols = d_model // (n_chunks * n_kv)
    blocks = [(c, n) for c in range(n_chunks) for n in range(n_kv)]

    def key_rows(c):
        return slice(c * k_stride, c * k_stride + span)

    for b, (c, n) in enumerate(blocks):
        q_rows = slice(c * CHUNK, (c + 1) * CHUNK)
        top, bot = (0, 3) if n == 0 else (2, 1)
        kd = jnp.concatenate([kvar[top, key_rows(c), :], key_pad_rows, kvar[bot, key_rows(c), :], key_pad_rows],
                             axis=0)
        q4 = jnp.concatenate(
            [q_ref[q_rows, (n * n_pairs + j) * LANES:(n * n_pairs + j + 1) * LANES] for j in range(n_pairs)], axis=0)
        if scale_is_pow2:
            q4 = q4 * scale
        s4 = lax.dot_general(q4, kd, (((1,), (1,)), ((), ())), preferred_element_type=F32)
        s_sc[b] = s4 if scale_is_pow2 else s4 * scale
    conv_branch()
    for b, (c, n) in enumerate(blocks):
        masked = (not sample) and c * CHUNK < window
        if masked:
            first_key = (pl.program_id(1) * n_chunks + c) * CHUNK - window
            valid = lax.broadcasted_iota(I32, (CHUNK, KEY_PAD), 1) + first_key >= 0
        for j in range(n_pairs):
            sink_halves = []
            for half in range(2):
                blk = (slice(j * CHUNK, (j + 1) * CHUNK), slice(half * KEY_PAD, (half + 1) * KEY_PAD))
                s = s_sc[b, blk[0], blk[1]] + bias_ref[n, blk[0], blk[1]]
                if masked:
                    s = jnp.where(valid, s, NEG_INF)
                sink = sinks_ref[0, n * q_per_kv + 2 * j + half]
                mx = jnp.maximum(jnp.max(s, axis=-1, keepdims=True), sink)
                e_sc[b, blk[0], blk[1]] = jnp.exp(s - mx).astype(BF16)
                sink_halves.append(jnp.exp(sink - mx))
            sink_sc[b, j * CHUNK:(j + 1) * CHUNK, :] = jnp.where(low_half, sink_halves[0], sink_halves[1])
        ya_parts.append(jnp.dot(ya_sc[...], wco_ref[:, b * ya_cols:(b + 1) * ya_cols], preferred_element_type=F32))
    for b, (c, n) in enumerate(blocks):
        q_rows = slice(c * CHUNK, (c + 1) * CHUNK)
        top, bot = (0, 3) if n == 0 else (2, 1)
        vd = jnp.concatenate([kvar[KV_VARIANTS + top, key_rows(c), :], key_pad_rows,
                              kvar[KV_VARIANTS + bot, key_rows(c), :], key_pad_rows], axis=0)
        od = jnp.dot(e_sc[b], jnp.concatenate([vd, ones_d], axis=1), preferred_element_type=F32)
        o4 = od[:, 0:LANES] / (od[:, LANES:2 * LANES] + sink_sc[b])
        for j in range(n_pairs):
            o_sc[q_rows, (n * n_pairs + j) * LANES:(n * n_pairs + j + 1) * LANES] = (
                o4[j * CHUNK:(j + 1) * CHUNK].astype(BF16))
    if not sample:
        kvar[:, 0:window, :] = kvar[:, t_rows:t_rows + window, :]
    y_a = jnp.concatenate(ya_parts, axis=1)
    y_b = jnp.dot(o_sc[...], wao_ref[...], preferred_element_type=F32)

    m = jax.nn.sigmoid(ga_ref[...].astype(F32)) * y_a + jax.nn.sigmoid(gb_ref[...].astype(F32)) * y_b
    xp = x_ref[...] + jnp.dot(m.astype(BF16), wo_ref[...], preferred_element_type=F32)
    xp_ref[...] = xp

    h2 = _rms_rows(xp, g2_ref[...])
    h2_ref[...] = _pack_halves(h2).reshape(t_rows, 1, d_model // 2)
    lt = lax.dot_general(wr_ref[...], h2.astype(BF16), (((1,), (1,)), ((), ())),
                         preferred_element_type=F32) + br_ref[...]
    lg = lt[0:n_groups]
    eg = jnp.exp(lg - jnp.max(lg, axis=0, keepdims=True))
    gp = eg / jnp.sum(eg, axis=0, keepdims=True)
    gw = jnp.max(gp, axis=0, keepdims=True)
    gi = lax.broadcasted_iota(I32, gp.shape, 0).astype(F32)
    gsel = jnp.min(jnp.where(gp == gw, gi, float(n_groups)), axis=0, keepdims=True)
    el = jnp.zeros((epg, t_rows), F32)
    for g in range(n_groups):
        el = jnp.where(gsel == float(g), lt[GROUP_ROW0 + g * epg:GROUP_ROW0 + (g + 1) * epg], el)
    ei = lax.broadcasted_iota(I32, el.shape, 0).astype(F32)
    v1 = jnp.max(el, axis=0, keepdims=True)
    i1 = jnp.min(jnp.where(el == v1, ei, float(epg)), axis=0, keepdims=True)
    el2 = jnp.where(ei == i1, -jnp.inf, el)
    v2 = jnp.max(el2, axis=0, keepdims=True)
    i2 = jnp.min(jnp.where(el2 == v2, ei, float(epg)), axis=0, keepdims=True)
    a1 = jnp.exp(v1 - v1)
    a2 = jnp.exp(v2 - v1)
    den = a1 + a2
    c1 = gw * (a1 / den)
    c2 = gw * (a2 / den)
    e1 = (gsel * float(epg) + i1).astype(I32)
    e2 = (gsel * float(epg) + i2).astype(I32)
    row8 = lax.broadcasted_iota(I32, (SUBLANES, t_rows), 0)
    re_ref[...] = jnp.where(row8 == 0, e1, jnp.where(row8 == 1, e2, 0))
    rc_ref[...] = jnp.where(row8 == 0, c1, jnp.where(row8 == 1, c2, 0.0))


def _mix(x2d, proj, kv, caches, weights, *, n_seq, sample, dims):
    n_heads, n_kv, head_dim, window, n_groups, n_exp, d_conv = dims
    n, d = x2d.shape
    t = MIX_ROWS
    n_chunks = t // CHUNK
    span = window + CHUNK
    d_attn = n_heads * head_dim
    d_kv = n_kv * head_dim
    convw, wco, wao, wo, bias, sinks, g2, wr, br = weights
    if sample:
        n_t = n // t
        grid = (n_t, 1)
        tok = lambda i, j: (i, 0)
        n_state = n // CHUNK
        state_blk = n_chunks
        st = lambda i, j: (i, 0, 0)
    else:
        n_t = (n // n_seq) // t
        grid = (n_seq, n_t)
        tok = lambda i, j: (i * n_t + j, 0)
        n_state = n_seq
        state_blk = 1
        st = lambda i, j: (i, 0, 0)
    const2 = lambda i, j: (0, 0)

    def col(width, idx):
        return pl.BlockSpec((t, width), lambda i, j: (tok(i, j)[0], idx))

    def resident(shape):
        return pl.BlockSpec(shape, const2, pipeline_mode=pl.Buffered(1))

    in_specs = [
        pl.BlockSpec((t, d), tok),
        col(d, 2), col(d, 3),
        col(d_conv, 0), col(d_conv, 1), col(d_conv, 2), col(d_attn, 3),
        pl.BlockSpec((t, 2 * d_kv), tok),
    ]
    args = [x2d, proj, proj, proj, proj, proj, proj, kv]
    if sample:
        cconv, ck, cv = caches
        in_specs += [
            pl.BlockSpec((n_chunks, cconv.shape[1], d_conv), st),
            pl.BlockSpec((n_chunks, window, d_kv), st),
            pl.BlockSpec((n_chunks, window, d_kv), st),
        ]
        args += [cconv, ck, cv]
    in_specs += [
        resident(convw.shape), resident(wco.shape), resident(wao.shape), resident(wo.shape),
        pl.BlockSpec(bias.shape, lambda i, j: (0, 0, 0), pipeline_mode=pl.Buffered(1)),
        pl.BlockSpec(memory_space=pltpu.SMEM),
        resident(g2.shape), resident(wr.shape), resident(br.shape),
    ]
    args += [convw, wco, wao, wo, bias, sinks, g2, wr, br]
    out_specs = [
        pl.BlockSpec((t, d), tok),
        pl.BlockSpec((t, 1, d // 2), lambda i, j: (tok(i, j)[0], 0, 0)),
        pl.BlockSpec((SUBLANES, t), lambda i, j: (0, tok(i, j)[0])),
        pl.BlockSpec((SUBLANES, t), lambda i, j: (0, tok(i, j)[0])),
        pl.BlockSpec((state_blk, 2, d_conv), st),
        pl.BlockSpec((state_blk, window, d_kv), st),
        pl.BlockSpec((state_blk, window, d_kv), st),
    ]
    out_shape = [
        jax.ShapeDtypeStruct((n, d), F32),
        jax.ShapeDtypeStruct((n, 1, d // 2), U32),
        jax.ShapeDtypeStruct((SUBLANES, n), I32),
        jax.ShapeDtypeStruct((SUBLANES, n), F32),
        jax.ShapeDtypeStruct((n_state, 2, d_conv), F32),
        jax.ShapeDtypeStruct((n_state, window, d_kv), F32),
        jax.ShapeDtypeStruct((n_state, window, d_kv), F32),
    ]
    kv_rows = n_chunks * span if sample else window + t
    scratch = [
        pltpu.VMEM((2 * KV_VARIANTS, kv_rows, d_kv), BF16),
        pltpu.VMEM((t, d_attn), BF16), pltpu.VMEM((t, d_conv), BF16),
        pltpu.VMEM((SUBLANES, d_conv), F32),
        pltpu.VMEM((n_chunks * n_kv, (n_heads // n_kv // 2) * CHUNK, 2 * KEY_PAD), F32),
        pltpu.VMEM((n_chunks * n_kv, (n_heads // n_kv // 2) * CHUNK, 2 * KEY_PAD), BF16),
        pltpu.VMEM((n_chunks * n_kv, (n_heads // n_kv // 2) * CHUNK, LANES), F32),
    ]
    est = (2 * t * d * 4 * 3 + 2 * 2 * t * d * 2 + 2 * 4 * t * d_conv * 2
           + (2 * d_conv * d + d * d) * 2 + 12 * t * d * 4)
    kern = functools.partial(_mix_kernel, sample=sample, n_heads=n_heads, n_kv=n_kv, head_dim=head_dim,
                             window=window, n_groups=n_groups, n_exp=n_exp)
    return pl.pallas_call(
        kern, grid=grid, in_specs=in_specs, out_specs=out_specs, out_shape=out_shape,
        scratch_shapes=scratch,
        compiler_params=pltpu.CompilerParams(
            dimension_semantics=("arbitrary", "arbitrary"), vmem_limit_bytes=_vmem_limit(est)),
        name="mix_sample" if sample else "mix_prompt",
    )(*args)


def _route_kernel(re_ref, pos_ref, tmap_ref, *, n_exp, tile_rows, blk):
    n = re_ref.shape[1]
    n_blk = n // blk
    erow = lax.broadcasted_iota(I32, (n_exp, blk), 0)

    def onehots(j):
        c0 = pl.multiple_of(j * blk, blk)
        oh0 = (erow == re_ref[0:1, pl.ds(c0, blk)]).astype(F32)
        oh1 = (erow == re_ref[1:2, pl.ds(c0, blk)]).astype(F32)
        return c0, oh0, oh1

    def count_body(j, cnt):
        _, oh0, oh1 = onehots(j)
        return cnt + jnp.sum(oh0 + oh1, axis=1, keepdims=True)

    cnt = lax.fori_loop(0, n_blk, count_body, jnp.zeros((n_exp, 1), F32))
    cnt = jnp.broadcast_to(cnt, (n_exp, LANES))
    padded = jnp.ceil(cnt / tile_rows) * tile_rows
    ends = padded
    prow = lax.broadcasted_iota(I32, ends.shape, 0)
    step = 1
    while step < n_exp:
        ends = ends + jnp.where(prow >= step, pltpu.roll(ends, step, axis=0), 0.0)
        step *= 2
    offs = ends - padded
    off1 = offs[:, 0:1]

    tri = (lax.broadcasted_iota(I32, (blk, blk), 0) <= lax.broadcasted_iota(I32, (blk, blk), 1)).astype(BF16)
    row8 = lax.broadcasted_iota(I32, (SUBLANES, blk), 0)

    def pos_body(j, run):
        c0, oh0, oh1 = onehots(j)
        both = oh0 + oh1
        csum = jnp.dot(both.astype(BF16), tri, preferred_element_type=F32) + run
        slot = off1 + csum - 1.0
        p0 = jnp.sum(oh0 * slot, axis=0, keepdims=True).astype(I32)
        p1 = jnp.sum(oh1 * slot, axis=0, keepdims=True).astype(I32)
        pos_ref[:, pl.ds(c0, blk)] = jnp.where(row8 == 0, p0, jnp.where(row8 == 1, p1, 0))
        return run + jnp.sum(both, axis=1, keepdims=True)

    lax.fori_loop(0, n_blk, pos_body, jnp.zeros((n_exp, 1), F32))

    n_tiles_pad = tmap_ref.shape[1]
    start = (lax.broadcasted_iota(I32, (1, n_tiles_pad), 1) * tile_rows).astype(F32)
    end1 = ends[:, 0:1]
    te = jnp.sum((end1 <= start).astype(F32), axis=0, keepdims=True)
    trow = lax.broadcasted_iota(I32, (n_exp, n_tiles_pad), 0).astype(F32)
    used_end = jnp.sum(jnp.where(trow == te, off1 + cnt[:, 0:1], 0.0), axis=0, keepdims=True)
    n_rows = jnp.clip(used_end - start, 0.0, float(tile_rows))
    has_tokens = cnt[:, 0:1] > 0.0
    te = jnp.minimum(te, jnp.max(jnp.where(has_tokens, trow, 0.0), axis=0, keepdims=True))
    nxt = jnp.min(jnp.where(trow > te, jnp.where(has_tokens, trow, float(n_exp)), float(n_exp)), axis=0, keepdims=True)
    r8 = lax.broadcasted_iota(I32, (SUBLANES, n_tiles_pad), 0)
    tmap_ref[...] = jnp.where(r8 == 0, te.astype(I32),
                              jnp.where(r8 == 1, n_rows.astype(I32), jnp.where(r8 == 2, nxt.astype(I32), 0)))


def _route(re, n_exp, n_tiles):
    n = re.shape[1]
    assert n % ROUTE_BLOCK == 0
    n_tiles_pad = pl.cdiv(n_tiles, LANES) * LANES
    kern = functools.partial(_route_kernel, n_exp=n_exp, tile_rows=SLOT_ROWS, blk=ROUTE_BLOCK)
    return pl.pallas_call(
        kern,
        out_shape=[jax.ShapeDtypeStruct((SUBLANES, n), I32), jax.ShapeDtypeStruct((SUBLANES, n_tiles_pad), I32)],
        name="route",
    )(re)


def _dispatch_kernel(*refs, aliased):
    if aliased:
        p0_ref, p1_ref, h_ref, _, xs_ref, sem = refs
    else:
        p0_ref, p1_ref, h_ref, xs_ref, sem = refs
    rows = h_ref.shape[0]

    def start(t, carry):
        pltpu.make_async_copy(h_ref.at[t], xs_ref.at[p0_ref[t]], sem).start(priority=0)
        pltpu.make_async_copy(h_ref.at[t], xs_ref.at[p1_ref[t]], sem).start(priority=1)
        return carry

    lax.fori_loop(0, rows, start, 0)
    for _ in range(TOP_K):
        pltpu.make_async_copy(h_ref, xs_ref.at[pl.ds(0, rows)], sem).wait()


def _dispatch(h2, p0, p1, xs, n_slots):
    n, _, d = h2.shape
    t = min(MOVE_ROWS, n)
    assert n % t == 0
    aliased = xs is not None
    in_specs = [
        pl.BlockSpec((t,), lambda i: (i,), memory_space=pltpu.SMEM),
        pl.BlockSpec((t,), lambda i: (i,), memory_space=pltpu.SMEM),
        pl.BlockSpec((t, 1, d), lambda i: (i, 0, 0)),
    ]
    args = [p0, p1, h2]
    if aliased:
        in_specs.append(pl.BlockSpec(memory_space=pl.ANY))
        args.append(xs)
    return pl.pallas_call(
        functools.partial(_dispatch_kernel, aliased=aliased),
        grid=(n // t,),
        in_specs=in_specs,
        out_specs=pl.BlockSpec(memory_space=pl.ANY),
        out_shape=jax.ShapeDtypeStruct((n_slots, 1, d), h2.dtype),
        scratch_shapes=[pltpu.SemaphoreType.DMA(())],
        input_output_aliases={3: 0} if aliased else {},
        compiler_params=pltpu.CompilerParams(dimension_semantics=("arbitrary",), has_side_effects=True),
        name="dispatch",
    )(*args)


def _experts_kernel(te_ref, nr_ref, nx_ref, xs_ref, wg_hbm, wu_hbm, wd_hbm, y_ref,
                    wg_st, wu_st, wd_st, wg_sc, wu_sc, wd_sc, sems, n_changes, *, n_exp, tile_rows):
    dh = xs_ref.shape[2]
    tiles_per_step = xs_ref.shape[0] // tile_rows

    def copies(e, slot):
        return (pltpu.make_async_copy(wg_hbm.at[e], wg_st.at[slot], sems.at[slot]),
                pltpu.make_async_copy(wu_hbm.at[e], wu_st.at[slot], sems.at[slot]),
                pltpu.make_async_copy(wd_hbm.at[e], wd_st.at[slot], sems.at[slot]))

    def one_tile(sub):
        tile = pl.program_id(0) * tiles_per_step + sub
        rows = pl.ds(sub * tile_rows, tile_rows)
        expert = te_ref[tile]

        @pl.when(tile == 0)
        def _():
            n_changes[0] = 0
            for cp in copies(expert, 0):
                cp.start(priority=1)

        @pl.when(jnp.logical_or(tile == 0, expert != te_ref[jnp.maximum(tile - 1, 0)]))
        def _():
            slot = lax.rem(n_changes[0], 2)
            n_changes[0] = n_changes[0] + 1
            for cp in copies(expert, slot):
                cp.wait()
            nxt = nx_ref[tile]

            @pl.when(nxt < n_exp)
            def _():
                for cp in copies(nxt, 1 - slot):
                    cp.start(priority=1)

            wg_sc[...] = wg_st[slot].astype(BF16)
            wu_sc[...] = wu_st[slot].astype(BF16)
            wd_sc[...] = wd_st[slot].astype(BF16)

        n_rows = nr_ref[tile]

        @pl.when(n_rows > 0)
        def _():
            live = lax.broadcasted_iota(I32, (tile_rows, 2 * dh), 0) < n_rows
            x = jnp.where(live, _unpack_halves(xs_ref[rows].reshape(tile_rows, dh)), 0.0).astype(BF16)
            gate = jnp.dot(x, wg_sc[...], preferred_element_type=F32)
            up = jnp.dot(x, wu_sc[...], preferred_element_type=F32)
            hid = (jax.nn.silu(gate) * up).astype(BF16)
            y = jnp.dot(hid, wd_sc[...], preferred_element_type=F32)
            y_ref[rows] = _pack_halves(y).reshape(tile_rows, 1, dh)

        @pl.when(n_rows <= 0)
        def _():
            y_ref[rows] = _pack_halves(jnp.zeros((tile_rows, 2 * dh), F32)).reshape(tile_rows, 1, dh)

    for sub in range(tiles_per_step):
        one_tile(sub)


def _experts(xs, te, nr, nx, w_gate, w_up, w_down):
    n_slots, _, dh = xs.shape
    n_exp, d, d_e = w_gate.shape
    t = SLOT_ROWS * EXPERT_STEP_TILES
    est = 2 * 2 * t * d * 2 + 2 * 3 * d * d_e * 4 + 3 * d * d_e * 2 + 4 * SLOT_ROWS * d * 4
    row_spec = pl.BlockSpec((t, 1, dh), lambda i, te, nr, nx: (i, 0, 0))
    return pl.pallas_call(
        functools.partial(_experts_kernel, n_exp=n_exp, tile_rows=SLOT_ROWS),
        grid_spec=pltpu.PrefetchScalarGridSpec(
            num_scalar_prefetch=3,
            grid=(n_slots // t,),
            in_specs=[row_spec, pl.BlockSpec(memory_space=pl.ANY), pl.BlockSpec(memory_space=pl.ANY),
                      pl.BlockSpec(memory_space=pl.ANY)],
            out_specs=row_spec,
            scratch_shapes=[
                pltpu.VMEM((2, d, d_e), F32), pltpu.VMEM((2, d, d_e), F32), pltpu.VMEM((2, d_e, d), F32),
                pltpu.VMEM((d, d_e), BF16), pltpu.VMEM((d, d_e), BF16), pltpu.VMEM((d_e, d), BF16),
                pltpu.SemaphoreType.DMA((2,)), pltpu.SMEM((1,), I32)],
        ),
        out_shape=jax.ShapeDtypeStruct((n_slots, 1, dh), U32),
        compiler_params=pltpu.CompilerParams(
            dimension_semantics=("arbitrary",), vmem_limit_bytes=_vmem_limit(est)),
        name="experts",
    )(te, nr, nx, xs, w_gate, w_up, w_down)


def _combine_kernel(p0_ref, p1_ref, p0n_ref, p1n_ref, xp_ref, rc_ref, g_ref, y_hbm, out_ref, y0_buf, y1_buf, sems,
                    *, n_tiles):
    i = pl.program_id(0)
    rows, d = xp_ref.shape

    def gather(pa_ref, pb_ref, slot):
        base = slot * rows
        def body(t, carry):
            pltpu.make_async_copy(y_hbm.at[pa_ref[t]], y0_buf.at[base + t], sems.at[slot]).start(priority=0)
            pltpu.make_async_copy(y_hbm.at[pb_ref[t]], y1_buf.at[base + t], sems.at[slot]).start(priority=1)
            return carry
        lax.fori_loop(0, rows, body, 0)

    slot = lax.rem(i, 2)

    @pl.when(i == 0)
    def _():
        gather(p0_ref, p1_ref, 0)

    @pl.when(i + 1 < n_tiles)
    def _():
        gather(p0n_ref, p1n_ref, 1 - slot)

    cur = pl.ds(pl.multiple_of(slot * rows, rows), rows)
    for buf in (y0_buf, y1_buf):
        pltpu.make_async_copy(y_hbm.at[pl.ds(0, rows)], buf.at[cur], sems.at[slot]).wait()
    ct = rc_ref[...].T
    y0 = _unpack_halves(y0_buf[cur].reshape(rows, d // 2))
    y1 = _unpack_halves(y1_buf[cur].reshape(rows, d // 2))
    moe = ct[:, 0:1] * y0 + ct[:, 1:2] * y1
    out_ref[...] = _rms_rows(xp_ref[...] + moe, g_ref[...])


def _combine(xp, rc, p0, p1, y, g):
    n, d = xp.shape
    t = min(MOVE_ROWS, n)
    assert n % t == 0
    n_t = n // t
    est = 2 * 2 * t * d * 4 + 2 * 2 * t * d * 2 + 8 * t * d * 4
    nxt = lambda i: (jnp.minimum(i + 1, n_t - 1),)
    return pl.pallas_call(
        functools.partial(_combine_kernel, n_tiles=n_t),
        grid=(n_t,),
        in_specs=[
            pl.BlockSpec((t,), lambda i: (i,), memory_space=pltpu.SMEM),
            pl.BlockSpec((t,), lambda i: (i,), memory_space=pltpu.SMEM),
            pl.BlockSpec((t,), nxt, memory_space=pltpu.SMEM),
            pl.BlockSpec((t,), nxt, memory_space=pltpu.SMEM),
            pl.BlockSpec((t, d), lambda i: (i, 0)),
            pl.BlockSpec((SUBLANES, t), lambda i: (0, i)),
            pl.BlockSpec((1, d), lambda i: (0, 0)),
            pl.BlockSpec(memory_space=pl.ANY),
        ],
        out_specs=pl.BlockSpec((t, d), lambda i: (i, 0)),
        out_shape=jax.ShapeDtypeStruct((n, d), F32),
        scratch_shapes=[pltpu.VMEM((2 * t, 1, d // 2), U32), pltpu.VMEM((2 * t, 1, d // 2), U32),
                        pltpu.SemaphoreType.DMA((2,))],
        compiler_params=pltpu.CompilerParams(
            dimension_semantics=("arbitrary",), vmem_limit_bytes=_vmem_limit(est)),
        name="combine",
    )(p0, p1, p0, p1, xp, rc, g, y)


def _rel_buckets(rel):
    nb = N_BUCKETS // 2
    ret = (rel > 0).astype(I32) * nb
    n = jnp.abs(rel)
    max_exact = nb // 2
    nf = jnp.maximum(n, 1).astype(F32)
    large = max_exact + (jnp.log(nf / max_exact) / math.log(MAX_DISTANCE / max_exact)
                         * (nb - max_exact)).astype(I32)
    large = jnp.minimum(large, nb - 1)
    return ret + jnp.where(n < max_exact, n, large)


def _rel_bias(table, window, n_kv):
    n_heads = table.shape[1]
    span = window + CHUNK
    rows = table[_rel_buckets(jnp.arange(-(span - 1), CHUNK, dtype=I32))].astype(F32)
    bias = jnp.stack([rows[CHUNK - 1 - q:CHUNK - 1 - q + span] for q in range(CHUNK)])
    bias = jnp.transpose(bias, (2, 0, 1))
    bias = jnp.pad(bias, ((0, 0), (0, 0), (0, KEY_PAD - bias.shape[2])), constant_values=NEG_INF)
    bias = bias.reshape(n_kv, n_heads // n_kv // 2, 2, CHUNK, KEY_PAD)
    return jnp.transpose(bias, (0, 1, 3, 2, 4)).reshape(n_kv, (n_heads // n_kv // 2) * CHUNK, 2 * KEY_PAD)


def kernel(x_prompt, x_sample, cache_conv, cache_k, cache_v, rel_bias_table, norm_mix_g, w_in, conv_w, w_conv_out, attn_sinks, w_attn_out, w_o, norm_ffn_g, w_group, b_group, w_expert_router, b_expert_router, w_gate, w_up, w_down, final_norm_g):
    assert w_in.shape[0] == 1, "single-layer step"
    batch, seq, d = x_prompt.shape
    dec_batch, dec_seq, _ = x_sample.shape
    assert dec_seq == CHUNK and seq % MIX_ROWS == 0 and (dec_batch * dec_seq) % MIX_ROWS == 0
    d_conv = conv_w.shape[-1]
    window, n_kv, head_dim = cache_k.shape[2], cache_k.shape[3], cache_k.shape[4]
    n_heads = attn_sinks.shape[-1]
    d_attn, d_kv = n_heads * head_dim, n_kv * head_dim
    n_groups, n_exp = w_group.shape[-1], w_expert_router.shape[-1]
    assert n_groups <= GROUP_ROW0 and d_conv == d_attn and 2 * d_conv == d
    assert n_kv == 2 and d_kv == LANES and (n_heads // n_kv) % 2 == 0 and window + CHUNK <= KEY_PAD
    dims = (n_heads, n_kv, head_dim, window, n_groups, n_exp, d_conv)

    w_all = w_in[0].astype(BF16)
    kv0 = 3 * d_conv + d_attn
    g1 = norm_mix_g[0][None, :]
    wr = jnp.zeros((GROUP_ROW0 + n_exp, d), F32)
    wr = wr.at[:n_groups].set(w_group[0].T).at[GROUP_ROW0:].set(w_expert_router[0].T).astype(BF16)
    br = jnp.zeros((GROUP_ROW0 + n_exp, 1), F32)
    br = br.at[:n_groups, 0].set(b_group[0]).at[GROUP_ROW0:, 0].set(b_expert_router[0])
    weights = (conv_w[0], w_conv_out[0].astype(BF16), w_attn_out[0].astype(BF16), w_o[0].astype(BF16),
               _rel_bias(rel_bias_table, window, n_kv), attn_sinks, norm_ffn_g[0][None, :], wr, br)

    xp2d = x_prompt.reshape(batch * seq, d)
    xs2d = x_sample.reshape(dec_batch * dec_seq, d)
    n_p, n_s = xp2d.shape[0], xs2d.shape[0]
    n_tok = n_p + n_s

    proj_p, kv_p = _inproj(xp2d, g1, w_all, kv0, 2 * d_kv)
    proj_s, kv_s = _inproj(xs2d, g1, w_all, kv0, 2 * d_kv)
    xres_p, h2_p, re_p, rc_p, conv_p, k_p, v_p = _mix(
        xp2d, proj_p, kv_p, None, weights, n_seq=batch, sample=False, dims=dims)
    caches = (cache_conv[0], cache_k[0].reshape(dec_batch, window, d_kv), cache_v[0].reshape(dec_batch, window, d_kv))
    xres_s, h2_s, re_s, rc_s, conv_s, k_s, v_s = _mix(
        xs2d, proj_s, kv_s, caches, weights, n_seq=dec_batch, sample=True, dims=dims)

    n_tiles = pl.cdiv((TOP_K * n_tok) // SLOT_ROWS + n_exp, EXPERT_STEP_TILES) * EXPERT_STEP_TILES
    n_slots = n_tiles * SLOT_ROWS
    pos, tmap = _route(jnp.concatenate([re_p, re_s], axis=1), n_exp, n_tiles)
    p0, p1 = pos[0], pos[1]
    xs = _dispatch(h2_p, p0[:n_p], p1[:n_p], None, n_slots)
    xs = _dispatch(h2_s, p0[n_p:], p1[n_p:], xs, n_slots)
    y = _experts(xs, tmap[0, :n_tiles], tmap[1, :n_tiles], tmap[2, :n_tiles], w_gate[0], w_up[0], w_down[0])
    gf = final_norm_g[None, :]
    y_prompt = _combine(xres_p, rc_p, p0[:n_p], p1[:n_p], y, gf).reshape(batch, seq, d)
    y_sample = _combine(xres_s, rc_s, p0[n_p:], p1[n_p:], y, gf).reshape(dec_batch, dec_seq, d)

    kv_shape = (1, -1, window, n_kv, head_dim)
    return (y_prompt, y_sample, conv_p[None], k_p.reshape(kv_shape), v_p.reshape(kv_shape),
            conv_s[None], k_s.reshape(kv_shape), v_s.reshape(kv_shape))
```

```python
import functools
import math

import jax
import jax.numpy as jnp
from jax import lax
from jax.experimental import pallas as pl
from jax.experimental.pallas import tpu as pltpu

F32, BF16, I32, U32 = jnp.float32, jnp.bfloat16, jnp.int32, jnp.uint32

CHUNK = 64
N_BUCKETS = 32
MAX_DISTANCE = 128
EPS = 1e-6
NEG_INF = -1e30
TOP_K = 2

V7X_VMEM_BYTES = 64 * 1024 * 1024
SUBLANES = 8
LANES = 128

INPROJ_ROWS = 1024
INPROJ_COLS = 2048
MIX_ROWS = 256
MOVE_ROWS = 512
SLOT_ROWS = 256
EXPERT_STEP_TILES = 4
ROUTE_BLOCK = 512
GROUP_ROW0 = 8
KEY_PAD = 256
KV_VARIANTS = 4


def _vmem_limit(nbytes):
    return int(min(V7X_VMEM_BYTES - (4 << 20), max(nbytes, 32 << 20)))


def _pack_halves(x):
    half = x.shape[1] // 2
    return pltpu.pack_elementwise([x[:, :half], x[:, half:]], packed_dtype=BF16)


def _unpack_halves(w):
    lo = pltpu.unpack_elementwise(w, index=0, packed_dtype=BF16, unpacked_dtype=F32)
    hi = pltpu.unpack_elementwise(w, index=1, packed_dtype=BF16, unpacked_dtype=F32)
    return jnp.concatenate([lo, hi], axis=1)


def _rms_rows(x, g):
    r = lax.rsqrt(jnp.mean(x * x, axis=-1, keepdims=True) + EPS)
    return (x * r) * g


def _inproj_kernel(*refs, cast_steps, n_col_steps):
    n_cast = len(cast_steps)
    x_ref, g_ref, wm_ref, wkv_ref = refs[:4]
    cast_in = refs[4:4 + n_cast]
    proj_ref, kv_ref = refs[4 + n_cast:6 + n_cast]
    cast_out = refs[6 + n_cast:6 + 2 * n_cast]
    h_sc = refs[-1]

    step = pl.program_id(0) * n_col_steps + pl.program_id(1)
    for (s0, s1), src, dst in zip(cast_steps, cast_in, cast_out):
        @pl.when(jnp.logical_and(step >= s0, step < s1))
        def _(src=src, dst=dst):
            dst[...] = src[...].astype(BF16)

    @pl.when(pl.program_id(1) == 0)
    def _():
        rows = 128
        for i in range(x_ref.shape[0] // rows):
            r = slice(i * rows, (i + 1) * rows)
            h = _rms_rows(x_ref[r, :], g_ref[...]).astype(BF16)
            h_sc[r, :] = h
            kv_ref[r, :] = jnp.dot(h, wkv_ref[...], preferred_element_type=F32)

    proj_ref[...] = jnp.dot(h_sc[...], wm_ref[...], preferred_element_type=F32).astype(BF16)


def _inproj(x2d, g, w_all, kv0, n_kv, side_casts=()):
    n, d = x2d.shape
    tm = min(INPROJ_ROWS, n)
    tn = INPROJ_COLS
    n_main = w_all.shape[1] - n_kv
    assert kv0 % tn == 0 and n_main % tn == 0 and n_kv % LANES == 0
    grid = (n // tm, n_main // tn)
    n_steps = grid[0] * grid[1]
    cast_rows = sum(w.shape[0] for w in side_casts) // n_steps if side_casts else 0
    cast_steps, cast_specs, step0 = [], [], 0
    for w in side_casts:
        assert cast_rows % (2 * SUBLANES) == 0 and w.shape[0] % cast_rows == 0
        n_blk = w.shape[0] // cast_rows
        cast_steps.append((step0, step0 + n_blk))
        cast_specs.append(pl.BlockSpec(
            (cast_rows, w.shape[1]),
            lambda i, j, s0=step0, nb=n_blk: (jnp.clip(i * grid[1] + j - s0, 0, nb - 1), 0)))
        step0 += n_blk
    assert step0 == (n_steps if side_casts else 0)
    est = (2 * tm * d * 4 + tm * d * 2 + 2 * d * tn * 2 + 2 * d * n_kv * 2 + 2 * tm * tn * 2 + 2 * tm * n_kv * 4
           + tm * tn * 4 + sum(2 * cast_rows * w.shape[1] * 6 for w in side_casts))
    outs = pl.pallas_call(
        functools.partial(_inproj_kernel, cast_steps=tuple(cast_steps), n_col_steps=grid[1]),
        grid=grid,
        in_specs=[
            pl.BlockSpec((tm, d), lambda i, j: (i, 0)),
            pl.BlockSpec((1, d), lambda i, j: (0, 0)),
            pl.BlockSpec((pl.Element(d), pl.Element(tn)),
                         lambda i, j: (0, pl.multiple_of(jnp.where(j * tn < kv0, j * tn, j * tn + n_kv), LANES))),
            pl.BlockSpec((pl.Element(d), pl.Element(n_kv)), lambda i, j: (0, kv0)),
        ] + cast_specs,
        out_specs=[
            pl.BlockSpec((tm, tn), lambda i, j: (i, j)),
            pl.BlockSpec((tm, n_kv), lambda i, j: (i, 0)),
        ] + cast_specs,
        out_shape=[jax.ShapeDtypeStruct((n, n_main), BF16), jax.ShapeDtypeStruct((n, n_kv), F32)]
        + [jax.ShapeDtypeStruct(w.shape, BF16) for w in side_casts],
        scratch_shapes=[pltpu.VMEM((tm, d), BF16)],
        compiler_params=pltpu.CompilerParams(
            dimension_semantics=("arbitrary", "arbitrary"), vmem_limit_bytes=_vmem_limit(est + (8 << 20))),
        name="inproj",
    )(x2d, g, w_all, w_all, *side_casts)
    return outs[0], outs[1], tuple(outs[2:])


def _conv_rows(u, prev2, prev1, w):
    row = lax.broadcasted_iota(I32, u.shape, 0)
    u1 = jnp.where(row == 0, prev1, pltpu.roll(u, 1, axis=0))
    u2 = jnp.where(row == 0, prev2, jnp.where(row == 1, prev1, pltpu.roll(u, 2, axis=0)))
    return (w[0:1] * u2 + w[1:2] * u1) + w[2:3] * u


def _mix_kernel(*refs, sample, n_heads, n_kv, head_dim, window, n_groups, n_exp):
    if sample:
        (x_ref, ga_ref, gb_ref, b_ref, c_ref, xc_ref, q_ref, kv_ref, cconv_ref, ck_ref, cv_ref,
         convw_ref, wco_ref, wao_ref, wo_ref, bias_ref, sinks_ref, g2_ref, wr_ref, br_ref,
         xp_ref, h2_ref, re_ref, rc_ref, sconv_ref, sk_ref, sv_ref,
         kvar, o_sc, ya_sc, carry_u, s_sc, e_sc, sink_sc) = refs
    else:
        (x_ref, ga_ref, gb_ref, b_ref, c_ref, xc_ref, q_ref, kv_ref,
         convw_ref, wco_ref, wao_ref, wo_ref, bias_ref, sinks_ref, g2_ref, wr_ref, br_ref,
         xp_ref, h2_ref, re_ref, rc_ref, sconv_ref, sk_ref, sv_ref,
         kvar, o_sc, ya_sc, carry_u, s_sc, e_sc, sink_sc) = refs
    t_rows, d_model = x_ref.shape
    n_chunks = t_rows // CHUNK
    span = window + CHUNK
    q_per_kv = n_heads // n_kv
    d_kv = n_kv * head_dim
    scale = 1.0 / math.sqrt(head_dim)
    scale_is_pow2 = math.frexp(scale)[0] == 0.5
    epg = n_exp // n_groups
    seq_start = pl.program_id(1) == 0

    def conv_branch():
        w_conv = convw_ref[...]
        if sample:
            for s in range(n_chunks):
                rows = slice(s * CHUNK, (s + 1) * CHUNK)
                u = c_ref[rows, :].astype(F32) * xc_ref[rows, :].astype(F32)
                y = _conv_rows(u, cconv_ref[s, 0:1, :], cconv_ref[s, 1:2, :], w_conv)
                ya_sc[rows, :] = (b_ref[rows, :].astype(F32) * y).astype(BF16)
                sconv_ref[s] = u[CHUNK - 2:CHUNK, :]
        else:
            u = c_ref[...].astype(F32) * xc_ref[...].astype(F32)
            prev = jnp.where(seq_start, 0.0, carry_u[...])
            y = _conv_rows(u, prev[SUBLANES - 2:SUBLANES - 1], prev[SUBLANES - 1:SUBLANES], w_conv)
            ya_sc[...] = (b_ref[...].astype(F32) * y).astype(BF16)
            carry_u[...] = u[t_rows - SUBLANES:t_rows, :]
            sconv_ref[0] = u[t_rows - 2:t_rows, :]

    def store_kv(row0, k_rows, v_rows):
        n_rows = k_rows.shape[0]
        low = lax.broadcasted_iota(I32, k_rows.shape, 1) < head_dim
        for base, a in ((0, k_rows), (KV_VARIANTS, v_rows)):
            b = pltpu.roll(a, head_dim, axis=1)
            kvar[base + 0, row0:row0 + n_rows, :] = jnp.where(low, a, 0.0).astype(BF16)
            kvar[base + 1, row0:row0 + n_rows, :] = jnp.where(low, 0.0, a).astype(BF16)
            kvar[base + 2, row0:row0 + n_rows, :] = jnp.where(low, b, 0.0).astype(BF16)
            kvar[base + 3, row0:row0 + n_rows, :] = jnp.where(low, 0.0, b).astype(BF16)

    if sample:
        for s in range(n_chunks):
            rows = slice(s * CHUNK, (s + 1) * CHUNK)
            store_kv(s * span, ck_ref[s], cv_ref[s])
            store_kv(s * span + window, kv_ref[rows, 0:d_kv], kv_ref[rows, d_kv:2 * d_kv])
            sk_ref[s, 0:window - CHUNK, :] = ck_ref[s, CHUNK:window, :]
            sv_ref[s, 0:window - CHUNK, :] = cv_ref[s, CHUNK:window, :]
            sk_ref[s, window - CHUNK:window, :] = kv_ref[rows, 0:d_kv]
            sv_ref[s, window - CHUNK:window, :] = kv_ref[rows, d_kv:2 * d_kv]
        k_stride = span
    else:
        @pl.when(seq_start)
        def _():
            kvar[:, 0:window, :] = jnp.zeros((2 * KV_VARIANTS, window, d_kv), BF16)
        store_kv(window, kv_ref[:, 0:d_kv], kv_ref[:, d_kv:2 * d_kv])
        sk_ref[0] = kv_ref[t_rows - window:t_rows, 0:d_kv]
        sv_ref[0] = kv_ref[t_rows - window:t_rows, d_kv:2 * d_kv]
        k_stride = CHUNK

    n_pairs = q_per_kv // 2
    key_pad_rows = jnp.zeros((KEY_PAD - span, d_kv), BF16)
    ones_d = ((lax.broadcasted_iota(I32, (2 * KEY_PAD, LANES), 0) < KEY_PAD)
              == (lax.broadcasted_iota(I32, (2 * KEY_PAD, LANES), 1) < head_dim)).astype(BF16)
    low_half = lax.broadcasted_iota(I32, (CHUNK, LANES), 1) < head_dim
    ya_parts = []
    ya_cols = d_model // (n_chunks * n_kv)
    blocks = [(c, n) for c in range(n_chunks) for n in range(n_kv)]

    def key_rows(c):
        return slice(c * k_stride, c * k_stride + span)

    for b, (c, n) in enumerate(blocks):
        q_rows = slice(c * CHUNK, (c + 1) * CHUNK)
        top, bot = (0, 3) if n == 0 else (2, 1)
        kd = jnp.concatenate([kvar[top, key_rows(c), :], key_pad_rows, kvar[bot, key_rows(c), :], key_pad_rows],
                             axis=0)
        q4 = jnp.concatenate(
            [q_ref[q_rows, (n * n_pairs + j) * LANES:(n * n_pairs + j + 1) * LANES] for j in range(n_pairs)], axis=0)
        if scale_is_pow2:
            q4 = q4 * scale
        s4 = lax.dot_general(q4, kd, (((1,), (1,)), ((), ())), preferred_element_type=F32)
        s_sc[b] = s4 if scale_is_pow2 else s4 * scale
    conv_branch()
    for b, (c, n) in enumerate(blocks):
        masked = (not sample) and c * CHUNK < window
        if masked:
            first_key = (pl.program_id(1) * n_chunks + c) * CHUNK - window
            valid = lax.broadcasted_iota(I32, (CHUNK, KEY_PAD), 1) + first_key >= 0
        for j in range(n_pairs):
            sink_halves = []
            for half in range(2):
                blk = (slice(j * CHUNK, (j + 1) * CHUNK), slice(half * KEY_PAD, (half + 1) * KEY_PAD))
                s = s_sc[b, blk[0], blk[1]] + bias_ref[n, blk[0], blk[1]]
                if masked:
                    s = jnp.where(valid, s, NEG_INF)
                sink = sinks_ref[0, n * q_per_kv + 2 * j + half]
                mx = jnp.maximum(jnp.max(s, axis=-1, keepdims=True), sink)
                e_sc[b, blk[0], blk[1]] = jnp.exp(s - mx).astype(BF16)
                sink_halves.append(jnp.exp(sink - mx))
            sink_sc[b, j * CHUNK:(j + 1) * CHUNK, :] = jnp.where(low_half, sink_halves[0], sink_halves[1])
        ya_parts.append(jnp.dot(ya_sc[...], wco_ref[:, b * ya_cols:(b + 1) * ya_cols], preferred_element_type=F32))
    for b, (c, n) in enumerate(blocks):
        q_rows = slice(c * CHUNK, (c + 1) * CHUNK)
        top, bot = (0, 3) if n == 0 else (2, 1)
        vd = jnp.concatenate([kvar[KV_VARIANTS + top, key_rows(c), :], key_pad_rows,
                              kvar[KV_VARIANTS + bot, key_rows(c), :], key_pad_rows], axis=0)
        od = jnp.dot(e_sc[b], jnp.concatenate([vd, ones_d], axis=1), preferred_element_type=F32)
        o4 = od[:, 0:LANES] / (od[:, LANES:2 * LANES] + sink_sc[b])
        for j in range(n_pairs):
            o_sc[q_rows, (n * n_pairs + j) * LANES:(n * n_pairs + j + 1) * LANES] = (
                o4[j * CHUNK:(j + 1) * CHUNK].astype(BF16))
    if not sample:
        kvar[:, 0:window, :] = kvar[:, t_rows:t_rows + window, :]
    y_a = jnp.concatenate(ya_parts, axis=1)
    y_b = jnp.dot(o_sc[...], wao_ref[...], preferred_element_type=F32)

    m = jax.nn.sigmoid(ga_ref[...].astype(F32)) * y_a + jax.nn.sigmoid(gb_ref[...].astype(F32)) * y_b
    xp = x_ref[...] + jnp.dot(m.astype(BF16), wo_ref[...], preferred_element_type=F32)
    xp_ref[...] = xp

    h2 = _rms_rows(xp, g2_ref[...])
    h2_ref[...] = _pack_halves(h2).reshape(t_rows, 1, d_model // 2)
    lt = lax.dot_general(wr_ref[...], h2.astype(BF16), (((1,), (1,)), ((), ())),
                         preferred_element_type=F32) + br_ref[...]
    lg = lt[0:n_groups]
    eg = jnp.exp(lg - jnp.max(lg, axis=0, keepdims=True))
    gp = eg / jnp.sum(eg, axis=0, keepdims=True)
    gw = jnp.max(gp, axis=0, keepdims=True)
    gi = lax.broadcasted_iota(I32, gp.shape, 0).astype(F32)
    gsel = jnp.min(jnp.where(gp == gw, gi, float(n_groups)), axis=0, keepdims=True)
    el = jnp.zeros((epg, t_rows), F32)
    for g in range(n_groups):
        el = jnp.where(gsel == float(g), lt[GROUP_ROW0 + g * epg:GROUP_ROW0 + (g + 1) * epg], el)
    ei = lax.broadcasted_iota(I32, el.shape, 0).astype(F32)
    v1 = jnp.max(el, axis=0, keepdims=True)
    i1 = jnp.min(jnp.where(el == v1, ei, float(epg)), axis=0, keepdims=True)
    el2 = jnp.where(ei == i1, -jnp.inf, el)
    v2 = jnp.max(el2, axis=0, keepdims=True)
    i2 = jnp.min(jnp.where(el2 == v2, ei, float(epg)), axis=0, keepdims=True)
    a1 = jnp.exp(v1 - v1)
    a2 = jnp.exp(v2 - v1)
    den = a1 + a2
    c1 = gw * (a1 / den)
    c2 = gw * (a2 / den)
    e1 = (gsel * float(epg) + i1).astype(I32)
    e2 = (gsel * float(epg) + i2).astype(I32)
    row8 = lax.broadcasted_iota(I32, (SUBLANES, t_rows), 0)
    re_ref[...] = jnp.where(row8 == 0, e1, jnp.where(row8 == 1, e2, 0))
    rc_ref[...] = jnp.where(row8 == 0, c1, jnp.where(row8 == 1, c2, 0.0))


def _mix(x2d, proj, kv, caches, weights, *, n_seq, sample, dims):
    n_heads, n_kv, head_dim, window, n_groups, n_exp, d_conv = dims
    n, d = x2d.shape
    t = MIX_ROWS
    n_chunks = t // CHUNK
    span = window + CHUNK
    d_attn = n_heads * head_dim
    d_kv = n_kv * head_dim
    convw, wco, wao, wo, bias, sinks, g2, wr, br = weights
    if sample:
        n_t = n // t
        grid = (n_t, 1)
        tok = lambda i, j: (i, 0)
        n_state = n // CHUNK
        state_blk = n_chunks
        st = lambda i, j: (i, 0, 0)
    else:
        n_t = (n // n_seq) // t
        grid = (n_seq, n_t)
        tok = lambda i, j: (i * n_t + j, 0)
        n_state = n_seq
        state_blk = 1
        st = lambda i, j: (i, 0, 0)
    const2 = lambda i, j: (0, 0)

    def col(width, idx):
        return pl.BlockSpec((t, width), lambda i, j: (tok(i, j)[0], idx))

    def resident(shape):
        return pl.BlockSpec(shape, const2, pipeline_mode=pl.Buffered(1))

    in_specs = [
        pl.BlockSpec((t, d), tok),
        col(d, 2), col(d, 3),
        col(d_conv, 0), col(d_conv, 1), col(d_conv, 2), col(d_attn, 3),
        pl.BlockSpec((t, 2 * d_kv), tok),
    ]
    args = [x2d, proj, proj, proj, proj, proj, proj, kv]
    if sample:
        cconv, ck, cv = caches
        in_specs += [
            pl.BlockSpec((n_chunks, cconv.shape[1], d_conv), st),
            pl.BlockSpec((n_chunks, window, d_kv), st),
            pl.BlockSpec((n_chunks, window, d_kv), st),
        ]
        args += [cconv, ck, cv]
    in_specs += [
        resident(convw.shape), resident(wco.shape), resident(wao.shape), resident(wo.shape),
        pl.BlockSpec(bias.shape, lambda i, j: (0, 0, 0), pipeline_mode=pl.Buffered(1)),
        pl.BlockSpec(memory_space=pltpu.SMEM),
        resident(g2.shape), resident(wr.shape), resident(br.shape),
    ]
    args += [convw, wco, wao, wo, bias, sinks, g2, wr, br]
    out_specs = [
        pl.BlockSpec((t, d), tok),
        pl.BlockSpec((t, 1, d // 2), lambda i, j: (tok(i, j)[0], 0, 0)),
        pl.BlockSpec((SUBLANES, t), lambda i, j: (0, tok(i, j)[0])),
        pl.BlockSpec((SUBLANES, t), lambda i, j: (0, tok(i, j)[0])),
        pl.BlockSpec((state_blk, 2, d_conv), st),
        pl.BlockSpec((state_blk, window, d_kv), st),
        pl.BlockSpec((state_blk, window, d_kv), st),
    ]
    out_shape = [
        jax.ShapeDtypeStruct((n, d), F32),
        jax.ShapeDtypeStruct((n, 1, d // 2), U32),
        jax.ShapeDtypeStruct((SUBLANES, n), I32),
        jax.ShapeDtypeStruct((SUBLANES, n), F32),
        jax.ShapeDtypeStruct((n_state, 2, d_conv), F32),
        jax.ShapeDtypeStruct((n_state, window, d_kv), F32),
        jax.ShapeDtypeStruct((n_state, window, d_kv), F32),
    ]
    kv_rows = n_chunks * span if sample else window + t
    scratch = [
        pltpu.VMEM((2 * KV_VARIANTS, kv_rows, d_kv), BF16),
        pltpu.VMEM((t, d_attn), BF16), pltpu.VMEM((t, d_conv), BF16),
        pltpu.VMEM((SUBLANES, d_conv), F32),
        pltpu.VMEM((n_chunks * n_kv, (n_heads // n_kv // 2) * CHUNK, 2 * KEY_PAD), F32),
        pltpu.VMEM((n_chunks * n_kv, (n_heads // n_kv // 2) * CHUNK, 2 * KEY_PAD), BF16),
        pltpu.VMEM((n_chunks * n_kv, (n_heads // n_kv // 2) * CHUNK, LANES), F32),
    ]
    est = (2 * t * d * 4 * 3 + 2 * 2 * t * d * 2 + 2 * 4 * t * d_conv * 2
           + (2 * d_conv * d + d * d) * 2 + 12 * t * d * 4)
    kern = functools.partial(_mix_kernel, sample=sample, n_heads=n_heads, n_kv=n_kv, head_dim=head_dim,
                             window=window, n_groups=n_groups, n_exp=n_exp)
    return pl.pallas_call(
        kern, grid=grid, in_specs=in_specs, out_specs=out_specs, out_shape=out_shape,
        scratch_shapes=scratch,
        compiler_params=pltpu.CompilerParams(
            dimension_semantics=("arbitrary", "arbitrary"), vmem_limit_bytes=_vmem_limit(est)),
        name="mix_sample" if sample else "mix_prompt",
    )(*args)


def _route_kernel(re_ref, pos_ref, tmap_ref, *, n_exp, tile_rows, blk):
    n = re_ref.shape[1]
    n_blk = n // blk
    erow = lax.broadcasted_iota(I32, (n_exp, blk), 0)

    def onehots(j):
        c0 = pl.multiple_of(j * blk, blk)
        oh0 = (erow == re_ref[0:1, pl.ds(c0, blk)]).astype(F32)
        oh1 = (erow == re_ref[1:2, pl.ds(c0, blk)]).astype(F32)
        return c0, oh0, oh1

    def count_body(j, cnt):
        _, oh0, oh1 = onehots(j)
        return cnt + jnp.sum(oh0 + oh1, axis=1, keepdims=True)

    cnt = lax.fori_loop(0, n_blk, count_body, jnp.zeros((n_exp, 1), F32))
    cnt = jnp.broadcast_to(cnt, (n_exp, LANES))
    padded = jnp.ceil(cnt / tile_rows) * tile_rows
    ends = padded
    prow = lax.broadcasted_iota(I32, ends.shape, 0)
    step = 1
    while step < n_exp:
        ends = ends + jnp.where(prow >= step, pltpu.roll(ends, step, axis=0), 0.0)
        step *= 2
    offs = ends - padded
    off1 = offs[:, 0:1]

    tri = (lax.broadcasted_iota(I32, (blk, blk), 0) <= lax.broadcasted_iota(I32, (blk, blk), 1)).astype(BF16)
    row8 = lax.broadcasted_iota(I32, (SUBLANES, blk), 0)

    def pos_body(j, run):
        c0, oh0, oh1 = onehots(j)
        both = oh0 + oh1
        csum = jnp.dot(both.astype(BF16), tri, preferred_element_type=F32) + run
        slot = off1 + csum - 1.0
        p0 = jnp.sum(oh0 * slot, axis=0, keepdims=True).astype(I32)
        p1 = jnp.sum(oh1 * slot, axis=0, keepdims=True).astype(I32)
        pos_ref[:, pl.ds(c0, blk)] = jnp.where(row8 == 0, p0, jnp.where(row8 == 1, p1, 0))
        return run + jnp.sum(both, axis=1, keepdims=True)

    lax.fori_loop(0, n_blk, pos_body, jnp.zeros((n_exp, 1), F32))

    n_tiles_pad = tmap_ref.shape[1]
    start = (lax.broadcasted_iota(I32, (1, n_tiles_pad), 1) * tile_rows).astype(F32)
    end1 = ends[:, 0:1]
    te = jnp.sum((end1 <= start).astype(F32), axis=0, keepdims=True)
    trow = lax.broadcasted_iota(I32, (n_exp, n_tiles_pad), 0).astype(F32)
    used_end = jnp.sum(jnp.where(trow == te, off1 + cnt[:, 0:1], 0.0), axis=0, keepdims=True)
    n_rows = jnp.clip(used_end - start, 0.0, float(tile_rows))
    has_tokens = cnt[:, 0:1] > 0.0
    te = jnp.minimum(te, jnp.max(jnp.where(has_tokens, trow, 0.0), axis=0, keepdims=True))
    nxt = jnp.min(jnp.where(trow > te, jnp.where(has_tokens, trow, float(n_exp)), float(n_exp)), axis=0, keepdims=True)
    r8 = lax.broadcasted_iota(I32, (SUBLANES, n_tiles_pad), 0)
    tmap_ref[...] = jnp.where(r8 == 0, te.astype(I32),
                              jnp.where(r8 == 1, n_rows.astype(I32), jnp.where(r8 == 2, nxt.astype(I32), 0)))


def _route(re, n_exp, n_tiles):
    n = re.shape[1]
    assert n % ROUTE_BLOCK == 0
    n_tiles_pad = pl.cdiv(n_tiles, LANES) * LANES
    kern = functools.partial(_route_kernel, n_exp=n_exp, tile_rows=SLOT_ROWS, blk=ROUTE_BLOCK)
    return pl.pallas_call(
        kern,
        out_shape=[jax.ShapeDtypeStruct((SUBLANES, n), I32), jax.ShapeDtypeStruct((SUBLANES, n_tiles_pad), I32)],
        name="route",
    )(re)


def _dispatch_kernel(*refs, aliased):
    if aliased:
        p0_ref, p1_ref, h_ref, _, xs_ref, sem = refs
    else:
        p0_ref, p1_ref, h_ref, xs_ref, sem = refs
    rows = h_ref.shape[0]

    def start(t, carry):
        pltpu.make_async_copy(h_ref.at[t], xs_ref.at[p0_ref[t]], sem).start(priority=0)
        pltpu.make_async_copy(h_ref.at[t], xs_ref.at[p1_ref[t]], sem).start(priority=1)
        return carry

    lax.fori_loop(0, rows, start, 0)
    for _ in range(TOP_K):
        pltpu.make_async_copy(h_ref, xs_ref.at[pl.ds(0, rows)], sem).wait()


def _dispatch(h2, p0, p1, xs, n_slots):
    n, _, d = h2.shape
    t = min(MOVE_ROWS, n)
    assert n % t == 0
    aliased = xs is not None
    in_specs = [
        pl.BlockSpec((t,), lambda i: (i,), memory_space=pltpu.SMEM),
        pl.BlockSpec((t,), lambda i: (i,), memory_space=pltpu.SMEM),
        pl.BlockSpec((t, 1, d), lambda i: (i, 0, 0)),
    ]
    args = [p0, p1, h2]
    if aliased:
        in_specs.append(pl.BlockSpec(memory_space=pl.ANY))
        args.append(xs)
    return pl.pallas_call(
        functools.partial(_dispatch_kernel, aliased=aliased),
        grid=(n // t,),
        in_specs=in_specs,
        out_specs=pl.BlockSpec(memory_space=pl.ANY),
        out_shape=jax.ShapeDtypeStruct((n_slots, 1, d), h2.dtype),
        scratch_shapes=[pltpu.SemaphoreType.DMA(())],
        input_output_aliases={3: 0} if aliased else {},
        compiler_params=pltpu.CompilerParams(dimension_semantics=("arbitrary",), has_side_effects=True),
        name="dispatch",
    )(*args)


def _experts_kernel(te_ref, nr_ref, nx_ref, xs_ref, wg_hbm, wu_hbm, wd_hbm, y_ref,
                    wg_st, wu_st, wd_st, wg_sc, wu_sc, wd_sc, sems, n_changes, *, n_exp, tile_rows):
    dh = xs_ref.shape[2]
    tiles_per_step = xs_ref.shape[0] // tile_rows

    def copies(e, slot):
        return (pltpu.make_async_copy(wg_hbm.at[e], wg_st.at[slot], sems.at[slot]),
                pltpu.make_async_copy(wu_hbm.at[e], wu_st.at[slot], sems.at[slot]),
                pltpu.make_async_copy(wd_hbm.at[e], wd_st.at[slot], sems.at[slot]))

    def one_tile(sub):
        tile = pl.program_id(0) * tiles_per_step + sub
        rows = pl.ds(sub * tile_rows, tile_rows)
        expert = te_ref[tile]

        @pl.when(tile == 0)
        def _():
            n_changes[0] = 0
            for cp in copies(expert, 0):
                cp.start(priority=1)

        @pl.when(jnp.logical_or(tile == 0, expert != te_ref[jnp.maximum(tile - 1, 0)]))
        def _():
            slot = lax.rem(n_changes[0], 2)
            n_changes[0] = n_changes[0] + 1
            for cp in copies(expert, slot):
                cp.wait()
            nxt = nx_ref[tile]

            @pl.when(nxt < n_exp)
            def _():
                for cp in copies(nxt, 1 - slot):
                    cp.start(priority=1)

            wg_sc[...] = wg_st[slot].astype(BF16)
            wu_sc[...] = wu_st[slot].astype(BF16)
            wd_sc[...] = wd_st[slot].astype(BF16)

        n_rows = nr_ref[tile]

        @pl.when(n_rows > 0)
        def _():
            live = lax.broadcasted_iota(I32, (tile_rows, 2 * dh), 0) < n_rows
            x = jnp.where(live, _unpack_halves(xs_ref[rows].reshape(tile_rows, dh)), 0.0).astype(BF16)
            gate = jnp.dot(x, wg_sc[...], preferred_element_type=F32)
            up = jnp.dot(x, wu_sc[...], preferred_element_type=F32)
            hid = (jax.nn.silu(gate) * up).astype(BF16)
            y = jnp.dot(hid, wd_sc[...], preferred_element_type=F32)
            y_ref[rows] = _pack_halves(y).reshape(tile_rows, 1, dh)

        @pl.when(n_rows <= 0)
        def _():
            y_ref[rows] = _pack_halves(jnp.zeros((tile_rows, 2 * dh), F32)).reshape(tile_rows, 1, dh)

    for sub in range(tiles_per_step):
        one_tile(sub)


def _experts(xs, te, nr, nx, w_gate, w_up, w_down):
    n_slots, _, dh = xs.shape
    n_exp, d, d_e = w_gate.shape
    t = SLOT_ROWS * EXPERT_STEP_TILES
    est = 2 * 2 * t * d * 2 + 2 * 3 * d * d_e * 4 + 3 * d * d_e * 2 + 4 * SLOT_ROWS * d * 4
    row_spec = pl.BlockSpec((t, 1, dh), lambda i, te, nr, nx: (i, 0, 0))
    return pl.pallas_call(
        functools.partial(_experts_kernel, n_exp=n_exp, tile_rows=SLOT_ROWS),
        grid_spec=pltpu.PrefetchScalarGridSpec(
            num_scalar_prefetch=3,
            grid=(n_slots // t,),
            in_specs=[row_spec, pl.BlockSpec(memory_space=pl.ANY), pl.BlockSpec(memory_space=pl.ANY),
                      pl.BlockSpec(memory_space=pl.ANY)],
            out_specs=row_spec,
            scratch_shapes=[
                pltpu.VMEM((2, d, d_e), F32), pltpu.VMEM((2, d, d_e), F32), pltpu.VMEM((2, d_e, d), F32),
                pltpu.VMEM((d, d_e), BF16), pltpu.VMEM((d, d_e), BF16), pltpu.VMEM((d_e, d), BF16),
                pltpu.SemaphoreType.DMA((2,)), pltpu.SMEM((1,), I32)],
        ),
        out_shape=jax.ShapeDtypeStruct((n_slots, 1, dh), U32),
        compiler_params=pltpu.CompilerParams(
            dimension_semantics=("arbitrary",), vmem_limit_bytes=_vmem_limit(est)),
        name="experts",
    )(te, nr, nx, xs, w_gate, w_up, w_down)


def _combine_kernel(p0_ref, p1_ref, p0n_ref, p1n_ref, xp_ref, rc_ref, g_ref, y_hbm, out_ref, y0_buf, y1_buf, sems,
                    *, n_tiles):
    i = pl.program_id(0)
    rows, d = xp_ref.shape

    def gather(pa_ref, pb_ref, slot):
        base = slot * rows
        def body(t, carry):
            pltpu.make_async_copy(y_hbm.at[pa_ref[t]], y0_buf.at[base + t], sems.at[slot]).start(priority=0)
            pltpu.make_async_copy(y_hbm.at[pb_ref[t]], y1_buf.at[base + t], sems.at[slot]).start(priority=1)
            return carry
        lax.fori_loop(0, rows, body, 0)

    slot = lax.rem(i, 2)

    @pl.when(i == 0)
    def _():
        gather(p0_ref, p1_ref, 0)

    @pl.when(i + 1 < n_tiles)
    def _():
        gather(p0n_ref, p1n_ref, 1 - slot)

    cur = pl.ds(pl.multiple_of(slot * rows, rows), rows)
    for buf in (y0_buf, y1_buf):
        pltpu.make_async_copy(y_hbm.at[pl.ds(0, rows)], buf.at[cur], sems.at[slot]).wait()
    ct = rc_ref[...].T
    y0 = _unpack_halves(y0_buf[cur].reshape(rows, d // 2))
    y1 = _unpack_halves(y1_buf[cur].reshape(rows, d // 2))
    moe = ct[:, 0:1] * y0 + ct[:, 1:2] * y1
    out_ref[...] = _rms_rows(xp_ref[...] + moe, g_ref[...])


def _combine(xp, rc, p0, p1, y, g):
    n, d = xp.shape
    t = min(MOVE_ROWS, n)
    assert n % t == 0
    n_t = n // t
    est = 2 * 2 * t * d * 4 + 2 * 2 * t * d * 2 + 8 * t * d * 4
    nxt = lambda i: (jnp.minimum(i + 1, n_t - 1),)
    return pl.pallas_call(
        functools.partial(_combine_kernel, n_tiles=n_t),
        grid=(n_t,),
        in_specs=[
            pl.BlockSpec((t,), lambda i: (i,), memory_space=pltpu.SMEM),
            pl.BlockSpec((t,), lambda i: (i,), memory_space=pltpu.SMEM),
            pl.BlockSpec((t,), nxt, memory_space=pltpu.SMEM),
            pl.BlockSpec((t,), nxt, memory_space=pltpu.SMEM),
            pl.BlockSpec((t, d), lambda i: (i, 0)),
            pl.BlockSpec((SUBLANES, t), lambda i: (0, i)),
            pl.BlockSpec((1, d), lambda i: (0, 0)),
            pl.BlockSpec(memory_space=pl.ANY),
        ],
        out_specs=pl.BlockSpec((t, d), lambda i: (i, 0)),
        out_shape=jax.ShapeDtypeStruct((n, d), F32),
        scratch_shapes=[pltpu.VMEM((2 * t, 1, d // 2), U32), pltpu.VMEM((2 * t, 1, d // 2), U32),
                        pltpu.SemaphoreType.DMA((2,))],
        compiler_params=pltpu.CompilerParams(
            dimension_semantics=("arbitrary",), vmem_limit_bytes=_vmem_limit(est)),
        name="combine",
    )(p0, p1, p0, p1, xp, rc, g, y)


def _rel_buckets(rel):
    nb = N_BUCKETS // 2
    ret = (rel > 0).astype(I32) * nb
    n = jnp.abs(rel)
    max_exact = nb // 2
    nf = jnp.maximum(n, 1).astype(F32)
    large = max_exact + (jnp.log(nf / max_exact) / math.log(MAX_DISTANCE / max_exact)
                         * (nb - max_exact)).astype(I32)
    large = jnp.minimum(large, nb - 1)
    return ret + jnp.where(n < max_exact, n, large)


def _rel_bias(table, window, n_kv):
    n_heads = table.shape[1]
    span = window + CHUNK
    rows = table[_rel_buckets(jnp.arange(-(span - 1), CHUNK, dtype=I32))].astype(F32)
    bias = jnp.stack([rows[CHUNK - 1 - q:CHUNK - 1 - q + span] for q in range(CHUNK)])
    bias = jnp.transpose(bias, (2, 0, 1))
    bias = jnp.pad(bias, ((0, 0), (0, 0), (0, KEY_PAD - bias.shape[2])), constant_values=NEG_INF)
    bias = bias.reshape(n_kv, n_heads // n_kv // 2, 2, CHUNK, KEY_PAD)
    return jnp.transpose(bias, (0, 1, 3, 2, 4)).reshape(n_kv, (n_heads // n_kv // 2) * CHUNK, 2 * KEY_PAD)


def kernel(x_prompt, x_sample, cache_conv, cache_k, cache_v, rel_bias_table, norm_mix_g, w_in, conv_w, w_conv_out, attn_sinks, w_attn_out, w_o, norm_ffn_g, w_group, b_group, w_expert_router, b_expert_router, w_gate, w_up, w_down, final_norm_g):
    assert w_in.shape[0] == 1, "single-layer step"
    batch, seq, d = x_prompt.shape
    dec_batch, dec_seq, _ = x_sample.shape
    assert dec_seq == CHUNK and seq % MIX_ROWS == 0 and (dec_batch * dec_seq) % MIX_ROWS == 0
    d_conv = conv_w.shape[-1]
    window, n_kv, head_dim = cache_k.shape[2], cache_k.shape[3], cache_k.shape[4]
    n_heads = attn_sinks.shape[-1]
    d_attn, d_kv = n_heads * head_dim, n_kv * head_dim
    n_groups, n_exp = w_group.shape[-1], w_expert_router.shape[-1]
    assert n_groups <= GROUP_ROW0 and d_conv == d_attn and 2 * d_conv == d
    assert n_kv == 2 and d_kv == LANES and (n_heads // n_kv) % 2 == 0 and window + CHUNK <= KEY_PAD
    dims = (n_heads, n_kv, head_dim, window, n_groups, n_exp, d_conv)

    w_all = w_in[0].astype(BF16)
    kv0 = 3 * d_conv + d_attn
    g1 = norm_mix_g[0][None, :]
    wr = jnp.zeros((GROUP_ROW0 + n_exp, d), F32)
    wr = wr.at[:n_groups].set(w_group[0].T).at[GROUP_ROW0:].set(w_expert_router[0].T).astype(BF16)
    br = jnp.zeros((GROUP_ROW0 + n_exp, 1), F32)
    br = br.at[:n_groups, 0].set(b_group[0]).at[GROUP_ROW0:, 0].set(b_expert_router[0])
    xp2d = x_prompt.reshape(batch * seq, d)
    xs2d = x_sample.reshape(dec_batch * dec_seq, d)
    n_p, n_s = xp2d.shape[0], xs2d.shape[0]
    n_tok = n_p + n_s

    proj_p, kv_p, (wco, wao, wo) = _inproj(xp2d, g1, w_all, kv0, 2 * d_kv,
                                           side_casts=(w_conv_out[0], w_attn_out[0], w_o[0]))
    proj_s, kv_s, _ = _inproj(xs2d, g1, w_all, kv0, 2 * d_kv)
    weights = (conv_w[0], wco, wao, wo,
               _rel_bias(rel_bias_table, window, n_kv), attn_sinks, norm_ffn_g[0][None, :], wr, br)
    xres_p, h2_p, re_p, rc_p, conv_p, k_p, v_p = _mix(
        xp2d, proj_p, kv_p, None, weights, n_seq=batch, sample=False, dims=dims)
    caches = (cache_conv[0], cache_k[0].reshape(dec_batch, window, d_kv), cache_v[0].reshape(dec_batch, window, d_kv))
    xres_s, h2_s, re_s, rc_s, conv_s, k_s, v_s = _mix(
        xs2d, proj_s, kv_s, caches, weights, n_seq=dec_batch, sample=True, dims=dims)

    n_tiles = pl.cdiv((TOP_K * n_tok) // SLOT_ROWS + n_exp, EXPERT_STEP_TILES) * EXPERT_STEP_TILES
    n_slots = n_tiles * SLOT_ROWS
    pos, tmap = _route(jnp.concatenate([re_p, re_s], axis=1), n_exp, n_tiles)
    p0, p1 = pos[0], pos[1]
    xs = _dispatch(h2_p, p0[:n_p], p1[:n_p], None, n_slots)
    xs = _dispatch(h2_s, p0[n_p:], p1[n_p:], xs, n_slots)
    y = _experts(xs, tmap[0, :n_tiles], tmap[1, :n_tiles], tmap[2, :n_tiles], w_gate[0], w_up[0], w_down[0])
    gf = final_norm_g[None, :]
    y_prompt = _combine(xres_p, rc_p, p0[:n_p], p1[:n_p], y, gf).reshape(batch, seq, d)
    y_sample = _combine(xres_s, rc_s, p0[n_p:], p1[n_p:], y, gf).reshape(dec_batch, dec_seq, d)

    kv_shape = (1, -1, window, n_kv, head_dim)
    return (y_prompt, y_sample, conv_p[None], k_p.reshape(kv_shape), v_p.reshape(kv_shape),
            conv_s[None], k_s.reshape(kv_shape), v_s.reshape(kv_shape))
```

```python
import functools
import math

import jax
import jax.numpy as jnp
from jax import lax
from jax.experimental import pallas as pl
from jax.experimental.pallas import tpu as pltpu

F32, BF16, I32, U32 = jnp.float32, jnp.bfloat16, jnp.int32, jnp.uint32

CHUNK = 64
N_BUCKETS = 32
MAX_DISTANCE = 128
EPS = 1e-6
NEG_INF = -1e30
TOP_K = 2

V7X_VMEM_BYTES = 64 * 1024 * 1024
SUBLANES = 8
LANES = 128

INPROJ_ROWS = 1024
INPROJ_COLS = 2048
MIX_ROWS = 256
MOVE_ROWS = 512
SLOT_ROWS = 256
EXPERT_STEP_TILES = 4
ROUTE_BLOCK = 512
GROUP_ROW0 = 8
KEY_PAD = 256
KV_VARIANTS = 4


def _vmem_limit(nbytes):
    return int(min(V7X_VMEM_BYTES - (4 << 20), max(nbytes, 32 << 20)))


def _pack_halves(x):
    half = x.shape[1] // 2
    return pltpu.pack_elementwise([x[:, :half], x[:, half:]], packed_dtype=BF16)


def _unpack_halves(w):
    lo = pltpu.unpack_elementwise(w, index=0, packed_dtype=BF16, unpacked_dtype=F32)
    hi = pltpu.unpack_elementwise(w, index=1, packed_dtype=BF16, unpacked_dtype=F32)
    return jnp.concatenate([lo, hi], axis=1)


def _rms_rows(x, g):
    r = lax.rsqrt(jnp.mean(x * x, axis=-1, keepdims=True) + EPS)
    return (x * r) * g


def _inproj_kernel(*refs, cast_steps, n_col_steps):
    n_cast = len(cast_steps)
    x_ref, g_ref, wm_ref, wkv_ref = refs[:4]
    cast_in = refs[4:4 + n_cast]
    proj_ref, kv_ref = refs[4 + n_cast:6 + n_cast]
    cast_out = refs[6 + n_cast:6 + 2 * n_cast]
    h_sc = refs[-1]

    step = pl.program_id(0) * n_col_steps + pl.program_id(1)
    for (s0, s1), src, dst in zip(cast_steps, cast_in, cast_out):
        @pl.when(jnp.logical_and(step >= s0, step < s1))
        def _(src=src, dst=dst):
            dst[...] = src[...].astype(BF16)

    @pl.when(pl.program_id(1) == 0)
    def _():
        rows = 128
        for i in range(x_ref.shape[0] // rows):
            r = slice(i * rows, (i + 1) * rows)
            h = _rms_rows(x_ref[r, :], g_ref[...]).astype(BF16)
            h_sc[r, :] = h
            kv_ref[r, :] = jnp.dot(h, wkv_ref[...], preferred_element_type=F32)

    proj_ref[...] = jnp.dot(h_sc[...], wm_ref[...], preferred_element_type=F32).astype(BF16)


def _inproj(x2d, g, w_all, kv0, n_kv, side_casts=()):
    n, d = x2d.shape
    tm = min(INPROJ_ROWS, n)
    tn = INPROJ_COLS
    n_main = w_all.shape[1] - n_kv
    assert kv0 % tn == 0 and n_main % tn == 0 and n_kv % LANES == 0
    grid = (n // tm, n_main // tn)
    n_steps = grid[0] * grid[1]
    cast_rows = sum(w.shape[0] for w in side_casts) // n_steps if side_casts else 0
    cast_steps, cast_specs, step0 = [], [], 0
    for w in side_casts:
        assert cast_rows % (2 * SUBLANES) == 0 and w.shape[0] % cast_rows == 0
        n_blk = w.shape[0] // cast_rows
        cast_steps.append((step0, step0 + n_blk))
        cast_specs.append(pl.BlockSpec(
            (cast_rows, w.shape[1]),
            lambda i, j, s0=step0, nb=n_blk: (jnp.clip(i * grid[1] + j - s0, 0, nb - 1), 0)))
        step0 += n_blk
    assert step0 == (n_steps if side_casts else 0)
    est = (2 * tm * d * 4 + tm * d * 2 + 2 * d * tn * 2 + 2 * d * n_kv * 2 + 2 * tm * tn * 2 + 2 * tm * n_kv * 4
           + tm * tn * 4 + sum(2 * cast_rows * w.shape[1] * 6 for w in side_casts))
    outs = pl.pallas_call(
        functools.partial(_inproj_kernel, cast_steps=tuple(cast_steps), n_col_steps=grid[1]),
        grid=grid,
        in_specs=[
            pl.BlockSpec((tm, d), lambda i, j: (i, 0)),
            pl.BlockSpec((1, d), lambda i, j: (0, 0)),
            pl.BlockSpec((pl.Element(d), pl.Element(tn)),
                         lambda i, j: (0, pl.multiple_of(jnp.where(j * tn < kv0, j * tn, j * tn + n_kv), LANES))),
            pl.BlockSpec((pl.Element(d), pl.Element(n_kv)), lambda i, j: (0, kv0)),
        ] + cast_specs,
        out_specs=[
            pl.BlockSpec((tm, tn), lambda i, j: (i, j)),
            pl.BlockSpec((tm, n_kv), lambda i, j: (i, 0)),
        ] + cast_specs,
        out_shape=[jax.ShapeDtypeStruct((n, n_main), BF16), jax.ShapeDtypeStruct((n, n_kv), F32)]
        + [jax.ShapeDtypeStruct(w.shape, BF16) for w in side_casts],
        scratch_shapes=[pltpu.VMEM((tm, d), BF16)],
        compiler_params=pltpu.CompilerParams(
            dimension_semantics=("arbitrary", "arbitrary"), vmem_limit_bytes=_vmem_limit(est + (8 << 20))),
        name="inproj",
    )(x2d, g, w_all, w_all, *side_casts)
    return outs[0], outs[1], tuple(outs[2:])


def _conv_rows(u, prev2, prev1, w):
    row = lax.broadcasted_iota(I32, u.shape, 0)
    u1 = jnp.where(row == 0, prev1, pltpu.roll(u, 1, axis=0))
    u2 = jnp.where(row == 0, prev2, jnp.where(row == 1, prev1, pltpu.roll(u, 2, axis=0)))
    return (w[0:1] * u2 + w[1:2] * u1) + w[2:3] * u


def _mix_kernel(*refs, sample, n_heads, n_kv, head_dim, window, n_groups, n_exp):
    if sample:
        (x_ref, ga_ref, gb_ref, b_ref, c_ref, xc_ref, q_ref, kv_ref, cconv_ref, ck_ref, cv_ref,
         convw_ref, wco_ref, wao_ref, wo_ref, bias_ref, sinks_ref, g2_ref, wr_ref, br_ref,
         xp_ref, h2_ref, re_ref, rc_ref, sconv_ref, sk_ref, sv_ref,
         kvar, o_sc, ya_sc, carry_u, s_sc, e_sc, sink_sc) = refs
    else:
        (x_ref, ga_ref, gb_ref, b_ref, c_ref, xc_ref, q_ref, kv_ref,
         convw_ref, wco_ref, wao_ref, wo_ref, bias_ref, sinks_ref, g2_ref, wr_ref, br_ref,
         xp_ref, h2_ref, re_ref, rc_ref, sconv_ref, sk_ref, sv_ref,
         kvar, o_sc, ya_sc, carry_u, s_sc, e_sc, sink_sc) = refs
    t_rows, d_model = x_ref.shape
    n_chunks = t_rows // CHUNK
    span = window + CHUNK
    q_per_kv = n_heads // n_kv
    d_kv = n_kv * head_dim
    scale = 1.0 / math.sqrt(head_dim)
    scale_is_pow2 = math.frexp(scale)[0] == 0.5
    epg = n_exp // n_groups
    seq_start = pl.program_id(1) == 0

    def conv_branch():
        w_conv = convw_ref[...]
        if sample:
            for s in range(n_chunks):
                rows = slice(s * CHUNK, (s + 1) * CHUNK)
                u = c_ref[rows, :].astype(F32) * xc_ref[rows, :].astype(F32)
                y = _conv_rows(u, cconv_ref[s, 0:1, :], cconv_ref[s, 1:2, :], w_conv)
                ya_sc[rows, :] = (b_ref[rows, :].astype(F32) * y).astype(BF16)
                sconv_ref[s] = u[CHUNK - 2:CHUNK, :]
        else:
            u = c_ref[...].astype(F32) * xc_ref[...].astype(F32)
            prev = jnp.where(seq_start, 0.0, carry_u[...])
            y = _conv_rows(u, prev[SUBLANES - 2:SUBLANES - 1], prev[SUBLANES - 1:SUBLANES], w_conv)
            ya_sc[...] = (b_ref[...].astype(F32) * y).astype(BF16)
            carry_u[...] = u[t_rows - SUBLANES:t_rows, :]
            sconv_ref[0] = u[t_rows - 2:t_rows, :]

    def store_kv(row0, k_rows, v_rows):
        n_rows = k_rows.shape[0]
        low = lax.broadcasted_iota(I32, k_rows.shape, 1) < head_dim
        for base, a in ((0, k_rows), (KV_VARIANTS, v_rows)):
            b = pltpu.roll(a, head_dim, axis=1)
            kvar[base + 0, row0:row0 + n_rows, :] = jnp.where(low, a, 0.0).astype(BF16)
            kvar[base + 1, row0:row0 + n_rows, :] = jnp.where(low, 0.0, a).astype(BF16)
            kvar[base + 2, row0:row0 + n_rows, :] = jnp.where(low, b, 0.0).astype(BF16)
            kvar[base + 3, row0:row0 + n_rows, :] = jnp.where(low, 0.0, b).astype(BF16)

    if sample:
        for s in range(n_chunks):
            rows = slice(s * CHUNK, (s + 1) * CHUNK)
            ck = jnp.concatenate([ck_ref[s, :, h, :] for h in range(n_kv)], axis=1)
            cv = jnp.concatenate([cv_ref[s, :, h, :] for h in range(n_kv)], axis=1)
            store_kv(s * span, ck, cv)
            store_kv(s * span + window, kv_ref[rows, 0:d_kv], kv_ref[rows, d_kv:2 * d_kv])
            for h in range(n_kv):
                sk_ref[s, 0:window - CHUNK, h, :] = ck_ref[s, CHUNK:window, h, :]
                sv_ref[s, 0:window - CHUNK, h, :] = cv_ref[s, CHUNK:window, h, :]
                sk_ref[s, window - CHUNK:window, h, :] = kv_ref[rows, h * head_dim:(h + 1) * head_dim]
                sv_ref[s, window - CHUNK:window, h, :] = kv_ref[rows, d_kv + h * head_dim:d_kv + (h + 1) * head_dim]
        k_stride = span
    else:
        @pl.when(seq_start)
        def _():
            kvar[:, 0:window, :] = jnp.zeros((2 * KV_VARIANTS, window, d_kv), BF16)
        store_kv(window, kv_ref[:, 0:d_kv], kv_ref[:, d_kv:2 * d_kv])
        for h in range(n_kv):
            sk_ref[0, :, h, :] = kv_ref[t_rows - window:t_rows, h * head_dim:(h + 1) * head_dim]
            sv_ref[0, :, h, :] = kv_ref[t_rows - window:t_rows, d_kv + h * head_dim:d_kv + (h + 1) * head_dim]
        k_stride = CHUNK

    n_pairs = q_per_kv // 2
    key_pad_rows = jnp.zeros((KEY_PAD - span, d_kv), BF16)
    ones_d = ((lax.broadcasted_iota(I32, (2 * KEY_PAD, LANES), 0) < KEY_PAD)
              == (lax.broadcasted_iota(I32, (2 * KEY_PAD, LANES), 1) < head_dim)).astype(BF16)
    low_half = lax.broadcasted_iota(I32, (CHUNK, LANES), 1) < head_dim
    ya_parts = []
    ya_cols = d_model // (n_chunks * n_kv)
    blocks = [(c, n) for c in range(n_chunks) for n in range(n_kv)]

    def key_rows(c):
        return slice(c * k_stride, c * k_stride + span)

    for b, (c, n) in enumerate(blocks):
        q_rows = slice(c * CHUNK, (c + 1) * CHUNK)
        top, bot = (0, 3) if n == 0 else (2, 1)
        kd = jnp.concatenate([kvar[top, key_rows(c), :], key_pad_rows, kvar[bot, key_rows(c), :], key_pad_rows],
                             axis=0)
        q4 = jnp.concatenate(
            [q_ref[q_rows, (n * n_pairs + j) * LANES:(n * n_pairs + j + 1) * LANES] for j in range(n_pairs)], axis=0)
        if scale_is_pow2:
            q4 = q4 * scale
        s4 = lax.dot_general(q4, kd, (((1,), (1,)), ((), ())), preferred_element_type=F32)
        s_sc[b] = s4 if scale_is_pow2 else s4 * scale
    conv_branch()
    for b, (c, n) in enumerate(blocks):
        masked = (not sample) and c * CHUNK < window
        if masked:
            first_key = (pl.program_id(1) * n_chunks + c) * CHUNK - window
            valid = lax.broadcasted_iota(I32, (CHUNK, KEY_PAD), 1) + first_key >= 0
        for j in range(n_pairs):
            sink_halves = []
            for half in range(2):
                blk = (slice(j * CHUNK, (j + 1) * CHUNK), slice(half * KEY_PAD, (half + 1) * KEY_PAD))
                s = s_sc[b, blk[0], blk[1]] + bias_ref[n, blk[0], blk[1]]
                if masked:
                    s = jnp.where(valid, s, NEG_INF)
                sink = sinks_ref[0, n * q_per_kv + 2 * j + half]
                mx = jnp.maximum(jnp.max(s, axis=-1, keepdims=True), sink)
                e_sc[b, blk[0], blk[1]] = jnp.exp(s - mx).astype(BF16)
                sink_halves.append(jnp.exp(sink - mx))
            sink_sc[b, j * CHUNK:(j + 1) * CHUNK, :] = jnp.where(low_half, sink_halves[0], sink_halves[1])
        ya_parts.append(jnp.dot(ya_sc[...], wco_ref[:, b * ya_cols:(b + 1) * ya_cols], preferred_element_type=F32))
    for b, (c, n) in enumerate(blocks):
        q_rows = slice(c * CHUNK, (c + 1) * CHUNK)
        top, bot = (0, 3) if n == 0 else (2, 1)
        vd = jnp.concatenate([kvar[KV_VARIANTS + top, key_rows(c), :], key_pad_rows,
                              kvar[KV_VARIANTS + bot, key_rows(c), :], key_pad_rows], axis=0)
        od = jnp.dot(e_sc[b], jnp.concatenate([vd, ones_d], axis=1), preferred_element_type=F32)
        o4 = od[:, 0:LANES] / (od[:, LANES:2 * LANES] + sink_sc[b])
        for j in range(n_pairs):
            o_sc[q_rows, (n * n_pairs + j) * LANES:(n * n_pairs + j + 1) * LANES] = (
                o4[j * CHUNK:(j + 1) * CHUNK].astype(BF16))
    if not sample:
        kvar[:, 0:window, :] = kvar[:, t_rows:t_rows + window, :]
    y_a = jnp.concatenate(ya_parts, axis=1)
    y_b = jnp.dot(o_sc[...], wao_ref[...], preferred_element_type=F32)

    m = jax.nn.sigmoid(ga_ref[...].astype(F32)) * y_a + jax.nn.sigmoid(gb_ref[...].astype(F32)) * y_b
    xp = x_ref[...] + jnp.dot(m.astype(BF16), wo_ref[...], preferred_element_type=F32)
    xp_ref[...] = xp

    h2 = _rms_rows(xp, g2_ref[...])
    h2_ref[...] = _pack_halves(h2).reshape(t_rows, 1, d_model // 2)
    lt = lax.dot_general(wr_ref[...], h2.astype(BF16), (((1,), (1,)), ((), ())),
                         preferred_element_type=F32) + br_ref[...]
    lg = lt[0:n_groups]
    eg = jnp.exp(lg - jnp.max(lg, axis=0, keepdims=True))
    gp = eg / jnp.sum(eg, axis=0, keepdims=True)
    gw = jnp.max(gp, axis=0, keepdims=True)
    gi = lax.broadcasted_iota(I32, gp.shape, 0).astype(F32)
    gsel = jnp.min(jnp.where(gp == gw, gi, float(n_groups)), axis=0, keepdims=True)
    el = jnp.zeros((epg, t_rows), F32)
    for g in range(n_groups):
        el = jnp.where(gsel == float(g), lt[GROUP_ROW0 + g * epg:GROUP_ROW0 + (g + 1) * epg], el)
    ei = lax.broadcasted_iota(I32, el.shape, 0).astype(F32)
    v1 = jnp.max(el, axis=0, keepdims=True)
    i1 = jnp.min(jnp.where(el == v1, ei, float(epg)), axis=0, keepdims=True)
    el2 = jnp.where(ei == i1, -jnp.inf, el)
    v2 = jnp.max(el2, axis=0, keepdims=True)
    i2 = jnp.min(jnp.where(el2 == v2, ei, float(epg)), axis=0, keepdims=True)
    a1 = jnp.exp(v1 - v1)
    a2 = jnp.exp(v2 - v1)
    den = a1 + a2
    c1 = gw * (a1 / den)
    c2 = gw * (a2 / den)
    e1 = (gsel * float(epg) + i1).astype(I32)
    e2 = (gsel * float(epg) + i2).astype(I32)
    row8 = lax.broadcasted_iota(I32, (SUBLANES, t_rows), 0)
    re_ref[...] = jnp.where(row8 == 0, e1, jnp.where(row8 == 1, e2, 0))
    rc_ref[...] = jnp.where(row8 == 0, c1, jnp.where(row8 == 1, c2, 0.0))


def _mix(x2d, proj, kv, caches, weights, *, n_seq, sample, dims):
    n_heads, n_kv, head_dim, window, n_groups, n_exp, d_conv = dims
    n, d = x2d.shape
    t = MIX_ROWS
    n_chunks = t // CHUNK
    span = window + CHUNK
    d_attn = n_heads * head_dim
    d_kv = n_kv * head_dim
    convw, wco, wao, wo, bias, sinks, g2, wr, br = weights
    if sample:
        n_t = n // t
        grid = (n_t, 1)
        tok = lambda i, j: (i, 0)
        n_state = n // CHUNK
        state_blk = n_chunks
        st = lambda i, j: (i, 0, 0)
    else:
        n_t = (n // n_seq) // t
        grid = (n_seq, n_t)
        tok = lambda i, j: (i * n_t + j, 0)
        n_state = n_seq
        state_blk = 1
        st = lambda i, j: (i, 0, 0)
    const2 = lambda i, j: (0, 0)

    def col(width, idx):
        return pl.BlockSpec((t, width), lambda i, j: (tok(i, j)[0], idx))

    def resident(shape):
        return pl.BlockSpec(shape, const2, pipeline_mode=pl.Buffered(1))

    in_specs = [
        pl.BlockSpec((t, d), tok),
        col(d, 2), col(d, 3),
        col(d_conv, 0), col(d_conv, 1), col(d_conv, 2), col(d_attn, 3),
        pl.BlockSpec((t, 2 * d_kv), tok),
    ]
    args = [x2d, proj, proj, proj, proj, proj, proj, kv]
    if sample:
        cconv, ck, cv = caches
        in_specs += [
            pl.BlockSpec((n_chunks, cconv.shape[1], d_conv), st),
            pl.BlockSpec((n_chunks, window, n_kv, head_dim), lambda i, j: (i, 0, 0, 0)),
            pl.BlockSpec((n_chunks, window, n_kv, head_dim), lambda i, j: (i, 0, 0, 0)),
        ]
        args += [cconv, ck, cv]
    in_specs += [
        resident(convw.shape), resident(wco.shape), resident(wao.shape), resident(wo.shape),
        pl.BlockSpec(bias.shape, lambda i, j: (0, 0, 0), pipeline_mode=pl.Buffered(1)),
        pl.BlockSpec(memory_space=pltpu.SMEM),
        resident(g2.shape), resident(wr.shape), resident(br.shape),
    ]
    args += [convw, wco, wao, wo, bias, sinks, g2, wr, br]
    out_specs = [
        pl.BlockSpec((t, d), tok),
        pl.BlockSpec((t, 1, d // 2), lambda i, j: (tok(i, j)[0], 0, 0)),
        pl.BlockSpec((SUBLANES, t), lambda i, j: (0, tok(i, j)[0])),
        pl.BlockSpec((SUBLANES, t), lambda i, j: (0, tok(i, j)[0])),
        pl.BlockSpec((state_blk, 2, d_conv), st),
        pl.BlockSpec((state_blk, window, n_kv, head_dim), lambda i, j: st(i, j) + (0,)),
        pl.BlockSpec((state_blk, window, n_kv, head_dim), lambda i, j: st(i, j) + (0,)),
    ]
    out_shape = [
        jax.ShapeDtypeStruct((n, d), F32),
        jax.ShapeDtypeStruct((n, 1, d // 2), U32),
        jax.ShapeDtypeStruct((SUBLANES, n), I32),
        jax.ShapeDtypeStruct((SUBLANES, n), F32),
        jax.ShapeDtypeStruct((n_state, 2, d_conv), F32),
        jax.ShapeDtypeStruct((n_state, window, n_kv, head_dim), F32),
        jax.ShapeDtypeStruct((n_state, window, n_kv, head_dim), F32),
    ]
    kv_rows = n_chunks * span if sample else window + t
    scratch = [
        pltpu.VMEM((2 * KV_VARIANTS, kv_rows, d_kv), BF16),
        pltpu.VMEM((t, d_attn), BF16), pltpu.VMEM((t, d_conv), BF16),
        pltpu.VMEM((SUBLANES, d_conv), F32),
        pltpu.VMEM((n_chunks * n_kv, (n_heads // n_kv // 2) * CHUNK, 2 * KEY_PAD), F32),
        pltpu.VMEM((n_chunks * n_kv, (n_heads // n_kv // 2) * CHUNK, 2 * KEY_PAD), BF16),
        pltpu.VMEM((n_chunks * n_kv, (n_heads // n_kv // 2) * CHUNK, LANES), F32),
    ]
    est = (2 * t * d * 4 * 3 + 2 * 2 * t * d * 2 + 2 * 4 * t * d_conv * 2
           + (2 * d_conv * d + d * d) * 2 + 12 * t * d * 4)
    kern = functools.partial(_mix_kernel, sample=sample, n_heads=n_heads, n_kv=n_kv, head_dim=head_dim,
                             window=window, n_groups=n_groups, n_exp=n_exp)
    return pl.pallas_call(
        kern, grid=grid, in_specs=in_specs, out_specs=out_specs, out_shape=out_shape,
        scratch_shapes=scratch,
        compiler_params=pltpu.CompilerParams(
            dimension_semantics=("arbitrary", "arbitrary"), vmem_limit_bytes=_vmem_limit(est)),
        name="mix_sample" if sample else "mix_prompt",
    )(*args)


def _route_kernel(re_ref, pos_ref, tmap_ref, *, n_exp, tile_rows, blk):
    n = re_ref.shape[1]
    n_blk = n // blk
    erow = lax.broadcasted_iota(I32, (n_exp, blk), 0)

    def onehots(j):
        c0 = pl.multiple_of(j * blk, blk)
        oh0 = (erow == re_ref[0:1, pl.ds(c0, blk)]).astype(F32)
        oh1 = (erow == re_ref[1:2, pl.ds(c0, blk)]).astype(F32)
        return c0, oh0, oh1

    def count_body(j, cnt):
        _, oh0, oh1 = onehots(j)
        return cnt + jnp.sum(oh0 + oh1, axis=1, keepdims=True)

    cnt = lax.fori_loop(0, n_blk, count_body, jnp.zeros((n_exp, 1), F32))
    cnt = jnp.broadcast_to(cnt, (n_exp, LANES))
    padded = jnp.ceil(cnt / tile_rows) * tile_rows
    ends = padded
    prow = lax.broadcasted_iota(I32, ends.shape, 0)
    step = 1
    while step < n_exp:
        ends = ends + jnp.where(prow >= step, pltpu.roll(ends, step, axis=0), 0.0)
        step *= 2
    offs = ends - padded
    off1 = offs[:, 0:1]

    tri = (lax.broadcasted_iota(I32, (blk, blk), 0) <= lax.broadcasted_iota(I32, (blk, blk), 1)).astype(BF16)
    row8 = lax.broadcasted_iota(I32, (SUBLANES, blk), 0)

    def pos_body(j, run):
        c0, oh0, oh1 = onehots(j)
        both = oh0 + oh1
        csum = jnp.dot(both.astype(BF16), tri, preferred_element_type=F32) + run
        slot = off1 + csum - 1.0
        p0 = jnp.sum(oh0 * slot, axis=0, keepdims=True).astype(I32)
        p1 = jnp.sum(oh1 * slot, axis=0, keepdims=True).astype(I32)
        pos_ref[:, pl.ds(c0, blk)] = jnp.where(row8 == 0, p0, jnp.where(row8 == 1, p1, 0))
        return run + jnp.sum(both, axis=1, keepdims=True)

    lax.fori_loop(0, n_blk, pos_body, jnp.zeros((n_exp, 1), F32))

    n_tiles_pad = tmap_ref.shape[1]
    start = (lax.broadcasted_iota(I32, (1, n_tiles_pad), 1) * tile_rows).astype(F32)
    end1 = ends[:, 0:1]
    te = jnp.sum((end1 <= start).astype(F32), axis=0, keepdims=True)
    trow = lax.broadcasted_iota(I32, (n_exp, n_tiles_pad), 0).astype(F32)
    used_end = jnp.sum(jnp.where(trow == te, off1 + cnt[:, 0:1], 0.0), axis=0, keepdims=True)
    n_rows = jnp.clip(used_end - start, 0.0, float(tile_rows))
    has_tokens = cnt[:, 0:1] > 0.0
    te = jnp.minimum(te, jnp.max(jnp.where(has_tokens, trow, 0.0), axis=0, keepdims=True))
    nxt = jnp.min(jnp.where(trow > te, jnp.where(has_tokens, trow, float(n_exp)), float(n_exp)), axis=0, keepdims=True)
    r8 = lax.broadcasted_iota(I32, (SUBLANES, n_tiles_pad), 0)
    tmap_ref[...] = jnp.where(r8 == 0, te.astype(I32),
                              jnp.where(r8 == 1, n_rows.astype(I32), jnp.where(r8 == 2, nxt.astype(I32), 0)))


def _route(re, n_exp, n_tiles):
    n = re.shape[1]
    assert n % ROUTE_BLOCK == 0
    n_tiles_pad = pl.cdiv(n_tiles, LANES) * LANES
    kern = functools.partial(_route_kernel, n_exp=n_exp, tile_rows=SLOT_ROWS, blk=ROUTE_BLOCK)
    return pl.pallas_call(
        kern,
        out_shape=[jax.ShapeDtypeStruct((SUBLANES, n), I32), jax.ShapeDtypeStruct((SUBLANES, n_tiles_pad), I32)],
        name="route",
    )(re)


def _dispatch_kernel(*refs, aliased):
    if aliased:
        p0_ref, p1_ref, h_ref, _, xs_ref, sem = refs
    else:
        p0_ref, p1_ref, h_ref, xs_ref, sem = refs
    rows = h_ref.shape[0]

    def start(t, carry):
        pltpu.make_async_copy(h_ref.at[t], xs_ref.at[p0_ref[t]], sem).start(priority=0)
        pltpu.make_async_copy(h_ref.at[t], xs_ref.at[p1_ref[t]], sem).start(priority=1)
        return carry

    lax.fori_loop(0, rows, start, 0)
    for _ in range(TOP_K):
        pltpu.make_async_copy(h_ref, xs_ref.at[pl.ds(0, rows)], sem).wait()


def _dispatch(h2, p0, p1, xs, n_slots):
    n, _, d = h2.shape
    t = min(MOVE_ROWS, n)
    assert n % t == 0
    aliased = xs is not None
    in_specs = [
        pl.BlockSpec((t,), lambda i: (i,), memory_space=pltpu.SMEM),
        pl.BlockSpec((t,), lambda i: (i,), memory_space=pltpu.SMEM),
        pl.BlockSpec((t, 1, d), lambda i: (i, 0, 0)),
    ]
    args = [p0, p1, h2]
    if aliased:
        in_specs.append(pl.BlockSpec(memory_space=pl.ANY))
        args.append(xs)
    return pl.pallas_call(
        functools.partial(_dispatch_kernel, aliased=aliased),
        grid=(n // t,),
        in_specs=in_specs,
        out_specs=pl.BlockSpec(memory_space=pl.ANY),
        out_shape=jax.ShapeDtypeStruct((n_slots, 1, d), h2.dtype),
        scratch_shapes=[pltpu.SemaphoreType.DMA(())],
        input_output_aliases={3: 0} if aliased else {},
        compiler_params=pltpu.CompilerParams(dimension_semantics=("arbitrary",), has_side_effects=True),
        name="dispatch",
    )(*args)


def _experts_kernel(te_ref, nr_ref, nx_ref, xs_ref, wg_hbm, wu_hbm, wd_hbm, y_ref,
                    wg_st, wu_st, wd_st, wg_sc, wu_sc, wd_sc, sems, n_changes, *, n_exp, tile_rows):
    dh = xs_ref.shape[2]
    tiles_per_step = xs_ref.shape[0] // tile_rows

    def copies(e, slot):
        return (pltpu.make_async_copy(wg_hbm.at[e], wg_st.at[slot], sems.at[slot]),
                pltpu.make_async_copy(wu_hbm.at[e], wu_st.at[slot], sems.at[slot]),
                pltpu.make_async_copy(wd_hbm.at[e], wd_st.at[slot], sems.at[slot]))

    def one_tile(sub):
        tile = pl.program_id(0) * tiles_per_step + sub
        rows = pl.ds(sub * tile_rows, tile_rows)
        expert = te_ref[tile]

        @pl.when(tile == 0)
        def _():
            n_changes[0] = 0
            for cp in copies(expert, 0):
                cp.start(priority=1)

        @pl.when(jnp.logical_or(tile == 0, expert != te_ref[jnp.maximum(tile - 1, 0)]))
        def _():
            slot = lax.rem(n_changes[0], 2)
            n_changes[0] = n_changes[0] + 1
            for cp in copies(expert, slot):
                cp.wait()
            nxt = nx_ref[tile]

            @pl.when(nxt < n_exp)
            def _():
                for cp in copies(nxt, 1 - slot):
                    cp.start(priority=1)

            wg_sc[...] = wg_st[slot].astype(BF16)
            wu_sc[...] = wu_st[slot].astype(BF16)
            wd_sc[...] = wd_st[slot].astype(BF16)

        n_rows = nr_ref[tile]

        @pl.when(n_rows > 0)
        def _():
            live = lax.broadcasted_iota(I32, (tile_rows, 2 * dh), 0) < n_rows
            x = jnp.where(live, _unpack_halves(xs_ref[rows].reshape(tile_rows, dh)), 0.0).astype(BF16)
            gate = jnp.dot(x, wg_sc[...], preferred_element_type=F32)
            up = jnp.dot(x, wu_sc[...], preferred_element_type=F32)
            hid = (jax.nn.silu(gate) * up).astype(BF16)
            y = jnp.dot(hid, wd_sc[...], preferred_element_type=F32)
            y_ref[rows] = _pack_halves(y).reshape(tile_rows, 1, dh)

        @pl.when(n_rows <= 0)
        def _():
            y_ref[rows] = _pack_halves(jnp.zeros((tile_rows, 2 * dh), F32)).reshape(tile_rows, 1, dh)

    for sub in range(tiles_per_step):
        one_tile(sub)


def _experts(xs, te, nr, nx, w_gate, w_up, w_down):
    n_slots, _, dh = xs.shape
    n_exp, d, d_e = w_gate.shape
    t = SLOT_ROWS * EXPERT_STEP_TILES
    est = 2 * 2 * t * d * 2 + 2 * 3 * d * d_e * 4 + 3 * d * d_e * 2 + 4 * SLOT_ROWS * d * 4
    row_spec = pl.BlockSpec((t, 1, dh), lambda i, te, nr, nx: (i, 0, 0))
    return pl.pallas_call(
        functools.partial(_experts_kernel, n_exp=n_exp, tile_rows=SLOT_ROWS),
        grid_spec=pltpu.PrefetchScalarGridSpec(
            num_scalar_prefetch=3,
            grid=(n_slots // t,),
            in_specs=[row_spec, pl.BlockSpec(memory_space=pl.ANY), pl.BlockSpec(memory_space=pl.ANY),
                      pl.BlockSpec(memory_space=pl.ANY)],
            out_specs=row_spec,
            scratch_shapes=[
                pltpu.VMEM((2, d, d_e), F32), pltpu.VMEM((2, d, d_e), F32), pltpu.VMEM((2, d_e, d), F32),
                pltpu.VMEM((d, d_e), BF16), pltpu.VMEM((d, d_e), BF16), pltpu.VMEM((d_e, d), BF16),
                pltpu.SemaphoreType.DMA((2,)), pltpu.SMEM((1,), I32)],
        ),
        out_shape=jax.ShapeDtypeStruct((n_slots, 1, dh), U32),
        compiler_params=pltpu.CompilerParams(
            dimension_semantics=("arbitrary",), vmem_limit_bytes=_vmem_limit(est)),
        name="experts",
    )(te, nr, nx, xs, w_gate, w_up, w_down)


def _combine_kernel(p0_ref, p1_ref, p0n_ref, p1n_ref, xp_ref, rc_ref, g_ref, y_hbm, out_ref, y0_buf, y1_buf, sems,
                    *, n_tiles):
    i = pl.program_id(0)
    rows, d = xp_ref.shape

    def gather(pa_ref, pb_ref, slot):
        base = slot * rows
        def body(t, carry):
            pltpu.make_async_copy(y_hbm.at[pa_ref[t]], y0_buf.at[base + t], sems.at[slot]).start(priority=0)
            pltpu.make_async_copy(y_hbm.at[pb_ref[t]], y1_buf.at[base + t], sems.at[slot]).start(priority=1)
            return carry
        lax.fori_loop(0, rows, body, 0)

    slot = lax.rem(i, 2)

    @pl.when(i == 0)
    def _():
        gather(p0_ref, p1_ref, 0)

    @pl.when(i + 1 < n_tiles)
    def _():
        gather(p0n_ref, p1n_ref, 1 - slot)

    cur = pl.ds(pl.multiple_of(slot * rows, rows), rows)
    for buf in (y0_buf, y1_buf):
        pltpu.make_async_copy(y_hbm.at[pl.ds(0, rows)], buf.at[cur], sems.at[slot]).wait()
    ct = rc_ref[...].T
    y0 = _unpack_halves(y0_buf[cur].reshape(rows, d // 2))
    y1 = _unpack_halves(y1_buf[cur].reshape(rows, d // 2))
    moe = ct[:, 0:1] * y0 + ct[:, 1:2] * y1
    out_ref[...] = _rms_rows(xp_ref[...] + moe, g_ref[...])


def _combine(xp, rc, p0, p1, y, g):
    n, d = xp.shape
    t = min(MOVE_ROWS, n)
    assert n % t == 0
    n_t = n // t
    est = 2 * 2 * t * d * 4 + 2 * 2 * t * d * 2 + 8 * t * d * 4
    nxt = lambda i: (jnp.minimum(i + 1, n_t - 1),)
    return pl.pallas_call(
        functools.partial(_combine_kernel, n_tiles=n_t),
        grid=(n_t,),
        in_specs=[
            pl.BlockSpec((t,), lambda i: (i,), memory_space=pltpu.SMEM),
            pl.BlockSpec((t,), lambda i: (i,), memory_space=pltpu.SMEM),
            pl.BlockSpec((t,), nxt, memory_space=pltpu.SMEM),
            pl.BlockSpec((t,), nxt, memory_space=pltpu.SMEM),
            pl.BlockSpec((t, d), lambda i: (i, 0)),
            pl.BlockSpec((SUBLANES, t), lambda i: (0, i)),
            pl.BlockSpec((1, d), lambda i: (0, 0)),
            pl.BlockSpec(memory_space=pl.ANY),
        ],
        out_specs=pl.BlockSpec((t, d), lambda i: (i, 0)),
        out_shape=jax.ShapeDtypeStruct((n, d), F32),
        scratch_shapes=[pltpu.VMEM((2 * t, 1, d // 2), U32), pltpu.VMEM((2 * t, 1, d // 2), U32),
                        pltpu.SemaphoreType.DMA((2,))],
        compiler_params=pltpu.CompilerParams(
            dimension_semantics=("arbitrary",), vmem_limit_bytes=_vmem_limit(est)),
        name="combine",
    )(p0, p1, p0, p1, xp, rc, g, y)


def _rel_buckets(rel):
    nb = N_BUCKETS // 2
    ret = (rel > 0).astype(I32) * nb
    n = jnp.abs(rel)
    max_exact = nb // 2
    nf = jnp.maximum(n, 1).astype(F32)
    large = max_exact + (jnp.log(nf / max_exact) / math.log(MAX_DISTANCE / max_exact)
                         * (nb - max_exact)).astype(I32)
    large = jnp.minimum(large, nb - 1)
    return ret + jnp.where(n < max_exact, n, large)


def _rel_bias(table, window, n_kv):
    n_heads = table.shape[1]
    span = window + CHUNK
    rows = table[_rel_buckets(jnp.arange(-(span - 1), CHUNK, dtype=I32))].astype(F32)
    bias = jnp.stack([rows[CHUNK - 1 - q:CHUNK - 1 - q + span] for q in range(CHUNK)])
    bias = jnp.transpose(bias, (2, 0, 1))
    bias = jnp.pad(bias, ((0, 0), (0, 0), (0, KEY_PAD - bias.shape[2])), constant_values=NEG_INF)
    bias = bias.reshape(n_kv, n_heads // n_kv // 2, 2, CHUNK, KEY_PAD)
    return jnp.transpose(bias, (0, 1, 3, 2, 4)).reshape(n_kv, (n_heads // n_kv // 2) * CHUNK, 2 * KEY_PAD)


def kernel(x_prompt, x_sample, cache_conv, cache_k, cache_v, rel_bias_table, norm_mix_g, w_in, conv_w, w_conv_out, attn_sinks, w_attn_out, w_o, norm_ffn_g, w_group, b_group, w_expert_router, b_expert_router, w_gate, w_up, w_down, final_norm_g):
    assert w_in.shape[0] == 1, "single-layer step"
    batch, seq, d = x_prompt.shape
    dec_batch, dec_seq, _ = x_sample.shape
    assert dec_seq == CHUNK and seq % MIX_ROWS == 0 and (dec_batch * dec_seq) % MIX_ROWS == 0
    d_conv = conv_w.shape[-1]
    window, n_kv, head_dim = cache_k.shape[2], cache_k.shape[3], cache_k.shape[4]
    n_heads = attn_sinks.shape[-1]
    d_attn, d_kv = n_heads * head_dim, n_kv * head_dim
    n_groups, n_exp = w_group.shape[-1], w_expert_router.shape[-1]
    assert n_groups <= GROUP_ROW0 and d_conv == d_attn and 2 * d_conv == d
    assert n_kv == 2 and d_kv == LANES and (n_heads // n_kv) % 2 == 0 and window + CHUNK <= KEY_PAD
    dims = (n_heads, n_kv, head_dim, window, n_groups, n_exp, d_conv)

    w_all = w_in[0].astype(BF16)
    kv0 = 3 * d_conv + d_attn
    g1 = norm_mix_g[0][None, :]
    wr = jnp.zeros((GROUP_ROW0 + n_exp, d), F32)
    wr = wr.at[:n_groups].set(w_group[0].T).at[GROUP_ROW0:].set(w_expert_router[0].T).astype(BF16)
    br = jnp.zeros((GROUP_ROW0 + n_exp, 1), F32)
    br = br.at[:n_groups, 0].set(b_group[0]).at[GROUP_ROW0:, 0].set(b_expert_router[0])
    xp2d = x_prompt.reshape(batch * seq, d)
    xs2d = x_sample.reshape(dec_batch * dec_seq, d)
    n_p, n_s = xp2d.shape[0], xs2d.shape[0]
    n_tok = n_p + n_s

    proj_p, kv_p, (wco, wao, wo) = _inproj(xp2d, g1, w_all, kv0, 2 * d_kv,
                                           side_casts=(w_conv_out[0], w_attn_out[0], w_o[0]))
    proj_s, kv_s, _ = _inproj(xs2d, g1, w_all, kv0, 2 * d_kv)
    weights = (conv_w[0], wco, wao, wo,
               _rel_bias(rel_bias_table, window, n_kv), attn_sinks, norm_ffn_g[0][None, :], wr, br)
    xres_p, h2_p, re_p, rc_p, conv_p, k_p, v_p = _mix(
        xp2d, proj_p, kv_p, None, weights, n_seq=batch, sample=False, dims=dims)
    caches = (cache_conv[0], cache_k[0], cache_v[0])
    xres_s, h2_s, re_s, rc_s, conv_s, k_s, v_s = _mix(
        xs2d, proj_s, kv_s, caches, weights, n_seq=dec_batch, sample=True, dims=dims)

    n_tiles = pl.cdiv((TOP_K * n_tok) // SLOT_ROWS + n_exp, EXPERT_STEP_TILES) * EXPERT_STEP_TILES
    n_slots = n_tiles * SLOT_ROWS
    pos, tmap = _route(jnp.concatenate([re_p, re_s], axis=1), n_exp, n_tiles)
    p0, p1 = pos[0], pos[1]
    xs = _dispatch(h2_p, p0[:n_p], p1[:n_p], None, n_slots)
    xs = _dispatch(h2_s, p0[n_p:], p1[n_p:], xs, n_slots)
    y = _experts(xs, tmap[0, :n_tiles], tmap[1, :n_tiles], tmap[2, :n_tiles], w_gate[0], w_up[0], w_down[0])
    gf = final_norm_g[None, :]
    y_prompt = _combine(xres_p, rc_p, p0[:n_p], p1[:n_p], y, gf).reshape(batch, seq, d)
    y_sample = _combine(xres_s, rc_s, p0[n_p:], p1[n_p:], y, gf).reshape(dec_batch, dec_seq, d)

    return (y_prompt, y_sample, conv_p[None], k_p[None], v_p[None], conv_s[None], k_s[None], v_s[None])
```

```python
import functools
import math

import jax
import jax.numpy as jnp
from jax import lax
from jax.experimental import pallas as pl
from jax.experimental.pallas import tpu as pltpu

F32, BF16, I32, U32 = jnp.float32, jnp.bfloat16, jnp.int32, jnp.uint32

CHUNK = 64
N_BUCKETS = 32
MAX_DISTANCE = 128
EPS = 1e-6
NEG_INF = -1e30
TOP_K = 2

V7X_VMEM_BYTES = 64 * 1024 * 1024
SUBLANES = 8
LANES = 128

INPROJ_ROWS = 1024
INPROJ_COLS = 2048
MIX_ROWS = 256
MOVE_ROWS = 512
SLOT_ROWS = 256
EXPERT_STEP_TILES = 4
ROUTE_BLOCK = 512
GROUP_ROW0 = 8
KEY_PAD = 256
KV_VARIANTS = 4


def _vmem_limit(nbytes):
    return int(min(V7X_VMEM_BYTES - (4 << 20), max(nbytes, 32 << 20)))


def _pack_halves(x):
    half = x.shape[1] // 2
    return pltpu.pack_elementwise([x[:, :half], x[:, half:]], packed_dtype=BF16)


def _unpack_halves(w):
    lo = pltpu.unpack_elementwise(w, index=0, packed_dtype=BF16, unpacked_dtype=F32)
    hi = pltpu.unpack_elementwise(w, index=1, packed_dtype=BF16, unpacked_dtype=F32)
    return jnp.concatenate([lo, hi], axis=1)


def _rms_rows(x, g):
    r = lax.rsqrt(jnp.mean(x * x, axis=-1, keepdims=True) + EPS)
    return (x * r) * g


def _inproj_kernel(*refs, cast_steps, n_col_steps, emit_bf16):
    n_cast = len(cast_steps)
    x_ref, g_ref, wm_ref, wkv_ref = refs[:4]
    cast_in = refs[4:4 + n_cast]
    proj_ref, kv_ref = refs[4 + n_cast:6 + n_cast]
    cast_out = refs[6 + n_cast:6 + 2 * n_cast]
    h_sc = refs[-1]
    if emit_bf16:
        wmain_out, wkv_out = refs[6 + 2 * n_cast:8 + 2 * n_cast]

        @pl.when(pl.program_id(0) == 0)
        def _():
            wmain_out[...] = wm_ref[...].astype(BF16)

    step = pl.program_id(0) * n_col_steps + pl.program_id(1)
    for (s0, s1), src, dst in zip(cast_steps, cast_in, cast_out):
        @pl.when(jnp.logical_and(step >= s0, step < s1))
        def _(src=src, dst=dst):
            dst[...] = src[...].astype(BF16)

    @pl.when(pl.program_id(1) == 0)
    def _():
        rows = 128
        w_kv = wkv_ref[...].astype(BF16)
        for i in range(x_ref.shape[0] // rows):
            r = slice(i * rows, (i + 1) * rows)
            h = _rms_rows(x_ref[r, :], g_ref[...]).astype(BF16)
            h_sc[r, :] = h
            kv_ref[r, :] = jnp.dot(h, w_kv, preferred_element_type=F32)
        if emit_bf16:
            wkv_out[...] = w_kv

    proj_ref[...] = jnp.dot(h_sc[...], wm_ref[...].astype(BF16), preferred_element_type=F32).astype(BF16)


def _inproj(x2d, g, w, kv0, n_kv, side_casts=(), emit_bf16=False):
    n, d = x2d.shape
    tm = min(INPROJ_ROWS // 2 if emit_bf16 else INPROJ_ROWS, n)
    tn = INPROJ_COLS // 2 if emit_bf16 else INPROJ_COLS
    n_main = (w.shape[1] - n_kv) if emit_bf16 else w[0].shape[1]
    assert kv0 % tn == 0 and n_main % tn == 0 and n_kv % LANES == 0
    if emit_bf16:
        w_args = (w, w)
        w_specs = [
            pl.BlockSpec((pl.Element(d), pl.Element(tn)),
                         lambda i, j: (0, pl.multiple_of(jnp.where(j * tn < kv0, j * tn, j * tn + n_kv), LANES))),
            pl.BlockSpec((pl.Element(d), pl.Element(n_kv)), lambda i, j: (0, kv0)),
        ]
        last_col = n_main // tn - 1
        w_out_specs = [pl.BlockSpec((d, tn), lambda i, j: (0, jnp.where(i == 0, j, last_col))),
                       pl.BlockSpec((d, n_kv), lambda i, j: (0, 0))]
        w_out_shapes = [jax.ShapeDtypeStruct((d, n_main), BF16), jax.ShapeDtypeStruct((d, n_kv), BF16)]
    else:
        w_args = tuple(w)
        w_specs = [pl.BlockSpec((pl.Element(d), pl.Element(tn)), lambda i, j: (0, pl.multiple_of(j * tn, LANES))),
                   pl.BlockSpec((pl.Element(d), pl.Element(n_kv)), lambda i, j: (0, 0))]
        w_out_specs, w_out_shapes = [], []
    grid = (n // tm, n_main // tn)
    n_steps = grid[0] * grid[1]
    cast_rows = sum(c.shape[0] for c in side_casts) // n_steps if side_casts else 0
    cast_steps, cast_specs, step0 = [], [], 0
    for c in side_casts:
        assert cast_rows % (2 * SUBLANES) == 0 and c.shape[0] % cast_rows == 0
        n_blk = c.shape[0] // cast_rows
        cast_steps.append((step0, step0 + n_blk))
        cast_specs.append(pl.BlockSpec(
            (cast_rows, c.shape[1]),
            lambda i, j, s0=step0, nb=n_blk: (jnp.clip(i * grid[1] + j - s0, 0, nb - 1), 0)))
        step0 += n_blk
    assert step0 == (n_steps if side_casts else 0)
    w_bytes = 4 if emit_bf16 else 2
    est = (2 * tm * d * 4 + tm * d * 2 + 2 * d * (tn + n_kv) * w_bytes + 2 * tm * tn * 2 + 2 * tm * n_kv * 4
           + tm * tn * 4 + sum(2 * cast_rows * c.shape[1] * 6 for c in side_casts)
           + (2 * d * (tn + n_kv) * 2 + d * tn * 2 if emit_bf16 else 0))
    n_cast = len(side_casts)
    outs = pl.pallas_call(
        functools.partial(_inproj_kernel, cast_steps=tuple(cast_steps), n_col_steps=grid[1], emit_bf16=emit_bf16),
        grid=grid,
        in_specs=[
            pl.BlockSpec((tm, d), lambda i, j: (i, 0)),
            pl.BlockSpec((1, d), lambda i, j: (0, 0)),
        ] + w_specs + cast_specs,
        out_specs=[
            pl.BlockSpec((tm, tn), lambda i, j: (i, j)),
            pl.BlockSpec((tm, n_kv), lambda i, j: (i, 0)),
        ] + cast_specs + w_out_specs,
        out_shape=[jax.ShapeDtypeStruct((n, n_main), BF16), jax.ShapeDtypeStruct((n, n_kv), F32)]
        + [jax.ShapeDtypeStruct(c.shape, BF16) for c in side_casts] + w_out_shapes,
        scratch_shapes=[pltpu.VMEM((tm, d), BF16)],
        compiler_params=pltpu.CompilerParams(
            dimension_semantics=("arbitrary", "arbitrary"), vmem_limit_bytes=_vmem_limit(est + (8 << 20))),
        name="inproj",
    )(x2d, g, *w_args, *side_casts)
    return outs[0], outs[1], tuple(outs[2:2 + n_cast]), tuple(outs[2 + n_cast:])


def _conv_rows(u, prev2, prev1, w):
    row = lax.broadcasted_iota(I32, u.shape, 0)
    u1 = jnp.where(row == 0, prev1, pltpu.roll(u, 1, axis=0))
    u2 = jnp.where(row == 0, prev2, jnp.where(row == 1, prev1, pltpu.roll(u, 2, axis=0)))
    return (w[0:1] * u2 + w[1:2] * u1) + w[2:3] * u


def _mix_kernel(*refs, sample, n_heads, n_kv, head_dim, window, n_groups, n_exp):
    if sample:
        (x_ref, ga_ref, gb_ref, b_ref, c_ref, xc_ref, q_ref, kv_ref, cconv_ref, ck_ref, cv_ref,
         convw_ref, wco_ref, wao_ref, wo_ref, bias_ref, sinks_ref, g2_ref, wr_ref, br_ref,
         xp_ref, h2_ref, re_ref, rc_ref, sconv_ref, sk_ref, sv_ref,
         kvar, o_sc, ya_sc, carry_u, s_sc, e_sc, sink_sc) = refs
    else:
        (x_ref, ga_ref, gb_ref, b_ref, c_ref, xc_ref, q_ref, kv_ref,
         convw_ref, wco_ref, wao_ref, wo_ref, bias_ref, sinks_ref, g2_ref, wr_ref, br_ref,
         xp_ref, h2_ref, re_ref, rc_ref, sconv_ref, sk_ref, sv_ref,
         kvar, o_sc, ya_sc, carry_u, s_sc, e_sc, sink_sc) = refs
    t_rows, d_model = x_ref.shape
    n_chunks = t_rows // CHUNK
    span = window + CHUNK
    q_per_kv = n_heads // n_kv
    d_kv = n_kv * head_dim
    scale = 1.0 / math.sqrt(head_dim)
    scale_is_pow2 = math.frexp(scale)[0] == 0.5
    epg = n_exp // n_groups
    seq_start = pl.program_id(1) == 0

    def conv_branch():
        w_conv = convw_ref[...]
        if sample:
            for s in range(n_chunks):
                rows = slice(s * CHUNK, (s + 1) * CHUNK)
                u = c_ref[rows, :].astype(F32) * xc_ref[rows, :].astype(F32)
                y = _conv_rows(u, cconv_ref[s, 0:1, :], cconv_ref[s, 1:2, :], w_conv)
                ya_sc[rows, :] = (b_ref[rows, :].astype(F32) * y).astype(BF16)
                sconv_ref[s] = u[CHUNK - 2:CHUNK, :]
        else:
            u = c_ref[...].astype(F32) * xc_ref[...].astype(F32)
            prev = jnp.where(seq_start, 0.0, carry_u[...])
            y = _conv_rows(u, prev[SUBLANES - 2:SUBLANES - 1], prev[SUBLANES - 1:SUBLANES], w_conv)
            ya_sc[...] = (b_ref[...].astype(F32) * y).astype(BF16)
            carry_u[...] = u[t_rows - SUBLANES:t_rows, :]
            sconv_ref[0] = u[t_rows - 2:t_rows, :]

    def store_kv(row0, k_rows, v_rows):
        n_rows = k_rows.shape[0]
        low = lax.broadcasted_iota(I32, k_rows.shape, 1) < head_dim
        for base, a in ((0, k_rows), (KV_VARIANTS, v_rows)):
            b = pltpu.roll(a, head_dim, axis=1)
            kvar[base + 0, row0:row0 + n_rows, :] = jnp.where(low, a, 0.0).astype(BF16)
            kvar[base + 1, row0:row0 + n_rows, :] = jnp.where(low, 0.0, a).astype(BF16)
            kvar[base + 2, row0:row0 + n_rows, :] = jnp.where(low, b, 0.0).astype(BF16)
            kvar[base + 3, row0:row0 + n_rows, :] = jnp.where(low, 0.0, b).astype(BF16)

    if sample:
        for s in range(n_chunks):
            rows = slice(s * CHUNK, (s + 1) * CHUNK)
            store_kv(s * span, ck_ref[s], cv_ref[s])
            store_kv(s * span + window, kv_ref[rows, 0:d_kv], kv_ref[rows, d_kv:2 * d_kv])
            sk_ref[s, 0:window - CHUNK, :] = ck_ref[s, CHUNK:window, :]
            sv_ref[s, 0:window - CHUNK, :] = cv_ref[s, CHUNK:window, :]
            sk_ref[s, window - CHUNK:window, :] = kv_ref[rows, 0:d_kv]
            sv_ref[s, window - CHUNK:window, :] = kv_ref[rows, d_kv:2 * d_kv]
        k_stride = span
    else:
        @pl.when(seq_start)
        def _():
            kvar[:, 0:window, :] = jnp.zeros((2 * KV_VARIANTS, window, d_kv), BF16)
        store_kv(window, kv_ref[:, 0:d_kv], kv_ref[:, d_kv:2 * d_kv])
        sk_ref[0] = kv_ref[t_rows - window:t_rows, 0:d_kv]
        sv_ref[0] = kv_ref[t_rows - window:t_rows, d_kv:2 * d_kv]
        k_stride = CHUNK

    n_pairs = q_per_kv // 2
    key_pad_rows = jnp.zeros((KEY_PAD - span, d_kv), BF16)
    ones_d = ((lax.broadcasted_iota(I32, (2 * KEY_PAD, LANES), 0) < KEY_PAD)
              == (lax.broadcasted_iota(I32, (2 * KEY_PAD, LANES), 1) < head_dim)).astype(BF16)
    low_half = lax.broadcasted_iota(I32, (CHUNK, LANES), 1) < head_dim
    ya_parts = []
    ya_cols = d_model // (n_chunks * n_kv)
    blocks = [(c, n) for c in range(n_chunks) for n in range(n_kv)]

    def key_rows(c):
        return slice(c * k_stride, c * k_stride + span)

    for b, (c, n) in enumerate(blocks):
        q_rows = slice(c * CHUNK, (c + 1) * CHUNK)
        top, bot = (0, 3) if n == 0 else (2, 1)
        kd = jnp.concatenate([kvar[top, key_rows(c), :], key_pad_rows, kvar[bot, key_rows(c), :], key_pad_rows],
                             axis=0)
        q4 = jnp.concatenate(
            [q_ref[q_rows, (n * n_pairs + j) * LANES:(n * n_pairs + j + 1) * LANES] for j in range(n_pairs)], axis=0)
        if scale_is_pow2:
            q4 = q4 * scale
        s4 = lax.dot_general(q4, kd, (((1,), (1,)), ((), ())), preferred_element_type=F32)
        s_sc[b] = s4 if scale_is_pow2 else s4 * scale
    conv_branch()
    for b, (c, n) in enumerate(blocks):
        masked = (not sample) and c * CHUNK < window
        if masked:
            first_key = (pl.program_id(1) * n_chunks + c) * CHUNK - window
            valid = lax.broadcasted_iota(I32, (CHUNK, KEY_PAD), 1) + first_key >= 0
        for j in range(n_pairs):
            sink_halves = []
            for half in range(2):
                blk = (slice(j * CHUNK, (j + 1) * CHUNK), slice(half * KEY_PAD, (half + 1) * KEY_PAD))
                s = s_sc[b, blk[0], blk[1]] + bias_ref[n, blk[0], blk[1]]
                if masked:
                    s = jnp.where(valid, s, NEG_INF)
                sink = sinks_ref[0, n * q_per_kv + 2 * j + half]
                mx = jnp.maximum(jnp.max(s, axis=-1, keepdims=True), sink)
                e_sc[b, blk[0], blk[1]] = jnp.exp(s - mx).astype(BF16)
                sink_halves.append(jnp.exp(sink - mx))
            sink_sc[b, j * CHUNK:(j + 1) * CHUNK, :] = jnp.where(low_half, sink_halves[0], sink_halves[1])
        ya_parts.append(jnp.dot(ya_sc[...], wco_ref[:, b * ya_cols:(b + 1) * ya_cols], preferred_element_type=F32))
    for b, (c, n) in enumerate(blocks):
        q_rows = slice(c * CHUNK, (c + 1) * CHUNK)
        top, bot = (0, 3) if n == 0 else (2, 1)
        vd = jnp.concatenate([kvar[KV_VARIANTS + top, key_rows(c), :], key_pad_rows,
                              kvar[KV_VARIANTS + bot, key_rows(c), :], key_pad_rows], axis=0)
        od = jnp.dot(e_sc[b], jnp.concatenate([vd, ones_d], axis=1), preferred_element_type=F32)
        o4 = od[:, 0:LANES] / (od[:, LANES:2 * LANES] + sink_sc[b])
        for j in range(n_pairs):
            o_sc[q_rows, (n * n_pairs + j) * LANES:(n * n_pairs + j + 1) * LANES] = (
                o4[j * CHUNK:(j + 1) * CHUNK].astype(BF16))
    if not sample:
        kvar[:, 0:window, :] = kvar[:, t_rows:t_rows + window, :]
    y_a = jnp.concatenate(ya_parts, axis=1)
    y_b = jnp.dot(o_sc[...], wao_ref[...], preferred_element_type=F32)

    m = jax.nn.sigmoid(ga_ref[...].astype(F32)) * y_a + jax.nn.sigmoid(gb_ref[...].astype(F32)) * y_b
    xp = x_ref[...] + jnp.dot(m.astype(BF16), wo_ref[...], preferred_element_type=F32)
    xp_ref[...] = xp

    h2 = _rms_rows(xp, g2_ref[...])
    h2_ref[...] = _pack_halves(h2).reshape(t_rows, 1, d_model // 2)
    lt = lax.dot_general(wr_ref[...], h2.astype(BF16), (((1,), (1,)), ((), ())),
                         preferred_element_type=F32) + br_ref[...]
    lg = lt[0:n_groups]
    eg = jnp.exp(lg - jnp.max(lg, axis=0, keepdims=True))
    gp = eg / jnp.sum(eg, axis=0, keepdims=True)
    gw = jnp.max(gp, axis=0, keepdims=True)
    gi = lax.broadcasted_iota(I32, gp.shape, 0).astype(F32)
    gsel = jnp.min(jnp.where(gp == gw, gi, float(n_groups)), axis=0, keepdims=True)
    el = jnp.zeros((epg, t_rows), F32)
    for g in range(n_groups):
        el = jnp.where(gsel == float(g), lt[GROUP_ROW0 + g * epg:GROUP_ROW0 + (g + 1) * epg], el)
    ei = lax.broadcasted_iota(I32, el.shape, 0).astype(F32)
    v1 = jnp.max(el, axis=0, keepdims=True)
    i1 = jnp.min(jnp.where(el == v1, ei, float(epg)), axis=0, keepdims=True)
    el2 = jnp.where(ei == i1, -jnp.inf, el)
    v2 = jnp.max(el2, axis=0, keepdims=True)
    i2 = jnp.min(jnp.where(el2 == v2, ei, float(epg)), axis=0, keepdims=True)
    a1 = jnp.exp(v1 - v1)
    a2 = jnp.exp(v2 - v1)
    den = a1 + a2
    c1 = gw * (a1 / den)
    c2 = gw * (a2 / den)
    e1 = (gsel * float(epg) + i1).astype(I32)
    e2 = (gsel * float(epg) + i2).astype(I32)
    row8 = lax.broadcasted_iota(I32, (SUBLANES, t_rows), 0)
    re_ref[...] = jnp.where(row8 == 0, e1, jnp.where(row8 == 1, e2, 0))
    rc_ref[...] = jnp.where(row8 == 0, c1, jnp.where(row8 == 1, c2, 0.0))


def _mix(x2d, proj, kv, caches, weights, *, n_seq, sample, dims):
    n_heads, n_kv, head_dim, window, n_groups, n_exp, d_conv = dims
    n, d = x2d.shape
    t = MIX_ROWS
    n_chunks = t // CHUNK
    span = window + CHUNK
    d_attn = n_heads * head_dim
    d_kv = n_kv * head_dim
    convw, wco, wao, wo, bias, sinks, g2, wr, br = weights
    if sample:
        n_t = n // t
        grid = (n_t, 1)
        tok = lambda i, j: (i, 0)
        n_state = n // CHUNK
        state_blk = n_chunks
        st = lambda i, j: (i, 0, 0)
    else:
        n_t = (n // n_seq) // t
        grid = (n_seq, n_t)
        tok = lambda i, j: (i * n_t + j, 0)
        n_state = n_seq
        state_blk = 1
        st = lambda i, j: (i, 0, 0)
    const2 = lambda i, j: (0, 0)

    def col(width, idx):
        return pl.BlockSpec((t, width), lambda i, j: (tok(i, j)[0], idx))

    def resident(shape):
        return pl.BlockSpec(shape, const2, pipeline_mode=pl.Buffered(1))

    in_specs = [
        pl.BlockSpec((t, d), tok),
        col(d, 2), col(d, 3),
        col(d_conv, 0), col(d_conv, 1), col(d_conv, 2), col(d_attn, 3),
        pl.BlockSpec((t, 2 * d_kv), tok),
    ]
    args = [x2d, proj, proj, proj, proj, proj, proj, kv]
    if sample:
        cconv, ck, cv = caches
        in_specs += [
            pl.BlockSpec((n_chunks, cconv.shape[1], d_conv), st),
            pl.BlockSpec((n_chunks, window, d_kv), st),
            pl.BlockSpec((n_chunks, window, d_kv), st),
        ]
        args += [cconv, ck, cv]
    in_specs += [
        resident(convw.shape), resident(wco.shape), resident(wao.shape), resident(wo.shape),
        pl.BlockSpec(bias.shape, lambda i, j: (0, 0, 0), pipeline_mode=pl.Buffered(1)),
        pl.BlockSpec(memory_space=pltpu.SMEM),
        resident(g2.shape), resident(wr.shape), resident(br.shape),
    ]
    args += [convw, wco, wao, wo, bias, sinks, g2, wr, br]
    out_specs = [
        pl.BlockSpec((t, d), tok),
        pl.BlockSpec((t, 1, d // 2), lambda i, j: (tok(i, j)[0], 0, 0)),
        pl.BlockSpec((SUBLANES, t), lambda i, j: (0, tok(i, j)[0])),
        pl.BlockSpec((SUBLANES, t), lambda i, j: (0, tok(i, j)[0])),
        pl.BlockSpec((state_blk, 2, d_conv), st),
        pl.BlockSpec((state_blk, window, d_kv), st),
        pl.BlockSpec((state_blk, window, d_kv), st),
    ]
    out_shape = [
        jax.ShapeDtypeStruct((n, d), F32),
        jax.ShapeDtypeStruct((n, 1, d // 2), U32),
        jax.ShapeDtypeStruct((SUBLANES, n), I32),
        jax.ShapeDtypeStruct((SUBLANES, n), F32),
        jax.ShapeDtypeStruct((n_state, 2, d_conv), F32),
        jax.ShapeDtypeStruct((n_state, window, d_kv), F32),
        jax.ShapeDtypeStruct((n_state, window, d_kv), F32),
    ]
    kv_rows = n_chunks * span if sample else window + t
    scratch = [
        pltpu.VMEM((2 * KV_VARIANTS, kv_rows, d_kv), BF16),
        pltpu.VMEM((t, d_attn), BF16), pltpu.VMEM((t, d_conv), BF16),
        pltpu.VMEM((SUBLANES, d_conv), F32),
        pltpu.VMEM((n_chunks * n_kv, (n_heads // n_kv // 2) * CHUNK, 2 * KEY_PAD), F32),
        pltpu.VMEM((n_chunks * n_kv, (n_heads // n_kv // 2) * CHUNK, 2 * KEY_PAD), BF16),
        pltpu.VMEM((n_chunks * n_kv, (n_heads // n_kv // 2) * CHUNK, LANES), F32),
    ]
    est = (2 * t * d * 4 * 3 + 2 * 2 * t * d * 2 + 2 * 4 * t * d_conv * 2
           + (2 * d_conv * d + d * d) * 2 + 12 * t * d * 4)
    kern = functools.partial(_mix_kernel, sample=sample, n_heads=n_heads, n_kv=n_kv, head_dim=head_dim,
                             window=window, n_groups=n_groups, n_exp=n_exp)
    return pl.pallas_call(
        kern, grid=grid, in_specs=in_specs, out_specs=out_specs, out_shape=out_shape,
        scratch_shapes=scratch,
        compiler_params=pltpu.CompilerParams(
            dimension_semantics=("arbitrary", "arbitrary"), vmem_limit_bytes=_vmem_limit(est)),
        name="mix_sample" if sample else "mix_prompt",
    )(*args)


def _route_kernel(re_ref, pos_ref, tmap_ref, *, n_exp, tile_rows, blk):
    n = re_ref.shape[1]
    n_blk = n // blk
    erow = lax.broadcasted_iota(I32, (n_exp, blk), 0)

    def onehots(j):
        c0 = pl.multiple_of(j * blk, blk)
        oh0 = (erow == re_ref[0:1, pl.ds(c0, blk)]).astype(F32)
        oh1 = (erow == re_ref[1:2, pl.ds(c0, blk)]).astype(F32)
        return c0, oh0, oh1

    def count_body(j, cnt):
        _, oh0, oh1 = onehots(j)
        return cnt + jnp.sum(oh0 + oh1, axis=1, keepdims=True)

    cnt = lax.fori_loop(0, n_blk, count_body, jnp.zeros((n_exp, 1), F32))
    cnt = jnp.broadcast_to(cnt, (n_exp, LANES))
    padded = jnp.ceil(cnt / tile_rows) * tile_rows
    ends = padded
    prow = lax.broadcasted_iota(I32, ends.shape, 0)
    step = 1
    while step < n_exp:
        ends = ends + jnp.where(prow >= step, pltpu.roll(ends, step, axis=0), 0.0)
        step *= 2
    offs = ends - padded
    off1 = offs[:, 0:1]

    tri = (lax.broadcasted_iota(I32, (blk, blk), 0) <= lax.broadcasted_iota(I32, (blk, blk), 1)).astype(BF16)
    row8 = lax.broadcasted_iota(I32, (SUBLANES, blk), 0)

    def pos_body(j, run):
        c0, oh0, oh1 = onehots(j)
        both = oh0 + oh1
        csum = jnp.dot(both.astype(BF16), tri, preferred_element_type=F32) + run
        slot = off1 + csum - 1.0
        p0 = jnp.sum(oh0 * slot, axis=0, keepdims=True).astype(I32)
        p1 = jnp.sum(oh1 * slot, axis=0, keepdims=True).astype(I32)
        pos_ref[:, pl.ds(c0, blk)] = jnp.where(row8 == 0, p0, jnp.where(row8 == 1, p1, 0))
        return run + jnp.sum(both, axis=1, keepdims=True)

    lax.fori_loop(0, n_blk, pos_body, jnp.zeros((n_exp, 1), F32))

    n_tiles_pad = tmap_ref.shape[1]
    start = (lax.broadcasted_iota(I32, (1, n_tiles_pad), 1) * tile_rows).astype(F32)
    end1 = ends[:, 0:1]
    te = jnp.sum((end1 <= start).astype(F32), axis=0, keepdims=True)
    trow = lax.broadcasted_iota(I32, (n_exp, n_tiles_pad), 0).astype(F32)
    used_end = jnp.sum(jnp.where(trow == te, off1 + cnt[:, 0:1], 0.0), axis=0, keepdims=True)
    n_rows = jnp.clip(used_end - start, 0.0, float(tile_rows))
    has_tokens = cnt[:, 0:1] > 0.0
    te = jnp.minimum(te, jnp.max(jnp.where(has_tokens, trow, 0.0), axis=0, keepdims=True))
    nxt = jnp.min(jnp.where(trow > te, jnp.where(has_tokens, trow, float(n_exp)), float(n_exp)), axis=0, keepdims=True)
    r8 = lax.broadcasted_iota(I32, (SUBLANES, n_tiles_pad), 0)
    tmap_ref[...] = jnp.where(r8 == 0, te.astype(I32),
                              jnp.where(r8 == 1, n_rows.astype(I32), jnp.where(r8 == 2, nxt.astype(I32), 0)))


def _route(re, n_exp, n_tiles):
    n = re.shape[1]
    assert n % ROUTE_BLOCK == 0
    n_tiles_pad = pl.cdiv(n_tiles, LANES) * LANES
    kern = functools.partial(_route_kernel, n_exp=n_exp, tile_rows=SLOT_ROWS, blk=ROUTE_BLOCK)
    return pl.pallas_call(
        kern,
        out_shape=[jax.ShapeDtypeStruct((SUBLANES, n), I32), jax.ShapeDtypeStruct((SUBLANES, n_tiles_pad), I32)],
        name="route",
    )(re)


def _dispatch_kernel(*refs, aliased):
    if aliased:
        p0_ref, p1_ref, h_ref, _, xs_ref, sem = refs
    else:
        p0_ref, p1_ref, h_ref, xs_ref, sem = refs
    rows = h_ref.shape[0]

    def start(t, carry):
        pltpu.make_async_copy(h_ref.at[t], xs_ref.at[p0_ref[t]], sem).start(priority=0)
        pltpu.make_async_copy(h_ref.at[t], xs_ref.at[p1_ref[t]], sem).start(priority=1)
        return carry

    lax.fori_loop(0, rows, start, 0)
    for _ in range(TOP_K):
        pltpu.make_async_copy(h_ref, xs_ref.at[pl.ds(0, rows)], sem).wait()


def _dispatch(h2, p0, p1, xs, n_slots):
    n, _, d = h2.shape
    t = min(MOVE_ROWS, n)
    assert n % t == 0
    aliased = xs is not None
    in_specs = [
        pl.BlockSpec((t,), lambda i: (i,), memory_space=pltpu.SMEM),
        pl.BlockSpec((t,), lambda i: (i,), memory_space=pltpu.SMEM),
        pl.BlockSpec((t, 1, d), lambda i: (i, 0, 0)),
    ]
    args = [p0, p1, h2]
    if aliased:
        in_specs.append(pl.BlockSpec(memory_space=pl.ANY))
        args.append(xs)
    return pl.pallas_call(
        functools.partial(_dispatch_kernel, aliased=aliased),
        grid=(n // t,),
        in_specs=in_specs,
        out_specs=pl.BlockSpec(memory_space=pl.ANY),
        out_shape=jax.ShapeDtypeStruct((n_slots, 1, d), h2.dtype),
        scratch_shapes=[pltpu.SemaphoreType.DMA(())],
        input_output_aliases={3: 0} if aliased else {},
        compiler_params=pltpu.CompilerParams(dimension_semantics=("arbitrary",), has_side_effects=True),
        name="dispatch",
    )(*args)


def _experts_kernel(te_ref, nr_ref, nx_ref, xs_ref, wg_hbm, wu_hbm, wd_hbm, y_ref,
                    wg_st, wu_st, wd_st, wg_sc, wu_sc, wd_sc, sems, n_changes, *, n_exp, tile_rows):
    dh = xs_ref.shape[2]
    tiles_per_step = xs_ref.shape[0] // tile_rows

    def copies(e, slot):
        return (pltpu.make_async_copy(wg_hbm.at[e], wg_st.at[slot], sems.at[slot]),
                pltpu.make_async_copy(wu_hbm.at[e], wu_st.at[slot], sems.at[slot]),
                pltpu.make_async_copy(wd_hbm.at[e], wd_st.at[slot], sems.at[slot]))

    def one_tile(sub):
        tile = pl.program_id(0) * tiles_per_step + sub
        rows = pl.ds(sub * tile_rows, tile_rows)
        expert = te_ref[tile]

        @pl.when(tile == 0)
        def _():
            n_changes[0] = 0
            for cp in copies(expert, 0):
                cp.start(priority=1)

        @pl.when(jnp.logical_or(tile == 0, expert != te_ref[jnp.maximum(tile - 1, 0)]))
        def _():
            slot = lax.rem(n_changes[0], 2)
            n_changes[0] = n_changes[0] + 1
            for cp in copies(expert, slot):
                cp.wait()
            nxt = nx_ref[tile]

            @pl.when(nxt < n_exp)
            def _():
                for cp in copies(nxt, 1 - slot):
                    cp.start(priority=1)

            wg_sc[...] = wg_st[slot].astype(BF16)
            wu_sc[...] = wu_st[slot].astype(BF16)
            wd_sc[...] = wd_st[slot].astype(BF16)

        n_rows = nr_ref[tile]

        @pl.when(n_rows > 0)
        def _():
            live = lax.broadcasted_iota(I32, (tile_rows, 2 * dh), 0) < n_rows
            x = jnp.where(live, _unpack_halves(xs_ref[rows].reshape(tile_rows, dh)), 0.0).astype(BF16)
            gate = jnp.dot(x, wg_sc[...], preferred_element_type=F32)
            up = jnp.dot(x, wu_sc[...], preferred_element_type=F32)
            hid = (jax.nn.silu(gate) * up).astype(BF16)
            y = jnp.dot(hid, wd_sc[...], preferred_element_type=F32)
            y_ref[rows] = _pack_halves(y).reshape(tile_rows, 1, dh)

        @pl.when(n_rows <= 0)
        def _():
            y_ref[rows] = _pack_halves(jnp.zeros((tile_rows, 2 * dh), F32)).reshape(tile_rows, 1, dh)

    for sub in range(tiles_per_step):
        one_tile(sub)


def _experts(xs, te, nr, nx, w_gate, w_up, w_down):
    n_slots, _, dh = xs.shape
    n_exp, d, d_e = w_gate.shape
    t = SLOT_ROWS * EXPERT_STEP_TILES
    est = 2 * 2 * t * d * 2 + 2 * 3 * d * d_e * 4 + 3 * d * d_e * 2 + 4 * SLOT_ROWS * d * 4
    row_spec = pl.BlockSpec((t, 1, dh), lambda i, te, nr, nx: (i, 0, 0))
    return pl.pallas_call(
        functools.partial(_experts_kernel, n_exp=n_exp, tile_rows=SLOT_ROWS),
        grid_spec=pltpu.PrefetchScalarGridSpec(
            num_scalar_prefetch=3,
            grid=(n_slots // t,),
            in_specs=[row_spec, pl.BlockSpec(memory_space=pl.ANY), pl.BlockSpec(memory_space=pl.ANY),
                      pl.BlockSpec(memory_space=pl.ANY)],
            out_specs=row_spec,
            scratch_shapes=[
                pltpu.VMEM((2, d, d_e), F32), pltpu.VMEM((2, d, d_e), F32), pltpu.VMEM((2, d_e, d), F32),
                pltpu.VMEM((d, d_e), BF16), pltpu.VMEM((d, d_e), BF16), pltpu.VMEM((d_e, d), BF16),
                pltpu.SemaphoreType.DMA((2,)), pltpu.SMEM((1,), I32)],
        ),
        out_shape=jax.ShapeDtypeStruct((n_slots, 1, dh), U32),
        compiler_params=pltpu.CompilerParams(
            dimension_semantics=("arbitrary",), vmem_limit_bytes=_vmem_limit(est)),
        name="experts",
    )(te, nr, nx, xs, w_gate, w_up, w_down)


def _combine_kernel(p0_ref, p1_ref, p0n_ref, p1n_ref, xp_ref, rc_ref, g_ref, y_hbm, out_ref, y0_buf, y1_buf, sems,
                    *, n_tiles):
    i = pl.program_id(0)
    rows, d = xp_ref.shape

    def gather(pa_ref, pb_ref, slot):
        base = slot * rows
        def body(t, carry):
            pltpu.make_async_copy(y_hbm.at[pa_ref[t]], y0_buf.at[base + t], sems.at[slot]).start(priority=0)
            pltpu.make_async_copy(y_hbm.at[pb_ref[t]], y1_buf.at[base + t], sems.at[slot]).start(priority=1)
            return carry
        lax.fori_loop(0, rows, body, 0)

    slot = lax.rem(i, 2)

    @pl.when(i == 0)
    def _():
        gather(p0_ref, p1_ref, 0)

    @pl.when(i + 1 < n_tiles)
    def _():
        gather(p0n_ref, p1n_ref, 1 - slot)

    cur = pl.ds(pl.multiple_of(slot * rows, rows), rows)
    for buf in (y0_buf, y1_buf):
        pltpu.make_async_copy(y_hbm.at[pl.ds(0, rows)], buf.at[cur], sems.at[slot]).wait()
    ct = rc_ref[...].T
    y0 = _unpack_halves(y0_buf[cur].reshape(rows, d // 2))
    y1 = _unpack_halves(y1_buf[cur].reshape(rows, d // 2))
    moe = ct[:, 0:1] * y0 + ct[:, 1:2] * y1
    out_ref[...] = _rms_rows(xp_ref[...] + moe, g_ref[...])


def _combine(xp, rc, p0, p1, y, g):
    n, d = xp.shape
    t = min(MOVE_ROWS, n)
    assert n % t == 0
    n_t = n // t
    est = 2 * 2 * t * d * 4 + 2 * 2 * t * d * 2 + 8 * t * d * 4
    nxt = lambda i: (jnp.minimum(i + 1, n_t - 1),)
    return pl.pallas_call(
        functools.partial(_combine_kernel, n_tiles=n_t),
        grid=(n_t,),
        in_specs=[
            pl.BlockSpec((t,), lambda i: (i,), memory_space=pltpu.SMEM),
            pl.BlockSpec((t,), lambda i: (i,), memory_space=pltpu.SMEM),
            pl.BlockSpec((t,), nxt, memory_space=pltpu.SMEM),
            pl.BlockSpec((t,), nxt, memory_space=pltpu.SMEM),
            pl.BlockSpec((t, d), lambda i: (i, 0)),
            pl.BlockSpec((SUBLANES, t), lambda i: (0, i)),
            pl.BlockSpec((1, d), lambda i: (0, 0)),
            pl.BlockSpec(memory_space=pl.ANY),
        ],
        out_specs=pl.BlockSpec((t, d), lambda i: (i, 0)),
        out_shape=jax.ShapeDtypeStruct((n, d), F32),
        scratch_shapes=[pltpu.VMEM((2 * t, 1, d // 2), U32), pltpu.VMEM((2 * t, 1, d // 2), U32),
                        pltpu.SemaphoreType.DMA((2,))],
        compiler_params=pltpu.CompilerParams(
            dimension_semantics=("arbitrary",), vmem_limit_bytes=_vmem_limit(est)),
        name="combine",
    )(p0, p1, p0, p1, xp, rc, g, y)


def _rel_buckets(rel):
    nb = N_BUCKETS // 2
    ret = (rel > 0).astype(I32) * nb
    n = jnp.abs(rel)
    max_exact = nb // 2
    nf = jnp.maximum(n, 1).astype(F32)
    large = max_exact + (jnp.log(nf / max_exact) / math.log(MAX_DISTANCE / max_exact)
                         * (nb - max_exact)).astype(I32)
    large = jnp.minimum(large, nb - 1)
    return ret + jnp.where(n < max_exact, n, large)


def _rel_bias(table, window, n_kv):
    n_heads = table.shape[1]
    span = window + CHUNK
    rows = table[_rel_buckets(jnp.arange(-(span - 1), CHUNK, dtype=I32))].astype(F32)
    bias = jnp.stack([rows[CHUNK - 1 - q:CHUNK - 1 - q + span] for q in range(CHUNK)])
    bias = jnp.transpose(bias, (2, 0, 1))
    bias = jnp.pad(bias, ((0, 0), (0, 0), (0, KEY_PAD - bias.shape[2])), constant_values=NEG_INF)
    bias = bias.reshape(n_kv, n_heads // n_kv // 2, 2, CHUNK, KEY_PAD)
    return jnp.transpose(bias, (0, 1, 3, 2, 4)).reshape(n_kv, (n_heads // n_kv // 2) * CHUNK, 2 * KEY_PAD)


def kernel(x_prompt, x_sample, cache_conv, cache_k, cache_v, rel_bias_table, norm_mix_g, w_in, conv_w, w_conv_out, attn_sinks, w_attn_out, w_o, norm_ffn_g, w_group, b_group, w_expert_router, b_expert_router, w_gate, w_up, w_down, final_norm_g):
    assert w_in.shape[0] == 1, "single-layer step"
    batch, seq, d = x_prompt.shape
    dec_batch, dec_seq, _ = x_sample.shape
    assert dec_seq == CHUNK and seq % MIX_ROWS == 0 and (dec_batch * dec_seq) % MIX_ROWS == 0
    d_conv = conv_w.shape[-1]
    window, n_kv, head_dim = cache_k.shape[2], cache_k.shape[3], cache_k.shape[4]
    n_heads = attn_sinks.shape[-1]
    d_attn, d_kv = n_heads * head_dim, n_kv * head_dim
    n_groups, n_exp = w_group.shape[-1], w_expert_router.shape[-1]
    assert n_groups <= GROUP_ROW0 and d_conv == d_attn and 2 * d_conv == d
    assert n_kv == 2 and d_kv == LANES and (n_heads // n_kv) % 2 == 0 and window + CHUNK <= KEY_PAD
    dims = (n_heads, n_kv, head_dim, window, n_groups, n_exp, d_conv)

    kv0 = 3 * d_conv + d_attn
    g1 = norm_mix_g[0][None, :]
    wr = jnp.zeros((GROUP_ROW0 + n_exp, d), F32)
    wr = wr.at[:n_groups].set(w_group[0].T).at[GROUP_ROW0:].set(w_expert_router[0].T).astype(BF16)
    br = jnp.zeros((GROUP_ROW0 + n_exp, 1), F32)
    br = br.at[:n_groups, 0].set(b_group[0]).at[GROUP_ROW0:, 0].set(b_expert_router[0])
    xp2d = x_prompt.reshape(batch * seq, d)
    xs2d = x_sample.reshape(dec_batch * dec_seq, d)
    n_p, n_s = xp2d.shape[0], xs2d.shape[0]
    n_tok = n_p + n_s

    proj_s, kv_s, _, w_bf16 = _inproj(xs2d, g1, w_in[0], kv0, 2 * d_kv, emit_bf16=True)
    proj_p, kv_p, (wco, wao, wo), _ = _inproj(xp2d, g1, w_bf16, kv0, 2 * d_kv,
                                              side_casts=(w_conv_out[0], w_attn_out[0], w_o[0]))
    weights = (conv_w[0], wco, wao, wo,
               _rel_bias(rel_bias_table, window, n_kv), attn_sinks, norm_ffn_g[0][None, :], wr, br)
    xres_p, h2_p, re_p, rc_p, conv_p, k_p, v_p = _mix(
        xp2d, proj_p, kv_p, None, weights, n_seq=batch, sample=False, dims=dims)
    caches = (cache_conv[0], cache_k[0].reshape(dec_batch, window, d_kv), cache_v[0].reshape(dec_batch, window, d_kv))
    xres_s, h2_s, re_s, rc_s, conv_s, k_s, v_s = _mix(
        xs2d, proj_s, kv_s, caches, weights, n_seq=dec_batch, sample=True, dims=dims)

    n_tiles = pl.cdiv((TOP_K * n_tok) // SLOT_ROWS + n_exp, EXPERT_STEP_TILES) * EXPERT_STEP_TILES
    n_slots = n_tiles * SLOT_ROWS
    pos, tmap = _route(jnp.concatenate([re_p, re_s], axis=1), n_exp, n_tiles)
    p0, p1 = pos[0], pos[1]
    xs = _dispatch(h2_p, p0[:n_p], p1[:n_p], None, n_slots)
    xs = _dispatch(h2_s, p0[n_p:], p1[n_p:], xs, n_slots)
    y = _experts(xs, tmap[0, :n_tiles], tmap[1, :n_tiles], tmap[2, :n_tiles], w_gate[0], w_up[0], w_down[0])
    gf = final_norm_g[None, :]
    y_prompt = _combine(xres_p, rc_p, p0[:n_p], p1[:n_p], y, gf).reshape(batch, seq, d)
    y_sample = _combine(xres_s, rc_s, p0[n_p:], p1[n_p:], y, gf).reshape(dec_batch, dec_seq, d)

    kv_shape = (1, -1, window, n_kv, head_dim)
    return (y_prompt, y_sample, conv_p[None], k_p.reshape(kv_shape), v_p.reshape(kv_shape),
            conv_s[None], k_s.reshape(kv_shape), v_s.reshape(kv_shape))
```

```python
import functools
import math

import jax
import jax.numpy as jnp
from jax import lax
from jax.experimental import pallas as pl
from jax.experimental.pallas import tpu as pltpu

F32, BF16, I32, U32 = jnp.float32, jnp.bfloat16, jnp.int32, jnp.uint32

CHUNK = 64
N_BUCKETS = 32
MAX_DISTANCE = 128
EPS = 1e-6
NEG_INF = -1e30
TOP_K = 2

V7X_VMEM_BYTES = 64 * 1024 * 1024
SUBLANES = 8
LANES = 128

INPROJ_ROWS = 1024
INPROJ_COLS = 2048
MIX_ROWS = 256
MOVE_ROWS = 512
SLOT_ROWS = 256
EXPERT_STEP_TILES = 4
ROUTE_BLOCK = 512
GROUP_ROW0 = 8
KEY_PAD = 256
KV_VARIANTS = 4


def _vmem_limit(nbytes):
    return int(min(V7X_VMEM_BYTES - (4 << 20), max(nbytes, 32 << 20)))


def _pack_halves(x):
    half = x.shape[1] // 2
    return pltpu.pack_elementwise([x[:, :half], x[:, half:]], packed_dtype=BF16)


def _unpack_halves(w):
    lo = pltpu.unpack_elementwise(w, index=0, packed_dtype=BF16, unpacked_dtype=F32)
    hi = pltpu.unpack_elementwise(w, index=1, packed_dtype=BF16, unpacked_dtype=F32)
    return jnp.concatenate([lo, hi], axis=1)


def _rms_rows(x, g):
    r = lax.rsqrt(jnp.mean(x * x, axis=-1, keepdims=True) + EPS)
    return (x * r) * g


def _inproj_kernel(*refs, cast_steps, n_col_steps, emit_bf16):
    n_cast = len(cast_steps)
    x_ref, g_ref, wm_ref, wkv_ref = refs[:4]
    cast_in = refs[4:4 + n_cast]
    proj_ref, kv_ref = refs[4 + n_cast:6 + n_cast]
    cast_out = refs[6 + n_cast:6 + 2 * n_cast]
    h_sc = refs[-1]
    if emit_bf16:
        wmain_out, wkv_out = refs[6 + 2 * n_cast:8 + 2 * n_cast]

        @pl.when(pl.program_id(0) == 0)
        def _():
            wmain_out[...] = wm_ref[...].astype(BF16)

    step = pl.program_id(0) * n_col_steps + pl.program_id(1)
    for (s0, s1), src, dst in zip(cast_steps, cast_in, cast_out):
        @pl.when(jnp.logical_and(step >= s0, step < s1))
        def _(src=src, dst=dst):
            dst[...] = src[...].astype(BF16)

    @pl.when(pl.program_id(1) == 0)
    def _():
        rows = 128
        w_kv = wkv_ref[...].astype(BF16)
        for i in range(x_ref.shape[0] // rows):
            r = slice(i * rows, (i + 1) * rows)
            h = _rms_rows(x_ref[r, :], g_ref[...]).astype(BF16)
            h_sc[r, :] = h
            kv_ref[r, :] = jnp.dot(h, w_kv, preferred_element_type=F32)
        if emit_bf16:
            wkv_out[...] = w_kv

    proj_ref[...] = jnp.dot(h_sc[...], wm_ref[...].astype(BF16), preferred_element_type=F32).astype(BF16)


def _inproj(x2d, g, w, kv0, n_kv, side_casts=(), emit_bf16=False):
    n, d = x2d.shape
    tm = min(INPROJ_ROWS, n)
    tn = INPROJ_COLS // 2 if emit_bf16 else INPROJ_COLS
    n_main = (w.shape[1] - n_kv) if emit_bf16 else w[0].shape[1]
    assert kv0 % tn == 0 and n_main % tn == 0 and n_kv % LANES == 0
    if emit_bf16:
        w_args = (w, w)
        w_specs = [
            pl.BlockSpec((pl.Element(d), pl.Element(tn)),
                         lambda i, j: (0, pl.multiple_of(jnp.where(j * tn < kv0, j * tn, j * tn + n_kv), LANES))),
            pl.BlockSpec((pl.Element(d), pl.Element(n_kv)), lambda i, j: (0, kv0)),
        ]
        last_col = n_main // tn - 1
        w_out_specs = [pl.BlockSpec((d, tn), lambda i, j: (0, jnp.where(i == 0, j, last_col))),
                       pl.BlockSpec((d, n_kv), lambda i, j: (0, 0))]
        w_out_shapes = [jax.ShapeDtypeStruct((d, n_main), BF16), jax.ShapeDtypeStruct((d, n_kv), BF16)]
    else:
        w_args = tuple(w)
        w_specs = [pl.BlockSpec((pl.Element(d), pl.Element(tn)), lambda i, j: (0, pl.multiple_of(j * tn, LANES))),
                   pl.BlockSpec((pl.Element(d), pl.Element(n_kv)), lambda i, j: (0, 0))]
        w_out_specs, w_out_shapes = [], []
    grid = (n // tm, n_main // tn)
    n_steps = grid[0] * grid[1]
    cast_rows = sum(c.shape[0] for c in side_casts) // n_steps if side_casts else 0
    cast_steps, cast_specs, step0 = [], [], 0
    for c in side_casts:
        assert cast_rows % (2 * SUBLANES) == 0 and c.shape[0] % cast_rows == 0
        n_blk = c.shape[0] // cast_rows
        cast_steps.append((step0, step0 + n_blk))
        cast_specs.append(pl.BlockSpec(
            (cast_rows, c.shape[1]),
            lambda i, j, s0=step0, nb=n_blk: (jnp.clip(i * grid[1] + j - s0, 0, nb - 1), 0)))
        step0 += n_blk
    assert step0 == (n_steps if side_casts else 0)
    w_bytes = 4 if emit_bf16 else 2
    est = (2 * tm * d * 4 + tm * d * 2 + 2 * d * (tn + n_kv) * w_bytes + 2 * tm * tn * 2 + 2 * tm * n_kv * 4
           + tm * tn * 4 + sum(2 * cast_rows * c.shape[1] * 6 for c in side_casts)
           + (2 * d * (tn + n_kv) * 2 + d * tn * 2 if emit_bf16 else 0))
    n_cast = len(side_casts)
    outs = pl.pallas_call(
        functools.partial(_inproj_kernel, cast_steps=tuple(cast_steps), n_col_steps=grid[1], emit_bf16=emit_bf16),
        grid=grid,
        in_specs=[
            pl.BlockSpec((tm, d), lambda i, j: (i, 0)),
            pl.BlockSpec((1, d), lambda i, j: (0, 0)),
        ] + w_specs + cast_specs,
        out_specs=[
            pl.BlockSpec((tm, tn), lambda i, j: (i, j)),
            pl.BlockSpec((tm, n_kv), lambda i, j: (i, 0)),
        ] + cast_specs + w_out_specs,
        out_shape=[jax.ShapeDtypeStruct((n, n_main), BF16), jax.ShapeDtypeStruct((n, n_kv), F32)]
        + [jax.ShapeDtypeStruct(c.shape, BF16) for c in side_casts] + w_out_shapes,
        scratch_shapes=[pltpu.VMEM((tm, d), BF16)],
        compiler_params=pltpu.CompilerParams(
            dimension_semantics=("arbitrary", "arbitrary"), vmem_limit_bytes=_vmem_limit(est + (8 << 20))),
        name="inproj",
    )(x2d, g, *w_args, *side_casts)
    return outs[0], outs[1], tuple(outs[2:2 + n_cast]), tuple(outs[2 + n_cast:])


def _conv_rows(u, prev2, prev1, w):
    row = lax.broadcasted_iota(I32, u.shape, 0)
    u1 = jnp.where(row == 0, prev1, pltpu.roll(u, 1, axis=0))
    u2 = jnp.where(row == 0, prev2, jnp.where(row == 1, prev1, pltpu.roll(u, 2, axis=0)))
    return (w[0:1] * u2 + w[1:2] * u1) + w[2:3] * u


def _mix_kernel(*refs, sample, n_heads, n_kv, head_dim, window, n_groups, n_exp):
    if sample:
        (x_ref, ga_ref, gb_ref, b_ref, c_ref, xc_ref, q_ref, kv_ref, cconv_ref, ck_ref, cv_ref,
         convw_ref, wco_ref, wao_ref, wo_ref, bias_ref, sinks_ref, g2_ref, wr_ref, br_ref,
         xp_ref, h2_ref, re_ref, rc_ref, sconv_ref, sk_ref, sv_ref,
         kvar, o_sc, ya_sc, carry_u, s_sc, e_sc, sink_sc) = refs
    else:
        (x_ref, ga_ref, gb_ref, b_ref, c_ref, xc_ref, q_ref, kv_ref,
         convw_ref, wco_ref, wao_ref, wo_ref, bias_ref, sinks_ref, g2_ref, wr_ref, br_ref,
         xp_ref, h2_ref, re_ref, rc_ref, sconv_ref, sk_ref, sv_ref,
         kvar, o_sc, ya_sc, carry_u, s_sc, e_sc, sink_sc) = refs
    t_rows, d_model = x_ref.shape
    n_chunks = t_rows // CHUNK
    span = window + CHUNK
    q_per_kv = n_heads // n_kv
    d_kv = n_kv * head_dim
    scale = 1.0 / math.sqrt(head_dim)
    scale_is_pow2 = math.frexp(scale)[0] == 0.5
    epg = n_exp // n_groups
    seq_start = pl.program_id(1) == 0

    def conv_branch():
        w_conv = convw_ref[...]
        if sample:
            for s in range(n_chunks):
                rows = slice(s * CHUNK, (s + 1) * CHUNK)
                u = c_ref[rows, :].astype(F32) * xc_ref[rows, :].astype(F32)
                y = _conv_rows(u, cconv_ref[s, 0:1, :], cconv_ref[s, 1:2, :], w_conv)
                ya_sc[rows, :] = (b_ref[rows, :].astype(F32) * y).astype(BF16)
                sconv_ref[s] = u[CHUNK - 2:CHUNK, :]
        else:
            u = c_ref[...].astype(F32) * xc_ref[...].astype(F32)
            prev = jnp.where(seq_start, 0.0, carry_u[...])
            y = _conv_rows(u, prev[SUBLANES - 2:SUBLANES - 1], prev[SUBLANES - 1:SUBLANES], w_conv)
            ya_sc[...] = (b_ref[...].astype(F32) * y).astype(BF16)
            carry_u[...] = u[t_rows - SUBLANES:t_rows, :]
            sconv_ref[0] = u[t_rows - 2:t_rows, :]

    def store_kv(row0, k_rows, v_rows):
        n_rows = k_rows.shape[0]
        low = lax.broadcasted_iota(I32, k_rows.shape, 1) < head_dim
        for base, a in ((0, k_rows), (KV_VARIANTS, v_rows)):
            b = pltpu.roll(a, head_dim, axis=1)
            kvar[base + 0, row0:row0 + n_rows, :] = jnp.where(low, a, 0.0).astype(BF16)
            kvar[base + 1, row0:row0 + n_rows, :] = jnp.where(low, 0.0, a).astype(BF16)
            kvar[base + 2, row0:row0 + n_rows, :] = jnp.where(low, b, 0.0).astype(BF16)
            kvar[base + 3, row0:row0 + n_rows, :] = jnp.where(low, 0.0, b).astype(BF16)

    if sample:
        for s in range(n_chunks):
            rows = slice(s * CHUNK, (s + 1) * CHUNK)
            store_kv(s * span, ck_ref[s], cv_ref[s])
            store_kv(s * span + window, kv_ref[rows, 0:d_kv], kv_ref[rows, d_kv:2 * d_kv])
            sk_ref[s, 0:window - CHUNK, :] = ck_ref[s, CHUNK:window, :]
            sv_ref[s, 0:window - CHUNK, :] = cv_ref[s, CHUNK:window, :]
            sk_ref[s, window - CHUNK:window, :] = kv_ref[rows, 0:d_kv]
            sv_ref[s, window - CHUNK:window, :] = kv_ref[rows, d_kv:2 * d_kv]
        k_stride = span
    else:
        @pl.when(seq_start)
        def _():
            kvar[:, 0:window, :] = jnp.zeros((2 * KV_VARIANTS, window, d_kv), BF16)
        store_kv(window, kv_ref[:, 0:d_kv], kv_ref[:, d_kv:2 * d_kv])
        sk_ref[0] = kv_ref[t_rows - window:t_rows, 0:d_kv]
        sv_ref[0] = kv_ref[t_rows - window:t_rows, d_kv:2 * d_kv]
        k_stride = CHUNK

    n_pairs = q_per_kv // 2
    key_pad_rows = jnp.zeros((KEY_PAD - span, d_kv), BF16)
    ones_d = ((lax.broadcasted_iota(I32, (2 * KEY_PAD, LANES), 0) < KEY_PAD)
              == (lax.broadcasted_iota(I32, (2 * KEY_PAD, LANES), 1) < head_dim)).astype(BF16)
    low_half = lax.broadcasted_iota(I32, (CHUNK, LANES), 1) < head_dim
    ya_parts = []
    ya_cols = d_model // (n_chunks * n_kv)
    blocks = [(c, n) for c in range(n_chunks) for n in range(n_kv)]

    def key_rows(c):
        return slice(c * k_stride, c * k_stride + span)

    for b, (c, n) in enumerate(blocks):
        q_rows = slice(c * CHUNK, (c + 1) * CHUNK)
        top, bot = (0, 3) if n == 0 else (2, 1)
        kd = jnp.concatenate([kvar[top, key_rows(c), :], key_pad_rows, kvar[bot, key_rows(c), :], key_pad_rows],
                             axis=0)
        q4 = jnp.concatenate(
            [q_ref[q_rows, (n * n_pairs + j) * LANES:(n * n_pairs + j + 1) * LANES] for j in range(n_pairs)], axis=0)
        if scale_is_pow2:
            q4 = q4 * scale
        s4 = lax.dot_general(q4, kd, (((1,), (1,)), ((), ())), preferred_element_type=F32)
        s_sc[b] = s4 if scale_is_pow2 else s4 * scale
    conv_branch()
    for b, (c, n) in enumerate(blocks):
        masked = (not sample) and c * CHUNK < window
        if masked:
            first_key = (pl.program_id(1) * n_chunks + c) * CHUNK - window
            valid = lax.broadcasted_iota(I32, (CHUNK, KEY_PAD), 1) + first_key >= 0
        for j in range(n_pairs):
            sink_halves = []
            for half in range(2):
                blk = (slice(j * CHUNK, (j + 1) * CHUNK), slice(half * KEY_PAD, (half + 1) * KEY_PAD))
                s = s_sc[b, blk[0], blk[1]] + bias_ref[n, blk[0], blk[1]]
                if masked:
                    s = jnp.where(valid, s, NEG_INF)
                sink = sinks_ref[0, n * q_per_kv + 2 * j + half]
                mx = jnp.maximum(jnp.max(s, axis=-1, keepdims=True), sink)
                e_sc[b, blk[0], blk[1]] = jnp.exp(s - mx).astype(BF16)
                sink_halves.append(jnp.exp(sink - mx))
            sink_sc[b, j * CHUNK:(j + 1) * CHUNK, :] = jnp.where(low_half, sink_halves[0], sink_halves[1])
        ya_parts.append(jnp.dot(ya_sc[...], wco_ref[:, b * ya_cols:(b + 1) * ya_cols], preferred_element_type=F32))
    for b, (c, n) in enumerate(blocks):
        q_rows = slice(c * CHUNK, (c + 1) * CHUNK)
        top, bot = (0, 3) if n == 0 else (2, 1)
        vd = jnp.concatenate([kvar[KV_VARIANTS + top, key_rows(c), :], key_pad_rows,
                              kvar[KV_VARIANTS + bot, key_rows(c), :], key_pad_rows], axis=0)
        od = jnp.dot(e_sc[b], jnp.concatenate([vd, ones_d], axis=1), preferred_element_type=F32)
        o4 = od[:, 0:LANES] / (od[:, LANES:2 * LANES] + sink_sc[b])
        for j in range(n_pairs):
            o_sc[q_rows, (n * n_pairs + j) * LANES:(n * n_pairs + j + 1) * LANES] = (
                o4[j * CHUNK:(j + 1) * CHUNK].astype(BF16))
    if not sample:
        kvar[:, 0:window, :] = kvar[:, t_rows:t_rows + window, :]
    y_a = jnp.concatenate(ya_parts, axis=1)
    y_b = jnp.dot(o_sc[...], wao_ref[...], preferred_element_type=F32)

    m = jax.nn.sigmoid(ga_ref[...].astype(F32)) * y_a + jax.nn.sigmoid(gb_ref[...].astype(F32)) * y_b
    xp = x_ref[...] + jnp.dot(m.astype(BF16), wo_ref[...], preferred_element_type=F32)
    xp_ref[...] = xp

    h2 = _rms_rows(xp, g2_ref[...])
    h2_ref[...] = _pack_halves(h2).reshape(t_rows, 1, d_model // 2)
    lt = lax.dot_general(wr_ref[...], h2.astype(BF16), (((1,), (1,)), ((), ())),
                         preferred_element_type=F32) + br_ref[...]
    lg = lt[0:n_groups]
    eg = jnp.exp(lg - jnp.max(lg, axis=0, keepdims=True))
    gp = eg / jnp.sum(eg, axis=0, keepdims=True)
    gw = jnp.max(gp, axis=0, keepdims=True)
    gi = lax.broadcasted_iota(I32, gp.shape, 0).astype(F32)
    gsel = jnp.min(jnp.where(gp == gw, gi, float(n_groups)), axis=0, keepdims=True)
    el = jnp.zeros((epg, t_rows), F32)
    for g in range(n_groups):
        el = jnp.where(gsel == float(g), lt[GROUP_ROW0 + g * epg:GROUP_ROW0 + (g + 1) * epg], el)
    ei = lax.broadcasted_iota(I32, el.shape, 0).astype(F32)
    v1 = jnp.max(el, axis=0, keepdims=True)
    i1 = jnp.min(jnp.where(el == v1, ei, float(epg)), axis=0, keepdims=True)
    el2 = jnp.where(ei == i1, -jnp.inf, el)
    v2 = jnp.max(el2, axis=0, keepdims=True)
    i2 = jnp.min(jnp.where(el2 == v2, ei, float(epg)), axis=0, keepdims=True)
    a1 = jnp.exp(v1 - v1)
    a2 = jnp.exp(v2 - v1)
    den = a1 + a2
    c1 = gw * (a1 / den)
    c2 = gw * (a2 / den)
    e1 = (gsel * float(epg) + i1).astype(I32)
    e2 = (gsel * float(epg) + i2).astype(I32)
    row8 = lax.broadcasted_iota(I32, (SUBLANES, t_rows), 0)
    re_ref[...] = jnp.where(row8 == 0, e1, jnp.where(row8 == 1, e2, 0))
    rc_ref[...] = jnp.where(row8 == 0, c1, jnp.where(row8 == 1, c2, 0.0))


def _mix(x2d, proj, kv, caches, weights, *, n_seq, sample, dims):
    n_heads, n_kv, head_dim, window, n_groups, n_exp, d_conv = dims
    n, d = x2d.shape
    t = MIX_ROWS
    n_chunks = t // CHUNK
    span = window + CHUNK
    d_attn = n_heads * head_dim
    d_kv = n_kv * head_dim
    convw, wco, wao, wo, bias, sinks, g2, wr, br = weights
    if sample:
        n_t = n // t
        grid = (n_t, 1)
        tok = lambda i, j: (i, 0)
        n_state = n // CHUNK
        state_blk = n_chunks
        st = lambda i, j: (i, 0, 0)
    else:
        n_t = (n // n_seq) // t
        grid = (n_seq, n_t)
        tok = lambda i, j: (i * n_t + j, 0)
        n_state = n_seq
        state_blk = 1
        st = lambda i, j: (i, 0, 0)
    const2 = lambda i, j: (0, 0)

    def col(width, idx):
        return pl.BlockSpec((t, width), lambda i, j: (tok(i, j)[0], idx))

    def resident(shape):
        return pl.BlockSpec(shape, const2, pipeline_mode=pl.Buffered(1))

    in_specs = [
        pl.BlockSpec((t, d), tok),
        col(d, 2), col(d, 3),
        col(d_conv, 0), col(d_conv, 1), col(d_conv, 2), col(d_attn, 3),
        pl.BlockSpec((t, 2 * d_kv), tok),
    ]
    args = [x2d, proj, proj, proj, proj, proj, proj, kv]
    if sample:
        cconv, ck, cv = caches
        in_specs += [
            pl.BlockSpec((n_chunks, cconv.shape[1], d_conv), st),
            pl.BlockSpec((n_chunks, window, d_kv), st),
            pl.BlockSpec((n_chunks, window, d_kv), st),
        ]
        args += [cconv, ck, cv]
    in_specs += [
        resident(convw.shape), resident(wco.shape), resident(wao.shape), resident(wo.shape),
        pl.BlockSpec(bias.shape, lambda i, j: (0, 0, 0), pipeline_mode=pl.Buffered(1)),
        pl.BlockSpec(memory_space=pltpu.SMEM),
        resident(g2.shape), resident(wr.shape), resident(br.shape),
    ]
    args += [convw, wco, wao, wo, bias, sinks, g2, wr, br]
    out_specs = [
        pl.BlockSpec((t, d), tok),
        pl.BlockSpec((t, 1, d // 2), lambda i, j: (tok(i, j)[0], 0, 0)),
        pl.BlockSpec((SUBLANES, t), lambda i, j: (0, tok(i, j)[0])),
        pl.BlockSpec((SUBLANES, t), lambda i, j: (0, tok(i, j)[0])),
        pl.BlockSpec((state_blk, 2, d_conv), st),
        pl.BlockSpec((state_blk, window, d_kv), st),
        pl.BlockSpec((state_blk, window, d_kv), st),
    ]
    out_shape = [
        jax.ShapeDtypeStruct((n, d), F32),
        jax.ShapeDtypeStruct((n, 1, d // 2), U32),
        jax.ShapeDtypeStruct((SUBLANES, n), I32),
        jax.ShapeDtypeStruct((SUBLANES, n), F32),
        jax.ShapeDtypeStruct((n_state, 2, d_conv), F32),
        jax.ShapeDtypeStruct((n_state, window, d_kv), F32),
        jax.ShapeDtypeStruct((n_state, window, d_kv), F32),
    ]
    kv_rows = n_chunks * span if sample else window + t
    scratch = [
        pltpu.VMEM((2 * KV_VARIANTS, kv_rows, d_kv), BF16),
        pltpu.VMEM((t, d_attn), BF16), pltpu.VMEM((t, d_conv), BF16),
        pltpu.VMEM((SUBLANES, d_conv), F32),
        pltpu.VMEM((n_chunks * n_kv, (n_heads // n_kv // 2) * CHUNK, 2 * KEY_PAD), F32),
        pltpu.VMEM((n_chunks * n_kv, (n_heads // n_kv // 2) * CHUNK, 2 * KEY_PAD), BF16),
        pltpu.VMEM((n_chunks * n_kv, (n_heads // n_kv // 2) * CHUNK, LANES), F32),
    ]
    est = (2 * t * d * 4 * 3 + 2 * 2 * t * d * 2 + 2 * 4 * t * d_conv * 2
           + (2 * d_conv * d + d * d) * 2 + 12 * t * d * 4)
    kern = functools.partial(_mix_kernel, sample=sample, n_heads=n_heads, n_kv=n_kv, head_dim=head_dim,
                             window=window, n_groups=n_groups, n_exp=n_exp)
    return pl.pallas_call(
        kern, grid=grid, in_specs=in_specs, out_specs=out_specs, out_shape=out_shape,
        scratch_shapes=scratch,
        compiler_params=pltpu.CompilerParams(
            dimension_semantics=("arbitrary", "arbitrary"), vmem_limit_bytes=_vmem_limit(est)),
        name="mix_sample" if sample else "mix_prompt",
    )(*args)


def _route_kernel(re_ref, pos_ref, tmap_ref, *, n_exp, tile_rows, blk):
    n = re_ref.shape[1]
    n_blk = n // blk
    erow = lax.broadcasted_iota(I32, (n_exp, blk), 0)

    def onehots(j):
        c0 = pl.multiple_of(j * blk, blk)
        oh0 = (erow == re_ref[0:1, pl.ds(c0, blk)]).astype(F32)
        oh1 = (erow == re_ref[1:2, pl.ds(c0, blk)]).astype(F32)
        return c0, oh0, oh1

    def count_body(j, cnt):
        _, oh0, oh1 = onehots(j)
        return cnt + jnp.sum(oh0 + oh1, axis=1, keepdims=True)

    cnt = lax.fori_loop(0, n_blk, count_body, jnp.zeros((n_exp, 1), F32))
    cnt = jnp.broadcast_to(cnt, (n_exp, LANES))
    padded = jnp.ceil(cnt / tile_rows) * tile_rows
    ends = padded
    prow = lax.broadcasted_iota(I32, ends.shape, 0)
    step = 1
    while step < n_exp:
        ends = ends + jnp.where(prow >= step, pltpu.roll(ends, step, axis=0), 0.0)
        step *= 2
    offs = ends - padded
    off1 = offs[:, 0:1]

    tri = (lax.broadcasted_iota(I32, (blk, blk), 0) <= lax.broadcasted_iota(I32, (blk, blk), 1)).astype(BF16)
    row8 = lax.broadcasted_iota(I32, (SUBLANES, blk), 0)

    def pos_body(j, run):
        c0, oh0, oh1 = onehots(j)
        both = oh0 + oh1
        csum = jnp.dot(both.astype(BF16), tri, preferred_element_type=F32) + run
        slot = off1 + csum - 1.0
        p0 = jnp.sum(oh0 * slot, axis=0, keepdims=True).astype(I32)
        p1 = jnp.sum(oh1 * slot, axis=0, keepdims=True).astype(I32)
        pos_ref[:, pl.ds(c0, blk)] = jnp.where(row8 == 0, p0, jnp.where(row8 == 1, p1, 0))
        return run + jnp.sum(both, axis=1, keepdims=True)

    lax.fori_loop(0, n_blk, pos_body, jnp.zeros((n_exp, 1), F32))

    n_tiles_pad = tmap_ref.shape[1]
    start = (lax.broadcasted_iota(I32, (1, n_tiles_pad), 1) * tile_rows).astype(F32)
    end1 = ends[:, 0:1]
    te = jnp.sum((end1 <= start).astype(F32), axis=0, keepdims=True)
    trow = lax.broadcasted_iota(I32, (n_exp, n_tiles_pad), 0).astype(F32)
    used_end = jnp.sum(jnp.where(trow == te, off1 + cnt[:, 0:1], 0.0), axis=0, keepdims=True)
    n_rows = jnp.clip(used_end - start, 0.0, float(tile_rows))
    has_tokens = cnt[:, 0:1] > 0.0
    te = jnp.minimum(te, jnp.max(jnp.where(has_tokens, trow, 0.0), axis=0, keepdims=True))
    nxt = jnp.min(jnp.where(trow > te, jnp.where(has_tokens, trow, float(n_exp)), float(n_exp)), axis=0, keepdims=True)
    r8 = lax.broadcasted_iota(I32, (SUBLANES, n_tiles_pad), 0)
    tmap_ref[...] = jnp.where(r8 == 0, te.astype(I32),
                              jnp.where(r8 == 1, n_rows.astype(I32), jnp.where(r8 == 2, nxt.astype(I32), 0)))


def _route(re, n_exp, n_tiles):
    n = re.shape[1]
    assert n % ROUTE_BLOCK == 0
    n_tiles_pad = pl.cdiv(n_tiles, LANES) * LANES
    kern = functools.partial(_route_kernel, n_exp=n_exp, tile_rows=SLOT_ROWS, blk=ROUTE_BLOCK)
    return pl.pallas_call(
        kern,
        out_shape=[jax.ShapeDtypeStruct((SUBLANES, n), I32), jax.ShapeDtypeStruct((SUBLANES, n_tiles_pad), I32)],
        name="route",
    )(re)


def _dispatch_kernel(*refs, aliased):
    if aliased:
        p0_ref, p1_ref, h_ref, _, xs_ref, sem = refs
    else:
        p0_ref, p1_ref, h_ref, xs_ref, sem = refs
    rows = h_ref.shape[0]

    def start(t, carry):
        pltpu.make_async_copy(h_ref.at[t], xs_ref.at[p0_ref[t]], sem).start(priority=0)
        pltpu.make_async_copy(h_ref.at[t], xs_ref.at[p1_ref[t]], sem).start(priority=1)
        return carry

    lax.fori_loop(0, rows, start, 0)
    for _ in range(TOP_K):
        pltpu.make_async_copy(h_ref, xs_ref.at[pl.ds(0, rows)], sem).wait()


def _dispatch(h2, p0, p1, xs, n_slots):
    n, _, d = h2.shape
    t = min(MOVE_ROWS, n)
    assert n % t == 0
    aliased = xs is not None
    in_specs = [
        pl.BlockSpec((t,), lambda i: (i,), memory_space=pltpu.SMEM),
        pl.BlockSpec((t,), lambda i: (i,), memory_space=pltpu.SMEM),
        pl.BlockSpec((t, 1, d), lambda i: (i, 0, 0)),
    ]
    args = [p0, p1, h2]
    if aliased:
        in_specs.append(pl.BlockSpec(memory_space=pl.ANY))
        args.append(xs)
    return pl.pallas_call(
        functools.partial(_dispatch_kernel, aliased=aliased),
        grid=(n // t,),
        in_specs=in_specs,
        out_specs=pl.BlockSpec(memory_space=pl.ANY),
        out_shape=jax.ShapeDtypeStruct((n_slots, 1, d), h2.dtype),
        scratch_shapes=[pltpu.SemaphoreType.DMA(())],
        input_output_aliases={3: 0} if aliased else {},
        compiler_params=pltpu.CompilerParams(dimension_semantics=("arbitrary",), has_side_effects=True),
        name="dispatch",
    )(*args)


def _experts_kernel(te_ref, nr_ref, nx_ref, xs_ref, wg_hbm, wu_hbm, wd_hbm, y_ref,
                    wg_st, wu_st, wd_st, wg_sc, wu_sc, wd_sc, sems, n_changes, *, n_exp, tile_rows):
    dh = xs_ref.shape[2]
    tiles_per_step = xs_ref.shape[0] // tile_rows

    def copies(e, slot):
        return (pltpu.make_async_copy(wg_hbm.at[e], wg_st.at[slot], sems.at[slot]),
                pltpu.make_async_copy(wu_hbm.at[e], wu_st.at[slot], sems.at[slot]),
                pltpu.make_async_copy(wd_hbm.at[e], wd_st.at[slot], sems.at[slot]))

    def one_tile(sub):
        tile = pl.program_id(0) * tiles_per_step + sub
        rows = pl.ds(sub * tile_rows, tile_rows)
        expert = te_ref[tile]

        @pl.when(tile == 0)
        def _():
            n_changes[0] = 0
            for cp in copies(expert, 0):
                cp.start(priority=1)

        @pl.when(jnp.logical_or(tile == 0, expert != te_ref[jnp.maximum(tile - 1, 0)]))
        def _():
            slot = lax.rem(n_changes[0], 2)
            n_changes[0] = n_changes[0] + 1
            for cp in copies(expert, slot):
                cp.wait()
            nxt = nx_ref[tile]

            @pl.when(nxt < n_exp)
            def _():
                for cp in copies(nxt, 1 - slot):
                    cp.start(priority=1)

            wg_sc[...] = wg_st[slot].astype(BF16)
            wu_sc[...] = wu_st[slot].astype(BF16)
            wd_sc[...] = wd_st[slot].astype(BF16)

        n_rows = nr_ref[tile]

        @pl.when(n_rows > 0)
        def _():
            live = lax.broadcasted_iota(I32, (tile_rows, 2 * dh), 0) < n_rows
            x = jnp.where(live, _unpack_halves(xs_ref[rows].reshape(tile_rows, dh)), 0.0).astype(BF16)
            gate = jnp.dot(x, wg_sc[...], preferred_element_type=F32)
            up = jnp.dot(x, wu_sc[...], preferred_element_type=F32)
            hid = (jax.nn.silu(gate) * up).astype(BF16)
            y = jnp.dot(hid, wd_sc[...], preferred_element_type=F32)
            y_ref[rows] = _pack_halves(y).reshape(tile_rows, 1, dh)

        @pl.when(n_rows <= 0)
        def _():
            y_ref[rows] = _pack_halves(jnp.zeros((tile_rows, 2 * dh), F32)).reshape(tile_rows, 1, dh)

    for sub in range(tiles_per_step):
        one_tile(sub)


def _experts(xs, te, nr, nx, w_gate, w_up, w_down):
    n_slots, _, dh = xs.shape
    n_exp, d, d_e = w_gate.shape
    t = SLOT_ROWS * EXPERT_STEP_TILES
    est = 2 * 2 * t * d * 2 + 2 * 3 * d * d_e * 4 + 3 * d * d_e * 2 + 4 * SLOT_ROWS * d * 4
    row_spec = pl.BlockSpec((t, 1, dh), lambda i, te, nr, nx: (i, 0, 0))
    return pl.pallas_call(
        functools.partial(_experts_kernel, n_exp=n_exp, tile_rows=SLOT_ROWS),
        grid_spec=pltpu.PrefetchScalarGridSpec(
            num_scalar_prefetch=3,
            grid=(n_slots // t,),
            in_specs=[row_spec, pl.BlockSpec(memory_space=pl.ANY), pl.BlockSpec(memory_space=pl.ANY),
                      pl.BlockSpec(memory_space=pl.ANY)],
            out_specs=row_spec,
            scratch_shapes=[
                pltpu.VMEM((2, d, d_e), F32), pltpu.VMEM((2, d, d_e), F32), pltpu.VMEM((2, d_e, d), F32),
                pltpu.VMEM((d, d_e), BF16), pltpu.VMEM((d, d_e), BF16), pltpu.VMEM((d_e, d), BF16),
                pltpu.SemaphoreType.DMA((2,)), pltpu.SMEM((1,), I32)],
        ),
        out_shape=jax.ShapeDtypeStruct((n_slots, 1, dh), U32),
        compiler_params=pltpu.CompilerParams(
            dimension_semantics=("arbitrary",), vmem_limit_bytes=_vmem_limit(est)),
        name="experts",
    )(te, nr, nx, xs, w_gate, w_up, w_down)


def _combine_kernel(p0_ref, p1_ref, p0n_ref, p1n_ref, xp_ref, rc_ref, g_ref, y_hbm, out_ref, y0_buf, y1_buf, sems,
                    *, n_tiles):
    i = pl.program_id(0)
    rows, d = xp_ref.shape

    def gather(pa_ref, pb_ref, slot):
        base = slot * rows
        def body(t, carry):
            pltpu.make_async_copy(y_hbm.at[pa_ref[t]], y0_buf.at[base + t], sems.at[slot]).start(priority=0)
            pltpu.make_async_copy(y_hbm.at[pb_ref[t]], y1_buf.at[base + t], sems.at[slot]).start(priority=1)
            return carry
        lax.fori_loop(0, rows, body, 0)

    slot = lax.rem(i, 2)

    @pl.when(i == 0)
    def _():
        gather(p0_ref, p1_ref, 0)

    @pl.when(i + 1 < n_tiles)
    def _():
        gather(p0n_ref, p1n_ref, 1 - slot)

    cur = pl.ds(pl.multiple_of(slot * rows, rows), rows)
    for buf in (y0_buf, y1_buf):
        pltpu.make_async_copy(y_hbm.at[pl.ds(0, rows)], buf.at[cur], sems.at[slot]).wait()
    ct = rc_ref[...].T
    y0 = _unpack_halves(y0_buf[cur].reshape(rows, d // 2))
    y1 = _unpack_halves(y1_buf[cur].reshape(rows, d // 2))
    moe = ct[:, 0:1] * y0 + ct[:, 1:2] * y1
    out_ref[...] = _rms_rows(xp_ref[...] + moe, g_ref[...])


def _combine(xp, rc, p0, p1, y, g):
    n, d = xp.shape
    t = min(MOVE_ROWS, n)
    assert n % t == 0
    n_t = n // t
    est = 2 * 2 * t * d * 4 + 2 * 2 * t * d * 2 + 8 * t * d * 4
    nxt = lambda i: (jnp.minimum(i + 1, n_t - 1),)
    return pl.pallas_call(
        functools.partial(_combine_kernel, n_tiles=n_t),
        grid=(n_t,),
        in_specs=[
            pl.BlockSpec((t,), lambda i: (i,), memory_space=pltpu.SMEM),
            pl.BlockSpec((t,), lambda i: (i,), memory_space=pltpu.SMEM),
            pl.BlockSpec((t,), nxt, memory_space=pltpu.SMEM),
            pl.BlockSpec((t,), nxt, memory_space=pltpu.SMEM),
            pl.BlockSpec((t, d), lambda i: (i, 0)),
            pl.BlockSpec((SUBLANES, t), lambda i: (0, i)),
            pl.BlockSpec((1, d), lambda i: (0, 0)),
            pl.BlockSpec(memory_space=pl.ANY),
        ],
        out_specs=pl.BlockSpec((t, d), lambda i: (i, 0)),
        out_shape=jax.ShapeDtypeStruct((n, d), F32),
        scratch_shapes=[pltpu.VMEM((2 * t, 1, d // 2), U32), pltpu.VMEM((2 * t, 1, d // 2), U32),
                        pltpu.SemaphoreType.DMA((2,))],
        compiler_params=pltpu.CompilerParams(
            dimension_semantics=("arbitrary",), vmem_limit_bytes=_vmem_limit(est)),
        name="combine",
    )(p0, p1, p0, p1, xp, rc, g, y)


def _rel_buckets(rel):
    nb = N_BUCKETS // 2
    ret = (rel > 0).astype(I32) * nb
    n = jnp.abs(rel)
    max_exact = nb // 2
    nf = jnp.maximum(n, 1).astype(F32)
    large = max_exact + (jnp.log(nf / max_exact) / math.log(MAX_DISTANCE / max_exact)
                         * (nb - max_exact)).astype(I32)
    large = jnp.minimum(large, nb - 1)
    return ret + jnp.where(n < max_exact, n, large)


def _rel_bias(table, window, n_kv):
    n_heads = table.shape[1]
    span = window + CHUNK
    rows = table[_rel_buckets(jnp.arange(-(span - 1), CHUNK, dtype=I32))].astype(F32)
    bias = jnp.stack([rows[CHUNK - 1 - q:CHUNK - 1 - q + span] for q in range(CHUNK)])
    bias = jnp.transpose(bias, (2, 0, 1))
    bias = jnp.pad(bias, ((0, 0), (0, 0), (0, KEY_PAD - bias.shape[2])), constant_values=NEG_INF)
    bias = bias.reshape(n_kv, n_heads // n_kv // 2, 2, CHUNK, KEY_PAD)
    return jnp.transpose(bias, (0, 1, 3, 2, 4)).reshape(n_kv, (n_heads // n_kv // 2) * CHUNK, 2 * KEY_PAD)


def kernel(x_prompt, x_sample, cache_conv, cache_k, cache_v, rel_bias_table, norm_mix_g, w_in, conv_w, w_conv_out, attn_sinks, w_attn_out, w_o, norm_ffn_g, w_group, b_group, w_expert_router, b_expert_router, w_gate, w_up, w_down, final_norm_g):
    assert w_in.shape[0] == 1, "single-layer step"
    batch, seq, d = x_prompt.shape
    dec_batch, dec_seq, _ = x_sample.shape
    assert dec_seq == CHUNK and seq % MIX_ROWS == 0 and (dec_batch * dec_seq) % MIX_ROWS == 0
    d_conv = conv_w.shape[-1]
    window, n_kv, head_dim = cache_k.shape[2], cache_k.shape[3], cache_k.shape[4]
    n_heads = attn_sinks.shape[-1]
    d_attn, d_kv = n_heads * head_dim, n_kv * head_dim
    n_groups, n_exp = w_group.shape[-1], w_expert_router.shape[-1]
    assert n_groups <= GROUP_ROW0 and d_conv == d_attn and 2 * d_conv == d
    assert n_kv == 2 and d_kv == LANES and (n_heads // n_kv) % 2 == 0 and window + CHUNK <= KEY_PAD
    dims = (n_heads, n_kv, head_dim, window, n_groups, n_exp, d_conv)

    kv0 = 3 * d_conv + d_attn
    g1 = norm_mix_g[0][None, :]
    wr = jnp.zeros((GROUP_ROW0 + n_exp, d), F32)
    wr = wr.at[:n_groups].set(w_group[0].T).at[GROUP_ROW0:].set(w_expert_router[0].T).astype(BF16)
    br = jnp.zeros((GROUP_ROW0 + n_exp, 1), F32)
    br = br.at[:n_groups, 0].set(b_group[0]).at[GROUP_ROW0:, 0].set(b_expert_router[0])
    xp2d = x_prompt.reshape(batch * seq, d)
    xs2d = x_sample.reshape(dec_batch * dec_seq, d)
    n_p, n_s = xp2d.shape[0], xs2d.shape[0]
    n_tok = n_p + n_s

    proj_s, kv_s, _, w_bf16 = _inproj(xs2d, g1, w_in[0], kv0, 2 * d_kv, emit_bf16=True)
    proj_p, kv_p, (wco, wao, wo), _ = _inproj(xp2d, g1, w_bf16, kv0, 2 * d_kv,
                                              side_casts=(w_conv_out[0], w_attn_out[0], w_o[0]))
    weights = (conv_w[0], wco, wao, wo,
               _rel_bias(rel_bias_table, window, n_kv), attn_sinks, norm_ffn_g[0][None, :], wr, br)
    xres_p, h2_p, re_p, rc_p, conv_p, k_p, v_p = _mix(
        xp2d, proj_p, kv_p, None, weights, n_seq=batch, sample=False, dims=dims)
    caches = (cache_conv[0], cache_k[0].reshape(dec_batch, window, d_kv), cache_v[0].reshape(dec_batch, window, d_kv))
    xres_s, h2_s, re_s, rc_s, conv_s, k_s, v_s = _mix(
        xs2d, proj_s, kv_s, caches, weights, n_seq=dec_batch, sample=True, dims=dims)

    n_tiles = pl.cdiv((TOP_K * n_tok) // SLOT_ROWS + n_exp, EXPERT_STEP_TILES) * EXPERT_STEP_TILES
    n_slots = n_tiles * SLOT_ROWS
    pos, tmap = _route(jnp.concatenate([re_p, re_s], axis=1), n_exp, n_tiles)
    p0, p1 = pos[0], pos[1]
    xs = _dispatch(h2_p, p0[:n_p], p1[:n_p], None, n_slots)
    xs = _dispatch(h2_s, p0[n_p:], p1[n_p:], xs, n_slots)
    y = _experts(xs, tmap[0, :n_tiles], tmap[1, :n_tiles], tmap[2, :n_tiles], w_gate[0], w_up[0], w_down[0])
    gf = final_norm_g[None, :]
    y_prompt = _combine(xres_p, rc_p, p0[:n_p], p1[:n_p], y, gf).reshape(batch, seq, d)
    y_sample = _combine(xres_s, rc_s, p0[n_p:], p1[n_p:], y, gf).reshape(dec_batch, dec_seq, d)

    kv_shape = (1, -1, window, n_kv, head_dim)
    return (y_prompt, y_sample, conv_p[None], k_p.reshape(kv_shape), v_p.reshape(kv_shape),
            conv_s[None], k_s.reshape(kv_shape), v_s.reshape(kv_shape))
```
